```python
import jax, jax.numpy as jnp
from jax import lax
import numpy as np

D_MODEL = 2048
BATCH = 4
SEQ = 2048
DEPTH = 4
DEC_BATCH = 128
DEC_SEQ = 8
PAST_LEN = 16384
PAGE_SIZE = 128

RWKV_WIDTH = D_MODEL // 2
HEAD_DIM = 64
N_HEADS = RWKV_WIDTH // HEAD_DIM
POOL_WIDTH = D_MODEL // 2
POOL_WINDOWS = (2, 4, 8, 16)
N_POOL_GROUPS = len(POOL_WINDOWS)
POOL_GROUP = POOL_WIDTH // N_POOL_GROUPS
POOL_BUF = max(POOL_WINDOWS) - 1
PLE_DIM = 256
DECAY_RANK = max(32, int(round(1.8 * D_MODEL ** 0.5 / 32)) * 32)
AAA_RANK = max(32, int(round(1.8 * D_MODEL ** 0.5 / 32)) * 32)
MV_RANK = max(32, int(round(1.3 * D_MODEL ** 0.5 / 32)) * 32)
N_BRANCH = 2
IN_COLS = 4 * RWKV_WIDTH + 2 * POOL_WIDTH + N_BRANCH * D_MODEL
RMS_EPS = 1e-6
GN_EPS = 64e-5
DECAY_SCALE = 0.606531

kernel_name = "rwkv7_pool_gated_hybrid_step"


def _rmsnorm(x, g):
    xf = x.astype(jnp.float32)
    y = xf * lax.rsqrt(jnp.mean(xf * xf, axis=-1, keepdims=True) + RMS_EPS)
    return (y * g.astype(jnp.float32)).astype(x.dtype)


def _wkv7_scan(S0, r, decay, k, v, a_vec, b_vec):
    def step(S, inp):
        r_t, w_t, k_t, v_t, a_t, b_t = inp
        sa = jnp.einsum('bhij,bhj->bhi', S, a_t)
        S = S * w_t[..., None, :] + sa[..., :, None] * b_t[..., None, :] + v_t[..., :, None] * k_t[..., None, :]
        o = jnp.einsum('bhij,bhj->bhi', S, r_t)
        return S, o
    xs = tuple(jnp.swapaxes(t.astype(jnp.float32), 0, 1) for t in (r, decay, k, v, a_vec, b_vec))
    S, o = lax.scan(step, S0.astype(jnp.float32), xs)
    return S, jnp.swapaxes(o, 0, 1)


def _causal_pool(u, buf, n_past):
    T = u.shape[1]
    u_ext = jnp.concatenate([buf.astype(u.dtype), u], axis=1)
    cs = jnp.cumsum(u_ext.astype(jnp.float32), axis=1)
    cs = jnp.concatenate([jnp.zeros_like(cs[:, :1]), cs], axis=1)
    hi = cs[:, POOL_BUF + 1:]
    pos = jnp.arange(T) + n_past + 1
    outs = []
    for gi, win in enumerate(POOL_WINDOWS):
        sl = slice(gi * POOL_GROUP, (gi + 1) * POOL_GROUP)
        lo = cs[:, POOL_BUF + 1 - win: POOL_BUF + 1 - win + T, sl]
        cnt = jnp.minimum(pos, win).astype(jnp.float32)[None, :, None]
        outs.append((hi[..., sl] - lo) / cnt)
    mean = jnp.stack(outs, axis=2)
    return mean, u_ext[:, -POOL_BUF:]


def _hybrid_layer(x, p, wkv0, shift0, pool0, n_past, v_first, lw):
    B, T, _ = x.shape
    R, P, H, N, D = RWKV_WIDTH, POOL_WIDTH, N_HEADS, HEAD_DIM, D_MODEL
    f32 = jnp.float32
    h = _rmsnorm(x, lw['norm_pre'])
    z = h @ lw['w_in']
    z_rkv = z[..., :3 * R]
    prev0 = (shift0.astype(h.dtype) @ lw['w_in'][:, :3 * R])[:, None]
    z_rkv_prev = jnp.concatenate([prev0, z_rkv[:, :-1]], axis=1)
    z_rkv = z_rkv + (z_rkv_prev - z_rkv) * lw['mu_rkv'].reshape(3 * R)
    r, k, v = z_rkv[..., :R], z_rkv[..., R:2 * R], z_rkv[..., 2 * R:]
    o_ = 3 * R
    g_rwkv = z[..., o_:o_ + R]; o_ += R
    u = z[..., o_:o_ + P]; o_ += P
    g_pool = z[..., o_:o_ + P]; o_ += P
    gate_a = z[..., o_:o_ + D]; o_ += D
    gate_b = z[..., o_:o_ + D]
    h_prev = jnp.concatenate([shift0.astype(h.dtype)[:, None], h[:, :-1]], axis=1)
    dh = h_prev - h
    xw = h + dh * lw['mu_lora'][0]
    xa = h + dh * lw['mu_lora'][1]
    decay = jnp.exp(-DECAY_SCALE * jax.nn.sigmoid(
        (lw['w0'] + jnp.tanh(xw @ lw['w1']) @ lw['w2']).astype(f32)))
    a = jax.nn.sigmoid((lw['a0'] + (xa @ lw['a1']) @ lw['a2']).astype(f32))
    if v_first is None:
        v_first = v
    else:
        xv = h + dh * lw['mu_lora'][2]
        v = v + (v_first - v) * jax.nn.sigmoid(lw['v0'] + (xv @ lw['v1']) @ lw['v2'])
    kf = k.astype(f32)
    kk = (kf * lw['k_k']).reshape(B, T, H, N)
    kk = kk / jnp.maximum(jnp.linalg.norm(kk, axis=-1, keepdims=True), 1e-12)
    kf = kf * (1.0 + (a - 1.0) * lw['k_a'])
    rh = r.astype(f32).reshape(B, T, H, N)
    kh = kf.reshape(B, T, H, N)
    vh = v.astype(f32).reshape(B, T, H, N)
    ah = a.reshape(B, T, H, N)
    S, o = _wkv7_scan(wkv0, rh, decay.reshape(B, T, H, N), kh, vh, -kk, kk * ah)
    mu = jnp.mean(o, axis=-1, keepdims=True)
    var = jnp.mean(jnp.square(o - mu), axis=-1, keepdims=True)
    on = ((o - mu) * lax.rsqrt(var + GN_EPS)).reshape(B, T, R) * lw['ln_w'] + lw['ln_b']
    bonus = jnp.sum(rh * kh * lw['r_k'].astype(f32).reshape(H, N), axis=-1, keepdims=True) * vh
    y_a = (on + bonus.reshape(B, T, R)).astype(x.dtype) * jax.nn.silu(g_rwkv)
    mean, pool_new = _causal_pool(u, pool0, n_past)
    mixed = (mean - u.astype(f32).reshape(B, T, N_POOL_GROUPS, POOL_GROUP)).astype(x.dtype)
    y_b = jnp.einsum('btgc,gcd->btgd', mixed, lw['pool_w']).reshape(B, T, P)
    y_b = y_b * lw['pool_scale'] * jax.nn.silu(g_pool)
    m = jax.nn.sigmoid(gate_a) * (y_a @ lw['w_branch'][0]) + jax.nn.sigmoid(gate_b) * (y_b @ lw['w_branch'][1])
    x = x + _rmsnorm(m @ lw['w_out'], lw['norm_post'])
    x = x + jax.nn.sigmoid(x @ lw['w_ple_gate']) * (p.astype(x.dtype) @ lw['w_ple'])
    return x, v_first, S.astype(x.dtype), h[:, -1], pool_new


def setup_inputs(seed: int = 0) -> dict:
    key = jax.random.key(seed)
    ks = iter(jax.random.split(key, 48))
    nrm = lambda shape, s: jax.random.normal(next(ks), shape, jnp.float32) * s
    uni = lambda shape: jax.random.uniform(next(ks), shape, jnp.float32)
    L, R, P, D = DEPTH, RWKV_WIDTH, POOL_WIDTH, D_MODEL
    return {
        "x_prompt": nrm((BATCH, SEQ, D), 1.0),
        "x_sample": nrm((DEC_BATCH, DEC_SEQ, D), 1.0),
        "state_wkv": nrm((L, DEC_BATCH, N_HEADS, HEAD_DIM, HEAD_DIM), 0.3),
        "state_shift": nrm((L, DEC_BATCH, D), 1.0),
        "state_pool": nrm((L, DEC_BATCH, POOL_BUF, P), 1.0),
        "p_prompt": nrm((L, BATCH, SEQ, PLE_DIM), 1.0),
        "p_sample": nrm((L, DEC_BATCH, DEC_SEQ, PLE_DIM), 1.0),
        "norm_pre": 1.0 + nrm((L, D), 0.05),
        "norm_post": 1.0 + nrm((L, D), 0.05),
        "w_in": nrm((L, D, IN_COLS), D ** -0.5),
        "mu_rkv": uni((L, 3, R)),
        "mu_lora": uni((L, 3, D)),
        "w0": nrm((L, R), 1.0),
        "w1": nrm((L, D, DECAY_RANK), D ** -0.5),
        "w2": nrm((L, DECAY_RANK, R), 0.5 * DECAY_RANK ** -0.5),
        "a0": nrm((L, R), 0.1),
        "a1": nrm((L, D, AAA_RANK), D ** -0.5),
        "a2": nrm((L, AAA_RANK, R), 0.5 * AAA_RANK ** -0.5),
        "v0": nrm((L - 1, R), 0.1),
        "v1": nrm((L - 1, D, MV_RANK), D ** -0.5),
        "v2": nrm((L - 1, MV_RANK, R), 0.5 * MV_RANK ** -0.5),
        "k_k": 0.85 + nrm((L, R), 0.05),
        "k_a": 1.0 + nrm((L, R), 0.05),
        "r_k": nrm((L, R), 0.1),
        "ln_w": 1.0 + nrm((L, R), 0.05),
        "ln_b": nrm((L, R), 0.01),
        "pool_w": nrm((L, N_POOL_GROUPS, POOL_GROUP, POOL_GROUP), POOL_GROUP ** -0.5),
        "pool_scale": 1.0 + nrm((L, P), 0.05),
        "w_branch": nrm((L, N_BRANCH, R, D), R ** -0.5),
        "w_out": nrm((L, D, D), D ** -0.5),
        "w_ple": nrm((L, PLE_DIM, D), PLE_DIM ** -0.5),
        "w_ple_gate": nrm((L, D, D), D ** -0.5),
    }


def reference(x_prompt, x_sample, state_wkv, state_shift, state_pool, p_prompt, p_sample,
              norm_pre, norm_post, w_in, mu_rkv, mu_lora, w0, w1, w2, a0, a1, a2, v0, v1, v2,
              k_k, k_a, r_k, ln_w, ln_b, pool_w, pool_scale, w_branch, w_out, w_ple, w_ple_gate):
    def layer_weights(i):
        return dict(
            norm_pre=norm_pre[i], norm_post=norm_post[i], w_in=w_in[i], mu_rkv=mu_rkv[i],
            mu_lora=mu_lora[i], w0=w0[i], w1=w1[i], w2=w2[i], a0=a0[i], a1=a1[i], a2=a2[i],
            v0=v0[i - 1] if i > 0 else None, v1=v1[i - 1] if i > 0 else None,
            v2=v2[i - 1] if i > 0 else None,
            k_k=k_k[i], k_a=k_a[i], r_k=r_k[i], ln_w=ln_w[i], ln_b=ln_b[i],
            pool_w=pool_w[i], pool_scale=pool_scale[i], w_branch=w_branch[i], w_out=w_out[i],
            w_ple=w_ple[i], w_ple_gate=w_ple_gate[i])

    def run_group(x, p, wkv, shift, pool, n_past):
        v_first = None
        wkv_out, shift_out, pool_out = [], [], []
        for i in range(DEPTH):
            x, v_first, S, h_last, pool_new = _hybrid_layer(
                x, p[i], wkv[i], shift[i], pool[i], n_past, v_first, layer_weights(i))
            wkv_out.append(S); shift_out.append(h_last); pool_out.append(pool_new)
        return x, jnp.stack(wkv_out), jnp.stack(shift_out), jnp.stack(pool_out)

    dt = x_prompt.dtype
    zeros_wkv = jnp.zeros((DEPTH, BATCH, N_HEADS, HEAD_DIM, HEAD_DIM), dt)
    zeros_shift = jnp.zeros((DEPTH, BATCH, D_MODEL), dt)
    zeros_pool = jnp.zeros((DEPTH, BATCH, POOL_BUF, POOL_WIDTH), dt)
    y_prompt, wkv_p, shift_p, pool_p = run_group(x_prompt, p_prompt, zeros_wkv, zeros_shift, zeros_pool, 0)
    y_sample, wkv_s, shift_s, pool_s = run_group(x_sample, p_sample, state_wkv, state_shift, state_pool, PAST_LEN)
    return (y_prompt, y_sample, wkv_p, shift_p, pool_p, wkv_s, shift_s, pool_s)
```

```python
import functools

import jax
import jax.numpy as jnp
import numpy as np
from jax import lax
from jax.experimental import pallas as pl
from jax.experimental.pallas import tpu as pltpu

F32 = jnp.float32
BF16 = jnp.bfloat16

D_MODEL = 2048
DEPTH = 4
PAST_LEN = 16384
RWKV_WIDTH = 1024
HEAD_DIM = 64
N_HEADS = 16
POOL_WIDTH = 1024
POOL_WINDOWS = (2, 4, 8, 16)
POOL_GROUP = 256
POOL_BUF = 15
PLE_DIM = 256
DECAY_RANK = 96
AAA_RANK = 96
MV_RANK = 64
LORA_COLS = DECAY_RANK + AAA_RANK + MV_RANK
IN_COLS = 10240
EXT_COLS = IN_COLS + 2 * LORA_COLS
RMS_EPS = 1e-6
GN_EPS = 64e-5
DECAY_SCALE = 0.606531

HEADS_PER_GROUP = 4
GROUP_LANES = HEADS_PER_GROUP * HEAD_DIM
N_GROUPS = N_HEADS // HEADS_PER_GROUP
CHUNK = 64
HALO = 16

VMEM_LIMIT_BYTES = 56 * 1024 * 1024


def _sigmoid(x):
    return 1.0 / (1.0 + jnp.exp(-x))


def _split_bf16(x):
    hi = x.astype(BF16)
    lo = (x - hi.astype(F32)).astype(BF16)
    return hi, lo


_NN = (((1,), (0,)), ((), ()))
_NT = (((1,), (1,)), ((), ()))
_TN = (((0,), (0,)), ((), ()))


def _mm(a, b, dims=_NN, passes=1):
    dg = functools.partial(lax.dot_general, dimension_numbers=dims, preferred_element_type=F32)
    if passes == 1:
        return dg(a.astype(BF16), b.astype(BF16))
    ah, al = _split_bf16(a)
    bh, bl = _split_bf16(b)
    return dg(ah, bh) + (dg(ah, bl) + dg(al, bh))


def _segsum(x, bones_bf16):
    hi, lo = _split_bf16(x)
    dg = functools.partial(lax.dot_general, dimension_numbers=_NN, preferred_element_type=F32)
    return dg(hi, bones_bf16) + dg(lo, bones_bf16)


def _proj_kernel(x_ref, g_ref, w_ref, z_ref, hb_ref, *, normalize):
    @pl.when(pl.program_id(1) == 0)
    def _():
        x = x_ref[...]
        if normalize:
            x = x * lax.rsqrt(jnp.mean(x * x, axis=-1, keepdims=True) + RMS_EPS) * g_ref[...]
        hb_ref[...] = x.astype(BF16)

    z_ref[...] = jnp.dot(hb_ref[...], w_ref[...], preferred_element_type=F32)


def _proj(x2d, gain, w_bf16, *, normalize, tm, tn):
    m, d = x2d.shape
    n = w_bf16.shape[1]
    assert m % tm == 0 and n % tn == 0
    return pl.pallas_call(
        functools.partial(_proj_kernel, normalize=normalize),
        grid=(m // tm, n // tn),
        in_specs=[
            pl.BlockSpec((tm, d), lambda i, j: (i, 0)),
            pl.BlockSpec((1, d), lambda i, j: (0, 0)),
            pl.BlockSpec((d, tn), lambda i, j: (0, j)),
        ],
        out_specs=pl.BlockSpec((tm, tn), lambda i, j: (i, j)),
        out_shape=jax.ShapeDtypeStruct((m, n), F32),
        scratch_shapes=[pltpu.VMEM((tm, d), BF16)],
        name="proj",
        compiler_params=pltpu.CompilerParams(
            dimension_semantics=("parallel", "arbitrary"), vmem_limit_bytes=VMEM_LIMIT_BYTES),
    )(x2d, gain.reshape(1, d), w_bf16)


def _norm_rows_kernel(x_ref, g_ref, o_ref):
    x = x_ref[...]
    o_ref[...] = x * lax.rsqrt(jnp.mean(x * x, axis=-1, keepdims=True) + RMS_EPS) * g_ref[...]


def _norm_rows(x2d, gain):
    return pl.pallas_call(
        _norm_rows_kernel, out_shape=jax.ShapeDtypeStruct(x2d.shape, F32), name="norm_rows",
    )(x2d, gain.reshape(1, -1))


_V_MU_R, _V_MU_K, _V_MU_V, _V_W0, _V_A0, _V_V0, _V_KK, _V_KA, _V_RK, _V_LNW, _V_LNB = range(11)
_N_VEC_ROWS = 16


def _wkv_constants(n_seq, t_seq):
    c, e = CHUNK, HEADS_PER_GROUP * CHUNK
    assert n_seq * t_seq == c
    row = np.arange(c)
    q_of = row // t_seq
    tri = (row[:, None] >= row[None, :]) & (q_of[:, None] == q_of[None, :])
    erow = np.arange(e)
    es = erow % c
    same = q_of[:, None] == (es // t_seq)[None, :]
    mask_s = same & (row[:, None] > es[None, :])
    mask_i = same & (row[:, None] >= es[None, :])
    lane = np.arange(GROUP_LANES)
    bones = (lane[:, None] // HEAD_DIM) == (lane[None, :] // HEAD_DIM)
    levels = [np.eye(e, dtype=bool), (erow[:, None] // 2) == (erow[None, :] // 2)]
    m = 2
    while m < t_seq:
        levels.append(((erow[:, None] // (2 * m)) == (erow[None, :] // (2 * m)))
                      & ((erow[:, None] % (2 * m)) >= m) & ((erow[None, :] % (2 * m)) < m))
        m *= 2
    col = np.arange(2 * c)
    qmask = ((col[None, :] % c) // t_seq) == np.arange(n_seq)[:, None]
    f = lambda a: jnp.asarray(a.astype(np.float32))
    return dict(tri=f(tri).astype(BF16), mask_s=f(mask_s), mask_i=f(mask_i), bones=f(bones),
                bones_bf16=f(bones).astype(BF16), levels=f(np.stack(levels)),
                qmask=f(qmask).reshape(n_seq, 1, 2 * c))


def _wkv_kernel(*refs, n_seq, t_seq, n_chunks, has_vres, n_levels, rec_passes):
    it = iter(refs)
    zr_ref, zk_ref, zv_ref, zg_ref, zl_ref = (next(it) for _ in range(5))
    p0r_ref, p0k_ref, p0v_ref, p0l_ref = (next(it) for _ in range(4))
    vfirst_ref = next(it) if has_vres else None
    w2w_ref, w2a_ref = next(it), next(it)
    w2v_ref = next(it) if has_vres else None
    vec_ref, s0_ref = next(it), next(it)
    tri_ref, ms_ref, mi_ref, bones_ref, bonesb_ref, lev_ref, qmask_ref = (next(it) for _ in range(7))
    ya_ref = next(it)
    vout_ref = None if has_vres else next(it)
    sout_ref = next(it)
    sbd_ref, prev_ref = next(it), next(it)

    c_idx = pl.program_id(2)
    gl = GROUP_LANES
    bones = bones_ref[...]
    bones_b = bonesb_ref[...]

    def expand(x):
        return jnp.concatenate([x] * HEADS_PER_GROUP, axis=0) * bones

    @pl.when(c_idx == 0)
    def _():
        for q in range(n_seq):
            sbd_ref[q] = expand(s0_ref[q, 0])
        prev_ref[:, 0 * gl:1 * gl] = p0r_ref[0]
        prev_ref[:, 1 * gl:2 * gl] = p0k_ref[0]
        prev_ref[:, 2 * gl:3 * gl] = p0v_ref[0]
        prev_ref[:, 3 * gl:4 * gl] = p0l_ref[0]

    row = lax.broadcasted_iota(jnp.int32, (CHUNK, 1), 0)
    is_first = (row % t_seq) == 0

    def shifted(z, p0):
        rolled = pltpu.roll(z, 1, axis=0)
        if n_seq == 1:
            first = jnp.broadcast_to(p0, z.shape)
        else:
            first = jnp.broadcast_to(p0[:, None, :], (n_seq, t_seq, gl)).reshape(z.shape)
        return jnp.where(is_first, first, rolled)

    vec = vec_ref[...]
    vrow = lambda i: vec[i:i + 1, :]

    zr, zk, zv = zr_ref[...], zk_ref[...], zv_ref[...]
    zl = zl_ref[...]
    la, lb = zl[:, :LORA_COLS], zl[:, LORA_COLS:]
    zr_p = shifted(zr, prev_ref[:, 0 * gl:1 * gl])
    zk_p = shifted(zk, prev_ref[:, 1 * gl:2 * gl])
    zv_p = shifted(zv, prev_ref[:, 2 * gl:3 * gl])
    lb_p = shifted(lb, prev_ref[:, 3 * gl:4 * gl])
    if n_chunks > 1:
        assert n_seq == 1
        prev_ref[:, 0 * gl:1 * gl] = zr[CHUNK - 1:CHUNK, :]
        prev_ref[:, 1 * gl:2 * gl] = zk[CHUNK - 1:CHUNK, :]
        prev_ref[:, 2 * gl:3 * gl] = zv[CHUNK - 1:CHUNK, :]
        prev_ref[:, 3 * gl:4 * gl] = lb[CHUNK - 1:CHUNK, :]

    r = zr + (zr_p - zr) * vrow(_V_MU_R)
    k = zk + (zk_p - zk) * vrow(_V_MU_K)
    v = zv + (zv_p - zv) * vrow(_V_MU_V)

    lin = la + lb_p
    lane = lax.broadcasted_iota(jnp.int32, (1, LORA_COLS), 1)
    lin = jnp.where(lane < DECAY_RANK, jnp.tanh(lin), lin).astype(BF16)
    dw = jnp.dot(lin, w2w_ref[...], preferred_element_type=F32)
    da = jnp.dot(lin, w2a_ref[...], preferred_element_type=F32)
    wlog = -DECAY_SCALE * _sigmoid(vrow(_V_W0) + dw)
    alr = _sigmoid(vrow(_V_A0) + da)
    if has_vres:
        dv = jnp.dot(lin, w2v_ref[...], preferred_element_type=F32)
        v = v + (vfirst_ref[...] - v) * _sigmoid(vrow(_V_V0) + dv)
    else:
        vout_ref[...] = v

    kkr = k * vrow(_V_KK)
    kk = kkr / jnp.maximum(jnp.sqrt(_segsum(kkr * kkr, bones_b)), 1e-12)
    kf = k * (1.0 + (alr - 1.0) * vrow(_V_KA))

    w_hi, w_lo = _split_bf16(wlog)
    w_lo2 = (wlog - w_hi.astype(F32) - w_lo.astype(F32)).astype(BF16)
    tri = tri_ref[...]
    cum = (jnp.dot(tri, w_hi, preferred_element_type=F32) + jnp.dot(tri, w_lo, preferred_element_type=F32)
           + jnp.dot(tri, w_lo2, preferred_element_type=F32))
    p_inc = jnp.exp(cum)
    inv_p = jnp.exp(-cum)
    a_t = -kk * jnp.exp(cum - wlog)
    b_t = kk * alr * inv_p
    k_t = kf * inv_p
    r_t = r * p_inc

    mm = functools.partial(_mm, passes=rec_passes)
    b_e, k_e, v_e = expand(b_t), expand(k_t), expand(v)
    ar = jnp.concatenate([a_t, r_t], axis=0)
    sc = mm(ar, jnp.concatenate([b_e, k_e], axis=0), _NT)
    e = HEADS_PER_GROUP * CHUNK
    mask_s, mask_i = ms_ref[...], mi_ref[...]
    ab_c = sc[:CHUNK, :e] * mask_s
    ak_c = sc[:CHUNK, e:] * mask_s
    rb_c = sc[CHUNK:, :e] * mask_i
    rk_c = sc[CHUNK:, e:] * mask_i

    n_bd = jnp.concatenate([ab_c] * HEADS_PER_GROUP, axis=0) * bones
    x = lev_ref[0] + n_bd * lev_ref[1]
    for lvl in range(2, n_levels):
        x = x + mm(mm(x, n_bd * lev_ref[lvl]), x)
    x_c = x[0:CHUNK] + x[CHUNK:2 * CHUNK] + x[2 * CHUNK:3 * CHUNK] + x[3 * CHUNK:4 * CHUNK]

    parts_a, parts_r = [], []
    for q in range(n_seq):
        lo_, hi_ = q * t_seq, (q + 1) * t_seq
        lhs = jnp.concatenate([a_t[lo_:hi_], r_t[lo_:hi_]], axis=0)
        res = mm(lhs, sbd_ref[q], _NT)
        parts_a.append(res[:t_seq])
        parts_r.append(res[t_seq:])
    a_s = parts_a[0] if n_seq == 1 else jnp.concatenate(parts_a, axis=0)
    r_s = parts_r[0] if n_seq == 1 else jnp.concatenate(parts_r, axis=0)

    y = a_s + mm(ak_c, v_e)
    u = mm(x_c, expand(y))
    o = r_s + mm(jnp.concatenate([rb_c, rk_c], axis=1), jnp.concatenate([expand(u), v_e], axis=0))

    uv_t = jnp.concatenate([u, v], axis=0).T
    bk = jnp.concatenate([b_t, k_t], axis=0)
    for q in range(n_seq):
        lhs = uv_t if n_seq == 1 else uv_t * qmask_ref[q]
        p_last = p_inc[(q + 1) * t_seq - 1:(q + 1) * t_seq, :]
        sbd_ref[q] = bones * ((sbd_ref[q] + mm(lhs, bk)) * p_last)

    mean = _segsum(o, bones_b) * (1.0 / HEAD_DIM)
    d = o - mean
    var = _segsum(d * d, bones_b) * (1.0 / HEAD_DIM)
    on = d * lax.rsqrt(var + GN_EPS) * vrow(_V_LNW) + vrow(_V_LNB)
    bonus = _segsum(r * kf * vrow(_V_RK), bones_b) * v
    g = zg_ref[...]
    ya_ref[...] = ((on + bonus) * (g * _sigmoid(g))).astype(ya_ref.dtype)

    @pl.when(c_idx == n_chunks - 1)
    def _():
        for q in range(n_seq):
            s = sbd_ref[q]
            sout_ref[q, 0] = (s[0:HEAD_DIM] + s[HEAD_DIM:2 * HEAD_DIM]
                              + s[2 * HEAD_DIM:3 * HEAD_DIM] + s[3 * HEAD_DIM:4 * HEAD_DIM])


def _wkv(z, prev0, vfirst, w2e, vecs, s0, *, n_batch, t_len, n_seq, rec_passes):
    m = z.shape[0]
    t_seq = CHUNK // n_seq
    n_chunks = t_len // t_seq
    assert n_chunks == 1 or n_seq == 1
    nb = n_batch // n_seq
    has_vres = vfirst is not None
    consts = _wkv_constants(n_seq, t_seq)
    n_levels = consts["levels"].shape[0]
    gl = GROUP_LANES
    rb = lambda b, g, c: b * n_chunks + c

    def zspec(col0):
        return pl.BlockSpec((CHUNK, gl), lambda b, g, c: (rb(b, g, c), col0 + g))

    def const_spec(a):
        nd = a.ndim
        return pl.BlockSpec(a.shape, lambda b, g, c: (0,) * nd)

    p0spec = lambda col0: pl.BlockSpec((1, n_seq, gl), lambda b, g, c: (b, 0, col0 + g))
    in_specs = [zspec(0), zspec(4), zspec(8), zspec(12),
                pl.BlockSpec((CHUNK, 2 * LORA_COLS), lambda b, g, c: (rb(b, g, c), IN_COLS // (2 * LORA_COLS))),
                p0spec(0), p0spec(4), p0spec(8),
                pl.BlockSpec((1, n_seq, gl), lambda b, g, c: (b, 0, 12))]
    args = [z, z, z, z, z, prev0, prev0, prev0, prev0]
    if has_vres:
        in_specs.append(pl.BlockSpec((CHUNK, gl), lambda b, g, c: (rb(b, g, c), g)))
        args.append(vfirst)
    w2spec = lambda col0: pl.BlockSpec((LORA_COLS, gl), lambda b, g, c: (0, col0 + g))
    in_specs += [w2spec(0), w2spec(4)]
    args += [w2e, w2e]
    if has_vres:
        in_specs.append(w2spec(8))
        args.append(w2e)
    sspec = pl.BlockSpec((n_seq, 1, HEAD_DIM, gl), lambda b, g, c: (b, g, 0, 0))
    in_specs += [pl.BlockSpec((_N_VEC_ROWS, gl), lambda b, g, c: (0, g)), sspec]
    args += [vecs, s0]
    for name in ("tri", "mask_s", "mask_i", "bones", "bones_bf16", "levels", "qmask"):
        in_specs.append(const_spec(consts[name]))
        args.append(consts[name])

    row_spec = pl.BlockSpec((CHUNK, gl), lambda b, g, c: (rb(b, g, c), g))
    out_specs = [row_spec]
    out_shape = [jax.ShapeDtypeStruct((m, RWKV_WIDTH), BF16)]
    if not has_vres:
        out_specs.append(row_spec)
        out_shape.append(jax.ShapeDtypeStruct((m, RWKV_WIDTH), F32))
    out_specs.append(sspec)
    out_shape.append(jax.ShapeDtypeStruct(s0.shape, F32))

    outs = pl.pallas_call(
        functools.partial(_wkv_kernel, n_seq=n_seq, t_seq=t_seq, n_chunks=n_chunks, has_vres=has_vres,
                          n_levels=n_levels, rec_passes=rec_passes),
        grid=(nb, N_GROUPS, n_chunks),
        in_specs=in_specs, out_specs=out_specs, out_shape=out_shape,
        scratch_shapes=[pltpu.VMEM((n_seq, gl, gl), F32), pltpu.VMEM((n_seq, 4 * gl), F32)],
        name="wkv",
        compiler_params=pltpu.CompilerParams(
            dimension_semantics=("parallel", "parallel", "arbitrary"), vmem_limit_bytes=VMEM_LIMIT_BYTES),
    )(*args)
    if has_vres:
        ya, s_new = outs
        return ya, vfirst, s_new
    return outs


def _mix_kernel(ya_ref, u_ref, gp_ref, ga_ref, gb_ref, halo_ref, pw_ref, ps_ref, wb0_ref, wb1_ref, m_ref,
                *, n_seq, t_seq, tiles_per_seq, n_past):
    rows = n_seq * t_seq
    u = u_ref[...]
    e = jnp.concatenate([halo_ref[...], u.reshape(n_seq, t_seq, POOL_WIDTH)], axis=1)
    t0 = (pl.program_id(0) % tiles_per_seq) * t_seq
    pos = lax.broadcasted_iota(jnp.int32, (1, t_seq, 1), 1) + (t0 + 1 + n_past)
    yb_parts = []
    for gi, win in enumerate(POOL_WINDOWS):
        sl = slice(gi * POOL_GROUP, (gi + 1) * POOL_GROUP)
        acc = e[:, :, sl]
        span, length = 1, HALO + t_seq
        while span < win:
            length -= span
            acc = acc[:, span:span + length] + acc[:, :length]
            span *= 2
        ws = acc[:, length - t_seq:]
        cnt = jnp.minimum(pos, win).astype(F32)
        mixed = (ws / cnt - e[:, HALO:, sl]).reshape(rows, POOL_GROUP)
        yb_parts.append(jnp.dot(mixed.astype(BF16), pw_ref[gi], preferred_element_type=F32))
    gp = gp_ref[...]
    y_b = jnp.concatenate(yb_parts, axis=1) * ps_ref[...] * (gp * _sigmoid(gp))
    pa = jnp.dot(ya_ref[...], wb0_ref[...], preferred_element_type=F32)
    pb = jnp.dot(y_b.astype(BF16), wb1_ref[...], preferred_element_type=F32)
    m_ref[...] = (_sigmoid(ga_ref[...]) * pa + _sigmoid(gb_ref[...]) * pb).astype(m_ref.dtype)


def _mix(ya, z, halo, pool_w_bf16, pool_scale, wb0, wb1, *, n_seq, t_seq, tiles_per_seq, n_past):
    m = z.shape[0]
    rows = n_seq * t_seq
    full = lambda a: pl.BlockSpec(a.shape, lambda i: (0,) * a.ndim)
    return pl.pallas_call(
        functools.partial(_mix_kernel, n_seq=n_seq, t_seq=t_seq, tiles_per_seq=tiles_per_seq, n_past=n_past),
        grid=(m // rows,),
        in_specs=[
            pl.BlockSpec((rows, RWKV_WIDTH), lambda i: (i, 0)),
            pl.BlockSpec((rows, POOL_WIDTH), lambda i: (i, 4)),
            pl.BlockSpec((rows, POOL_WIDTH), lambda i: (i, 5)),
            pl.BlockSpec((rows, D_MODEL), lambda i: (i, 3)),
            pl.BlockSpec((rows, D_MODEL), lambda i: (i, 4)),
            pl.BlockSpec((n_seq, HALO, POOL_WIDTH), lambda i: (i, 0, 0)),
            full(pool_w_bf16), full(pool_scale), full(wb0), full(wb1),
        ],
        out_specs=pl.BlockSpec((rows, D_MODEL), lambda i: (i, 0)),
        out_shape=jax.ShapeDtypeStruct((m, D_MODEL), BF16),
        name="mix",
        compiler_params=pltpu.CompilerParams(
            dimension_semantics=("parallel",), vmem_limit_bytes=VMEM_LIMIT_BYTES),
    )(ya, z, z, z, z, halo, pool_w_bf16, pool_scale, wb0, wb1)


def _out_kernel(m_ref, x_ref, p_ref, g_ref, wo_ref, wg_ref, wp_ref, o_ref):
    out = jnp.dot(m_ref[...], wo_ref[...], preferred_element_type=F32)
    out = out * lax.rsqrt(jnp.mean(out * out, axis=-1, keepdims=True) + RMS_EPS) * g_ref[...]
    x = x_ref[...] + out
    gate = _sigmoid(jnp.dot(x.astype(BF16), wg_ref[...], preferred_element_type=F32))
    ple = jnp.dot(p_ref[...].astype(BF16), wp_ref[...], preferred_element_type=F32)
    o_ref[...] = x + gate * ple


def _out(mm_, x2d, p2d, gain, w_out, w_gate, w_ple, *, rows):
    m = x2d.shape[0]
    full = lambda a: pl.BlockSpec(a.shape, lambda i: (0,) * a.ndim, pipeline_mode=pl.Buffered(1))
    return pl.pallas_call(
        _out_kernel,
        grid=(m // rows,),
        in_specs=[
            pl.BlockSpec((rows, D_MODEL), lambda i: (i, 0)),
            pl.BlockSpec((rows, D_MODEL), lambda i: (i, 0)),
            pl.BlockSpec((rows, PLE_DIM), lambda i: (i, 0)),
            full(gain), full(w_out), full(w_gate), full(w_ple),
        ],
        out_specs=pl.BlockSpec((rows, D_MODEL), lambda i: (i, 0)),
        out_shape=jax.ShapeDtypeStruct((m, D_MODEL), F32),
        name="out",
        compiler_params=pltpu.CompilerParams(
            dimension_semantics=("parallel",), vmem_limit_bytes=VMEM_LIMIT_BYTES),
    )(mm_, x2d, p2d, gain, w_out, w_gate, w_ple)


def _group_lanes(s):
    b = s.shape[0]
    return s.reshape(b, N_GROUPS, HEADS_PER_GROUP, HEAD_DIM, HEAD_DIM).transpose(0, 1, 3, 2, 4).reshape(
        b, N_GROUPS, HEAD_DIM, GROUP_LANES)


def _ungroup_lanes(s):
    b = s.shape[0]
    return s.reshape(b, N_GROUPS, HEAD_DIM, HEADS_PER_GROUP, HEAD_DIM).transpose(0, 1, 3, 2, 4).reshape(
        b, N_HEADS, HEAD_DIM, HEAD_DIM)


def _layer_weights(i, w_in, mu_rkv, mu_lora, w0, w1, w2, a0, a1, a2, v0, v1, v2, k_k, k_a, r_k, ln_w, ln_b):
    r_ = RWKV_WIDTH
    mu = mu_lora[i]
    if i > 0:
        v1_i, v2_i, v0_i = v1[i - 1], v2[i - 1], v0[i - 1]
    else:
        v1_i = jnp.zeros((D_MODEL, MV_RANK), F32)
        v2_i = jnp.zeros((MV_RANK, r_), F32)
        v0_i = jnp.zeros((r_,), F32)
    la = jnp.concatenate([(1.0 - mu[0])[:, None] * w1[i], (1.0 - mu[1])[:, None] * a1[i],
                          (1.0 - mu[2])[:, None] * v1_i], axis=1)
    lb = jnp.concatenate([mu[0][:, None] * w1[i], mu[1][:, None] * a1[i], mu[2][:, None] * v1_i], axis=1)
    w_ext = jnp.concatenate([w_in[i], la, lb], axis=1).astype(BF16)
    w_prev = jnp.concatenate([w_ext[:, :3 * r_], w_ext[:, IN_COLS + LORA_COLS:]], axis=1)
    z96 = jnp.zeros((DECAY_RANK, r_), F32)
    z64 = jnp.zeros((MV_RANK, r_), F32)
    w2e = jnp.concatenate([
        jnp.concatenate([w2[i], z96, z64], axis=0),
        jnp.concatenate([z96, a2[i], z64], axis=0),
        jnp.concatenate([z96, z96, v2_i], axis=0)], axis=1).astype(BF16)
    rows = [mu_rkv[i, 0], mu_rkv[i, 1], mu_rkv[i, 2], w0[i], a0[i], v0_i, k_k[i], k_a[i], r_k[i], ln_w[i], ln_b[i]]
    vecs = jnp.concatenate([jnp.stack(rows), jnp.zeros((_N_VEC_ROWS - len(rows), r_), F32)], axis=0)
    return w_ext, w_prev, w2e, vecs


def _run_group(x, p, wkv, shift, pool, n_past, lw, params, *, n_seq, proj_tm, mix_seq, mix_t, out_rows,
               zero_state):
    (norm_pre, norm_post, pool_w, pool_scale, w_branch, w_out, w_ple, w_ple_gate) = params
    b, t, d = x.shape
    m = b * t
    x2d = x.reshape(m, d)
    wkv_out, shift_out, pool_out = [], [], []
    v_first = None
    tiles_per_seq = t // mix_t
    for i in range(DEPTH):
        w_ext, w_prev, w2e, vecs = lw[i]
        z = _proj(x2d, norm_pre[i], w_ext, normalize=True, tm=proj_tm, tn=512)
        shift_out.append(_norm_rows(x2d.reshape(b, t, d)[:, -1], norm_pre[i]))
        if zero_state:
            prev0 = jnp.zeros((b // n_seq, n_seq, 3 * RWKV_WIDTH + LORA_COLS), F32)
            s0 = jnp.zeros((b, N_GROUPS, HEAD_DIM, GROUP_LANES), F32)
            buf = jnp.zeros((b, POOL_BUF, POOL_WIDTH), F32)
        else:
            prev0 = _proj(shift[i], norm_pre[i], w_prev, normalize=False, tm=b, tn=256).reshape(
                b // n_seq, n_seq, -1)
            s0 = _group_lanes(wkv[i])
            buf = pool[i]
        ya, v_first, s_new = _wkv(z, prev0, v_first, w2e, vecs, s0, n_batch=b, t_len=t, n_seq=n_seq,
                                  rec_passes=3)
        wkv_out.append(_ungroup_lanes(s_new))
        u3 = z[:, 4 * RWKV_WIDTH:4 * RWKV_WIDTH + POOL_WIDTH].reshape(b, t, POOL_WIDTH)
        u_ext = jnp.concatenate([jnp.zeros((b, 1, POOL_WIDTH), F32), buf, u3], axis=1)
        if tiles_per_seq == 1:
            halo = u_ext[:, :HALO]
        else:
            halo = u_ext[:, :t].reshape(b, tiles_per_seq, mix_t, POOL_WIDTH)[:, :, :HALO].reshape(
                b * tiles_per_seq, HALO, POOL_WIDTH)
        pool_out.append(u_ext[:, -POOL_BUF:])
        mm_ = _mix(ya, z, halo, pool_w[i].astype(BF16), pool_scale[i].reshape(1, -1),
                   w_branch[i, 0].astype(BF16), w_branch[i, 1].astype(BF16),
                   n_seq=mix_seq, t_seq=mix_t, tiles_per_seq=tiles_per_seq, n_past=n_past)
        x2d = _out(mm_, x2d, p[i].reshape(m, PLE_DIM), norm_post[i].reshape(1, -1), w_out[i].astype(BF16),
                   w_ple_gate[i].astype(BF16), w_ple[i].astype(BF16), rows=out_rows)
    return x2d.reshape(b, t, d), jnp.stack(wkv_out), jnp.stack(shift_out), jnp.stack(pool_out)


def kernel(x_prompt, x_sample, state_wkv, state_shift, state_pool, p_prompt, p_sample, norm_pre, norm_post, w_in, mu_rkv, mu_lora, w0, w1, w2, a0, a1, a2, v0, v1, v2, k_k, k_a, r_k, ln_w, ln_b, pool_w, pool_scale, w_branch, w_out, w_ple, w_ple_gate):
    lw = [_layer_weights(i, w_in, mu_rkv, mu_lora, w0, w1, w2, a0, a1, a2, v0, v1, v2, k_k, k_a, r_k, ln_w, ln_b)
          for i in range(DEPTH)]
    params = (norm_pre, norm_post, pool_w, pool_scale, w_branch, w_out, w_ple, w_ple_gate)
    y_p, wkv_p, shift_p, pool_p = _run_group(
        x_prompt, p_prompt, None, None, None, 0, lw, params,
        n_seq=1, proj_tm=1024, mix_seq=1, mix_t=256, out_rows=256, zero_state=True)
    y_s, wkv_s, shift_s, pool_s = _run_group(
        x_sample, p_sample, state_wkv, state_shift, state_pool, PAST_LEN, lw, params,
        n_seq=8, proj_tm=1024, mix_seq=32, mix_t=8, out_rows=256, zero_state=False)
    return (y_p, y_s, wkv_p, shift_p, pool_p, wkv_s, shift_s, pool_s)
```

```python
import functools

import jax
import jax.numpy as jnp
import numpy as np
from jax import lax
from jax.experimental import pallas as pl
from jax.experimental.pallas import tpu as pltpu

F32 = jnp.float32
BF16 = jnp.bfloat16

D_MODEL = 2048
DEPTH = 4
PAST_LEN = 16384
RWKV_WIDTH = 1024
HEAD_DIM = 64
N_HEADS = 16
POOL_WIDTH = 1024
POOL_WINDOWS = (2, 4, 8, 16)
POOL_GROUP = 256
POOL_BUF = 15
PLE_DIM = 256
DECAY_RANK = 96
AAA_RANK = 96
MV_RANK = 64
LORA_COLS = DECAY_RANK + AAA_RANK + MV_RANK
IN_COLS = 10240
RMS_EPS = 1e-6
GN_EPS = 64e-5
DECAY_SCALE = 0.606531

HEADS_PER_GROUP = 4
GROUP_LANES = HEADS_PER_GROUP * HEAD_DIM
N_GROUPS = N_HEADS // HEADS_PER_GROUP
CHUNK = 64
EXPANDED = HEADS_PER_GROUP * CHUNK
HALO = 16
PROJ_TN = 2 * LORA_COLS
N_MAIN_TILES = IN_COLS // PROJ_TN
PREV_COLS = 3 * RWKV_WIDTH + PROJ_TN
LB_OFF = 3 * RWKV_WIDTH + LORA_COLS

VMEM_LIMIT_BYTES = 56 * 1024 * 1024


def _sigmoid(x):
    return 1.0 / (1.0 + jnp.exp(-x))


def _split_bf16(x):
    hi = x.astype(BF16)
    lo = (x - hi.astype(F32)).astype(BF16)
    return hi, lo


_NN = (((1,), (0,)), ((), ()))
_NT = (((1,), (1,)), ((), ()))


def _dot(a, b, dims=_NN):
    return lax.dot_general(a, b, dimension_numbers=dims, preferred_element_type=F32)


def _segsum(x, bones_b):
    hi, lo = _split_bf16(x)
    return _dot(hi, bones_b) + _dot(lo, bones_b)


def _proj_kernel(x_ref, g_ref, w_ref, wl_ref, z_ref, hb_ref, *, normalize, n_main):
    j = pl.program_id(1)

    @pl.when(j == 0)
    def _():
        x = x_ref[...]
        if normalize:
            x = x * lax.rsqrt(jnp.mean(x * x, axis=-1, keepdims=True) + RMS_EPS) * g_ref[...]
        hb_ref[...] = x.astype(BF16)

    @pl.when(j < n_main)
    def _():
        z_ref[...] = jnp.dot(hb_ref[...], w_ref[...].astype(BF16), preferred_element_type=F32)

    @pl.when(j == n_main)
    def _():
        z_ref[...] = jnp.dot(hb_ref[...], wl_ref[...], preferred_element_type=F32)


def _proj(x2d, gain, w_in, layer, w_lora, *, normalize, tm, n_main):
    m, d = x2d.shape
    tn = PROJ_TN
    assert m % tm == 0
    return pl.pallas_call(
        functools.partial(_proj_kernel, normalize=normalize, n_main=n_main),
        grid=(m // tm, n_main + 1),
        in_specs=[
            pl.BlockSpec((tm, d), lambda i, j: (i, 0)),
            pl.BlockSpec((1, d), lambda i, j: (0, 0)),
            pl.BlockSpec((None, d, tn), lambda i, j: (layer, 0, jnp.minimum(j, n_main - 1))),
            pl.BlockSpec((d, tn), lambda i, j: (0, 0)),
        ],
        out_specs=pl.BlockSpec((tm, tn), lambda i, j: (i, j)),
        out_shape=jax.ShapeDtypeStruct((m, (n_main + 1) * tn), F32),
        scratch_shapes=[pltpu.VMEM((tm, d), BF16)],
        name="proj",
        compiler_params=pltpu.CompilerParams(
            dimension_semantics=("parallel", "arbitrary"), vmem_limit_bytes=VMEM_LIMIT_BYTES),
    )(x2d, gain.reshape(1, d), w_in, w_lora)


def _norm_rows_kernel(x_ref, g_ref, o_ref):
    x = x_ref[...]
    o_ref[...] = x * lax.rsqrt(jnp.mean(x * x, axis=-1, keepdims=True) + RMS_EPS) * g_ref[...]


def _norm_rows(x2d, gain):
    return pl.pallas_call(
        _norm_rows_kernel, out_shape=jax.ShapeDtypeStruct(x2d.shape, F32), name="norm_rows",
    )(x2d, gain.reshape(1, -1))


_V_MU_R, _V_MU_K, _V_MU_V, _V_W0, _V_A0, _V_V0, _V_KK, _V_KA, _V_RK, _V_LNW, _V_LNB = range(11)
_N_VEC_ROWS = 16


def _wkv_constants(n_seq, t_seq):
    c, e = CHUNK, EXPANDED
    assert n_seq * t_seq == c
    row = np.arange(c)
    q_of = row // t_seq
    tri = (row[:, None] >= row[None, :]) & (q_of[:, None] == q_of[None, :])
    erow = np.arange(e)
    es = erow % c
    same = q_of[:, None] == (es // t_seq)[None, :]
    mask_s = same & (row[:, None] > es[None, :])
    mask_i = same & (row[:, None] >= es[None, :])
    mask_sc = np.block([[mask_s, mask_s], [mask_i, mask_i]])
    lane = np.arange(GROUP_LANES)
    bones = (lane[:, None] // HEAD_DIM) == (lane[None, :] // HEAD_DIM)
    levels = [np.eye(e, dtype=bool), (erow[:, None] // 2) == (erow[None, :] // 2)]
    m = 2
    while m < t_seq:
        levels.append(((erow[:, None] // (2 * m)) == (erow[None, :] // (2 * m)))
                      & ((erow[:, None] % (2 * m)) >= m) & ((erow[None, :] % (2 * m)) < m))
        m *= 2
    col = np.arange(2 * c)
    qmask = ((col[None, :] % c) // t_seq) == np.arange(n_seq)[:, None]
    f = lambda a: jnp.asarray(a.astype(np.float32))
    return dict(tri=f(tri).astype(BF16), mask_sc=f(mask_sc).astype(BF16), bones=f(bones),
                bones_b=f(bones).astype(BF16), levels=f(np.stack(levels)).astype(BF16),
                qmask=f(qmask).reshape(n_seq, 1, 2 * c))


def _wkv_kernel(*refs, n_seq, t_seq, n_chunks, has_vres, n_levels):
    it = iter(refs)
    zr_ref, zk_ref, zv_ref, zg_ref, zl_ref, p0_ref = (next(it) for _ in range(6))
    vfirst_ref = next(it) if has_vres else None
    w2_ref, vec_ref, s0_ref = next(it), next(it), next(it)
    tri_ref, msc_ref, bones_ref, bonesb_ref, lev_ref, qmask_ref = (next(it) for _ in range(6))
    ya_ref = next(it)
    vout_ref = None if has_vres else next(it)
    sout_ref = next(it)
    sbd_ref, prev_ref = next(it), next(it)

    c_idx = pl.program_id(1)
    gl, rw = GROUP_LANES, RWKV_WIDTH
    bones = bones_ref[...]
    bones_b = bonesb_ref[...]

    def expand(x):
        return jnp.concatenate([x] * HEADS_PER_GROUP, axis=0) * bones_b.astype(x.dtype)

    @pl.when(c_idx == 0)
    def _():
        for q in range(n_seq):
            for g in range(N_GROUPS):
                sbd_ref[q, g] = expand(s0_ref[q, g])
        prev_ref[...] = p0_ref[0]

    row = lax.broadcasted_iota(jnp.int32, (CHUNK, 1), 0)
    is_first = (row % t_seq) == 0

    def shifted(z, p0):
        rolled = pltpu.roll(z, 1, axis=0)
        if n_seq == 1:
            first = jnp.broadcast_to(p0, z.shape)
        else:
            first = jnp.broadcast_to(p0[:, None, :], (n_seq, t_seq, z.shape[1])).reshape(z.shape)
        return jnp.where(is_first, first, rolled)

    vec = vec_ref[...]
    vrow = lambda i: vec[i:i + 1, :]

    zr, zk, zv = zr_ref[...], zk_ref[...], zv_ref[...]
    zl = zl_ref[...]
    la, lb = zl[:, :LORA_COLS], zl[:, LORA_COLS:]
    zr_p = shifted(zr, prev_ref[:, 0:rw])
    zk_p = shifted(zk, prev_ref[:, rw:2 * rw])
    zv_p = shifted(zv, prev_ref[:, 2 * rw:3 * rw])
    lb_p = shifted(lb, prev_ref[:, LB_OFF:LB_OFF + LORA_COLS])
    if n_chunks > 1:
        assert n_seq == 1
        prev_ref[:, 0:rw] = zr[CHUNK - 1:CHUNK, :]
        prev_ref[:, rw:2 * rw] = zk[CHUNK - 1:CHUNK, :]
        prev_ref[:, 2 * rw:3 * rw] = zv[CHUNK - 1:CHUNK, :]
        prev_ref[:, LB_OFF:LB_OFF + LORA_COLS] = lb[CHUNK - 1:CHUNK, :]

    r = zr + (zr_p - zr) * vrow(_V_MU_R)
    k = zk + (zk_p - zk) * vrow(_V_MU_K)
    v = zv + (zv_p - zv) * vrow(_V_MU_V)

    lin = la + lb_p
    lane = lax.broadcasted_iota(jnp.int32, (1, LORA_COLS), 1)
    lin = jnp.where(lane < DECAY_RANK, jnp.tanh(lin), lin).astype(BF16)
    d_all = jnp.dot(lin, w2_ref[...], preferred_element_type=F32)
    wlog = -DECAY_SCALE * _sigmoid(vrow(_V_W0) + d_all[:, 0:rw])
    alr = _sigmoid(vrow(_V_A0) + d_all[:, rw:2 * rw])
    if has_vres:
        v = v + (vfirst_ref[...] - v) * _sigmoid(vrow(_V_V0) + d_all[:, 2 * rw:3 * rw])
    else:
        vout_ref[...] = v
    kkr = k * vrow(_V_KK)
    kf = k * (1.0 + (alr - 1.0) * vrow(_V_KA))
    rkr = r * kf * vrow(_V_RK)
    g_all = zg_ref[...]
    gate = g_all * _sigmoid(g_all)

    w_hi, w_lo = _split_bf16(wlog)
    w_lo2 = (wlog - w_hi.astype(F32) - w_lo.astype(F32)).astype(BF16)
    tri = tri_ref[...]
    cum = (jnp.dot(tri, w_hi, preferred_element_type=F32) + jnp.dot(tri, w_lo, preferred_element_type=F32)
           + jnp.dot(tri, w_lo2, preferred_element_type=F32))
    p_inc = jnp.exp(cum)
    inv_p = jnp.exp(-cum)
    p_exc = jnp.exp(cum - wlog)

    mask_sc = msc_ref[...]
    eye_b = lev_ref[0]
    e = EXPANDED

    groups = range(N_GROUPS)
    sls = [slice(gi * gl, (gi + 1) * gl) for gi in groups]
    kk_n2 = [_segsum(kkr[:, sl] * kkr[:, sl], bones_b) for sl in sls]
    kk = [kkr[:, sl] / jnp.maximum(jnp.sqrt(n2), 1e-12) for sl, n2 in zip(sls, kk_n2)]
    v_g = [v[:, sl] for sl in sls]
    a_f = [-kk[gi] * p_exc[:, sls[gi]] for gi in groups]
    r_f = [r[:, sl] * p_inc[:, sl] for sl in sls]
    a_t = [x.astype(BF16) for x in a_f]
    r_t = [x.astype(BF16) for x in r_f]
    b_t = [(kk[gi] * alr[:, sls[gi]] * inv_p[:, sls[gi]]).astype(BF16) for gi in groups]
    k_t = [(kf[:, sl] * inv_p[:, sl]).astype(BF16) for sl in sls]
    v_e = [expand(x.astype(BF16)) for x in v_g]
    ar = [jnp.concatenate([a_t[gi], r_t[gi]], axis=0) for gi in groups]
    sc = [_dot(ar[gi], jnp.concatenate([expand(b_t[gi]), expand(k_t[gi])], axis=0), _NT) for gi in groups]
    scm = [x.astype(BF16) * mask_sc for x in sc]

    n_bd = [expand(x[:CHUNK, :e]) for x in scm]
    xs = [eye_b + n * lev_ref[1] for n in n_bd]
    for lvl in range(2, n_levels):
        t1 = [_dot(xs[gi], n_bd[gi] * lev_ref[lvl]).astype(BF16) + eye_b for gi in groups]
        xs = [_dot(t1[gi], xs[gi]).astype(BF16) for gi in groups]
    x_c = [(x[0:CHUNK] + x[CHUNK:2 * CHUNK]) + (x[2 * CHUNK:3 * CHUNK] + x[3 * CHUNK:4 * CHUNK]) for x in xs]

    a_s, r_s = [], []
    for gi in groups:
        parts_a, parts_r = [], []
        for q in range(n_seq):
            lo_, hi_ = q * t_seq, (q + 1) * t_seq
            lhs = ar[gi] if n_seq == 1 else jnp.concatenate(
                [a_f[gi][lo_:hi_], r_f[gi][lo_:hi_]], axis=0).astype(BF16)
            res = _dot(lhs, sbd_ref[q, gi].astype(BF16), _NT)
            parts_a.append(res[:t_seq])
            parts_r.append(res[t_seq:])
        a_s.append(parts_a[0] if n_seq == 1 else jnp.concatenate(parts_a, axis=0))
        r_s.append(parts_r[0] if n_seq == 1 else jnp.concatenate(parts_r, axis=0))

    y = [a_s[gi] + _dot(scm[gi][:CHUNK, e:], v_e[gi]) for gi in groups]
    u = [_dot(x_c[gi], expand(y[gi].astype(BF16))) for gi in groups]
    o = [r_s[gi] + _dot(scm[gi][CHUNK:, :], jnp.concatenate([expand(u[gi].astype(BF16)), v_e[gi]], axis=0))
         for gi in groups]

    for gi in groups:
        uv_t = jnp.concatenate([u[gi], v_g[gi]], axis=0).T
        bk = jnp.concatenate([b_t[gi], k_t[gi]], axis=0)
        for q in range(n_seq):
            lhs = uv_t if n_seq == 1 else uv_t * qmask_ref[q]
            p_last = p_inc[(q + 1) * t_seq - 1:(q + 1) * t_seq, sls[gi]]
            sbd_ref[q, gi] = bones * ((sbd_ref[q, gi] + _dot(lhs.astype(BF16), bk)) * p_last)

    mean = [_segsum(x, bones_b) * (1.0 / HEAD_DIM) for x in o]
    d = [o[gi] - mean[gi] for gi in groups]
    var = [_segsum(x * x, bones_b) * (1.0 / HEAD_DIM) for x in d]
    bonus = [_segsum(rkr[:, sl], bones_b) for sl in sls]
    for gi in groups:
        sl = sls[gi]
        on = d[gi] * lax.rsqrt(var[gi] + GN_EPS) * vec[_V_LNW:_V_LNW + 1, sl] + vec[_V_LNB:_V_LNB + 1, sl]
        ya_ref[:, sl] = ((on + bonus[gi] * v_g[gi]) * gate[:, sl]).astype(ya_ref.dtype)


    @pl.when(c_idx == n_chunks - 1)
    def _():
        for q in range(n_seq):
            for g in range(N_GROUPS):
                s = sbd_ref[q, g]
                sout_ref[q, g] = ((s[0:HEAD_DIM] + s[HEAD_DIM:2 * HEAD_DIM])
                                  + (s[2 * HEAD_DIM:3 * HEAD_DIM] + s[3 * HEAD_DIM:4 * HEAD_DIM]))


def _wkv(z, prev0, vfirst, w2e, vecs, s0, *, n_batch, t_len, n_seq):
    m = z.shape[0]
    t_seq = CHUNK // n_seq
    n_chunks = t_len // t_seq
    assert n_chunks == 1 or n_seq == 1
    nb = n_batch // n_seq
    has_vres = vfirst is not None
    consts = _wkv_constants(n_seq, t_seq)
    n_levels = consts["levels"].shape[0]
    rw = RWKV_WIDTH
    rb = lambda b, c: b * n_chunks + c

    zspec = lambda col: pl.BlockSpec((CHUNK, rw), lambda b, c: (rb(b, c), col))
    full = lambda a: pl.BlockSpec(a.shape, lambda b, c: (0,) * a.ndim)
    sspec = pl.BlockSpec((n_seq, N_GROUPS, HEAD_DIM, GROUP_LANES), lambda b, c: (b, 0, 0, 0))

    in_specs = [zspec(0), zspec(1), zspec(2), zspec(3),
                pl.BlockSpec((CHUNK, PROJ_TN), lambda b, c: (rb(b, c), N_MAIN_TILES)),
                pl.BlockSpec((1, n_seq, PREV_COLS), lambda b, c: (b, 0, 0))]
    args = [z, z, z, z, z, prev0]
    if has_vres:
        in_specs.append(zspec(0))
        args.append(vfirst)
    else:
        w2e = w2e[:, :2 * rw]
    in_specs += [full(w2e), full(vecs), sspec]
    args += [w2e, vecs, s0]
    for name in ("tri", "mask_sc", "bones", "bones_b", "levels", "qmask"):
        in_specs.append(full(consts[name]))
        args.append(consts[name])

    out_specs = [zspec(0)]
    out_shape = [jax.ShapeDtypeStruct((m, rw), BF16)]
    if not has_vres:
        out_specs.append(zspec(0))
        out_shape.append(jax.ShapeDtypeStruct((m, rw), F32))
    out_specs.append(sspec)
    out_shape.append(jax.ShapeDtypeStruct(s0.shape, F32))

    outs = pl.pallas_call(
        functools.partial(_wkv_kernel, n_seq=n_seq, t_seq=t_seq, n_chunks=n_chunks, has_vres=has_vres,
                          n_levels=n_levels),
        grid=(nb, n_chunks),
        in_specs=in_specs, out_specs=out_specs, out_shape=out_shape,
        scratch_shapes=[pltpu.VMEM((n_seq, N_GROUPS, GROUP_LANES, GROUP_LANES), F32),
                        pltpu.VMEM((n_seq, PREV_COLS), F32)],
        name="wkv",
        compiler_params=pltpu.CompilerParams(
            dimension_semantics=("parallel", "arbitrary"), vmem_limit_bytes=VMEM_LIMIT_BYTES),
    )(*args)
    if has_vres:
        ya, s_new = outs
        return ya, vfirst, s_new
    return outs


def _mix_kernel(ya_ref, u_ref, gp_ref, ga_ref, gb_ref, halo_ref, pw_ref, ps_ref, wb0_ref, wb1_ref, m_ref,
                *, n_seq, t_seq, tiles_per_seq, n_past, halo_from_z):
    rows = n_seq * t_seq
    u = u_ref[...]
    tile = pl.program_id(0) % tiles_per_seq
    if halo_from_z:
        halo = jnp.where(tile == 0, 0.0, halo_ref[...]).reshape(1, HALO, POOL_WIDTH)
    else:
        halo = halo_ref[...]
    e = jnp.concatenate([halo, u.reshape(n_seq, t_seq, POOL_WIDTH)], axis=1)
    pos = lax.broadcasted_iota(jnp.int32, (1, t_seq, 1), 1) + (tile * t_seq + 1 + n_past)
    yb_parts = []
    for gi, win in enumerate(POOL_WINDOWS):
        sl = slice(gi * POOL_GROUP, (gi + 1) * POOL_GROUP)
        acc = e[:, :, sl]
        span, length = 1, HALO + t_seq
        while span < win:
            length -= span
            acc = acc[:, span:span + length] + acc[:, :length]
            span *= 2
        ws = acc[:, length - t_seq:]
        cnt = jnp.minimum(pos, win).astype(F32)
        mixed = (ws / cnt - e[:, HALO:, sl]).reshape(rows, POOL_GROUP)
        yb_parts.append(jnp.dot(mixed.astype(BF16), pw_ref[gi], preferred_element_type=F32))
    gp = gp_ref[...]
    y_b = jnp.concatenate(yb_parts, axis=1) * ps_ref[...] * (gp * _sigmoid(gp))
    pa = jnp.dot(ya_ref[...], wb0_ref[...], preferred_element_type=F32)
    pb = jnp.dot(y_b.astype(BF16), wb1_ref[...], preferred_element_type=F32)
    m_ref[...] = (_sigmoid(ga_ref[...]) * pa + _sigmoid(gb_ref[...]) * pb).astype(m_ref.dtype)


def _mix(ya, z, halo, pool_w_bf16, pool_scale, wb0, wb1, *, n_seq, t_seq, tiles_per_seq, n_past):
    m = z.shape[0]
    rows = n_seq * t_seq
    halo_from_z = halo is None
    full = lambda a: pl.BlockSpec(a.shape, lambda i: (0,) * a.ndim)
    if halo_from_z:
        assert n_seq == 1 and rows % HALO == 0
        halo = z
        halo_spec = pl.BlockSpec((HALO, POOL_WIDTH), lambda i: (jnp.maximum(i * (rows // HALO) - 1, 0), 4))
    else:
        halo_spec = pl.BlockSpec((n_seq, HALO, POOL_WIDTH), lambda i: (i, 0, 0))
    return pl.pallas_call(
        functools.partial(_mix_kernel, n_seq=n_seq, t_seq=t_seq, tiles_per_seq=tiles_per_seq, n_past=n_past,
                          halo_from_z=halo_from_z),
        grid=(m // rows,),
        in_specs=[
            pl.BlockSpec((rows, RWKV_WIDTH), lambda i: (i, 0)),
            pl.BlockSpec((rows, POOL_WIDTH), lambda i: (i, 4)),
            pl.BlockSpec((rows, POOL_WIDTH), lambda i: (i, 5)),
            pl.BlockSpec((rows, D_MODEL), lambda i: (i, 3)),
            pl.BlockSpec((rows, D_MODEL), lambda i: (i, 4)),
            halo_spec,
            full(pool_w_bf16), full(pool_scale), full(wb0), full(wb1),
        ],
        out_specs=pl.BlockSpec((rows, D_MODEL), lambda i: (i, 0)),
        out_shape=jax.ShapeDtypeStruct((m, D_MODEL), BF16),
        name="mix",
        compiler_params=pltpu.CompilerParams(
            dimension_semantics=("parallel",), vmem_limit_bytes=VMEM_LIMIT_BYTES),
    )(ya, z, z, z, z, halo, pool_w_bf16, pool_scale, wb0, wb1)


def _out_kernel(m_ref, x_ref, p_ref, g_ref, wo_ref, wg_ref, wp_ref, o_ref):
    out = jnp.dot(m_ref[...], wo_ref[...], preferred_element_type=F32)
    out = out * lax.rsqrt(jnp.mean(out * out, axis=-1, keepdims=True) + RMS_EPS) * g_ref[...]
    x = x_ref[...] + out
    gate = _sigmoid(jnp.dot(x.astype(BF16), wg_ref[...], preferred_element_type=F32))
    ple = jnp.dot(p_ref[...].astype(BF16), wp_ref[...], preferred_element_type=F32)
    o_ref[...] = x + gate * ple


def _out(mm_, x2d, p2d, gain, w_out, w_gate, w_ple, *, rows):
    m = x2d.shape[0]
    full = lambda a: pl.BlockSpec(a.shape, lambda i: (0,) * a.ndim, pipeline_mode=pl.Buffered(1))
    return pl.pallas_call(
        _out_kernel,
        grid=(m // rows,),
        in_specs=[
            pl.BlockSpec((rows, D_MODEL), lambda i: (i, 0)),
            pl.BlockSpec((rows, D_MODEL), lambda i: (i, 0)),
            pl.BlockSpec((rows, PLE_DIM), lambda i: (i, 0)),
            full(gain), full(w_out), full(w_gate), full(w_ple),
        ],
        out_specs=pl.BlockSpec((rows, D_MODEL), lambda i: (i, 0)),
        out_shape=jax.ShapeDtypeStruct((m, D_MODEL), F32),
        name="out",
        compiler_params=pltpu.CompilerParams(
            dimension_semantics=("parallel",), vmem_limit_bytes=VMEM_LIMIT_BYTES),
    )(mm_, x2d, p2d, gain, w_out, w_gate, w_ple)


def _group_lanes(s):
    b = s.shape[0]
    return s.reshape(b, N_GROUPS, HEADS_PER_GROUP, HEAD_DIM, HEAD_DIM).transpose(0, 1, 3, 2, 4).reshape(
        b, N_GROUPS, HEAD_DIM, GROUP_LANES)


def _ungroup_lanes(s):
    b = s.shape[0]
    return s.reshape(b, N_GROUPS, HEAD_DIM, HEADS_PER_GROUP, HEAD_DIM).transpose(0, 1, 3, 2, 4).reshape(
        b, N_HEADS, HEAD_DIM, HEAD_DIM)


def _layer_weights(i, mu_rkv, mu_lora, w0, w1, w2, a0, a1, a2, v0, v1, v2, k_k, k_a, r_k, ln_w, ln_b):
    r_ = RWKV_WIDTH
    mu = mu_lora[i]
    if i > 0:
        v1_i, v2_i, v0_i = v1[i - 1], v2[i - 1], v0[i - 1]
    else:
        v1_i = jnp.zeros((D_MODEL, MV_RANK), F32)
        v2_i = jnp.zeros((MV_RANK, r_), F32)
        v0_i = jnp.zeros((r_,), F32)
    w_lora = jnp.concatenate(
        [(1.0 - mu[0])[:, None] * w1[i], (1.0 - mu[1])[:, None] * a1[i], (1.0 - mu[2])[:, None] * v1_i,
         mu[0][:, None] * w1[i], mu[1][:, None] * a1[i], mu[2][:, None] * v1_i], axis=1).astype(BF16)
    z96 = jnp.zeros((DECAY_RANK, r_), F32)
    z64 = jnp.zeros((MV_RANK, r_), F32)
    w2e = jnp.concatenate([
        jnp.concatenate([w2[i], z96, z64], axis=0),
        jnp.concatenate([z96, a2[i], z64], axis=0),
        jnp.concatenate([z96, z96, v2_i], axis=0)], axis=1).astype(BF16)
    rows = [mu_rkv[i, 0], mu_rkv[i, 1], mu_rkv[i, 2], w0[i], a0[i], v0_i, k_k[i], k_a[i], r_k[i], ln_w[i], ln_b[i]]
    vecs = jnp.concatenate([jnp.stack(rows), jnp.zeros((_N_VEC_ROWS - len(rows), r_), F32)], axis=0)
    return w_lora, w2e, vecs


def _run_group(x, p, wkv, shift, pool, n_past, lw, params, *, n_seq, proj_tm, mix_seq, mix_t, out_rows,
               zero_state):
    (w_in, norm_pre, norm_post, pool_w, pool_scale, w_branch, w_out, w_ple, w_ple_gate) = params
    b, t, d = x.shape
    m = b * t
    x2d = x.reshape(m, d)
    wkv_out, shift_out, pool_out = [], [], []
    v_first = None
    tiles_per_seq = t // mix_t
    for i in range(DEPTH):
        w_lora, w2e, vecs = lw[i]
        z = _proj(x2d, norm_pre[i], w_in, i, w_lora, normalize=True, tm=proj_tm, n_main=N_MAIN_TILES)
        shift_out.append(_norm_rows(x2d.reshape(b, t, d)[:, -1], norm_pre[i]))
        u_last = z.reshape(b, t, -1)[:, -min(t, POOL_BUF):, 4 * RWKV_WIDTH:4 * RWKV_WIDTH + POOL_WIDTH]
        if zero_state:
            assert t >= POOL_BUF
            prev0 = jnp.zeros((b // n_seq, n_seq, PREV_COLS), F32)
            s0 = jnp.zeros((b, N_GROUPS, HEAD_DIM, GROUP_LANES), F32)
            halo = None
            pool_out.append(u_last)
        else:
            prev0 = _proj(shift[i], norm_pre[i], w_in, i, w_lora, normalize=False, tm=b,
                          n_main=3 * RWKV_WIDTH // PROJ_TN).reshape(b // n_seq, n_seq, PREV_COLS)
            s0 = _group_lanes(wkv[i])
            assert tiles_per_seq == 1 and t < POOL_BUF
            halo = jnp.pad(pool[i], ((0, 0), (HALO - POOL_BUF, 0), (0, 0)))
            pool_out.append(jnp.concatenate([pool[i][:, t:], u_last], axis=1))
        ya, v_first, s_new = _wkv(z, prev0, v_first, w2e, vecs, s0, n_batch=b, t_len=t, n_seq=n_seq)
        wkv_out.append(_ungroup_lanes(s_new))
        mm_ = _mix(ya, z, halo, pool_w[i].astype(BF16), pool_scale[i].reshape(1, -1),
                   w_branch[i, 0].astype(BF16), w_branch[i, 1].astype(BF16),
                   n_seq=mix_seq, t_seq=mix_t, tiles_per_seq=tiles_per_seq, n_past=n_past)
        x2d = _out(mm_, x2d, p[i].reshape(m, PLE_DIM), norm_post[i].reshape(1, -1), w_out[i].astype(BF16),
                   w_ple_gate[i].astype(BF16), w_ple[i].astype(BF16), rows=out_rows)
    return x2d.reshape(b, t, d), jnp.stack(wkv_out), jnp.stack(shift_out), jnp.stack(pool_out)


def kernel(x_prompt, x_sample, state_wkv, state_shift, state_pool, p_prompt, p_sample, norm_pre, norm_post, w_in, mu_rkv, mu_lora, w0, w1, w2, a0, a1, a2, v0, v1, v2, k_k, k_a, r_k, ln_w, ln_b, pool_w, pool_scale, w_branch, w_out, w_ple, w_ple_gate):
    lw = [_layer_weights(i, mu_rkv, mu_lora, w0, w1, w2, a0, a1, a2, v0, v1, v2, k_k, k_a, r_k, ln_w, ln_b)
          for i in range(DEPTH)]
    params = (w_in, norm_pre, norm_post, pool_w, pool_scale, w_branch, w_out, w_ple, w_ple_gate)
    y_p, wkv_p, shift_p, pool_p = _run_group(
        x_prompt, p_prompt, None, None, None, 0, lw, params,
        n_seq=1, proj_tm=1024, mix_seq=1, mix_t=256, out_rows=256, zero_state=True)
    y_s, wkv_s, shift_s, pool_s = _run_group(
        x_sample, p_sample, state_wkv, state_shift, state_pool, PAST_LEN, lw, params,
        n_seq=8, proj_tm=1024, mix_seq=32, mix_t=8, out_rows=256, zero_state=False)
    return (y_p, y_s, wkv_p, shift_p, pool_p, wkv_s, shift_s, pool_s)
```

```python
import functools

import jax
import jax.numpy as jnp
import numpy as np
from jax import lax
from jax.experimental import pallas as pl
from jax.experimental.pallas import tpu as pltpu

F32 = jnp.float32
BF16 = jnp.bfloat16

D_MODEL = 2048
DEPTH = 4
PAST_LEN = 16384
RWKV_WIDTH = 1024
HEAD_DIM = 64
N_HEADS = 16
POOL_WIDTH = 1024
POOL_WINDOWS = (2, 4, 8, 16)
POOL_GROUP = 256
POOL_BUF = 15
PLE_DIM = 256
DECAY_RANK = 96
AAA_RANK = 96
MV_RANK = 64
LORA_COLS = DECAY_RANK + AAA_RANK + MV_RANK
IN_COLS = 10240
RMS_EPS = 1e-6
GN_EPS = 64e-5
DECAY_SCALE = 0.606531

HEADS_PER_GROUP = 4
GROUP_LANES = HEADS_PER_GROUP * HEAD_DIM
N_GROUPS = N_HEADS // HEADS_PER_GROUP
CHUNK = 64
EXPANDED = HEADS_PER_GROUP * CHUNK
HALO = 16
PROJ_TN = 2 * LORA_COLS
N_MAIN_TILES = IN_COLS // PROJ_TN
PREV_COLS = 3 * RWKV_WIDTH + PROJ_TN
LB_OFF = 3 * RWKV_WIDTH + LORA_COLS

VMEM_LIMIT_BYTES = 56 * 1024 * 1024


def _sigmoid(x):
    return 0.5 * jnp.tanh(0.5 * x) + 0.5


def _split_bf16(x):
    hi = x.astype(BF16)
    lo = (x - hi.astype(F32)).astype(BF16)
    return hi, lo


_NN = (((1,), (0,)), ((), ()))
_NT = (((1,), (1,)), ((), ()))


def _dot(a, b, dims=_NN):
    return lax.dot_general(a, b, dimension_numbers=dims, preferred_element_type=F32)


def _segsum(x, bones_b):
    hi, lo = _split_bf16(x)
    return _dot(hi, bones_b) + _dot(lo, bones_b)


def _proj_kernel(x_ref, g_ref, w_ref, wl_ref, z_ref, hb_ref, *, normalize, n_main):
    j = pl.program_id(1)

    @pl.when(j == 0)
    def _():
        x = x_ref[...]
        if normalize:
            x = x * lax.rsqrt(jnp.mean(x * x, axis=-1, keepdims=True) + RMS_EPS) * g_ref[...]
        hb_ref[...] = x.astype(BF16)

    @pl.when(j < n_main)
    def _():
        z_ref[...] = jnp.dot(hb_ref[...], w_ref[...].astype(BF16), preferred_element_type=F32).astype(z_ref.dtype)

    @pl.when(j == n_main)
    def _():
        z_ref[...] = jnp.dot(hb_ref[...], wl_ref[...], preferred_element_type=F32).astype(z_ref.dtype)


def _proj(x2d, norm_gain, w_in, layer, w_lora, *, normalize, tm, n_main, out_dtype):
    m, d = x2d.shape
    tn = PROJ_TN
    assert m % tm == 0
    once = pl.Buffered(1)
    return pl.pallas_call(
        functools.partial(_proj_kernel, normalize=normalize, n_main=n_main),
        grid=(m // tm, n_main + 1),
        in_specs=[
            pl.BlockSpec((tm, d), lambda i, j: (i, 0), pipeline_mode=once),
            pl.BlockSpec((None, 1, d), lambda i, j: (layer, 0, 0)),
            pl.BlockSpec((None, d, tn), lambda i, j: (layer, 0, jnp.minimum(j, n_main - 1))),
            pl.BlockSpec((d, tn), lambda i, j: (0, 0), pipeline_mode=once),
        ],
        out_specs=pl.BlockSpec((tm, tn), lambda i, j: (i, j)),
        out_shape=jax.ShapeDtypeStruct((m, (n_main + 1) * tn), out_dtype),
        scratch_shapes=[pltpu.VMEM((tm, d), BF16)],
        name="proj",
        compiler_params=pltpu.CompilerParams(
            dimension_semantics=("parallel", "arbitrary"), vmem_limit_bytes=VMEM_LIMIT_BYTES),
    )(x2d, norm_gain.reshape(DEPTH, 1, d), w_in, w_lora)


def _norm_rows_kernel(x_ref, g_ref, o_ref):
    x = x_ref[...]
    o_ref[...] = x * lax.rsqrt(jnp.mean(x * x, axis=-1, keepdims=True) + RMS_EPS) * g_ref[...]


def _norm_rows(x2d, gain):
    return pl.pallas_call(
        _norm_rows_kernel, out_shape=jax.ShapeDtypeStruct(x2d.shape, F32), name="norm_rows",
    )(x2d, gain.reshape(1, -1))


_V_MU_R, _V_MU_K, _V_MU_V, _V_W0, _V_A0, _V_V0, _V_KK, _V_KA, _V_RK, _V_LNW, _V_LNB = range(11)
_N_VEC_ROWS = 16


def _wkv_constants(n_seq, t_seq):
    c, e = CHUNK, EXPANDED
    assert n_seq * t_seq == c
    row = np.arange(c)
    q_of = row // t_seq
    tri = (row[:, None] >= row[None, :]) & (q_of[:, None] == q_of[None, :])
    erow = np.arange(e)
    es = erow % c
    same = q_of[:, None] == (es // t_seq)[None, :]
    mask_s = same & (row[:, None] > es[None, :])
    mask_i = same & (row[:, None] >= es[None, :])
    mask_sc = np.block([[mask_s, mask_s], [mask_i, mask_i]])
    lane = np.arange(GROUP_LANES)
    bones = (lane[:, None] // HEAD_DIM) == (lane[None, :] // HEAD_DIM)
    levels = [np.eye(e, dtype=bool), (erow[:, None] // 2) == (erow[None, :] // 2)]
    m = 2
    while m < t_seq:
        levels.append(((erow[:, None] // (2 * m)) == (erow[None, :] // (2 * m)))
                      & ((erow[:, None] % (2 * m)) >= m) & ((erow[None, :] % (2 * m)) < m))
        m *= 2
    col = np.arange(2 * c)
    qmask = ((col[None, :] % c) // t_seq) == np.arange(n_seq)[:, None]
    f = lambda a: jnp.asarray(a.astype(np.float32))
    return dict(tri=f(tri).astype(BF16), mask_sc=f(mask_sc).astype(BF16), bones=f(bones),
                bones_b=f(bones).astype(BF16), levels=f(np.stack(levels)).astype(BF16),
                qmask=f(qmask).reshape(n_seq, 1, 2 * c))


def _wkv_kernel(*refs, n_seq, t_seq, n_chunks, has_vres, n_levels):
    it = iter(refs)
    zr_ref, zk_ref, zv_ref, zg_ref, zl_ref, p0_ref = (next(it) for _ in range(6))
    vfirst_ref = next(it) if has_vres else None
    w2_ref, vec_ref, s0_ref = next(it), next(it), next(it)
    tri_ref, msc_ref, bones_ref, bonesb_ref, lev_ref, qmask_ref = (next(it) for _ in range(6))
    ya_ref = next(it)
    vout_ref = None if has_vres else next(it)
    sout_ref = next(it)
    sbd_ref, prev_ref = next(it), next(it)

    c_idx = pl.program_id(1)
    gl, rw = GROUP_LANES, RWKV_WIDTH
    bones = bones_ref[...]
    bones_b = bonesb_ref[...]

    def expand(x):
        return jnp.concatenate([x] * HEADS_PER_GROUP, axis=0) * bones_b.astype(x.dtype)

    @pl.when(c_idx == 0)
    def _():
        for q in range(n_seq):
            for g in range(N_GROUPS):
                sbd_ref[q, g] = expand(s0_ref[q, g])
        prev_ref[...] = p0_ref[0]

    row = lax.broadcasted_iota(jnp.int32, (CHUNK, 1), 0)
    is_first = (row % t_seq) == 0

    def shifted(z, p0):
        rolled = pltpu.roll(z, 1, axis=0)
        if n_seq == 1:
            first = jnp.broadcast_to(p0, z.shape)
        else:
            first = jnp.broadcast_to(p0[:, None, :], (n_seq, t_seq, z.shape[1])).reshape(z.shape)
        return jnp.where(is_first, first, rolled)

    vec = vec_ref[...]
    vrow = lambda i: vec[i:i + 1, :]

    zr, zk, zv = zr_ref[...].astype(F32), zk_ref[...].astype(F32), zv_ref[...].astype(F32)
    zl = zl_ref[...].astype(F32)
    la, lb = zl[:, :LORA_COLS], zl[:, LORA_COLS:]
    zr_p = shifted(zr, prev_ref[:, 0:rw])
    zk_p = shifted(zk, prev_ref[:, rw:2 * rw])
    zv_p = shifted(zv, prev_ref[:, 2 * rw:3 * rw])
    lb_p = shifted(lb, prev_ref[:, LB_OFF:LB_OFF + LORA_COLS])
    if n_chunks > 1:
        assert n_seq == 1
        prev_ref[:, 0:rw] = zr[CHUNK - 1:CHUNK, :]
        prev_ref[:, rw:2 * rw] = zk[CHUNK - 1:CHUNK, :]
        prev_ref[:, 2 * rw:3 * rw] = zv[CHUNK - 1:CHUNK, :]
        prev_ref[:, LB_OFF:LB_OFF + LORA_COLS] = lb[CHUNK - 1:CHUNK, :]

    r = zr + (zr_p - zr) * vrow(_V_MU_R)
    k = zk + (zk_p - zk) * vrow(_V_MU_K)
    v = zv + (zv_p - zv) * vrow(_V_MU_V)

    lin = la + lb_p
    lane = lax.broadcasted_iota(jnp.int32, (1, LORA_COLS), 1)
    lin = jnp.where(lane < DECAY_RANK, jnp.tanh(lin), lin).astype(BF16)
    d_all = jnp.dot(lin, w2_ref[...], preferred_element_type=F32)
    wlog = -DECAY_SCALE * _sigmoid(vrow(_V_W0) + d_all[:, 0:rw])
    alr = _sigmoid(vrow(_V_A0) + d_all[:, rw:2 * rw])
    if has_vres:
        v = v + (vfirst_ref[...] - v) * _sigmoid(vrow(_V_V0) + d_all[:, 2 * rw:3 * rw])
    else:
        vout_ref[...] = v
    kkr = k * vrow(_V_KK)
    kf = k * (1.0 + (alr - 1.0) * vrow(_V_KA))
    rkr = r * kf * vrow(_V_RK)
    g_all = zg_ref[...].astype(F32)
    gate = g_all * _sigmoid(g_all)

    w_hi, w_lo = _split_bf16(wlog)
    w_lo2 = (wlog - w_hi.astype(F32) - w_lo.astype(F32)).astype(BF16)
    tri = tri_ref[...]
    cum = (jnp.dot(tri, w_hi, preferred_element_type=F32) + jnp.dot(tri, w_lo, preferred_element_type=F32)
           + jnp.dot(tri, w_lo2, preferred_element_type=F32))
    p_inc = jnp.exp(cum)
    inv_p = jnp.exp(-cum)
    p_exc = jnp.exp(cum - wlog)

    mask_sc = msc_ref[...]
    eye_b = lev_ref[0]
    e = EXPANDED

    groups = range(N_GROUPS)
    sls = [slice(gi * gl, (gi + 1) * gl) for gi in groups]
    kk_n2 = [_segsum(kkr[:, sl] * kkr[:, sl], bones_b) for sl in sls]
    kk = [kkr[:, sl] * lax.rsqrt(jnp.maximum(n2, 1e-24)) for sl, n2 in zip(sls, kk_n2)]
    v_g = [v[:, sl] for sl in sls]
    a_f = [-kk[gi] * p_exc[:, sls[gi]] for gi in groups]
    r_f = [r[:, sl] * p_inc[:, sl] for sl in sls]
    a_t = [x.astype(BF16) for x in a_f]
    r_t = [x.astype(BF16) for x in r_f]
    b_t = [(kk[gi] * alr[:, sls[gi]] * inv_p[:, sls[gi]]).astype(BF16) for gi in groups]
    k_t = [(kf[:, sl] * inv_p[:, sl]).astype(BF16) for sl in sls]
    v_e = [expand(x.astype(BF16)) for x in v_g]
    ar = [jnp.concatenate([a_t[gi], r_t[gi]], axis=0) for gi in groups]
    sc = [_dot(ar[gi], jnp.concatenate([expand(b_t[gi]), expand(k_t[gi])], axis=0), _NT) for gi in groups]
    scm = [x.astype(BF16) * mask_sc for x in sc]

    n_bd = [expand(x[:CHUNK, :e]) for x in scm]
    xs = [eye_b + n * lev_ref[1] for n in n_bd]
    for lvl in range(2, n_levels):
        t1 = [_dot(xs[gi], n_bd[gi] * lev_ref[lvl]).astype(BF16) + eye_b for gi in groups]
        xs = [_dot(t1[gi], xs[gi]).astype(BF16) for gi in groups]
    x_c = [(x[0:CHUNK] + x[CHUNK:2 * CHUNK]) + (x[2 * CHUNK:3 * CHUNK] + x[3 * CHUNK:4 * CHUNK]) for x in xs]

    a_s, r_s = [], []
    for gi in groups:
        parts_a, parts_r = [], []
        for q in range(n_seq):
            lo_, hi_ = q * t_seq, (q + 1) * t_seq
            lhs = ar[gi] if n_seq == 1 else jnp.concatenate(
                [a_f[gi][lo_:hi_], r_f[gi][lo_:hi_]], axis=0).astype(BF16)
            res = _dot(lhs, sbd_ref[q, gi].astype(BF16), _NT)
            parts_a.append(res[:t_seq])
            parts_r.append(res[t_seq:])
        a_s.append(parts_a[0] if n_seq == 1 else jnp.concatenate(parts_a, axis=0))
        r_s.append(parts_r[0] if n_seq == 1 else jnp.concatenate(parts_r, axis=0))

    y = [a_s[gi] + _dot(scm[gi][:CHUNK, e:], v_e[gi]) for gi in groups]
    u = [_dot(x_c[gi], expand(y[gi].astype(BF16))) for gi in groups]
    o = [r_s[gi] + _dot(scm[gi][CHUNK:, :], jnp.concatenate([expand(u[gi].astype(BF16)), v_e[gi]], axis=0))
         for gi in groups]

    for gi in groups:
        uv_t = jnp.concatenate([u[gi], v_g[gi]], axis=0).T
        bk = jnp.concatenate([b_t[gi], k_t[gi]], axis=0)
        for q in range(n_seq):
            lhs = uv_t if n_seq == 1 else uv_t * qmask_ref[q]
            p_last = p_inc[(q + 1) * t_seq - 1:(q + 1) * t_seq, sls[gi]]
            sbd_ref[q, gi] = bones * ((sbd_ref[q, gi] + _dot(lhs.astype(BF16), bk)) * p_last)

    mean = [_segsum(x, bones_b) * (1.0 / HEAD_DIM) for x in o]
    d = [o[gi] - mean[gi] for gi in groups]
    var = [_segsum(x * x, bones_b) * (1.0 / HEAD_DIM) for x in d]
    bonus = [_segsum(rkr[:, sl], bones_b) for sl in sls]
    for gi in groups:
        sl = sls[gi]
        on = d[gi] * lax.rsqrt(var[gi] + GN_EPS) * vec[_V_LNW:_V_LNW + 1, sl] + vec[_V_LNB:_V_LNB + 1, sl]
        ya_ref[:, sl] = ((on + bonus[gi] * v_g[gi]) * gate[:, sl]).astype(ya_ref.dtype)


    @pl.when(c_idx == n_chunks - 1)
    def _():
        for q in range(n_seq):
            for g in range(N_GROUPS):
                s = sbd_ref[q, g]
                sout_ref[q, g] = ((s[0:HEAD_DIM] + s[HEAD_DIM:2 * HEAD_DIM])
                                  + (s[2 * HEAD_DIM:3 * HEAD_DIM] + s[3 * HEAD_DIM:4 * HEAD_DIM]))


def _wkv(z, prev0, vfirst, w2e, vecs, s_all, layer, *, n_batch, t_len, n_seq):
    m = z.shape[0]
    t_seq = CHUNK // n_seq
    n_chunks = t_len // t_seq
    assert n_chunks == 1 or n_seq == 1
    nb = n_batch // n_seq
    has_vres = vfirst is not None
    consts = _wkv_constants(n_seq, t_seq)
    n_levels = consts["levels"].shape[0]
    rw = RWKV_WIDTH
    rb = lambda b, c: b * n_chunks + c

    zspec = lambda col: pl.BlockSpec((CHUNK, rw), lambda b, c: (rb(b, c), col))
    full = lambda a: pl.BlockSpec(a.shape, lambda b, c: (0,) * a.ndim)
    sspec = pl.BlockSpec((None, n_seq, N_GROUPS, HEAD_DIM, GROUP_LANES), lambda b, c: (layer, b, 0, 0, 0))

    in_specs = [zspec(0), zspec(1), zspec(2), zspec(3),
                pl.BlockSpec((CHUNK, PROJ_TN), lambda b, c: (rb(b, c), N_MAIN_TILES)),
                pl.BlockSpec((1, n_seq, PREV_COLS), lambda b, c: (b, 0, 0))]
    args = [z, z, z, z, z, prev0]
    if has_vres:
        in_specs.append(zspec(0))
        args.append(vfirst)
    else:
        w2e = w2e[:, :2 * rw]
    in_specs += [full(w2e), full(vecs), sspec]
    args += [w2e, vecs, s_all]
    state_arg = len(args) - 1
    for name in ("tri", "mask_sc", "bones", "bones_b", "levels", "qmask"):
        in_specs.append(full(consts[name]))
        args.append(consts[name])

    out_specs = [zspec(0)]
    out_shape = [jax.ShapeDtypeStruct((m, rw), BF16)]
    if not has_vres:
        out_specs.append(zspec(0))
        out_shape.append(jax.ShapeDtypeStruct((m, rw), F32))
    out_specs.append(sspec)
    out_shape.append(jax.ShapeDtypeStruct(s_all.shape, F32))

    outs = pl.pallas_call(
        functools.partial(_wkv_kernel, n_seq=n_seq, t_seq=t_seq, n_chunks=n_chunks, has_vres=has_vres,
                          n_levels=n_levels),
        grid=(nb, n_chunks),
        in_specs=in_specs, out_specs=out_specs, out_shape=out_shape,
        input_output_aliases={state_arg: len(out_shape) - 1},
        scratch_shapes=[pltpu.VMEM((n_seq, N_GROUPS, GROUP_LANES, GROUP_LANES), F32),
                        pltpu.VMEM((n_seq, PREV_COLS), F32)],
        name="wkv",
        compiler_params=pltpu.CompilerParams(
            dimension_semantics=("parallel", "arbitrary"), vmem_limit_bytes=VMEM_LIMIT_BYTES),
    )(*args)
    if has_vres:
        ya, s_new = outs
        return ya, vfirst, s_new
    return outs


def _mix_kernel(ya_ref, u_ref, gp_ref, ga_ref, gb_ref, halo_ref, pw_ref, ps_ref, wb0_ref, wb1_ref, m_ref,
                *, n_seq, t_seq, tiles_per_seq, n_past, halo_from_z):
    rows = n_seq * t_seq
    u = u_ref[...].astype(F32)
    tile = pl.program_id(0) % tiles_per_seq
    if halo_from_z:
        halo = jnp.where(tile == 0, 0.0, halo_ref[...].astype(F32)).reshape(1, HALO, POOL_WIDTH)
    else:
        halo = halo_ref[...]
    e = jnp.concatenate([halo, u.reshape(n_seq, t_seq, POOL_WIDTH)], axis=1)
    pos = lax.broadcasted_iota(jnp.int32, (1, t_seq, 1), 1) + (tile * t_seq + 1 + n_past)
    yb_parts = []
    for gi, win in enumerate(POOL_WINDOWS):
        sl = slice(gi * POOL_GROUP, (gi + 1) * POOL_GROUP)
        acc = e[:, :, sl]
        span, length = 1, HALO + t_seq
        while span < win:
            length -= span
            acc = acc[:, span:span + length] + acc[:, :length]
            span *= 2
        ws = acc[:, length - t_seq:]
        cnt = jnp.minimum(pos, win).astype(F32)
        mixed = (ws / cnt - e[:, HALO:, sl]).reshape(rows, POOL_GROUP)
        yb_parts.append(jnp.dot(mixed.astype(BF16), pw_ref[gi], preferred_element_type=F32))
    gp = gp_ref[...].astype(F32)
    y_b = jnp.concatenate(yb_parts, axis=1) * ps_ref[...] * (gp * _sigmoid(gp))
    pa = jnp.dot(ya_ref[...], wb0_ref[...], preferred_element_type=F32)
    pb = jnp.dot(y_b.astype(BF16), wb1_ref[...], preferred_element_type=F32)
    m_ref[...] = (_sigmoid(ga_ref[...].astype(F32)) * pa + _sigmoid(gb_ref[...].astype(F32)) * pb).astype(m_ref.dtype)


def _mix(ya, z, halo, pool_w_b, pool_scale, w_branch_b, layer, *, n_seq, t_seq, tiles_per_seq, n_past):
    m = z.shape[0]
    rows = n_seq * t_seq
    halo_from_z = halo is None
    if halo_from_z:
        assert n_seq == 1 and rows % HALO == 0
        halo = z
        halo_spec = pl.BlockSpec((HALO, POOL_WIDTH), lambda i: (jnp.maximum(i * (rows // HALO) - 1, 0), 4))
    else:
        halo_spec = pl.BlockSpec((n_seq, HALO, POOL_WIDTH), lambda i: (i, 0, 0))
    return pl.pallas_call(
        functools.partial(_mix_kernel, n_seq=n_seq, t_seq=t_seq, tiles_per_seq=tiles_per_seq, n_past=n_past,
                          halo_from_z=halo_from_z),
        grid=(m // rows,),
        in_specs=[
            pl.BlockSpec((rows, RWKV_WIDTH), lambda i: (i, 0)),
            pl.BlockSpec((rows, POOL_WIDTH), lambda i: (i, 4)),
            pl.BlockSpec((rows, POOL_WIDTH), lambda i: (i, 5)),
            pl.BlockSpec((rows, D_MODEL), lambda i: (i, 3)),
            pl.BlockSpec((rows, D_MODEL), lambda i: (i, 4)),
            halo_spec,
            pl.BlockSpec((None, len(POOL_WINDOWS), POOL_GROUP, POOL_GROUP), lambda i: (layer, 0, 0, 0)),
            pl.BlockSpec((None, 1, POOL_WIDTH), lambda i: (layer, 0, 0)),
            pl.BlockSpec((None, None, RWKV_WIDTH, D_MODEL), lambda i: (layer, 0, 0, 0)),
            pl.BlockSpec((None, None, POOL_WIDTH, D_MODEL), lambda i: (layer, 1, 0, 0)),
        ],
        out_specs=pl.BlockSpec((rows, D_MODEL), lambda i: (i, 0)),
        out_shape=jax.ShapeDtypeStruct((m, D_MODEL), BF16),
        name="mix",
        compiler_params=pltpu.CompilerParams(
            dimension_semantics=("parallel",), vmem_limit_bytes=VMEM_LIMIT_BYTES),
    )(ya, z, z, z, z, halo, pool_w_b, pool_scale.reshape(DEPTH, 1, POOL_WIDTH), w_branch_b, w_branch_b)


def _out_kernel(m_ref, x_ref, p_ref, g_ref, wo_ref, wg_ref, wp_ref, o_ref):
    out = jnp.dot(m_ref[...], wo_ref[...], preferred_element_type=F32)
    out = out * lax.rsqrt(jnp.mean(out * out, axis=-1, keepdims=True) + RMS_EPS) * g_ref[...]
    x = x_ref[...] + out
    gate = _sigmoid(jnp.dot(x.astype(BF16), wg_ref[...], preferred_element_type=F32))
    ple = jnp.dot(p_ref[...].astype(BF16), wp_ref[...], preferred_element_type=F32)
    o_ref[...] = x + gate * ple


def _out(mm_, x2d, p3d, norm_gain, w_out_b, w_gate_b, w_ple_b, layer, *, rows):
    m = x2d.shape[0]
    once = pl.Buffered(1)
    wspec = lambda k: pl.BlockSpec((None, k, D_MODEL), lambda i: (layer, 0, 0), pipeline_mode=once)
    return pl.pallas_call(
        _out_kernel,
        grid=(m // rows,),
        in_specs=[
            pl.BlockSpec((rows, D_MODEL), lambda i: (i, 0)),
            pl.BlockSpec((rows, D_MODEL), lambda i: (i, 0)),
            pl.BlockSpec((None, rows, PLE_DIM), lambda i: (layer, i, 0)),
            wspec(1), wspec(D_MODEL), wspec(D_MODEL), wspec(PLE_DIM),
        ],
        out_specs=pl.BlockSpec((rows, D_MODEL), lambda i: (i, 0)),
        out_shape=jax.ShapeDtypeStruct((m, D_MODEL), F32),
        name="out",
        compiler_params=pltpu.CompilerParams(
            dimension_semantics=("parallel",), vmem_limit_bytes=VMEM_LIMIT_BYTES),
    )(mm_, x2d, p3d, norm_gain.reshape(DEPTH, 1, D_MODEL), w_out_b, w_gate_b, w_ple_b)


def _group_lanes(s):
    l, b = s.shape[:2]
    return s.reshape(l, b, N_GROUPS, HEADS_PER_GROUP, HEAD_DIM, HEAD_DIM).transpose(0, 1, 2, 4, 3, 5).reshape(
        l, b, N_GROUPS, HEAD_DIM, GROUP_LANES)


def _ungroup_lanes(s):
    l, b = s.shape[:2]
    return s.reshape(l, b, N_GROUPS, HEAD_DIM, HEADS_PER_GROUP, HEAD_DIM).transpose(0, 1, 2, 4, 3, 5).reshape(
        l, b, N_HEADS, HEAD_DIM, HEAD_DIM)


def _layer_weights(i, mu_rkv, mu_lora, w0, w1, w2, a0, a1, a2, v0, v1, v2, k_k, k_a, r_k, ln_w, ln_b):
    r_ = RWKV_WIDTH
    mu = mu_lora[i]
    if i > 0:
        v1_i, v2_i, v0_i = v1[i - 1], v2[i - 1], v0[i - 1]
    else:
        v1_i = jnp.zeros((D_MODEL, MV_RANK), F32)
        v2_i = jnp.zeros((MV_RANK, r_), F32)
        v0_i = jnp.zeros((r_,), F32)
    w_lora = jnp.concatenate(
        [(1.0 - mu[0])[:, None] * w1[i], (1.0 - mu[1])[:, None] * a1[i], (1.0 - mu[2])[:, None] * v1_i,
         mu[0][:, None] * w1[i], mu[1][:, None] * a1[i], mu[2][:, None] * v1_i], axis=1).astype(BF16)
    z96 = jnp.zeros((DECAY_RANK, r_), F32)
    z64 = jnp.zeros((MV_RANK, r_), F32)
    w2e = jnp.concatenate([
        jnp.concatenate([w2[i], z96, z64], axis=0),
        jnp.concatenate([z96, a2[i], z64], axis=0),
        jnp.concatenate([z96, z96, v2_i], axis=0)], axis=1).astype(BF16)
    rows = [mu_rkv[i, 0], mu_rkv[i, 1], mu_rkv[i, 2], w0[i], a0[i], v0_i, k_k[i], k_a[i], r_k[i], ln_w[i], ln_b[i]]
    vecs = jnp.concatenate([jnp.stack(rows), jnp.zeros((_N_VEC_ROWS - len(rows), r_), F32)], axis=0)
    return w_lora, w2e, vecs


def _run_group(x, p, wkv, shift, pool, n_past, lw, params, *, n_seq, proj_tm, mix_seq, mix_t, out_rows):
    (w_in, norm_pre, norm_post, pool_w_b, pool_scale, w_branch_b, w_out_b, w_ple_b, w_gate_b) = params
    b, t, d = x.shape
    m = b * t
    x2d = x.reshape(m, d)
    p3d = p.reshape(DEPTH, m, PLE_DIM)
    zero_state = wkv is None
    shift_out, pool_out = [], []
    v_first = None
    tiles_per_seq = t // mix_t
    s_all = jnp.zeros((DEPTH, b, N_GROUPS, HEAD_DIM, GROUP_LANES), F32) if zero_state else _group_lanes(wkv)
    for i in range(DEPTH):
        w_lora, w2e, vecs = lw[i]
        z = _proj(x2d, norm_pre, w_in, i, w_lora, normalize=True, tm=proj_tm, n_main=N_MAIN_TILES, out_dtype=BF16)
        shift_out.append(_norm_rows(x2d.reshape(b, t, d)[:, -1], norm_pre[i]))
        u_last = z.reshape(b, t, -1)[:, -min(t, POOL_BUF):, 4 * RWKV_WIDTH:4 * RWKV_WIDTH + POOL_WIDTH].astype(F32)
        if zero_state:
            assert t >= POOL_BUF
            prev0 = jnp.zeros((b // n_seq, n_seq, PREV_COLS), F32)
            halo = None
            pool_out.append(u_last)
        else:
            prev0 = _proj(shift[i], norm_pre, w_in, i, w_lora, normalize=False, tm=b,
                          n_main=3 * RWKV_WIDTH // PROJ_TN, out_dtype=F32).reshape(b // n_seq, n_seq, PREV_COLS)
            assert tiles_per_seq == 1 and t < POOL_BUF
            halo = jnp.pad(pool[i], ((0, 0), (HALO - POOL_BUF, 0), (0, 0)))
            pool_out.append(jnp.concatenate([pool[i][:, t:], u_last], axis=1))
        ya, v_first, s_all = _wkv(z, prev0, v_first, w2e, vecs, s_all, i, n_batch=b, t_len=t, n_seq=n_seq)
        mm_ = _mix(ya, z, halo, pool_w_b, pool_scale, w_branch_b, i,
                   n_seq=mix_seq, t_seq=mix_t, tiles_per_seq=tiles_per_seq, n_past=n_past)
        x2d = _out(mm_, x2d, p3d, norm_post, w_out_b, w_gate_b, w_ple_b, i, rows=out_rows)
    return x2d.reshape(b, t, d), _ungroup_lanes(s_all), jnp.stack(shift_out), jnp.stack(pool_out)


def kernel(x_prompt, x_sample, state_wkv, state_shift, state_pool, p_prompt, p_sample, norm_pre, norm_post, w_in, mu_rkv, mu_lora, w0, w1, w2, a0, a1, a2, v0, v1, v2, k_k, k_a, r_k, ln_w, ln_b, pool_w, pool_scale, w_branch, w_out, w_ple, w_ple_gate):
    lw = [_layer_weights(i, mu_rkv, mu_lora, w0, w1, w2, a0, a1, a2, v0, v1, v2, k_k, k_a, r_k, ln_w, ln_b)
          for i in range(DEPTH)]
    params = (w_in, norm_pre, norm_post, pool_w.astype(BF16), pool_scale, w_branch.astype(BF16),
              w_out.astype(BF16), w_ple.astype(BF16), w_ple_gate.astype(BF16))
    y_p, wkv_p, shift_p, pool_p = _run_group(
        x_prompt, p_prompt, None, None, None, 0, lw, params,
        n_seq=1, proj_tm=2048, mix_seq=1, mix_t=256, out_rows=256)
    y_s, wkv_s, shift_s, pool_s = _run_group(
        x_sample, p_sample, state_wkv, state_shift, state_pool, PAST_LEN, lw, params,
        n_seq=8, proj_tm=1024, mix_seq=32, mix_t=8, out_rows=256)
    return (y_p, y_s, wkv_p, shift_p, pool_p, wkv_s, shift_s, pool_s)
```

```python
import functools

import jax
import jax.numpy as jnp
import numpy as np
from jax import lax
from jax.experimental import pallas as pl
from jax.experimental.pallas import tpu as pltpu

F32 = jnp.float32
BF16 = jnp.bfloat16

D_MODEL = 2048
DEPTH = 4
PAST_LEN = 16384
RWKV_WIDTH = 1024
HEAD_DIM = 64
N_HEADS = 16
POOL_WIDTH = 1024
POOL_WINDOWS = (2, 4, 8, 16)
POOL_GROUP = 256
POOL_BUF = 15
PLE_DIM = 256
DECAY_RANK = 96
AAA_RANK = 96
MV_RANK = 64
LORA_COLS = DECAY_RANK + AAA_RANK + MV_RANK
IN_COLS = 10240
RMS_EPS = 1e-6
GN_EPS = 64e-5
DECAY_SCALE = 0.606531

HEADS_PER_GROUP = 4
GROUP_LANES = HEADS_PER_GROUP * HEAD_DIM
N_GROUPS = N_HEADS // HEADS_PER_GROUP
CHUNK = 64
EXPANDED = HEADS_PER_GROUP * CHUNK
HALO = 16
PROJ_TN = 2 * LORA_COLS
N_MAIN_TILES = IN_COLS // PROJ_TN
PREV_COLS = 3 * RWKV_WIDTH + PROJ_TN
LB_OFF = 3 * RWKV_WIDTH + LORA_COLS

VMEM_LIMIT_BYTES = 56 * 1024 * 1024


def _sigmoid(x):
    return 0.5 * jnp.tanh(0.5 * x) + 0.5


def _split_bf16(x):
    hi = x.astype(BF16)
    lo = (x - hi.astype(F32)).astype(BF16)
    return hi, lo


_NN = (((1,), (0,)), ((), ()))
_NT = (((1,), (1,)), ((), ()))


def _dot(a, b, dims=_NN):
    return lax.dot_general(a, b, dimension_numbers=dims, preferred_element_type=F32)


def _segsum(x, bones_b):
    hi, lo = _split_bf16(x)
    return _dot(hi, bones_b) + _dot(lo, bones_b)


def _proj_kernel(x_ref, g_ref, w_ref, wl_ref, z_ref, hb_ref, *, normalize, n_main):
    j = pl.program_id(1)

    @pl.when(j == 0)
    def _():
        x = x_ref[...]
        if normalize:
            x = x * lax.rsqrt(jnp.mean(x * x, axis=-1, keepdims=True) + RMS_EPS) * g_ref[...]
        hb_ref[...] = x.astype(BF16)

    @pl.when(j < n_main)
    def _():
        z_ref[...] = jnp.dot(hb_ref[...], w_ref[...].astype(BF16), preferred_element_type=F32).astype(z_ref.dtype)

    @pl.when(j == n_main)
    def _():
        z_ref[...] = jnp.dot(hb_ref[...], wl_ref[...], preferred_element_type=F32).astype(z_ref.dtype)


def _proj(x2d, norm_gain, w_in, layer, w_lora, *, normalize, tm, n_main, out_dtype):
    m, d = x2d.shape
    tn = PROJ_TN
    assert m % tm == 0
    once = pl.Buffered(1)
    return pl.pallas_call(
        functools.partial(_proj_kernel, normalize=normalize, n_main=n_main),
        grid=(m // tm, n_main + 1),
        in_specs=[
            pl.BlockSpec((tm, d), lambda i, j: (i, 0), pipeline_mode=once),
            pl.BlockSpec((None, 1, d), lambda i, j: (layer, 0, 0)),
            pl.BlockSpec((None, d, tn), lambda i, j: (layer, 0, jnp.minimum(j, n_main - 1))),
            pl.BlockSpec((d, tn), lambda i, j: (0, 0), pipeline_mode=once),
        ],
        out_specs=pl.BlockSpec((tm, tn), lambda i, j: (i, j)),
        out_shape=jax.ShapeDtypeStruct((m, (n_main + 1) * tn), out_dtype),
        scratch_shapes=[pltpu.VMEM((tm, d), BF16)],
        name="proj",
        compiler_params=pltpu.CompilerParams(
            dimension_semantics=("parallel", "arbitrary"), vmem_limit_bytes=VMEM_LIMIT_BYTES),
    )(x2d, norm_gain.reshape(DEPTH, 1, d), w_in, w_lora)


def _norm_rows_kernel(x_ref, g_ref, o_ref):
    x = x_ref[...]
    o_ref[...] = x * lax.rsqrt(jnp.mean(x * x, axis=-1, keepdims=True) + RMS_EPS) * g_ref[...]


def _norm_rows(x2d, gain):
    return pl.pallas_call(
        _norm_rows_kernel, out_shape=jax.ShapeDtypeStruct(x2d.shape, F32), name="norm_rows",
    )(x2d, gain.reshape(1, -1))


_V_MU_R, _V_MU_K, _V_MU_V, _V_W0, _V_A0, _V_V0, _V_KK, _V_KA, _V_RK, _V_LNW, _V_LNB = range(11)
_N_VEC_ROWS = 16


def _wkv_constants(n_seq, t_seq):
    c, e = CHUNK, EXPANDED
    assert n_seq * t_seq == c
    row = np.arange(c)
    q_of = row // t_seq
    tri = (row[:, None] >= row[None, :]) & (q_of[:, None] == q_of[None, :])
    erow = np.arange(e)
    es = erow % c
    same = q_of[:, None] == (es // t_seq)[None, :]
    mask_s = same & (row[:, None] > es[None, :])
    mask_i = same & (row[:, None] >= es[None, :])
    mask_sc = np.block([[mask_s, mask_s], [mask_i, mask_i]])
    lane = np.arange(GROUP_LANES)
    bones = (lane[:, None] // HEAD_DIM) == (lane[None, :] // HEAD_DIM)
    levels = [np.eye(e, dtype=bool), (erow[:, None] // 2) == (erow[None, :] // 2)]
    m = 2
    while m < t_seq:
        levels.append(((erow[:, None] // (2 * m)) == (erow[None, :] // (2 * m)))
                      & ((erow[:, None] % (2 * m)) >= m) & ((erow[None, :] % (2 * m)) < m))
        m *= 2
    col = np.arange(2 * c)
    qmask = ((col[None, :] % c) // t_seq) == np.arange(n_seq)[:, None]
    f = lambda a: jnp.asarray(a.astype(np.float32))
    return dict(tri=f(tri).astype(BF16), mask_sc=f(mask_sc).astype(BF16), bones=f(bones),
                bones_b=f(bones).astype(BF16), levels=f(np.stack(levels)).astype(BF16),
                qmask=f(qmask).reshape(n_seq, 1, 2 * c))


def _wkv_kernel(*refs, n_seq, t_seq, n_chunks, has_vres, n_levels):
    it = iter(refs)
    zr_ref, zk_ref, zv_ref, zg_ref, zl_ref, p0_ref = (next(it) for _ in range(6))
    vfirst_ref = next(it) if has_vres else None
    w2_ref, vec_ref, s0_ref = next(it), next(it), next(it)
    tri_ref, msc_ref, bones_ref, bonesb_ref, lev_ref, qmask_ref = (next(it) for _ in range(6))
    ya_ref = next(it)
    vout_ref = None if has_vres else next(it)
    sout_ref = next(it)
    sbd_ref, prev_ref = next(it), next(it)

    c_idx = pl.program_id(1)
    gl, rw = GROUP_LANES, RWKV_WIDTH
    bones = bones_ref[...]
    bones_b = bonesb_ref[...]

    def expand(x):
        return jnp.concatenate([x] * HEADS_PER_GROUP, axis=0) * bones_b.astype(x.dtype)

    low_lanes = lax.broadcasted_iota(jnp.int32, (1, 2 * HEAD_DIM), 1) < HEAD_DIM


    def head_slot(hd):
        h = hd % HEADS_PER_GROUP
        return hd // HEADS_PER_GROUP, h * HEAD_DIM, h // 2, h % 2 == 1

    def load_state(q, g):
        return jnp.concatenate([sbd_ref[q, g, 0], sbd_ref[q, g, 1]], axis=1)

    def store_state(q, g, s):
        sbd_ref[q, g, 0] = s[:, :2 * HEAD_DIM]
        sbd_ref[q, g, 1] = s[:, 2 * HEAD_DIM:]

    @pl.when(c_idx == 0)
    def _():
        for q in range(n_seq):
            for hd in range(N_HEADS):
                g, row0, half, upper = head_slot(hd)
                packed = s0_ref[q, hd]
                even = jnp.where(low_lanes, packed, 0.0)
                odd = jnp.where(low_lanes, 0.0, packed)
                if upper:
                    even = pltpu.roll(even, HEAD_DIM, axis=1)
                else:
                    odd = pltpu.roll(odd, HEAD_DIM, axis=1)
                sbd_ref[q, g, half, pl.ds(row0, HEAD_DIM // 2, stride=2), :] = even
                sbd_ref[q, g, half, pl.ds(row0 + 1, HEAD_DIM // 2, stride=2), :] = odd
                sbd_ref[q, g, 1 - half, row0:row0 + HEAD_DIM, :] = jnp.zeros((HEAD_DIM, 2 * HEAD_DIM), F32)
        prev_ref[...] = p0_ref[0]

    row = lax.broadcasted_iota(jnp.int32, (CHUNK, 1), 0)
    is_first = (row % t_seq) == 0

    def shifted(z, p0):
        rolled = pltpu.roll(z, 1, axis=0)
        if n_seq == 1:
            first = jnp.broadcast_to(p0, z.shape)
        else:
            first = jnp.broadcast_to(p0[:, None, :], (n_seq, t_seq, z.shape[1])).reshape(z.shape)
        return jnp.where(is_first, first, rolled)

    vec = vec_ref[...]
    vrow = lambda i: vec[i:i + 1, :]

    zr, zk, zv = zr_ref[...].astype(F32), zk_ref[...].astype(F32), zv_ref[...].astype(F32)
    zl = zl_ref[...].astype(F32)
    la, lb = zl[:, :LORA_COLS], zl[:, LORA_COLS:]
    zr_p = shifted(zr, prev_ref[:, 0:rw])
    zk_p = shifted(zk, prev_ref[:, rw:2 * rw])
    zv_p = shifted(zv, prev_ref[:, 2 * rw:3 * rw])
    lb_p = shifted(lb, prev_ref[:, LB_OFF:LB_OFF + LORA_COLS])
    if n_chunks > 1:
        assert n_seq == 1
        prev_ref[:, 0:rw] = zr[CHUNK - 1:CHUNK, :]
        prev_ref[:, rw:2 * rw] = zk[CHUNK - 1:CHUNK, :]
        prev_ref[:, 2 * rw:3 * rw] = zv[CHUNK - 1:CHUNK, :]
        prev_ref[:, LB_OFF:LB_OFF + LORA_COLS] = lb[CHUNK - 1:CHUNK, :]

    r = zr + (zr_p - zr) * vrow(_V_MU_R)
    k = zk + (zk_p - zk) * vrow(_V_MU_K)
    v = zv + (zv_p - zv) * vrow(_V_MU_V)

    lin = la + lb_p
    lane = lax.broadcasted_iota(jnp.int32, (1, LORA_COLS), 1)
    lin = jnp.where(lane < DECAY_RANK, jnp.tanh(lin), lin).astype(BF16)
    d_all = jnp.dot(lin, w2_ref[...], preferred_element_type=F32)
    wlog = -DECAY_SCALE * _sigmoid(vrow(_V_W0) + d_all[:, 0:rw])
    alr = _sigmoid(vrow(_V_A0) + d_all[:, rw:2 * rw])
    if has_vres:
        v = v + (vfirst_ref[...] - v) * _sigmoid(vrow(_V_V0) + d_all[:, 2 * rw:3 * rw])
    else:
        vout_ref[...] = v
    kkr = k * vrow(_V_KK)
    kf = k * (1.0 + (alr - 1.0) * vrow(_V_KA))
    rkr = r * kf * vrow(_V_RK)
    g_all = zg_ref[...].astype(F32)
    gate = g_all * _sigmoid(g_all)

    w_hi, w_lo = _split_bf16(wlog)
    w_lo2 = (wlog - w_hi.astype(F32) - w_lo.astype(F32)).astype(BF16)
    tri = tri_ref[...]
    cum = (jnp.dot(tri, w_hi, preferred_element_type=F32) + jnp.dot(tri, w_lo, preferred_element_type=F32)
           + jnp.dot(tri, w_lo2, preferred_element_type=F32))
    p_inc = jnp.exp(cum)
    inv_p = jnp.exp(-cum)
    p_exc = jnp.exp(cum - wlog)

    mask_sc = msc_ref[...]
    eye_b = lev_ref[0]
    e = EXPANDED

    groups = range(N_GROUPS)
    sls = [slice(gi * gl, (gi + 1) * gl) for gi in groups]
    kk_n2 = [_segsum(kkr[:, sl] * kkr[:, sl], bones_b) for sl in sls]
    kk = [kkr[:, sl] * lax.rsqrt(jnp.maximum(n2, 1e-24)) for sl, n2 in zip(sls, kk_n2)]
    v_g = [v[:, sl] for sl in sls]
    a_f = [-kk[gi] * p_exc[:, sls[gi]] for gi in groups]
    r_f = [r[:, sl] * p_inc[:, sl] for sl in sls]
    a_t = [x.astype(BF16) for x in a_f]
    r_t = [x.astype(BF16) for x in r_f]
    b_t = [(kk[gi] * alr[:, sls[gi]] * inv_p[:, sls[gi]]).astype(BF16) for gi in groups]
    k_t = [(kf[:, sl] * inv_p[:, sl]).astype(BF16) for sl in sls]
    v_e = [expand(x.astype(BF16)) for x in v_g]
    ar = [jnp.concatenate([a_t[gi], r_t[gi]], axis=0) for gi in groups]
    sc = [_dot(ar[gi], jnp.concatenate([expand(b_t[gi]), expand(k_t[gi])], axis=0), _NT) for gi in groups]
    scm = [x.astype(BF16) * mask_sc for x in sc]

    n_bd = [expand(x[:CHUNK, :e]) for x in scm]
    xs = [eye_b + n * lev_ref[1] for n in n_bd]
    for lvl in range(2, n_levels):
        t1 = [_dot(xs[gi], n_bd[gi] * lev_ref[lvl]).astype(BF16) + eye_b for gi in groups]
        xs = [_dot(t1[gi], xs[gi]).astype(BF16) for gi in groups]
    x_c = [(x[0:CHUNK] + x[CHUNK:2 * CHUNK]) + (x[2 * CHUNK:3 * CHUNK] + x[3 * CHUNK:4 * CHUNK]) for x in xs]

    a_s, r_s = [], []
    for gi in groups:
        parts_a, parts_r = [], []
        for q in range(n_seq):
            lo_, hi_ = q * t_seq, (q + 1) * t_seq
            lhs = ar[gi] if n_seq == 1 else jnp.concatenate(
                [a_f[gi][lo_:hi_], r_f[gi][lo_:hi_]], axis=0).astype(BF16)
            res = _dot(lhs, load_state(q, gi).astype(BF16), _NT)
            parts_a.append(res[:t_seq])
            parts_r.append(res[t_seq:])
        a_s.append(parts_a[0] if n_seq == 1 else jnp.concatenate(parts_a, axis=0))
        r_s.append(parts_r[0] if n_seq == 1 else jnp.concatenate(parts_r, axis=0))

    y = [a_s[gi] + _dot(scm[gi][:CHUNK, e:], v_e[gi]) for gi in groups]
    u = [_dot(x_c[gi], expand(y[gi].astype(BF16))) for gi in groups]
    o = [r_s[gi] + _dot(scm[gi][CHUNK:, :], jnp.concatenate([expand(u[gi].astype(BF16)), v_e[gi]], axis=0))
         for gi in groups]

    for gi in groups:
        uv_t = jnp.concatenate([u[gi], v_g[gi]], axis=0).T
        bk = jnp.concatenate([b_t[gi], k_t[gi]], axis=0)
        for q in range(n_seq):
            lhs = uv_t if n_seq == 1 else uv_t * qmask_ref[q]
            p_last = p_inc[(q + 1) * t_seq - 1:(q + 1) * t_seq, sls[gi]]
            store_state(q, gi, bones * ((load_state(q, gi) + _dot(lhs.astype(BF16), bk)) * p_last))

    mean = [_segsum(x, bones_b) * (1.0 / HEAD_DIM) for x in o]
    d = [o[gi] - mean[gi] for gi in groups]
    var = [_segsum(x * x, bones_b) * (1.0 / HEAD_DIM) for x in d]
    bonus = [_segsum(rkr[:, sl], bones_b) for sl in sls]
    for gi in groups:
        sl = sls[gi]
        on = d[gi] * lax.rsqrt(var[gi] + GN_EPS) * vec[_V_LNW:_V_LNW + 1, sl] + vec[_V_LNB:_V_LNB + 1, sl]
        ya_ref[:, sl] = ((on + bonus[gi] * v_g[gi]) * gate[:, sl]).astype(ya_ref.dtype)


    @pl.when(c_idx == n_chunks - 1)
    def _():
        for q in range(n_seq):
            for hd in range(N_HEADS):
                g, row0, half, upper = head_slot(hd)
                even = sbd_ref[q, g, half, pl.ds(row0, HEAD_DIM // 2, stride=2), :]
                odd = sbd_ref[q, g, half, pl.ds(row0 + 1, HEAD_DIM // 2, stride=2), :]
                if upper:
                    even = pltpu.roll(even, HEAD_DIM, axis=1)
                else:
                    odd = pltpu.roll(odd, HEAD_DIM, axis=1)
                sout_ref[q, hd] = jnp.where(low_lanes, even, odd)


def _wkv(z, prev0, vfirst, w2e, vecs, s_all, layer, *, n_batch, t_len, n_seq):
    m = z.shape[0]
    t_seq = CHUNK // n_seq
    n_chunks = t_len // t_seq
    assert n_chunks == 1 or n_seq == 1
    nb = n_batch // n_seq
    has_vres = vfirst is not None
    consts = _wkv_constants(n_seq, t_seq)
    n_levels = consts["levels"].shape[0]
    rw = RWKV_WIDTH
    rb = lambda b, c: b * n_chunks + c

    zspec = lambda col: pl.BlockSpec((CHUNK, rw), lambda b, c: (rb(b, c), col))
    full = lambda a: pl.BlockSpec(a.shape, lambda b, c: (0,) * a.ndim)
    sspec = pl.BlockSpec((None, n_seq, N_HEADS, HEAD_DIM // 2, 2 * HEAD_DIM), lambda b, c: (layer, b, 0, 0, 0))

    in_specs = [zspec(0), zspec(1), zspec(2), zspec(3),
                pl.BlockSpec((CHUNK, PROJ_TN), lambda b, c: (rb(b, c), N_MAIN_TILES)),
                pl.BlockSpec((1, n_seq, PREV_COLS), lambda b, c: (b, 0, 0))]
    args = [z, z, z, z, z, prev0]
    if has_vres:
        in_specs.append(zspec(0))
        args.append(vfirst)
    else:
        w2e = w2e[:, :2 * rw]
    in_specs += [full(w2e), full(vecs), sspec]
    args += [w2e, vecs, s_all]
    state_arg = len(args) - 1
    for name in ("tri", "mask_sc", "bones", "bones_b", "levels", "qmask"):
        in_specs.append(full(consts[name]))
        args.append(consts[name])

    out_specs = [zspec(0)]
    out_shape = [jax.ShapeDtypeStruct((m, rw), BF16)]
    if not has_vres:
        out_specs.append(zspec(0))
        out_shape.append(jax.ShapeDtypeStruct((m, rw), F32))
    out_specs.append(sspec)
    out_shape.append(jax.ShapeDtypeStruct(s_all.shape, F32))

    outs = pl.pallas_call(
        functools.partial(_wkv_kernel, n_seq=n_seq, t_seq=t_seq, n_chunks=n_chunks, has_vres=has_vres,
                          n_levels=n_levels),
        grid=(nb, n_chunks),
        in_specs=in_specs, out_specs=out_specs, out_shape=out_shape,
        input_output_aliases={state_arg: len(out_shape) - 1},
        scratch_shapes=[pltpu.VMEM((n_seq, N_GROUPS, 2, GROUP_LANES, GROUP_LANES // 2), F32),
                        pltpu.VMEM((n_seq, PREV_COLS), F32)],
        name="wkv",
        compiler_params=pltpu.CompilerParams(
            dimension_semantics=("parallel", "arbitrary"), vmem_limit_bytes=VMEM_LIMIT_BYTES),
    )(*args)
    if has_vres:
        ya, s_new = outs
        return ya, vfirst, s_new
    return outs


def _mix_kernel(ya_ref, u_ref, gp_ref, ga_ref, gb_ref, halo_ref, pw_ref, ps_ref, wb0_ref, wb1_ref, m_ref,
                *, n_seq, t_seq, tiles_per_seq, n_past, halo_from_z):
    rows = n_seq * t_seq
    u = u_ref[...].astype(F32)
    tile = pl.program_id(0) % tiles_per_seq
    if halo_from_z:
        halo = jnp.where(tile == 0, 0.0, halo_ref[...].astype(F32)).reshape(1, HALO, POOL_WIDTH)
    else:
        halo = halo_ref[...]
    e = jnp.concatenate([halo, u.reshape(n_seq, t_seq, POOL_WIDTH)], axis=1)
    pos = lax.broadcasted_iota(jnp.int32, (1, t_seq, 1), 1) + (tile * t_seq + 1 + n_past)
    yb_parts = []
    for gi, win in enumerate(POOL_WINDOWS):
        sl = slice(gi * POOL_GROUP, (gi + 1) * POOL_GROUP)
        acc = e[:, :, sl]
        span, length = 1, HALO + t_seq
        while span < win:
            length -= span
            acc = acc[:, span:span + length] + acc[:, :length]
            span *= 2
        ws = acc[:, length - t_seq:]
        cnt = jnp.minimum(pos, win).astype(F32)
        mixed = (ws / cnt - e[:, HALO:, sl]).reshape(rows, POOL_GROUP)
        yb_parts.append(jnp.dot(mixed.astype(BF16), pw_ref[gi], preferred_element_type=F32))
    gp = gp_ref[...].astype(F32)
    y_b = jnp.concatenate(yb_parts, axis=1) * ps_ref[...] * (gp * _sigmoid(gp))
    pa = jnp.dot(ya_ref[...], wb0_ref[...], preferred_element_type=F32)
    pb = jnp.dot(y_b.astype(BF16), wb1_ref[...], preferred_element_type=F32)
    m_ref[...] = (_sigmoid(ga_ref[...].astype(F32)) * pa + _sigmoid(gb_ref[...].astype(F32)) * pb).astype(m_ref.dtype)


def _mix(ya, z, halo, pool_w_b, pool_scale, w_branch_b, layer, *, n_seq, t_seq, tiles_per_seq, n_past):
    m = z.shape[0]
    rows = n_seq * t_seq
    halo_from_z = halo is None
    if halo_from_z:
        assert n_seq == 1 and rows % HALO == 0
        halo = z
        halo_spec = pl.BlockSpec((HALO, POOL_WIDTH), lambda i: (jnp.maximum(i * (rows // HALO) - 1, 0), 4))
    else:
        halo_spec = pl.BlockSpec((n_seq, HALO, POOL_WIDTH), lambda i: (i, 0, 0))
    return pl.pallas_call(
        functools.partial(_mix_kernel, n_seq=n_seq, t_seq=t_seq, tiles_per_seq=tiles_per_seq, n_past=n_past,
                          halo_from_z=halo_from_z),
        grid=(m // rows,),
        in_specs=[
            pl.BlockSpec((rows, RWKV_WIDTH), lambda i: (i, 0)),
            pl.BlockSpec((rows, POOL_WIDTH), lambda i: (i, 4)),
            pl.BlockSpec((rows, POOL_WIDTH), lambda i: (i, 5)),
            pl.BlockSpec((rows, D_MODEL), lambda i: (i, 3)),
            pl.BlockSpec((rows, D_MODEL), lambda i: (i, 4)),
            halo_spec,
            pl.BlockSpec((None, len(POOL_WINDOWS), POOL_GROUP, POOL_GROUP), lambda i: (layer, 0, 0, 0)),
            pl.BlockSpec((None, 1, POOL_WIDTH), lambda i: (layer, 0, 0)),
            pl.BlockSpec((None, None, RWKV_WIDTH, D_MODEL), lambda i: (layer, 0, 0, 0)),
            pl.BlockSpec((None, None, POOL_WIDTH, D_MODEL), lambda i: (layer, 1, 0, 0)),
        ],
        out_specs=pl.BlockSpec((rows, D_MODEL), lambda i: (i, 0)),
        out_shape=jax.ShapeDtypeStruct((m, D_MODEL), BF16),
        name="mix",
        compiler_params=pltpu.CompilerParams(
            dimension_semantics=("parallel",), vmem_limit_bytes=VMEM_LIMIT_BYTES),
    )(ya, z, z, z, z, halo, pool_w_b, pool_scale.reshape(DEPTH, 1, POOL_WIDTH), w_branch_b, w_branch_b)


def _out_kernel(m_ref, x_ref, p_ref, g_ref, wo_ref, wg_ref, wp_ref, o_ref):
    out = jnp.dot(m_ref[...], wo_ref[...], preferred_element_type=F32)
    out = out * lax.rsqrt(jnp.mean(out * out, axis=-1, keepdims=True) + RMS_EPS) * g_ref[...]
    x = x_ref[...] + out
    gate = _sigmoid(jnp.dot(x.astype(BF16), wg_ref[...], preferred_element_type=F32))
    ple = jnp.dot(p_ref[...].astype(BF16), wp_ref[...], preferred_element_type=F32)
    o_ref[...] = x + gate * ple


def _out(mm_, x2d, p3d, norm_gain, w_out_b, w_gate_b, w_ple_b, layer, *, rows):
    m = x2d.shape[0]
    once = pl.Buffered(1)
    wspec = lambda k: pl.BlockSpec((None, k, D_MODEL), lambda i: (layer, 0, 0), pipeline_mode=once)
    return pl.pallas_call(
        _out_kernel,
        grid=(m // rows,),
        in_specs=[
            pl.BlockSpec((rows, D_MODEL), lambda i: (i, 0)),
            pl.BlockSpec((rows, D_MODEL), lambda i: (i, 0)),
            pl.BlockSpec((None, rows, PLE_DIM), lambda i: (layer, i, 0)),
            wspec(1), wspec(D_MODEL), wspec(D_MODEL), wspec(PLE_DIM),
        ],
        out_specs=pl.BlockSpec((rows, D_MODEL), lambda i: (i, 0)),
        out_shape=jax.ShapeDtypeStruct((m, D_MODEL), F32),
        name="out",
        compiler_params=pltpu.CompilerParams(
            dimension_semantics=("parallel",), vmem_limit_bytes=VMEM_LIMIT_BYTES),
    )(mm_, x2d, p3d, norm_gain.reshape(DEPTH, 1, D_MODEL), w_out_b, w_gate_b, w_ple_b)


def _pair_rows(s):
    l, b = s.shape[:2]
    return s.reshape(l, b, N_HEADS, HEAD_DIM // 2, 2 * HEAD_DIM)


def _unpair_rows(s):
    l, b = s.shape[:2]
    return s.reshape(l, b, N_HEADS, HEAD_DIM, HEAD_DIM)


def _layer_weights(i, mu_rkv, mu_lora, w0, w1, w2, a0, a1, a2, v0, v1, v2, k_k, k_a, r_k, ln_w, ln_b):
    r_ = RWKV_WIDTH
    mu = mu_lora[i]
    if i > 0:
        v1_i, v2_i, v0_i = v1[i - 1], v2[i - 1], v0[i - 1]
    else:
        v1_i = jnp.zeros((D_MODEL, MV_RANK), F32)
        v2_i = jnp.zeros((MV_RANK, r_), F32)
        v0_i = jnp.zeros((r_,), F32)
    w_lora = jnp.concatenate(
        [(1.0 - mu[0])[:, None] * w1[i], (1.0 - mu[1])[:, None] * a1[i], (1.0 - mu[2])[:, None] * v1_i,
         mu[0][:, None] * w1[i], mu[1][:, None] * a1[i], mu[2][:, None] * v1_i], axis=1).astype(BF16)
    z96 = jnp.zeros((DECAY_RANK, r_), F32)
    z64 = jnp.zeros((MV_RANK, r_), F32)
    w2e = jnp.concatenate([
        jnp.concatenate([w2[i], z96, z64], axis=0),
        jnp.concatenate([z96, a2[i], z64], axis=0),
        jnp.concatenate([z96, z96, v2_i], axis=0)], axis=1).astype(BF16)
    rows = [mu_rkv[i, 0], mu_rkv[i, 1], mu_rkv[i, 2], w0[i], a0[i], v0_i, k_k[i], k_a[i], r_k[i], ln_w[i], ln_b[i]]
    vecs = jnp.concatenate([jnp.stack(rows), jnp.zeros((_N_VEC_ROWS - len(rows), r_), F32)], axis=0)
    return w_lora, w2e, vecs


def _run_group(x, p, wkv, shift, pool, n_past, lw, params, *, n_seq, proj_tm, mix_seq, mix_t, out_rows):
    (w_in, norm_pre, norm_post, pool_w_b, pool_scale, w_branch_b, w_out_b, w_ple_b, w_gate_b) = params
    b, t, d = x.shape
    m = b * t
    x2d = x.reshape(m, d)
    p3d = p.reshape(DEPTH, m, PLE_DIM)
    zero_state = wkv is None
    shift_out, pool_out = [], []
    v_first = None
    tiles_per_seq = t // mix_t
    s_all = _pair_rows(jnp.zeros((DEPTH, b, N_HEADS, HEAD_DIM, HEAD_DIM), F32) if zero_state else wkv)
    for i in range(DEPTH):
        w_lora, w2e, vecs = lw[i]
        z = _proj(x2d, norm_pre, w_in, i, w_lora, normalize=True, tm=proj_tm, n_main=N_MAIN_TILES, out_dtype=BF16)
        shift_out.append(_norm_rows(x2d.reshape(b, t, d)[:, -1], norm_pre[i]))
        u_last = z.reshape(b, t, -1)[:, -min(t, POOL_BUF):, 4 * RWKV_WIDTH:4 * RWKV_WIDTH + POOL_WIDTH].astype(F32)
        if zero_state:
            assert t >= POOL_BUF
            prev0 = jnp.zeros((b // n_seq, n_seq, PREV_COLS), F32)
            halo = None
            pool_out.append(u_last)
        else:
            prev0 = _proj(shift[i], norm_pre, w_in, i, w_lora, normalize=False, tm=b,
                          n_main=3 * RWKV_WIDTH // PROJ_TN, out_dtype=F32).reshape(b // n_seq, n_seq, PREV_COLS)
            assert tiles_per_seq == 1 and t < POOL_BUF
            halo = jnp.pad(pool[i], ((0, 0), (HALO - POOL_BUF, 0), (0, 0)))
            pool_out.append(jnp.concatenate([pool[i][:, t:], u_last], axis=1))
        ya, v_first, s_all = _wkv(z, prev0, v_first, w2e, vecs, s_all, i, n_batch=b, t_len=t, n_seq=n_seq)
        mm_ = _mix(ya, z, halo, pool_w_b, pool_scale, w_branch_b, i,
                   n_seq=mix_seq, t_seq=mix_t, tiles_per_seq=tiles_per_seq, n_past=n_past)
        x2d = _out(mm_, x2d, p3d, norm_post, w_out_b, w_gate_b, w_ple_b, i, rows=out_rows)
    return x2d.reshape(b, t, d), _unpair_rows(s_all), jnp.stack(shift_out), jnp.stack(pool_out)


def kernel(x_prompt, x_sample, state_wkv, state_shift, state_pool, p_prompt, p_sample, norm_pre, norm_post, w_in, mu_rkv, mu_lora, w0, w1, w2, a0, a1, a2, v0, v1, v2, k_k, k_a, r_k, ln_w, ln_b, pool_w, pool_scale, w_branch, w_out, w_ple, w_ple_gate):
    lw = [_layer_weights(i, mu_rkv, mu_lora, w0, w1, w2, a0, a1, a2, v0, v1, v2, k_k, k_a, r_k, ln_w, ln_b)
          for i in range(DEPTH)]
    params = (w_in, norm_pre, norm_post, pool_w.astype(BF16), pool_scale, w_branch.astype(BF16),
              w_out.astype(BF16), w_ple.astype(BF16), w_ple_gate.astype(BF16))
    y_p, wkv_p, shift_p, pool_p = _run_group(
        x_prompt, p_prompt, None, None, None, 0, lw, params,
        n_seq=1, proj_tm=2048, mix_seq=1, mix_t=256, out_rows=256)
    y_s, wkv_s, shift_s, pool_s = _run_group(
        x_sample, p_sample, state_wkv, state_shift, state_pool, PAST_LEN, lw, params,
        n_seq=8, proj_tm=1024, mix_seq=32, mix_t=8, out_rows=256)
    return (y_p, y_s, wkv_p, shift_p, pool_p, wkv_s, shift_s, pool_s)
```

```python
import functools

import jax
import jax.numpy as jnp
import numpy as np
from jax import lax
from jax.experimental import pallas as pl
from jax.experimental.pallas import tpu as pltpu

F32 = jnp.float32
BF16 = jnp.bfloat16

D_MODEL = 2048
DEPTH = 4
PAST_LEN = 16384
RWKV_WIDTH = 1024
HEAD_DIM = 64
N_HEADS = 16
POOL_WIDTH = 1024
POOL_WINDOWS = (2, 4, 8, 16)
POOL_GROUP = 256
POOL_BUF = 15
PLE_DIM = 256
DECAY_RANK = 96
AAA_RANK = 96
MV_RANK = 64
LORA_COLS = DECAY_RANK + AAA_RANK + MV_RANK
IN_COLS = 10240
RMS_EPS = 1e-6
GN_EPS = 64e-5
DECAY_SCALE = 0.606531

HEADS_PER_GROUP = 4
GROUP_LANES = HEADS_PER_GROUP * HEAD_DIM
N_GROUPS = N_HEADS // HEADS_PER_GROUP
CHUNK = 64
EXPANDED = HEADS_PER_GROUP * CHUNK
BF16_ROWS = 16
HALO = 16
PROJ_TN = 2 * LORA_COLS
N_MAIN_TILES = IN_COLS // PROJ_TN
PREV_COLS = 3 * RWKV_WIDTH + PROJ_TN
LB_OFF = 3 * RWKV_WIDTH + LORA_COLS

VMEM_LIMIT_BYTES = 56 * 1024 * 1024


def _sigmoid(x):
    return 0.5 * jnp.tanh(0.5 * x) + 0.5


def _split_bf16(x):
    hi = x.astype(BF16)
    lo = (x - hi.astype(F32)).astype(BF16)
    return hi, lo


_NN = (((1,), (0,)), ((), ()))
_NT = (((1,), (1,)), ((), ()))


def _dot(a, b, dims=_NN):
    return lax.dot_general(a, b, dimension_numbers=dims, preferred_element_type=F32)


def _segsum(x, bones_b):
    return _dot(x.astype(BF16), bones_b)


def _proj_kernel(x_ref, g_ref, w_ref, wl_ref, z_ref, hb_ref, *, normalize, n_main):
    j = pl.program_id(1)

    @pl.when(j == 0)
    def _():
        x = x_ref[...]
        if normalize:
            x = x * lax.rsqrt(jnp.mean(x * x, axis=-1, keepdims=True) + RMS_EPS) * g_ref[...]
        hb_ref[...] = x.astype(BF16)

    @pl.when(j < n_main)
    def _():
        z_ref[...] = jnp.dot(hb_ref[...], w_ref[...].astype(BF16), preferred_element_type=F32).astype(z_ref.dtype)

    @pl.when(j == n_main)
    def _():
        z_ref[...] = jnp.dot(hb_ref[...], wl_ref[...], preferred_element_type=F32).astype(z_ref.dtype)


def _proj(x2d, norm_gain, w_in, layer, w_lora, *, normalize, tm, n_main, out_dtype):
    m, d = x2d.shape
    tn = PROJ_TN
    assert m % tm == 0
    once = pl.Buffered(1)
    return pl.pallas_call(
        functools.partial(_proj_kernel, normalize=normalize, n_main=n_main),
        grid=(m // tm, n_main + 1),
        in_specs=[
            pl.BlockSpec((tm, d), lambda i, j: (i, 0), pipeline_mode=once),
            pl.BlockSpec((None, 1, d), lambda i, j: (layer, 0, 0)),
            pl.BlockSpec((None, d, tn), lambda i, j: (layer, 0, jnp.minimum(j, n_main - 1))),
            pl.BlockSpec((d, tn), lambda i, j: (0, 0), pipeline_mode=once),
        ],
        out_specs=pl.BlockSpec((tm, tn), lambda i, j: (i, j)),
        out_shape=jax.ShapeDtypeStruct((m, (n_main + 1) * tn), out_dtype),
        scratch_shapes=[pltpu.VMEM((tm, d), BF16)],
        name="proj",
        compiler_params=pltpu.CompilerParams(
            dimension_semantics=("parallel", "arbitrary"), vmem_limit_bytes=VMEM_LIMIT_BYTES),
    )(x2d, norm_gain.reshape(DEPTH, 1, d), w_in, w_lora)


def _norm_rows_kernel(x_ref, g_ref, o_ref):
    x = x_ref[...]
    o_ref[...] = x * lax.rsqrt(jnp.mean(x * x, axis=-1, keepdims=True) + RMS_EPS) * g_ref[...]


def _norm_rows(x2d, gain):
    return pl.pallas_call(
        _norm_rows_kernel, out_shape=jax.ShapeDtypeStruct(x2d.shape, F32), name="norm_rows",
    )(x2d, gain.reshape(1, -1))


_V_MU_R, _V_MU_K, _V_MU_V, _V_W0, _V_A0, _V_V0, _V_KK, _V_KA, _V_RK, _V_LNW, _V_LNB = range(11)
_N_VEC_ROWS = 16


def _wkv_constants(n_seq, t_seq):
    c, e = CHUNK, EXPANDED
    assert n_seq * t_seq == c
    row = np.arange(c)
    q_of = row // t_seq
    tri = (row[:, None] >= row[None, :]) & (q_of[:, None] == q_of[None, :])
    erow = np.arange(e)
    es = erow % c
    same = q_of[:, None] == (es // t_seq)[None, :]
    mask_s = same & (row[:, None] > es[None, :])
    mask_i = same & (row[:, None] >= es[None, :])
    mask_sc = np.block([[mask_s, mask_s], [mask_i, mask_i]])
    lane = np.arange(GROUP_LANES)
    bones = (lane[:, None] // HEAD_DIM) == (lane[None, :] // HEAD_DIM)
    levels = [np.eye(e, dtype=bool), (erow[:, None] // 2) == (erow[None, :] // 2)]
    m = 2
    while m < t_seq:
        levels.append(((erow[:, None] // (2 * m)) == (erow[None, :] // (2 * m)))
                      & ((erow[:, None] % (2 * m)) >= m) & ((erow[None, :] % (2 * m)) < m))
        m *= 2
    col = np.arange(2 * c)
    qmask = ((col[None, :] % c) // t_seq) == np.arange(n_seq)[:, None]
    f = lambda a: jnp.asarray(a.astype(np.float32))
    return dict(tri=f(tri).astype(BF16), mask_sc=f(mask_sc).astype(BF16), bones=f(bones),
                bones_b=f(bones).astype(BF16), levels=f(np.stack(levels)).astype(BF16),
                qmask=f(qmask).reshape(n_seq, 1, 2 * c))


def _wkv_kernel(*refs, n_seq, t_seq, n_sub, chain, n_steps, has_vres, n_levels):
    it = iter(refs)
    zr_ref, zk_ref, zv_ref, zg_ref, zl_ref, p0_ref = (next(it) for _ in range(6))
    vfirst_ref = next(it) if has_vres else None
    w2_ref, vec_ref, s0_ref = next(it), next(it), next(it)
    tri_ref, msc_ref, bones_ref, bonesb_ref, lev_ref, qmask_ref = (next(it) for _ in range(6))
    ya_ref = next(it)
    vout_ref = None if has_vres else next(it)
    sout_ref = next(it)
    sbd_ref, prev_ref = next(it), next(it)

    step = pl.program_id(1)
    gl, rw, e = GROUP_LANES, RWKV_WIDTH, EXPANDED
    n_slots = n_seq if chain else n_sub * n_seq
    slot = (lambda j, q: q) if chain else (lambda j, q: j * n_seq + q)
    bones = bones_ref[...]
    bones_b = bonesb_ref[...]
    mask_sc = msc_ref[...]
    eye_b = lev_ref[0]
    tri = tri_ref[...]
    vec = vec_ref[...]
    vrow = lambda i: vec[i:i + 1, :]
    groups = range(N_GROUPS)
    sls = [slice(gi * gl, (gi + 1) * gl) for gi in groups]

    def expand(x):
        return jnp.concatenate([x] * HEADS_PER_GROUP, axis=0) * bones_b

    low_lanes = lax.broadcasted_iota(jnp.int32, (1, 2 * HEAD_DIM), 1) < HEAD_DIM

    def head_slot(hd):
        h = hd % HEADS_PER_GROUP
        return hd // HEADS_PER_GROUP, h * HEAD_DIM, h // 2, h % 2 == 1

    def load_state(s, g):
        return jnp.concatenate([sbd_ref[s, g, 0], sbd_ref[s, g, 1]], axis=1)

    def store_state(s, g, val):
        sbd_ref[s, g, 0] = val[:, :2 * HEAD_DIM]
        sbd_ref[s, g, 1] = val[:, 2 * HEAD_DIM:]

    @pl.when(step == 0)
    def _():
        for s in range(n_slots):
            for hd in range(N_HEADS):
                g, row0, half, upper = head_slot(hd)
                packed = s0_ref[s, hd]
                even = jnp.where(low_lanes, packed, 0.0)
                odd = jnp.where(low_lanes, 0.0, packed)
                if upper:
                    even = pltpu.roll(even, HEAD_DIM, axis=1)
                else:
                    odd = pltpu.roll(odd, HEAD_DIM, axis=1)
                sbd_ref[s, g, half, pl.ds(row0, HEAD_DIM // 2, stride=2), :] = even
                sbd_ref[s, g, half, pl.ds(row0 + 1, HEAD_DIM // 2, stride=2), :] = odd
                sbd_ref[s, g, 1 - half, row0:row0 + HEAD_DIM, :] = jnp.zeros((HEAD_DIM, 2 * HEAD_DIM), F32)
        prev_ref[...] = p0_ref[0]

    row = lax.broadcasted_iota(jnp.int32, (CHUNK, 1), 0)
    is_first = (row % t_seq) == 0
    lora_lane = lax.broadcasted_iota(jnp.int32, (1, LORA_COLS), 1)

    def shifted(z, p0):
        rolled = pltpu.roll(z, 1, axis=0)
        if n_seq == 1:
            first = jnp.broadcast_to(p0, z.shape)
        else:
            first = jnp.broadcast_to(p0[:, None, :], (n_seq, t_seq, z.shape[1])).reshape(z.shape)
        return jnp.where(is_first, first, rolled)

    ctx = [dict() for _ in range(n_sub)]

    def prep(j):
        c = ctx[j]
        rows = slice(j * CHUNK, (j + 1) * CHUNK)
        zr, zk, zv = zr_ref[rows, :].astype(F32), zk_ref[rows, :].astype(F32), zv_ref[rows, :].astype(F32)
        zl = zl_ref[rows, :].astype(F32)
        la, lb = zl[:, :LORA_COLS], zl[:, LORA_COLS:]
        c["last"] = [x[CHUNK - 1:CHUNK, :] for x in (zr, zk, zv, lb)]
        if chain and j > 0:
            p_r, p_k, p_v, p_l = ctx[j - 1]["last"]
        else:
            ps = slice(0, n_seq) if chain else slice(j * n_seq, (j + 1) * n_seq)
            p_r, p_k, p_v = prev_ref[ps, 0:rw], prev_ref[ps, rw:2 * rw], prev_ref[ps, 2 * rw:3 * rw]
            p_l = prev_ref[ps, LB_OFF:LB_OFF + LORA_COLS]
        r = zr + (shifted(zr, p_r) - zr) * vrow(_V_MU_R)
        k = zk + (shifted(zk, p_k) - zk) * vrow(_V_MU_K)
        v = zv + (shifted(zv, p_v) - zv) * vrow(_V_MU_V)
        lin = la + shifted(lb, p_l)
        lin = jnp.where(lora_lane < DECAY_RANK, jnp.tanh(lin), lin).astype(BF16)
        d_all = jnp.dot(lin, w2_ref[...], preferred_element_type=F32)
        wlog = -DECAY_SCALE * _sigmoid(vrow(_V_W0) + d_all[:, 0:rw])
        alr = _sigmoid(vrow(_V_A0) + d_all[:, rw:2 * rw])
        if has_vres:
            v = v + (vfirst_ref[rows, :] - v) * _sigmoid(vrow(_V_V0) + d_all[:, 2 * rw:3 * rw])
        else:
            vout_ref[rows, :] = v
        kkr = k * vrow(_V_KK)
        kf = k * (1.0 + (alr - 1.0) * vrow(_V_KA))
        c["rkr"] = r * kf * vrow(_V_RK)
        g_all = zg_ref[rows, :].astype(F32)
        c["gate"] = g_all * _sigmoid(g_all)
        w_hi, w_lo = _split_bf16(wlog)
        cum = jnp.dot(tri, w_hi, preferred_element_type=F32) + jnp.dot(tri, w_lo, preferred_element_type=F32)
        p_inc = jnp.exp(cum)
        inv_p = jnp.exp(-cum)
        p_exc = jnp.exp(cum - wlog)
        c["p_inc"] = p_inc
        yield
        kk_n2 = [_segsum(kkr[:, sl] * kkr[:, sl], bones_b) for sl in sls]
        kk = [kkr[:, sl] * lax.rsqrt(jnp.maximum(n2, 1e-24)) for sl, n2 in zip(sls, kk_n2)]
        c["v_g"] = [v[:, sl] for sl in sls]
        c["a_f"] = [-kk[gi] * p_exc[:, sls[gi]] for gi in groups]
        c["r_f"] = [r[:, sl] * p_inc[:, sl] for sl in sls]
        a_t = [x.astype(BF16) for x in c["a_f"]]
        r_t = [x.astype(BF16) for x in c["r_f"]]
        c["b_t"] = [(kk[gi] * alr[:, sls[gi]] * inv_p[:, sls[gi]]).astype(BF16) for gi in groups]
        c["k_t"] = [(kf[:, sl] * inv_p[:, sl]).astype(BF16) for sl in sls]
        c["v_e"] = [expand(x.astype(BF16)) for x in c["v_g"]]
        c["ar"] = [jnp.concatenate([a_t[gi], r_t[gi]], axis=0) for gi in groups]
        yield
        sc = [_dot(c["ar"][gi], jnp.concatenate([expand(c["b_t"][gi]), expand(c["k_t"][gi])], axis=0), _NT)
              for gi in groups]
        c["scm"] = [x.astype(BF16) * mask_sc for x in sc]
        c["n_bd"] = [expand(x[:CHUNK, :e]) for x in c["scm"]]
        c["xs"] = [eye_b + n * lev_ref[1] for n in c["n_bd"]]
        yield

    def inverse(j):
        c = ctx[j]
        m = 2
        for lvl in range(2, n_levels):
            n_c = [c["n_bd"][gi] * lev_ref[lvl] for gi in groups]
            if m % BF16_ROWS == 0:
                lower = lambda x: jnp.concatenate(
                    [x[b * 2 * m + m:(b + 1) * 2 * m] for b in range(e // (2 * m))], axis=0)
                eye_l = lower(eye_b)
                t1 = [_dot(lower(c["xs"][gi]), n_c[gi]).astype(BF16) + eye_l for gi in groups]
                yield
                new = [_dot(t1[gi], c["xs"][gi]).astype(BF16) for gi in groups]
                c["xs"] = [jnp.concatenate(
                    [piece for b in range(e // (2 * m))
                     for piece in (x[b * 2 * m:b * 2 * m + m], n[b * m:(b + 1) * m])], axis=0)
                    for x, n in zip(c["xs"], new)]
            else:
                t1 = [_dot(c["xs"][gi], n_c[gi]).astype(BF16) + eye_b for gi in groups]
                yield
                c["xs"] = [_dot(t1[gi], c["xs"][gi]).astype(BF16) for gi in groups]
            yield
            m *= 2

    def tail(j):
        c = ctx[j]
        rows = slice(j * CHUNK, (j + 1) * CHUNK)
        x_c = [(x[0:CHUNK] + x[CHUNK:2 * CHUNK]) + (x[2 * CHUNK:3 * CHUNK] + x[3 * CHUNK:4 * CHUNK])
               for x in c["xs"]]
        a_s, r_s = [], []
        for gi in groups:
            parts_a, parts_r = [], []
            for q in range(n_seq):
                lo_, hi_ = q * t_seq, (q + 1) * t_seq
                lhs = c["ar"][gi] if n_seq == 1 else jnp.concatenate(
                    [c["a_f"][gi][lo_:hi_], c["r_f"][gi][lo_:hi_]], axis=0).astype(BF16)
                res = _dot(lhs, load_state(slot(j, q), gi).astype(BF16), _NT)
                parts_a.append(res[:t_seq])
                parts_r.append(res[t_seq:])
            a_s.append(parts_a[0] if n_seq == 1 else jnp.concatenate(parts_a, axis=0))
            r_s.append(parts_r[0] if n_seq == 1 else jnp.concatenate(parts_r, axis=0))
        yield
        y = [a_s[gi] + _dot(c["scm"][gi][:CHUNK, e:], c["v_e"][gi]) for gi in groups]
        yield
        u = [_dot(x_c[gi], expand(y[gi].astype(BF16))) for gi in groups]
        yield
        o = [r_s[gi] + _dot(c["scm"][gi][CHUNK:, :],
                            jnp.concatenate([expand(u[gi].astype(BF16)), c["v_e"][gi]], axis=0)) for gi in groups]
        yield
        for gi in groups:
            uv_t = jnp.concatenate([u[gi], c["v_g"][gi]], axis=0).T
            bk = jnp.concatenate([c["b_t"][gi], c["k_t"][gi]], axis=0)
            for q in range(n_seq):
                lhs = uv_t if n_seq == 1 else uv_t * qmask_ref[q]
                p_last = c["p_inc"][(q + 1) * t_seq - 1:(q + 1) * t_seq, sls[gi]]
                s = slot(j, q)
                store_state(s, gi, bones * ((load_state(s, gi) + _dot(lhs.astype(BF16), bk)) * p_last))
        yield
        mean = [_segsum(x, bones_b) * (1.0 / HEAD_DIM) for x in o]
        d = [o[gi] - mean[gi] for gi in groups]
        var = [_segsum(x * x, bones_b) * (1.0 / HEAD_DIM) for x in d]
        bonus = [_segsum(c["rkr"][:, sl], bones_b) for sl in sls]
        yield
        for gi in groups:
            sl = sls[gi]
            on = d[gi] * lax.rsqrt(var[gi] + GN_EPS) * vec[_V_LNW:_V_LNW + 1, sl] + vec[_V_LNB:_V_LNB + 1, sl]
            ya_ref[rows, sl] = ((on + bonus[gi] * c["v_g"][gi]) * c["gate"][:, sl]).astype(ya_ref.dtype)
        c.clear()
        c["last"] = None

    last_rows = None
    preps, invs, tails = ([f(j) for j in range(n_sub)] for f in (prep, inverse, tail))
    for phase in range(n_sub + 2):
        active = []
        if 0 <= phase - 2 < n_sub:
            active.append(tails[phase - 2])
        if 0 <= phase - 1 < n_sub:
            active.append(invs[phase - 1])
        if phase < n_sub:
            active.append(preps[phase])
        if phase == n_sub - 1 + 1:
            last_rows = ctx[n_sub - 1]["last"]
        while active:
            for gen in list(active):
                try:
                    next(gen)
                except StopIteration:
                    active.remove(gen)

    if chain and n_steps > 1:
        assert n_seq == 1
        prev_ref[:, 0:rw] = last_rows[0]
        prev_ref[:, rw:2 * rw] = last_rows[1]
        prev_ref[:, 2 * rw:3 * rw] = last_rows[2]
        prev_ref[:, LB_OFF:LB_OFF + LORA_COLS] = last_rows[3]

    @pl.when(step == n_steps - 1)
    def _():
        for s in range(n_slots):
            for hd in range(N_HEADS):
                g, row0, half, upper = head_slot(hd)
                even = sbd_ref[s, g, half, pl.ds(row0, HEAD_DIM // 2, stride=2), :]
                odd = sbd_ref[s, g, half, pl.ds(row0 + 1, HEAD_DIM // 2, stride=2), :]
                if upper:
                    even = pltpu.roll(even, HEAD_DIM, axis=1)
                else:
                    odd = pltpu.roll(odd, HEAD_DIM, axis=1)
                sout_ref[s, hd] = jnp.where(low_lanes, even, odd)


def _wkv(z, prev0, vfirst, w2e, vecs, s_all, layer, *, n_batch, t_len, n_seq, n_sub):
    m = z.shape[0]
    t_seq = CHUNK // n_seq
    n_chunks = t_len // t_seq
    chain = n_seq == 1
    if chain:
        assert n_chunks % n_sub == 0
        n_steps, nb, n_slots = n_chunks // n_sub, n_batch, n_seq
    else:
        assert n_chunks == 1 and n_batch % (n_seq * n_sub) == 0
        n_steps, nb, n_slots = 1, n_batch // (n_seq * n_sub), n_seq * n_sub
    has_vres = vfirst is not None
    consts = _wkv_constants(n_seq, t_seq)
    n_levels = consts["levels"].shape[0]
    rw = RWKV_WIDTH
    rows = n_sub * CHUNK
    rb = lambda b, s: b * n_steps + s
    prev0 = prev0.reshape(nb, n_slots, PREV_COLS)

    zspec = lambda col: pl.BlockSpec((rows, rw), lambda b, s: (rb(b, s), col))
    full = lambda a: pl.BlockSpec(a.shape, lambda b, s: (0,) * a.ndim)
    sspec = pl.BlockSpec((None, n_slots, N_HEADS, HEAD_DIM // 2, 2 * HEAD_DIM), lambda b, s: (layer, b, 0, 0, 0))

    in_specs = [zspec(0), zspec(1), zspec(2), zspec(3),
                pl.BlockSpec((rows, PROJ_TN), lambda b, s: (rb(b, s), N_MAIN_TILES)),
                pl.BlockSpec((1, n_slots, PREV_COLS), lambda b, s: (b, 0, 0))]
    args = [z, z, z, z, z, prev0]
    if has_vres:
        in_specs.append(zspec(0))
        args.append(vfirst)
    else:
        w2e = w2e[:, :2 * rw]
    in_specs += [full(w2e), full(vecs), sspec]
    args += [w2e, vecs, s_all]
    state_arg = len(args) - 1
    for name in ("tri", "mask_sc", "bones", "bones_b", "levels", "qmask"):
        in_specs.append(full(consts[name]))
        args.append(consts[name])

    out_specs = [zspec(0)]
    out_shape = [jax.ShapeDtypeStruct((m, rw), BF16)]
    if not has_vres:
        out_specs.append(zspec(0))
        out_shape.append(jax.ShapeDtypeStruct((m, rw), F32))
    out_specs.append(sspec)
    out_shape.append(jax.ShapeDtypeStruct(s_all.shape, F32))

    outs = pl.pallas_call(
        functools.partial(_wkv_kernel, n_seq=n_seq, t_seq=t_seq, n_sub=n_sub, chain=chain, n_steps=n_steps,
                          has_vres=has_vres, n_levels=n_levels),
        grid=(nb, n_steps),
        in_specs=in_specs, out_specs=out_specs, out_shape=out_shape,
        input_output_aliases={state_arg: len(out_shape) - 1},
        scratch_shapes=[pltpu.VMEM((n_slots, N_GROUPS, 2, GROUP_LANES, GROUP_LANES // 2), F32),
                        pltpu.VMEM((n_slots, PREV_COLS), F32)],
        name="wkv",
        compiler_params=pltpu.CompilerParams(
            dimension_semantics=("parallel", "arbitrary"), vmem_limit_bytes=VMEM_LIMIT_BYTES),
    )(*args)
    if has_vres:
        ya, s_new = outs
        return ya, vfirst, s_new
    return outs


def _mix_kernel(ya_ref, u_ref, gp_ref, ga_ref, gb_ref, halo_ref, pw_ref, ps_ref, wb0_ref, wb1_ref, m_ref,
                *, n_seq, t_seq, tiles_per_seq, n_past, halo_from_z):
    rows = n_seq * t_seq
    u = u_ref[...].astype(F32)
    tile = pl.program_id(0) % tiles_per_seq
    if halo_from_z:
        halo = jnp.where(tile == 0, 0.0, halo_ref[...].astype(F32)).reshape(1, HALO, POOL_WIDTH)
    else:
        halo = halo_ref[...]
    e = jnp.concatenate([halo, u.reshape(n_seq, t_seq, POOL_WIDTH)], axis=1)
    pos = lax.broadcasted_iota(jnp.int32, (1, t_seq, 1), 1) + (tile * t_seq + 1 + n_past)
    yb_parts = []
    for gi, win in enumerate(POOL_WINDOWS):
        sl = slice(gi * POOL_GROUP, (gi + 1) * POOL_GROUP)
        acc = e[:, :, sl]
        span, length = 1, HALO + t_seq
        while span < win:
            length -= span
            acc = acc[:, span:span + length] + acc[:, :length]
            span *= 2
        ws = acc[:, length - t_seq:]
        cnt = jnp.minimum(pos, win).astype(F32)
        mixed = (ws / cnt - e[:, HALO:, sl]).reshape(rows, POOL_GROUP)
        yb_parts.append(jnp.dot(mixed.astype(BF16), pw_ref[gi], preferred_element_type=F32))
    gp = gp_ref[...].astype(F32)
    y_b = jnp.concatenate(yb_parts, axis=1) * ps_ref[...] * (gp * _sigmoid(gp))
    pa = jnp.dot(ya_ref[...], wb0_ref[...], preferred_element_type=F32)
    pb = jnp.dot(y_b.astype(BF16), wb1_ref[...], preferred_element_type=F32)
    m_ref[...] = (_sigmoid(ga_ref[...].astype(F32)) * pa + _sigmoid(gb_ref[...].astype(F32)) * pb).astype(m_ref.dtype)


def _mix(ya, z, halo, pool_w_b, pool_scale, w_branch_b, layer, *, n_seq, t_seq, tiles_per_seq, n_past):
    m = z.shape[0]
    rows = n_seq * t_seq
    halo_from_z = halo is None
    if halo_from_z:
        assert n_seq == 1 and rows % HALO == 0
        halo = z
        halo_spec = pl.BlockSpec((HALO, POOL_WIDTH), lambda i: (jnp.maximum(i * (rows // HALO) - 1, 0), 4))
    else:
        halo_spec = pl.BlockSpec((n_seq, HALO, POOL_WIDTH), lambda i: (i, 0, 0))
    return pl.pallas_call(
        functools.partial(_mix_kernel, n_seq=n_seq, t_seq=t_seq, tiles_per_seq=tiles_per_seq, n_past=n_past,
                          halo_from_z=halo_from_z),
        grid=(m // rows,),
        in_specs=[
            pl.BlockSpec((rows, RWKV_WIDTH), lambda i: (i, 0)),
            pl.BlockSpec((rows, POOL_WIDTH), lambda i: (i, 4)),
            pl.BlockSpec((rows, POOL_WIDTH), lambda i: (i, 5)),
            pl.BlockSpec((rows, D_MODEL), lambda i: (i, 3)),
            pl.BlockSpec((rows, D_MODEL), lambda i: (i, 4)),
            halo_spec,
            pl.BlockSpec((None, len(POOL_WINDOWS), POOL_GROUP, POOL_GROUP), lambda i: (layer, 0, 0, 0)),
            pl.BlockSpec((None, 1, POOL_WIDTH), lambda i: (layer, 0, 0)),
            pl.BlockSpec((None, None, RWKV_WIDTH, D_MODEL), lambda i: (layer, 0, 0, 0)),
            pl.BlockSpec((None, None, POOL_WIDTH, D_MODEL), lambda i: (layer, 1, 0, 0)),
        ],
        out_specs=pl.BlockSpec((rows, D_MODEL), lambda i: (i, 0)),
        out_shape=jax.ShapeDtypeStruct((m, D_MODEL), BF16),
        name="mix",
        compiler_params=pltpu.CompilerParams(
            dimension_semantics=("parallel",), vmem_limit_bytes=VMEM_LIMIT_BYTES),
    )(ya, z, z, z, z, halo, pool_w_b, pool_scale.reshape(DEPTH, 1, POOL_WIDTH), w_branch_b, w_branch_b)


def _out_kernel(m_ref, x_ref, p_ref, g_ref, wo_ref, wg_ref, wp_ref, o_ref, *, n_split):
    rows = m_ref.shape[0] // n_split
    parts = [slice(i * rows, (i + 1) * rows) for i in range(n_split)]
    outs = [jnp.dot(m_ref[sl, :], wo_ref[...], preferred_element_type=F32) for sl in parts]
    ples = [jnp.dot(p_ref[sl, :].astype(BF16), wp_ref[...], preferred_element_type=F32) for sl in parts]
    xs = [x_ref[sl, :] + o * lax.rsqrt(jnp.mean(o * o, axis=-1, keepdims=True) + RMS_EPS) * g_ref[...]
          for sl, o in zip(parts, outs)]
    gates = [_sigmoid(jnp.dot(x.astype(BF16), wg_ref[...], preferred_element_type=F32)) for x in xs]
    for sl, x, gate, ple in zip(parts, xs, gates, ples):
        o_ref[sl, :] = x + gate * ple


def _out(mm_, x2d, p3d, norm_gain, w_out_b, w_gate_b, w_ple_b, layer, *, rows):
    m = x2d.shape[0]
    once = pl.Buffered(1)
    wspec = lambda k: pl.BlockSpec((None, k, D_MODEL), lambda i: (layer, 0, 0), pipeline_mode=once)
    return pl.pallas_call(
        functools.partial(_out_kernel, n_split=2),
        grid=(m // rows,),
        in_specs=[
            pl.BlockSpec((rows, D_MODEL), lambda i: (i, 0)),
            pl.BlockSpec((rows, D_MODEL), lambda i: (i, 0)),
            pl.BlockSpec((None, rows, PLE_DIM), lambda i: (layer, i, 0)),
            wspec(1), wspec(D_MODEL), wspec(D_MODEL), wspec(PLE_DIM),
        ],
        out_specs=pl.BlockSpec((rows, D_MODEL), lambda i: (i, 0)),
        out_shape=jax.ShapeDtypeStruct((m, D_MODEL), F32),
        name="out",
        compiler_params=pltpu.CompilerParams(
            dimension_semantics=("parallel",), vmem_limit_bytes=VMEM_LIMIT_BYTES),
    )(mm_, x2d, p3d, norm_gain.reshape(DEPTH, 1, D_MODEL), w_out_b, w_gate_b, w_ple_b)


def _pair_rows(s):
    l, b = s.shape[:2]
    return s.reshape(l, b, N_HEADS, HEAD_DIM // 2, 2 * HEAD_DIM)


def _unpair_rows(s):
    l, b = s.shape[:2]
    return s.reshape(l, b, N_HEADS, HEAD_DIM, HEAD_DIM)


def _layer_weights(i, mu_rkv, mu_lora, w0, w1, w2, a0, a1, a2, v0, v1, v2, k_k, k_a, r_k, ln_w, ln_b):
    r_ = RWKV_WIDTH
    mu = mu_lora[i]
    if i > 0:
        v1_i, v2_i, v0_i = v1[i - 1], v2[i - 1], v0[i - 1]
    else:
        v1_i = jnp.zeros((D_MODEL, MV_RANK), F32)
        v2_i = jnp.zeros((MV_RANK, r_), F32)
        v0_i = jnp.zeros((r_,), F32)
    w_lora = jnp.concatenate(
        [(1.0 - mu[0])[:, None] * w1[i], (1.0 - mu[1])[:, None] * a1[i], (1.0 - mu[2])[:, None] * v1_i,
         mu[0][:, None] * w1[i], mu[1][:, None] * a1[i], mu[2][:, None] * v1_i], axis=1).astype(BF16)
    z96 = jnp.zeros((DECAY_RANK, r_), F32)
    z64 = jnp.zeros((MV_RANK, r_), F32)
    w2e = jnp.concatenate([
        jnp.concatenate([w2[i], z96, z64], axis=0),
        jnp.concatenate([z96, a2[i], z64], axis=0),
        jnp.concatenate([z96, z96, v2_i], axis=0)], axis=1).astype(BF16)
    rows = [mu_rkv[i, 0], mu_rkv[i, 1], mu_rkv[i, 2], w0[i], a0[i], v0_i, k_k[i], k_a[i], r_k[i], ln_w[i], ln_b[i]]
    vecs = jnp.concatenate([jnp.stack(rows), jnp.zeros((_N_VEC_ROWS - len(rows), r_), F32)], axis=0)
    return w_lora, w2e, vecs


def _run_group(x, p, wkv, shift, pool, n_past, lw, params, *, n_seq, n_sub, proj_tm, mix_seq, mix_t, out_rows):
    (w_in, norm_pre, norm_post, pool_w_b, pool_scale, w_branch_b, w_out_b, w_ple_b, w_gate_b) = params
    b, t, d = x.shape
    m = b * t
    x2d = x.reshape(m, d)
    p3d = p.reshape(DEPTH, m, PLE_DIM)
    zero_state = wkv is None
    shift_out, pool_out = [], []
    v_first = None
    tiles_per_seq = t // mix_t
    s_all = _pair_rows(jnp.zeros((DEPTH, b, N_HEADS, HEAD_DIM, HEAD_DIM), F32) if zero_state else wkv)
    for i in range(DEPTH):
        w_lora, w2e, vecs = lw[i]
        z = _proj(x2d, norm_pre, w_in, i, w_lora, normalize=True, tm=proj_tm, n_main=N_MAIN_TILES, out_dtype=BF16)
        shift_out.append(_norm_rows(x2d.reshape(b, t, d)[:, -1], norm_pre[i]))
        u_last = z.reshape(b, t, -1)[:, -min(t, POOL_BUF):, 4 * RWKV_WIDTH:4 * RWKV_WIDTH + POOL_WIDTH].astype(F32)
        if zero_state:
            assert t >= POOL_BUF
            prev0 = jnp.zeros((b, PREV_COLS), F32)
            halo = None
            pool_out.append(u_last)
        else:
            prev0 = _proj(shift[i], norm_pre, w_in, i, w_lora, normalize=False, tm=b,
                          n_main=3 * RWKV_WIDTH // PROJ_TN, out_dtype=F32)
            assert tiles_per_seq == 1 and t < POOL_BUF
            halo = jnp.pad(pool[i], ((0, 0), (HALO - POOL_BUF, 0), (0, 0)))
            pool_out.append(jnp.concatenate([pool[i][:, t:], u_last], axis=1))
        ya, v_first, s_all = _wkv(z, prev0, v_first, w2e, vecs, s_all, i, n_batch=b, t_len=t, n_seq=n_seq,
                                  n_sub=n_sub)
        mm_ = _mix(ya, z, halo, pool_w_b, pool_scale, w_branch_b, i,
                   n_seq=mix_seq, t_seq=mix_t, tiles_per_seq=tiles_per_seq, n_past=n_past)
        x2d = _out(mm_, x2d, p3d, norm_post, w_out_b, w_gate_b, w_ple_b, i, rows=out_rows)
    return x2d.reshape(b, t, d), _unpair_rows(s_all), jnp.stack(shift_out), jnp.stack(pool_out)


def kernel(x_prompt, x_sample, state_wkv, state_shift, state_pool, p_prompt, p_sample, norm_pre, norm_post, w_in, mu_rkv, mu_lora, w0, w1, w2, a0, a1, a2, v0, v1, v2, k_k, k_a, r_k, ln_w, ln_b, pool_w, pool_scale, w_branch, w_out, w_ple, w_ple_gate):
    lw = [_layer_weights(i, mu_rkv, mu_lora, w0, w1, w2, a0, a1, a2, v0, v1, v2, k_k, k_a, r_k, ln_w, ln_b)
          for i in range(DEPTH)]
    params = (w_in, norm_pre, norm_post, pool_w.astype(BF16), pool_scale, w_branch.astype(BF16),
              w_out.astype(BF16), w_ple.astype(BF16), w_ple_gate.astype(BF16))
    y_p, wkv_p, shift_p, pool_p = _run_group(
        x_prompt, p_prompt, None, None, None, 0, lw, params,
        n_seq=1, n_sub=4, proj_tm=2048, mix_seq=1, mix_t=256, out_rows=256)
    y_s, wkv_s, shift_s, pool_s = _run_group(
        x_sample, p_sample, state_wkv, state_shift, state_pool, PAST_LEN, lw, params,
        n_seq=8, n_sub=2, proj_tm=1024, mix_seq=32, mix_t=8, out_rows=256)
    return (y_p, y_s, wkv_p, shift_p, pool_p, wkv_s, shift_s, pool_s)
```

```python
import functools

import jax
import jax.numpy as jnp
import numpy as np
from jax import lax
from jax.experimental import pallas as pl
from jax.experimental.pallas import tpu as pltpu

F32 = jnp.float32
BF16 = jnp.bfloat16

D_MODEL = 2048
DEPTH = 4
PAST_LEN = 16384
RWKV_WIDTH = 1024
HEAD_DIM = 64
N_HEADS = 16
POOL_WIDTH = 1024
POOL_WINDOWS = (2, 4, 8, 16)
POOL_GROUP = 256
POOL_BUF = 15
PLE_DIM = 256
DECAY_RANK = 96
AAA_RANK = 96
MV_RANK = 64
LORA_COLS = DECAY_RANK + AAA_RANK + MV_RANK
IN_COLS = 10240
RMS_EPS = 1e-6
GN_EPS = 64e-5
DECAY_SCALE = 0.606531

HEADS_PER_GROUP = 4
GROUP_LANES = HEADS_PER_GROUP * HEAD_DIM
N_GROUPS = N_HEADS // HEADS_PER_GROUP
CHUNK = 64
EXPANDED = HEADS_PER_GROUP * CHUNK
BF16_ROWS = 16
HALO = 16
PROJ_TN = 2 * LORA_COLS
N_MAIN_TILES = IN_COLS // PROJ_TN
PREV_COLS = 3 * RWKV_WIDTH + PROJ_TN
LB_OFF = 3 * RWKV_WIDTH + LORA_COLS

VMEM_LIMIT_BYTES = 56 * 1024 * 1024


def _sigmoid(x):
    return 0.5 * jnp.tanh(0.5 * x) + 0.5


def _split_bf16(x):
    hi = x.astype(BF16)
    lo = (x - hi.astype(F32)).astype(BF16)
    return hi, lo


_NN = (((1,), (0,)), ((), ()))
_NT = (((1,), (1,)), ((), ()))


def _dot(a, b, dims=_NN):
    return lax.dot_general(a, b, dimension_numbers=dims, preferred_element_type=F32)


def _segsum(x, bones_b):
    return _dot(x.astype(BF16), bones_b)


def _proj_kernel(x_ref, g_ref, w_ref, wl_ref, z_ref, hb_ref, *, normalize, n_main):
    j = pl.program_id(1)

    @pl.when(j == 0)
    def _():
        x = x_ref[...]
        if normalize:
            x = x * lax.rsqrt(jnp.mean(x * x, axis=-1, keepdims=True) + RMS_EPS) * g_ref[...]
        hb_ref[...] = x.astype(BF16)

    @pl.when(j < n_main)
    def _():
        z_ref[...] = jnp.dot(hb_ref[...], w_ref[...].astype(BF16), preferred_element_type=F32).astype(z_ref.dtype)

    @pl.when(j == n_main)
    def _():
        z_ref[...] = jnp.dot(hb_ref[...], wl_ref[...], preferred_element_type=F32).astype(z_ref.dtype)


def _proj(x2d, norm_gain, w_in, layer, w_lora, *, normalize, tm, n_main, out_dtype):
    m, d = x2d.shape
    tn = PROJ_TN
    assert m % tm == 0
    once = pl.Buffered(1)
    return pl.pallas_call(
        functools.partial(_proj_kernel, normalize=normalize, n_main=n_main),
        grid=(m // tm, n_main + 1),
        in_specs=[
            pl.BlockSpec((tm, d), lambda i, j: (i, 0), pipeline_mode=once),
            pl.BlockSpec((None, 1, d), lambda i, j: (layer, 0, 0)),
            pl.BlockSpec((None, d, tn), lambda i, j: (layer, 0, jnp.minimum(j, n_main - 1))),
            pl.BlockSpec((d, tn), lambda i, j: (0, 0), pipeline_mode=once),
        ],
        out_specs=pl.BlockSpec((tm, tn), lambda i, j: (i, j)),
        out_shape=jax.ShapeDtypeStruct((m, (n_main + 1) * tn), out_dtype),
        scratch_shapes=[pltpu.VMEM((tm, d), BF16)],
        name="proj",
        compiler_params=pltpu.CompilerParams(
            dimension_semantics=("parallel", "arbitrary"), vmem_limit_bytes=VMEM_LIMIT_BYTES),
    )(x2d, norm_gain.reshape(DEPTH, 1, d), w_in, w_lora)


def _norm_rows_kernel(x_ref, g_ref, o_ref):
    x = x_ref[...]
    o_ref[...] = x * lax.rsqrt(jnp.mean(x * x, axis=-1, keepdims=True) + RMS_EPS) * g_ref[...]


def _norm_rows(x2d, gain):
    return pl.pallas_call(
        _norm_rows_kernel, out_shape=jax.ShapeDtypeStruct(x2d.shape, F32), name="norm_rows",
    )(x2d, gain.reshape(1, -1))


_V_MU_R, _V_MU_K, _V_MU_V, _V_W0, _V_A0, _V_V0, _V_KK, _V_KA, _V_RK, _V_LNW, _V_LNB = range(11)
_N_VEC_ROWS = 16


def _wkv_constants(n_seq, t_seq):
    c, e = CHUNK, EXPANDED
    assert n_seq * t_seq == c
    row = np.arange(c)
    q_of = row // t_seq
    tri = (row[:, None] >= row[None, :]) & (q_of[:, None] == q_of[None, :])
    erow = np.arange(e)
    es = erow % c
    same = q_of[:, None] == (es // t_seq)[None, :]
    mask_s = same & (row[:, None] > es[None, :])
    mask_i = same & (row[:, None] >= es[None, :])
    mask_sc = np.block([[mask_s, mask_s], [mask_i, mask_i]])
    lane = np.arange(GROUP_LANES)
    bones = (lane[:, None] // HEAD_DIM) == (lane[None, :] // HEAD_DIM)
    levels = [np.eye(e, dtype=bool), (erow[:, None] // 2) == (erow[None, :] // 2)]
    m = 2
    while m < t_seq:
        levels.append(((erow[:, None] // (2 * m)) == (erow[None, :] // (2 * m)))
                      & ((erow[:, None] % (2 * m)) >= m) & ((erow[None, :] % (2 * m)) < m))
        m *= 2
    col = np.arange(2 * c)
    qmask = ((col[None, :] % c) // t_seq) == np.arange(n_seq)[:, None]
    f = lambda a: jnp.asarray(a.astype(np.float32))
    return dict(tri=f(tri).astype(BF16), mask_sc=f(mask_sc).astype(BF16), bones=f(bones),
                bones_b=f(bones).astype(BF16), levels=f(np.stack(levels)).astype(BF16),
                qmask=f(qmask).reshape(n_seq, 1, 2 * c))


def _wkv_kernel(*refs, n_seq, t_seq, n_sub, chain, n_steps, has_vres, n_levels):
    it = iter(refs)
    zr_ref, zk_ref, zv_ref, zg_ref, zl_ref, p0_ref = (next(it) for _ in range(6))
    vfirst_ref = next(it) if has_vres else None
    w2_ref, vec_ref, s0_ref = next(it), next(it), next(it)
    tri_ref, msc_ref, bones_ref, bonesb_ref, lev_ref, qmask_ref = (next(it) for _ in range(6))
    ya_ref = next(it)
    vout_ref = None if has_vres else next(it)
    sout_ref = next(it)
    sbd_ref, prev_ref = next(it), next(it)

    step = pl.program_id(1)
    gl, rw, e = GROUP_LANES, RWKV_WIDTH, EXPANDED
    n_slots = n_seq if chain else n_sub * n_seq
    slot = (lambda j, q: q) if chain else (lambda j, q: j * n_seq + q)
    bones = bones_ref[...]
    bones_b = bonesb_ref[...]
    mask_sc = msc_ref[...]
    eye_b = lev_ref[0]
    tri = tri_ref[...]
    vec = vec_ref[...]
    vrow = lambda i: vec[i:i + 1, :]
    groups = range(N_GROUPS)
    sls = [slice(gi * gl, (gi + 1) * gl) for gi in groups]

    def expand(x):
        return jnp.concatenate([x] * HEADS_PER_GROUP, axis=0) * bones_b

    low_lanes = lax.broadcasted_iota(jnp.int32, (1, 2 * HEAD_DIM), 1) < HEAD_DIM

    def head_slot(hd):
        h = hd % HEADS_PER_GROUP
        return hd // HEADS_PER_GROUP, h * HEAD_DIM, h // 2, h % 2 == 1

    def load_state(s, g):
        return jnp.concatenate([sbd_ref[s, g, 0], sbd_ref[s, g, 1]], axis=1)

    def store_state(s, g, val):
        sbd_ref[s, g, 0] = val[:, :2 * HEAD_DIM]
        sbd_ref[s, g, 1] = val[:, 2 * HEAD_DIM:]

    @pl.when(step == 0)
    def _():
        for s in range(n_slots):
            for hd in range(N_HEADS):
                g, row0, half, upper = head_slot(hd)
                packed = s0_ref[s, hd]
                even = jnp.where(low_lanes, packed, 0.0)
                odd = jnp.where(low_lanes, 0.0, packed)
                if upper:
                    even = pltpu.roll(even, HEAD_DIM, axis=1)
                else:
                    odd = pltpu.roll(odd, HEAD_DIM, axis=1)
                sbd_ref[s, g, half, pl.ds(row0, HEAD_DIM // 2, stride=2), :] = even
                sbd_ref[s, g, half, pl.ds(row0 + 1, HEAD_DIM // 2, stride=2), :] = odd
                sbd_ref[s, g, 1 - half, row0:row0 + HEAD_DIM, :] = jnp.zeros((HEAD_DIM, 2 * HEAD_DIM), F32)
        prev_ref[...] = p0_ref[0]

    row = lax.broadcasted_iota(jnp.int32, (CHUNK, 1), 0)
    is_first = (row % t_seq) == 0
    lora_lane = lax.broadcasted_iota(jnp.int32, (1, LORA_COLS), 1)

    def shifted(z, p0):
        rolled = pltpu.roll(z, 1, axis=0)
        if n_seq == 1:
            first = jnp.broadcast_to(p0, z.shape)
        else:
            first = jnp.broadcast_to(p0[:, None, :], (n_seq, t_seq, z.shape[1])).reshape(z.shape)
        return jnp.where(is_first, first, rolled)

    ctx = [dict() for _ in range(n_sub)]

    def prep(j):
        c = ctx[j]
        rows = slice(j * CHUNK, (j + 1) * CHUNK)
        zr, zk, zv = zr_ref[rows, :].astype(F32), zk_ref[rows, :].astype(F32), zv_ref[rows, :].astype(F32)
        zl = zl_ref[rows, :].astype(F32)
        la, lb = zl[:, :LORA_COLS], zl[:, LORA_COLS:]
        c["last"] = [x[CHUNK - 1:CHUNK, :] for x in (zr, zk, zv, lb)]
        if chain and j > 0:
            p_r, p_k, p_v, p_l = ctx[j - 1]["last"]
        else:
            ps = slice(0, n_seq) if chain else slice(j * n_seq, (j + 1) * n_seq)
            p_r, p_k, p_v = prev_ref[ps, 0:rw], prev_ref[ps, rw:2 * rw], prev_ref[ps, 2 * rw:3 * rw]
            p_l = prev_ref[ps, LB_OFF:LB_OFF + LORA_COLS]
        r = zr + (shifted(zr, p_r) - zr) * vrow(_V_MU_R)
        k = zk + (shifted(zk, p_k) - zk) * vrow(_V_MU_K)
        v = zv + (shifted(zv, p_v) - zv) * vrow(_V_MU_V)
        lin = la + shifted(lb, p_l)
        lin = jnp.where(lora_lane < DECAY_RANK, jnp.tanh(lin), lin).astype(BF16)
        d_all = jnp.dot(lin, w2_ref[...], preferred_element_type=F32)
        wlog = -DECAY_SCALE * _sigmoid(vrow(_V_W0) + d_all[:, 0:rw])
        alr = _sigmoid(vrow(_V_A0) + d_all[:, rw:2 * rw])
        if has_vres:
            v = v + (vfirst_ref[rows, :] - v) * _sigmoid(vrow(_V_V0) + d_all[:, 2 * rw:3 * rw])
        else:
            vout_ref[rows, :] = v
        kkr = k * vrow(_V_KK)
        kf = k * (1.0 + (alr - 1.0) * vrow(_V_KA))
        c["rkr"] = r * kf * vrow(_V_RK)
        g_all = zg_ref[rows, :].astype(F32)
        c["gate"] = g_all * _sigmoid(g_all)
        w_hi, w_lo = _split_bf16(wlog)
        cum = jnp.dot(tri, w_hi, preferred_element_type=F32) + jnp.dot(tri, w_lo, preferred_element_type=F32)
        p_inc = jnp.exp(cum)
        inv_p = jnp.exp(-cum)
        p_exc = jnp.exp(cum - wlog)
        c["p_inc"] = p_inc
        yield
        kk_n2 = [_segsum(kkr[:, sl] * kkr[:, sl], bones_b) for sl in sls]
        kk = [kkr[:, sl] * lax.rsqrt(jnp.maximum(n2, 1e-24)) for sl, n2 in zip(sls, kk_n2)]
        c["v_g"] = [v[:, sl] for sl in sls]
        c["a_f"] = [-kk[gi] * p_exc[:, sls[gi]] for gi in groups]
        c["r_f"] = [r[:, sl] * p_inc[:, sl] for sl in sls]
        a_t = [x.astype(BF16) for x in c["a_f"]]
        r_t = [x.astype(BF16) for x in c["r_f"]]
        c["b_t"] = [(kk[gi] * alr[:, sls[gi]] * inv_p[:, sls[gi]]).astype(BF16) for gi in groups]
        c["k_t"] = [(kf[:, sl] * inv_p[:, sl]).astype(BF16) for sl in sls]
        c["v_e"] = [expand(x.astype(BF16)) for x in c["v_g"]]
        c["ar"] = [jnp.concatenate([a_t[gi], r_t[gi]], axis=0) for gi in groups]
        yield
        sc = [_dot(c["ar"][gi], jnp.concatenate([expand(c["b_t"][gi]), expand(c["k_t"][gi])], axis=0), _NT)
              for gi in groups]
        c["scm"] = [x.astype(BF16) * mask_sc for x in sc]
        c["n_bd"] = [expand(x[:CHUNK, :e]) for x in c["scm"]]
        c["xs"] = [eye_b + n * lev_ref[1] for n in c["n_bd"]]
        yield

    def inverse(j):
        c = ctx[j]
        m = 2
        for lvl in range(2, n_levels):
            n_c = [c["n_bd"][gi] * lev_ref[lvl] for gi in groups]
            if m % BF16_ROWS == 0:
                lower = lambda x: jnp.concatenate(
                    [x[b * 2 * m + m:(b + 1) * 2 * m] for b in range(e // (2 * m))], axis=0)
                eye_l = lower(eye_b)
                t1 = [_dot(lower(c["xs"][gi]), n_c[gi]).astype(BF16) + eye_l for gi in groups]
                yield
                new = [_dot(t1[gi], c["xs"][gi]).astype(BF16) for gi in groups]
                c["xs"] = [jnp.concatenate(
                    [piece for b in range(e // (2 * m))
                     for piece in (x[b * 2 * m:b * 2 * m + m], n[b * m:(b + 1) * m])], axis=0)
                    for x, n in zip(c["xs"], new)]
            else:
                t1 = [_dot(c["xs"][gi], n_c[gi]).astype(BF16) + eye_b for gi in groups]
                yield
                c["xs"] = [_dot(t1[gi], c["xs"][gi]).astype(BF16) for gi in groups]
            yield
            m *= 2

    def tail(j):
        c = ctx[j]
        rows = slice(j * CHUNK, (j + 1) * CHUNK)
        x_c = [(x[0:CHUNK] + x[CHUNK:2 * CHUNK]) + (x[2 * CHUNK:3 * CHUNK] + x[3 * CHUNK:4 * CHUNK])
               for x in c["xs"]]
        a_s, r_s = [], []
        for gi in groups:
            parts_a, parts_r = [], []
            for q in range(n_seq):
                lo_, hi_ = q * t_seq, (q + 1) * t_seq
                lhs = c["ar"][gi] if n_seq == 1 else jnp.concatenate(
                    [c["a_f"][gi][lo_:hi_], c["r_f"][gi][lo_:hi_]], axis=0).astype(BF16)
                res = _dot(lhs, load_state(slot(j, q), gi).astype(BF16), _NT)
                parts_a.append(res[:t_seq])
                parts_r.append(res[t_seq:])
            a_s.append(parts_a[0] if n_seq == 1 else jnp.concatenate(parts_a, axis=0))
            r_s.append(parts_r[0] if n_seq == 1 else jnp.concatenate(parts_r, axis=0))
        yield
        y = [a_s[gi] + _dot(c["scm"][gi][:CHUNK, e:], c["v_e"][gi]) for gi in groups]
        yield
        u = [_dot(x_c[gi], expand(y[gi].astype(BF16))) for gi in groups]
        yield
        o = [r_s[gi] + _dot(c["scm"][gi][CHUNK:, :],
                            jnp.concatenate([expand(u[gi].astype(BF16)), c["v_e"][gi]], axis=0)) for gi in groups]
        yield
        for gi in groups:
            uv_t = jnp.concatenate([u[gi], c["v_g"][gi]], axis=0).T
            bk = jnp.concatenate([c["b_t"][gi], c["k_t"][gi]], axis=0)
            for q in range(n_seq):
                lhs = uv_t if n_seq == 1 else uv_t * qmask_ref[q]
                p_last = c["p_inc"][(q + 1) * t_seq - 1:(q + 1) * t_seq, sls[gi]]
                s = slot(j, q)
                store_state(s, gi, bones * ((load_state(s, gi) + _dot(lhs.astype(BF16), bk)) * p_last))
        yield
        mean = [_segsum(x, bones_b) * (1.0 / HEAD_DIM) for x in o]
        d = [o[gi] - mean[gi] for gi in groups]
        var = [_segsum(x * x, bones_b) * (1.0 / HEAD_DIM) for x in d]
        bonus = [_segsum(c["rkr"][:, sl], bones_b) for sl in sls]
        yield
        for gi in groups:
            sl = sls[gi]
            on = d[gi] * lax.rsqrt(var[gi] + GN_EPS) * vec[_V_LNW:_V_LNW + 1, sl] + vec[_V_LNB:_V_LNB + 1, sl]
            ya_ref[rows, sl] = ((on + bonus[gi] * c["v_g"][gi]) * c["gate"][:, sl]).astype(ya_ref.dtype)
        c.clear()
        c["last"] = None

    last_rows = None
    preps, invs, tails = ([f(j) for j in range(n_sub)] for f in (prep, inverse, tail))
    for phase in range(n_sub + 2):
        active = []
        if 0 <= phase - 2 < n_sub:
            active.append(tails[phase - 2])
        if 0 <= phase - 1 < n_sub:
            active.append(invs[phase - 1])
        if phase < n_sub:
            active.append(preps[phase])
        if phase == n_sub - 1 + 1:
            last_rows = ctx[n_sub - 1]["last"]
        while active:
            for gen in list(active):
                try:
                    next(gen)
                except StopIteration:
                    active.remove(gen)

    if chain and n_steps > 1:
        assert n_seq == 1
        prev_ref[:, 0:rw] = last_rows[0]
        prev_ref[:, rw:2 * rw] = last_rows[1]
        prev_ref[:, 2 * rw:3 * rw] = last_rows[2]
        prev_ref[:, LB_OFF:LB_OFF + LORA_COLS] = last_rows[3]

    @pl.when(step == n_steps - 1)
    def _():
        for s in range(n_slots):
            for hd in range(N_HEADS):
                g, row0, half, upper = head_slot(hd)
                even = sbd_ref[s, g, half, pl.ds(row0, HEAD_DIM // 2, stride=2), :]
                odd = sbd_ref[s, g, half, pl.ds(row0 + 1, HEAD_DIM // 2, stride=2), :]
                if upper:
                    even = pltpu.roll(even, HEAD_DIM, axis=1)
                else:
                    odd = pltpu.roll(odd, HEAD_DIM, axis=1)
                sout_ref[s, hd] = jnp.where(low_lanes, even, odd)


def _wkv(z, prev0, vfirst, w2e, vecs, s_all, layer, *, n_batch, t_len, n_seq, n_sub):
    m = z.shape[0]
    t_seq = CHUNK // n_seq
    n_chunks = t_len // t_seq
    chain = n_seq == 1
    if chain:
        assert n_chunks % n_sub == 0
        n_steps, nb, n_slots = n_chunks // n_sub, n_batch, n_seq
    else:
        assert n_chunks == 1 and n_batch % (n_seq * n_sub) == 0
        n_steps, nb, n_slots = 1, n_batch // (n_seq * n_sub), n_seq * n_sub
    has_vres = vfirst is not None
    consts = _wkv_constants(n_seq, t_seq)
    n_levels = consts["levels"].shape[0]
    rw = RWKV_WIDTH
    rows = n_sub * CHUNK
    rb = lambda b, s: b * n_steps + s
    prev0 = prev0.reshape(nb, n_slots, PREV_COLS)

    zspec = lambda col: pl.BlockSpec((rows, rw), lambda b, s: (rb(b, s), col))
    full = lambda a: pl.BlockSpec(a.shape, lambda b, s: (0,) * a.ndim)
    sspec = pl.BlockSpec((None, n_slots, N_HEADS, HEAD_DIM // 2, 2 * HEAD_DIM), lambda b, s: (layer, b, 0, 0, 0))

    in_specs = [zspec(0), zspec(1), zspec(2), zspec(3),
                pl.BlockSpec((rows, PROJ_TN), lambda b, s: (rb(b, s), N_MAIN_TILES)),
                pl.BlockSpec((1, n_slots, PREV_COLS), lambda b, s: (b, 0, 0))]
    args = [z, z, z, z, z, prev0]
    if has_vres:
        in_specs.append(zspec(0))
        args.append(vfirst)
    else:
        w2e = w2e[:, :2 * rw]
    in_specs += [full(w2e), full(vecs), sspec]
    args += [w2e, vecs, s_all]
    state_arg = len(args) - 1
    for name in ("tri", "mask_sc", "bones", "bones_b", "levels", "qmask"):
        in_specs.append(full(consts[name]))
        args.append(consts[name])

    out_specs = [zspec(0)]
    out_shape = [jax.ShapeDtypeStruct((m, rw), BF16)]
    if not has_vres:
        out_specs.append(zspec(0))
        out_shape.append(jax.ShapeDtypeStruct((m, rw), F32))
    out_specs.append(sspec)
    out_shape.append(jax.ShapeDtypeStruct(s_all.shape, F32))

    outs = pl.pallas_call(
        functools.partial(_wkv_kernel, n_seq=n_seq, t_seq=t_seq, n_sub=n_sub, chain=chain, n_steps=n_steps,
                          has_vres=has_vres, n_levels=n_levels),
        grid=(nb, n_steps),
        in_specs=in_specs, out_specs=out_specs, out_shape=out_shape,
        input_output_aliases={state_arg: len(out_shape) - 1},
        scratch_shapes=[pltpu.VMEM((n_slots, N_GROUPS, 2, GROUP_LANES, GROUP_LANES // 2), F32),
                        pltpu.VMEM((n_slots, PREV_COLS), F32)],
        name="wkv",
        compiler_params=pltpu.CompilerParams(
            dimension_semantics=("parallel", "arbitrary"), vmem_limit_bytes=VMEM_LIMIT_BYTES),
    )(*args)
    if has_vres:
        ya, s_new = outs
        return ya, vfirst, s_new
    return outs


def _mix_kernel(*refs, n_seq, t_seq, tiles_per_seq, n_past, carry_pool):
    if carry_pool:
        (ya_ref, u_ref, gp_ref, ga_ref, gb_ref, pool_ref, perm_ref, permt_ref, pw_ref, ps_ref, wb0_ref, wb1_ref,
         m_ref, pool_out_ref) = refs
    else:
        ya_ref, u_ref, gp_ref, ga_ref, gb_ref, halo_ref, pw_ref, ps_ref, wb0_ref, wb1_ref, m_ref = refs
    rows = n_seq * t_seq
    tile = pl.program_id(0) % tiles_per_seq
    if carry_pool:
        perm = perm_ref[...]
        u_t = jnp.dot(perm, u_ref[...], preferred_element_type=F32).reshape(t_seq, n_seq, POOL_WIDTH)
        gp_all = jnp.dot(perm, gp_ref[...], preferred_element_type=F32)
        e = jnp.concatenate([jnp.zeros((HALO - POOL_BUF, n_seq, POOL_WIDTH), F32), pool_ref[...], u_t], axis=0)
        pool_out_ref[...] = e[HALO + t_seq - POOL_BUF:]
        t_axis = 0
        pos = lax.broadcasted_iota(jnp.int32, (t_seq, 1, 1), 0) + (1 + n_past)
    else:
        u = u_ref[...].astype(F32)
        halo = jnp.where(tile == 0, 0.0, halo_ref[...].astype(F32)).reshape(1, HALO, POOL_WIDTH)
        e = jnp.concatenate([halo, u.reshape(n_seq, t_seq, POOL_WIDTH)], axis=1)
        t_axis = 1
        pos = lax.broadcasted_iota(jnp.int32, (1, t_seq, 1), 1) + (tile * t_seq + 1 + n_past)
    take = lambda x, start, size: lax.slice_in_dim(x, start, start + size, axis=t_axis)
    n_col = len(POOL_WINDOWS)
    cw = D_MODEL // n_col
    cols = [slice(ci * cw, (ci + 1) * cw) for ci in range(n_col)]
    ya = ya_ref[...]
    gated_a, yb_parts = [], []
    for gi, win in enumerate(POOL_WINDOWS):
        pa = jnp.dot(ya, wb0_ref[:, cols[gi]], preferred_element_type=F32)
        sl = slice(gi * POOL_GROUP, (gi + 1) * POOL_GROUP)
        e_g = e[:, :, sl]
        acc = e_g
        span, length = 1, HALO + t_seq
        while span < win:
            length -= span
            acc = take(acc, span, length) + take(acc, 0, length)
            span *= 2
        ws = take(acc, length - t_seq, t_seq)
        inv_cnt = 1.0 / jnp.minimum(pos, win).astype(F32)
        mixed = (ws * inv_cnt - take(e_g, HALO, t_seq)).reshape(rows, POOL_GROUP)
        y_g = jnp.dot(mixed.astype(BF16), pw_ref[gi], preferred_element_type=F32)
        gp = gp_all[:, sl] if carry_pool else gp_ref[:, sl].astype(F32)
        yb_parts.append((y_g * ps_ref[:, sl] * (gp * _sigmoid(gp))).astype(BF16))
        gated_a.append(_sigmoid(ga_ref[:, cols[gi]].astype(F32)) * pa)
    y_b = jnp.concatenate(yb_parts, axis=1)
    if carry_pool:
        y_b = jnp.dot(permt_ref[...], y_b, preferred_element_type=F32).astype(BF16)
    pb = [None] * n_col
    pb[0] = jnp.dot(y_b, wb1_ref[:, cols[0]], preferred_element_type=F32)
    for ci in range(n_col):
        if ci + 1 < n_col:
            pb[ci + 1] = jnp.dot(y_b, wb1_ref[:, cols[ci + 1]], preferred_element_type=F32)
        m_ref[:, cols[ci]] = (gated_a[ci] + _sigmoid(gb_ref[:, cols[ci]].astype(F32)) * pb[ci]).astype(m_ref.dtype)


def _mix(ya, z, pool_all, pool_w_b, pool_scale, w_branch_b, layer, *, n_seq, t_seq, tiles_per_seq, n_past):
    m = z.shape[0]
    rows = n_seq * t_seq
    carry_pool = pool_all is not None
    in_specs = [
        pl.BlockSpec((rows, RWKV_WIDTH), lambda i: (i, 0)),
        pl.BlockSpec((rows, POOL_WIDTH), lambda i: (i, 4)),
        pl.BlockSpec((rows, POOL_WIDTH), lambda i: (i, 5)),
        pl.BlockSpec((rows, D_MODEL), lambda i: (i, 3)),
        pl.BlockSpec((rows, D_MODEL), lambda i: (i, 4)),
    ]
    args = [ya, z, z, z, z]
    out_specs = [pl.BlockSpec((rows, D_MODEL), lambda i: (i, 0))]
    out_shape = [jax.ShapeDtypeStruct((m, D_MODEL), BF16)]
    aliases = {}
    if carry_pool:
        assert tiles_per_seq == 1 and t_seq < POOL_BUF
        r = np.arange(rows)
        perm = np.zeros((rows, rows), np.float32)
        perm[(r % t_seq) * n_seq + r // t_seq, r] = 1.0
        pool_spec = pl.BlockSpec((None, POOL_BUF, n_seq, POOL_WIDTH), lambda i: (layer, 0, i, 0))
        const = pl.BlockSpec((rows, rows), lambda i: (0, 0))
        in_specs += [pool_spec, const, const]
        aliases = {len(args): 1}
        args += [pool_all, jnp.asarray(perm, BF16), jnp.asarray(perm.T, BF16)]
        out_specs.append(pool_spec)
        out_shape.append(jax.ShapeDtypeStruct(pool_all.shape, F32))
    else:
        assert n_seq == 1 and rows % HALO == 0
        in_specs.append(pl.BlockSpec((HALO, POOL_WIDTH), lambda i: (jnp.maximum(i * (rows // HALO) - 1, 0), 4)))
        args.append(z)
    in_specs += [
        pl.BlockSpec((None, len(POOL_WINDOWS), POOL_GROUP, POOL_GROUP), lambda i: (layer, 0, 0, 0)),
        pl.BlockSpec((None, 1, POOL_WIDTH), lambda i: (layer, 0, 0)),
        pl.BlockSpec((None, None, RWKV_WIDTH, D_MODEL), lambda i: (layer, 0, 0, 0)),
        pl.BlockSpec((None, None, POOL_WIDTH, D_MODEL), lambda i: (layer, 1, 0, 0)),
    ]
    args += [pool_w_b, pool_scale.reshape(DEPTH, 1, POOL_WIDTH), w_branch_b, w_branch_b]
    outs = pl.pallas_call(
        functools.partial(_mix_kernel, n_seq=n_seq, t_seq=t_seq, tiles_per_seq=tiles_per_seq, n_past=n_past,
                          carry_pool=carry_pool),
        grid=(m // rows,),
        in_specs=in_specs, out_specs=out_specs, out_shape=out_shape,
        input_output_aliases=aliases,
        name="mix",
        compiler_params=pltpu.CompilerParams(
            dimension_semantics=("parallel",), vmem_limit_bytes=VMEM_LIMIT_BYTES),
    )(*args)
    return outs if carry_pool else (outs[0], None)


def _out_kernel(m_ref, x_ref, p_ref, g_ref, wo_ref, wg_ref, wp_ref, o_ref, *, n_split):
    rows = m_ref.shape[0] // n_split
    parts = [slice(i * rows, (i + 1) * rows) for i in range(n_split)]
    outs = [jnp.dot(m_ref[sl, :], wo_ref[...], preferred_element_type=F32) for sl in parts]
    ples = [jnp.dot(p_ref[sl, :].astype(BF16), wp_ref[...], preferred_element_type=F32) for sl in parts]
    xs = [x_ref[sl, :] + o * lax.rsqrt(jnp.mean(o * o, axis=-1, keepdims=True) + RMS_EPS) * g_ref[...]
          for sl, o in zip(parts, outs)]
    gates = [_sigmoid(jnp.dot(x.astype(BF16), wg_ref[...], preferred_element_type=F32)) for x in xs]
    for sl, x, gate, ple in zip(parts, xs, gates, ples):
        o_ref[sl, :] = x + gate * ple


def _out(mm_, x2d, p3d, norm_gain, w_out_b, w_gate_b, w_ple_b, layer, *, rows):
    m = x2d.shape[0]
    once = pl.Buffered(1)
    wspec = lambda k: pl.BlockSpec((None, k, D_MODEL), lambda i: (layer, 0, 0), pipeline_mode=once)
    return pl.pallas_call(
        functools.partial(_out_kernel, n_split=2),
        grid=(m // rows,),
        in_specs=[
            pl.BlockSpec((rows, D_MODEL), lambda i: (i, 0)),
            pl.BlockSpec((rows, D_MODEL), lambda i: (i, 0)),
            pl.BlockSpec((None, rows, PLE_DIM), lambda i: (layer, i, 0)),
            wspec(1), wspec(D_MODEL), wspec(D_MODEL), wspec(PLE_DIM),
        ],
        out_specs=pl.BlockSpec((rows, D_MODEL), lambda i: (i, 0)),
        out_shape=jax.ShapeDtypeStruct((m, D_MODEL), F32),
        name="out",
        compiler_params=pltpu.CompilerParams(
            dimension_semantics=("parallel",), vmem_limit_bytes=VMEM_LIMIT_BYTES),
    )(mm_, x2d, p3d, norm_gain.reshape(DEPTH, 1, D_MODEL), w_out_b, w_gate_b, w_ple_b)


def _pair_rows(s):
    l, b = s.shape[:2]
    return s.reshape(l, b, N_HEADS, HEAD_DIM // 2, 2 * HEAD_DIM)


def _unpair_rows(s):
    l, b = s.shape[:2]
    return s.reshape(l, b, N_HEADS, HEAD_DIM, HEAD_DIM)


def _layer_weights(i, mu_rkv, mu_lora, w0, w1, w2, a0, a1, a2, v0, v1, v2, k_k, k_a, r_k, ln_w, ln_b):
    r_ = RWKV_WIDTH
    mu = mu_lora[i]
    if i > 0:
        v1_i, v2_i, v0_i = v1[i - 1], v2[i - 1], v0[i - 1]
    else:
        v1_i = jnp.zeros((D_MODEL, MV_RANK), F32)
        v2_i = jnp.zeros((MV_RANK, r_), F32)
        v0_i = jnp.zeros((r_,), F32)
    w_lora = jnp.concatenate(
        [(1.0 - mu[0])[:, None] * w1[i], (1.0 - mu[1])[:, None] * a1[i], (1.0 - mu[2])[:, None] * v1_i,
         mu[0][:, None] * w1[i], mu[1][:, None] * a1[i], mu[2][:, None] * v1_i], axis=1).astype(BF16)
    z96 = jnp.zeros((DECAY_RANK, r_), F32)
    z64 = jnp.zeros((MV_RANK, r_), F32)
    w2e = jnp.concatenate([
        jnp.concatenate([w2[i], z96, z64], axis=0),
        jnp.concatenate([z96, a2[i], z64], axis=0),
        jnp.concatenate([z96, z96, v2_i], axis=0)], axis=1).astype(BF16)
    rows = [mu_rkv[i, 0], mu_rkv[i, 1], mu_rkv[i, 2], w0[i], a0[i], v0_i, k_k[i], k_a[i], r_k[i], ln_w[i], ln_b[i]]
    vecs = jnp.concatenate([jnp.stack(rows), jnp.zeros((_N_VEC_ROWS - len(rows), r_), F32)], axis=0)
    return w_lora, w2e, vecs


def _run_group(x, p, wkv, shift, pool, n_past, lw, params, *, n_seq, n_sub, proj_tm, mix_seq, mix_t, out_rows):
    (w_in, norm_pre, norm_post, pool_w_b, pool_scale, w_branch_b, w_out_b, w_ple_b, w_gate_b) = params
    b, t, d = x.shape
    m = b * t
    x2d = x.reshape(m, d)
    p3d = p.reshape(DEPTH, m, PLE_DIM)
    zero_state = wkv is None
    shift_out, pool_out = [], []
    v_first = None
    tiles_per_seq = t // mix_t
    s_all = _pair_rows(jnp.zeros((DEPTH, b, N_HEADS, HEAD_DIM, HEAD_DIM), F32) if zero_state else wkv)
    pool_all = None if zero_state else jnp.transpose(pool, (0, 2, 1, 3))
    for i in range(DEPTH):
        w_lora, w2e, vecs = lw[i]
        z = _proj(x2d, norm_pre, w_in, i, w_lora, normalize=True, tm=proj_tm, n_main=N_MAIN_TILES, out_dtype=BF16)
        shift_out.append(_norm_rows(x2d.reshape(b, t, d)[:, -1], norm_pre[i]))
        if zero_state:
            assert t >= POOL_BUF
            prev0 = jnp.zeros((b, PREV_COLS), F32)
            pool_out.append(z.reshape(b, t, -1)[:, -POOL_BUF:, 4 * RWKV_WIDTH:4 * RWKV_WIDTH + POOL_WIDTH].astype(F32))
        else:
            prev0 = _proj(shift[i], norm_pre, w_in, i, w_lora, normalize=False, tm=b,
                          n_main=3 * RWKV_WIDTH // PROJ_TN, out_dtype=F32)
        ya, v_first, s_all = _wkv(z, prev0, v_first, w2e, vecs, s_all, i, n_batch=b, t_len=t, n_seq=n_seq,
                                  n_sub=n_sub)
        mm_, pool_all = _mix(ya, z, pool_all, pool_w_b, pool_scale, w_branch_b, i,
                             n_seq=mix_seq, t_seq=mix_t, tiles_per_seq=tiles_per_seq, n_past=n_past)
        x2d = _out(mm_, x2d, p3d, norm_post, w_out_b, w_gate_b, w_ple_b, i, rows=out_rows)
    pool_new = jnp.stack(pool_out) if zero_state else jnp.transpose(pool_all, (0, 2, 1, 3))
    return x2d.reshape(b, t, d), _unpair_rows(s_all), jnp.stack(shift_out), pool_new


def kernel(x_prompt, x_sample, state_wkv, state_shift, state_pool, p_prompt, p_sample, norm_pre, norm_post, w_in, mu_rkv, mu_lora, w0, w1, w2, a0, a1, a2, v0, v1, v2, k_k, k_a, r_k, ln_w, ln_b, pool_w, pool_scale, w_branch, w_out, w_ple, w_ple_gate):
    lw = [_layer_weights(i, mu_rkv, mu_lora, w0, w1, w2, a0, a1, a2, v0, v1, v2, k_k, k_a, r_k, ln_w, ln_b)
          for i in range(DEPTH)]
    params = (w_in, norm_pre, norm_post, pool_w.astype(BF16), pool_scale, w_branch.astype(BF16),
              w_out.astype(BF16), w_ple.astype(BF16), w_ple_gate.astype(BF16))
    y_p, wkv_p, shift_p, pool_p = _run_group(
        x_prompt, p_prompt, None, None, None, 0, lw, params,
        n_seq=1, n_sub=4, proj_tm=2048, mix_seq=1, mix_t=256, out_rows=256)
    y_s, wkv_s, shift_s, pool_s = _run_group(
        x_sample, p_sample, state_wkv, state_shift, state_pool, PAST_LEN, lw, params,
        n_seq=8, n_sub=2, proj_tm=1024, mix_seq=32, mix_t=8, out_rows=256)
    return (y_p, y_s, wkv_p, shift_p, pool_p, wkv_s, shift_s, pool_s)
```

```python
import functools

import jax
import jax.numpy as jnp
import numpy as np
from jax import lax
from jax.experimental import pallas as pl
from jax.experimental.pallas import tpu as pltpu

F32 = jnp.float32
BF16 = jnp.bfloat16

D_MODEL = 2048
DEPTH = 4
PAST_LEN = 16384
RWKV_WIDTH = 1024
HEAD_DIM = 64
N_HEADS = 16
POOL_WIDTH = 1024
POOL_WINDOWS = (2, 4, 8, 16)
POOL_GROUP = 256
POOL_BUF = 15
PLE_DIM = 256
DECAY_RANK = 96
AAA_RANK = 96
MV_RANK = 64
LORA_COLS = DECAY_RANK + AAA_RANK + MV_RANK
IN_COLS = 10240
RMS_EPS = 1e-6
GN_EPS = 64e-5
DECAY_SCALE = 0.606531

HEADS_PER_GROUP = 4
GROUP_LANES = HEADS_PER_GROUP * HEAD_DIM
N_GROUPS = N_HEADS // HEADS_PER_GROUP
CHUNK = 64
EXPANDED = HEADS_PER_GROUP * CHUNK
BF16_ROWS = 16
HALO = 16
PROJ_TN = 2 * LORA_COLS
N_MAIN_TILES = IN_COLS // PROJ_TN
PREV_COLS = 3 * RWKV_WIDTH + PROJ_TN
LB_OFF = 3 * RWKV_WIDTH + LORA_COLS

VMEM_LIMIT_BYTES = 56 * 1024 * 1024


def _sigmoid(x):
    return 0.5 * jnp.tanh(0.5 * x) + 0.5


def _split_bf16(x):
    hi = x.astype(BF16)
    lo = (x - hi.astype(F32)).astype(BF16)
    return hi, lo


_NN = (((1,), (0,)), ((), ()))
_NT = (((1,), (1,)), ((), ()))


def _dot(a, b, dims=_NN):
    return lax.dot_general(a, b, dimension_numbers=dims, preferred_element_type=F32)


def _segsum(x, bones_b):
    return _dot(x.astype(BF16), bones_b)


def _proj_kernel(h_ref, w_ref, wl_ref, z_ref, *, n_main):
    j = pl.program_id(1)

    @pl.when(j < n_main)
    def _():
        z_ref[...] = jnp.dot(h_ref[...], w_ref[...].astype(BF16), preferred_element_type=F32).astype(z_ref.dtype)

    @pl.when(j == n_main)
    def _():
        z_ref[...] = jnp.dot(h_ref[...], wl_ref[...], preferred_element_type=F32).astype(z_ref.dtype)


def _proj(h2d, w_in, layer, w_lora, *, tm, n_main, out_dtype):
    m, d = h2d.shape
    tn = PROJ_TN
    assert m % tm == 0
    return pl.pallas_call(
        functools.partial(_proj_kernel, n_main=n_main),
        grid=(m // tm, n_main + 1),
        in_specs=[
            pl.BlockSpec((tm, d), lambda i, j: (i, 0)),
            pl.BlockSpec((None, d, tn), lambda i, j: (layer, 0, jnp.minimum(j, n_main - 1))),
            pl.BlockSpec((d, tn), lambda i, j: (0, 0), pipeline_mode=pl.Buffered(1)),
        ],
        out_specs=pl.BlockSpec((tm, tn), lambda i, j: (i, j)),
        out_shape=jax.ShapeDtypeStruct((m, (n_main + 1) * tn), out_dtype),
        name="proj",
        compiler_params=pltpu.CompilerParams(
            dimension_semantics=("parallel", "arbitrary"), vmem_limit_bytes=VMEM_LIMIT_BYTES),
    )(h2d, w_in, w_lora)


def _rmsnorm(x, gain):
    return x * lax.rsqrt(jnp.mean(x * x, axis=-1, keepdims=True) + RMS_EPS) * gain


def _norm_kernel(x_ref, g_ref, o_ref):
    o_ref[...] = _rmsnorm(x_ref[...], g_ref[...]).astype(o_ref.dtype)


def _norm(x2d, gain, *, rows, out_dtype):
    m, d = x2d.shape
    rows = min(rows, m)
    assert m % rows == 0
    return pl.pallas_call(
        _norm_kernel,
        grid=(m // rows,),
        in_specs=[pl.BlockSpec((rows, d), lambda i: (i, 0)), pl.BlockSpec((1, d), lambda i: (0, 0))],
        out_specs=pl.BlockSpec((rows, d), lambda i: (i, 0)),
        out_shape=jax.ShapeDtypeStruct((m, d), out_dtype),
        name="norm",
        compiler_params=pltpu.CompilerParams(dimension_semantics=("parallel",)),
    )(x2d, gain.reshape(1, d))


_V_MU_R, _V_MU_K, _V_MU_V, _V_W0, _V_A0, _V_V0, _V_KK, _V_KA, _V_RK, _V_LNW, _V_LNB = range(11)
_N_VEC_ROWS = 16


def _wkv_constants(n_seq, t_seq):
    c, e = CHUNK, EXPANDED
    assert n_seq * t_seq == c
    row = np.arange(c)
    q_of = row // t_seq
    tri = (row[:, None] >= row[None, :]) & (q_of[:, None] == q_of[None, :])
    erow = np.arange(e)
    es = erow % c
    same = q_of[:, None] == (es // t_seq)[None, :]
    mask_s = same & (row[:, None] > es[None, :])
    mask_i = same & (row[:, None] >= es[None, :])
    mask_sc = np.block([[mask_s, mask_s], [mask_i, mask_i]])
    lane = np.arange(GROUP_LANES)
    bones = (lane[:, None] // HEAD_DIM) == (lane[None, :] // HEAD_DIM)
    levels = [np.eye(e, dtype=bool), (erow[:, None] // 2) == (erow[None, :] // 2)]
    m = 2
    while m < t_seq:
        levels.append(((erow[:, None] // (2 * m)) == (erow[None, :] // (2 * m)))
                      & ((erow[:, None] % (2 * m)) >= m) & ((erow[None, :] % (2 * m)) < m))
        m *= 2
    col = np.arange(2 * c)
    qmask = ((col[None, :] % c) // t_seq) == np.arange(n_seq)[:, None]
    f = lambda a: jnp.asarray(a.astype(np.float32))
    return dict(tri=f(tri).astype(BF16), mask_sc=f(mask_sc).astype(BF16), bones=f(bones),
                bones_b=f(bones).astype(BF16), levels=f(np.stack(levels)).astype(BF16),
                qmask=f(qmask).reshape(n_seq, 1, 2 * c))


def _wkv_kernel(*refs, n_seq, t_seq, n_sub, chain, n_steps, has_vres, n_levels):
    it = iter(refs)
    zr_ref, zk_ref, zv_ref, zg_ref, zl_ref, p0_ref = (next(it) for _ in range(6))
    vfirst_ref = next(it) if has_vres else None
    w2_ref, vec_ref, s0_ref = next(it), next(it), next(it)
    tri_ref, msc_ref, bones_ref, bonesb_ref, lev_ref, qmask_ref = (next(it) for _ in range(6))
    ya_ref = next(it)
    vout_ref = None if has_vres else next(it)
    sout_ref = next(it)
    sbd_ref, prev_ref = next(it), next(it)

    step = pl.program_id(1)
    gl, rw, e = GROUP_LANES, RWKV_WIDTH, EXPANDED
    n_slots = n_seq if chain else n_sub * n_seq
    slot = (lambda j, q: q) if chain else (lambda j, q: j * n_seq + q)
    bones = bones_ref[...]
    bones_b = bonesb_ref[...]
    mask_sc = msc_ref[...]
    eye_b = lev_ref[0]
    tri = tri_ref[...]
    vec = vec_ref[...]
    vrow = lambda i: vec[i:i + 1, :]
    groups = range(N_GROUPS)
    sls = [slice(gi * gl, (gi + 1) * gl) for gi in groups]

    def expand(x):
        return jnp.concatenate([x] * HEADS_PER_GROUP, axis=0) * bones_b

    low_lanes = lax.broadcasted_iota(jnp.int32, (1, 2 * HEAD_DIM), 1) < HEAD_DIM

    def head_slot(hd):
        h = hd % HEADS_PER_GROUP
        return hd // HEADS_PER_GROUP, h * HEAD_DIM, h // 2, h % 2 == 1

    def load_state(s, g):
        return jnp.concatenate([sbd_ref[s, g, 0], sbd_ref[s, g, 1]], axis=1)

    def store_state(s, g, val):
        sbd_ref[s, g, 0] = val[:, :2 * HEAD_DIM]
        sbd_ref[s, g, 1] = val[:, 2 * HEAD_DIM:]

    @pl.when(step == 0)
    def _():
        for s in range(n_slots):
            for hd in range(N_HEADS):
                g, row0, half, upper = head_slot(hd)
                packed = s0_ref[s, hd]
                even = jnp.where(low_lanes, packed, 0.0)
                odd = jnp.where(low_lanes, 0.0, packed)
                if upper:
                    even = pltpu.roll(even, HEAD_DIM, axis=1)
                else:
                    odd = pltpu.roll(odd, HEAD_DIM, axis=1)
                sbd_ref[s, g, half, pl.ds(row0, HEAD_DIM // 2, stride=2), :] = even
                sbd_ref[s, g, half, pl.ds(row0 + 1, HEAD_DIM // 2, stride=2), :] = odd
                sbd_ref[s, g, 1 - half, row0:row0 + HEAD_DIM, :] = jnp.zeros((HEAD_DIM, 2 * HEAD_DIM), F32)
        prev_ref[...] = p0_ref[0]

    row = lax.broadcasted_iota(jnp.int32, (CHUNK, 1), 0)
    is_first = (row % t_seq) == 0
    lora_lane = lax.broadcasted_iota(jnp.int32, (1, LORA_COLS), 1)

    def shifted(z, p0):
        rolled = pltpu.roll(z, 1, axis=0)
        if n_seq == 1:
            first = jnp.broadcast_to(p0, z.shape)
        else:
            first = jnp.broadcast_to(p0[:, None, :], (n_seq, t_seq, z.shape[1])).reshape(z.shape)
        return jnp.where(is_first, first, rolled)

    ctx = [dict() for _ in range(n_sub)]

    def prep(j):
        c = ctx[j]
        rows = slice(j * CHUNK, (j + 1) * CHUNK)
        zr, zk, zv = zr_ref[rows, :].astype(F32), zk_ref[rows, :].astype(F32), zv_ref[rows, :].astype(F32)
        zl = zl_ref[rows, :].astype(F32)
        la, lb = zl[:, :LORA_COLS], zl[:, LORA_COLS:]
        c["last"] = [x[CHUNK - 1:CHUNK, :] for x in (zr, zk, zv, lb)]
        if chain and j > 0:
            p_r, p_k, p_v, p_l = ctx[j - 1]["last"]
        else:
            ps = slice(0, n_seq) if chain else slice(j * n_seq, (j + 1) * n_seq)
            p_r, p_k, p_v = prev_ref[ps, 0:rw], prev_ref[ps, rw:2 * rw], prev_ref[ps, 2 * rw:3 * rw]
            p_l = prev_ref[ps, LB_OFF:LB_OFF + LORA_COLS]
        r = zr + (shifted(zr, p_r) - zr) * vrow(_V_MU_R)
        k = zk + (shifted(zk, p_k) - zk) * vrow(_V_MU_K)
        v = zv + (shifted(zv, p_v) - zv) * vrow(_V_MU_V)
        lin = la + shifted(lb, p_l)
        lin = jnp.where(lora_lane < DECAY_RANK, jnp.tanh(lin), lin).astype(BF16)
        d_all = jnp.dot(lin, w2_ref[...], preferred_element_type=F32)
        wlog = -DECAY_SCALE * _sigmoid(vrow(_V_W0) + d_all[:, 0:rw])
        alr = _sigmoid(vrow(_V_A0) + d_all[:, rw:2 * rw])
        if has_vres:
            v = v + (vfirst_ref[rows, :] - v) * _sigmoid(vrow(_V_V0) + d_all[:, 2 * rw:3 * rw])
        else:
            vout_ref[rows, :] = v
        kkr = k * vrow(_V_KK)
        kf = k * (1.0 + (alr - 1.0) * vrow(_V_KA))
        c["rkr"] = r * kf * vrow(_V_RK)
        g_all = zg_ref[rows, :].astype(F32)
        c["gate"] = g_all * _sigmoid(g_all)
        w_hi, w_lo = _split_bf16(wlog)
        cum = jnp.dot(tri, w_hi, preferred_element_type=F32) + jnp.dot(tri, w_lo, preferred_element_type=F32)
        p_inc = jnp.exp(cum)
        inv_p = jnp.exp(-cum)
        p_exc = jnp.exp(cum - wlog)
        c["p_inc"] = p_inc
        yield
        kk_n2 = [_segsum(kkr[:, sl] * kkr[:, sl], bones_b) for sl in sls]
        kk = [kkr[:, sl] * lax.rsqrt(jnp.maximum(n2, 1e-24)) for sl, n2 in zip(sls, kk_n2)]
        c["v_g"] = [v[:, sl] for sl in sls]
        c["a_f"] = [-kk[gi] * p_exc[:, sls[gi]] for gi in groups]
        c["r_f"] = [r[:, sl] * p_inc[:, sl] for sl in sls]
        a_t = [x.astype(BF16) for x in c["a_f"]]
        r_t = [x.astype(BF16) for x in c["r_f"]]
        c["b_t"] = [(kk[gi] * alr[:, sls[gi]] * inv_p[:, sls[gi]]).astype(BF16) for gi in groups]
        c["k_t"] = [(kf[:, sl] * inv_p[:, sl]).astype(BF16) for sl in sls]
        c["v_e"] = [expand(x.astype(BF16)) for x in c["v_g"]]
        c["ar"] = [jnp.concatenate([a_t[gi], r_t[gi]], axis=0) for gi in groups]
        yield
        sc = [_dot(c["ar"][gi], jnp.concatenate([expand(c["b_t"][gi]), expand(c["k_t"][gi])], axis=0), _NT)
              for gi in groups]
        c["scm"] = [x.astype(BF16) * mask_sc for x in sc]
        c["n_bd"] = [expand(x[:CHUNK, :e]) for x in c["scm"]]
        c["xs"] = [eye_b + n * lev_ref[1] for n in c["n_bd"]]
        yield

    def inverse(j):
        c = ctx[j]
        m = 2
        for lvl in range(2, n_levels):
            n_c = [c["n_bd"][gi] * lev_ref[lvl] for gi in groups]
            if m % BF16_ROWS == 0:
                lower = lambda x: jnp.concatenate(
                    [x[b * 2 * m + m:(b + 1) * 2 * m] for b in range(e // (2 * m))], axis=0)
                eye_l = lower(eye_b)
                t1 = [_dot(lower(c["xs"][gi]), n_c[gi]).astype(BF16) + eye_l for gi in groups]
                yield
                new = [_dot(t1[gi], c["xs"][gi]).astype(BF16) for gi in groups]
                c["xs"] = [jnp.concatenate(
                    [piece for b in range(e // (2 * m))
                     for piece in (x[b * 2 * m:b * 2 * m + m], n[b * m:(b + 1) * m])], axis=0)
                    for x, n in zip(c["xs"], new)]
            else:
                t1 = [_dot(c["xs"][gi], n_c[gi]).astype(BF16) + eye_b for gi in groups]
                yield
                c["xs"] = [_dot(t1[gi], c["xs"][gi]).astype(BF16) for gi in groups]
            yield
            m *= 2

    def tail(j):
        c = ctx[j]
        rows = slice(j * CHUNK, (j + 1) * CHUNK)
        x_c = [(x[0:CHUNK] + x[CHUNK:2 * CHUNK]) + (x[2 * CHUNK:3 * CHUNK] + x[3 * CHUNK:4 * CHUNK])
               for x in c["xs"]]
        a_s, r_s = [], []
        for gi in groups:
            parts_a, parts_r = [], []
            for q in range(n_seq):
                lo_, hi_ = q * t_seq, (q + 1) * t_seq
                lhs = c["ar"][gi] if n_seq == 1 else jnp.concatenate(
                    [c["a_f"][gi][lo_:hi_], c["r_f"][gi][lo_:hi_]], axis=0).astype(BF16)
                res = _dot(lhs, load_state(slot(j, q), gi).astype(BF16), _NT)
                parts_a.append(res[:t_seq])
                parts_r.append(res[t_seq:])
            a_s.append(parts_a[0] if n_seq == 1 else jnp.concatenate(parts_a, axis=0))
            r_s.append(parts_r[0] if n_seq == 1 else jnp.concatenate(parts_r, axis=0))
        yield
        y = [a_s[gi] + _dot(c["scm"][gi][:CHUNK, e:], c["v_e"][gi]) for gi in groups]
        yield
        u = [_dot(x_c[gi], expand(y[gi].astype(BF16))) for gi in groups]
        yield
        o = [r_s[gi] + _dot(c["scm"][gi][CHUNK:, :],
                            jnp.concatenate([expand(u[gi].astype(BF16)), c["v_e"][gi]], axis=0)) for gi in groups]
        yield
        for gi in groups:
            uv_t = jnp.concatenate([u[gi], c["v_g"][gi]], axis=0).T
            bk = jnp.concatenate([c["b_t"][gi], c["k_t"][gi]], axis=0)
            for q in range(n_seq):
                lhs = uv_t if n_seq == 1 else uv_t * qmask_ref[q]
                p_last = c["p_inc"][(q + 1) * t_seq - 1:(q + 1) * t_seq, sls[gi]]
                s = slot(j, q)
                store_state(s, gi, bones * ((load_state(s, gi) + _dot(lhs.astype(BF16), bk)) * p_last))
        yield
        mean = [_segsum(x, bones_b) * (1.0 / HEAD_DIM) for x in o]
        d = [o[gi] - mean[gi] for gi in groups]
        var = [_segsum(x * x, bones_b) * (1.0 / HEAD_DIM) for x in d]
        bonus = [_segsum(c["rkr"][:, sl], bones_b) for sl in sls]
        yield
        for gi in groups:
            sl = sls[gi]
            on = d[gi] * lax.rsqrt(var[gi] + GN_EPS) * vec[_V_LNW:_V_LNW + 1, sl] + vec[_V_LNB:_V_LNB + 1, sl]
            ya_ref[rows, sl] = ((on + bonus[gi] * c["v_g"][gi]) * c["gate"][:, sl]).astype(ya_ref.dtype)
        c.clear()
        c["last"] = None

    last_rows = None
    preps, invs, tails = ([f(j) for j in range(n_sub)] for f in (prep, inverse, tail))
    for phase in range(n_sub + 2):
        active = []
        if 0 <= phase - 2 < n_sub:
            active.append(tails[phase - 2])
        if 0 <= phase - 1 < n_sub:
            active.append(invs[phase - 1])
        if phase < n_sub:
            active.append(preps[phase])
        if phase == n_sub - 1 + 1:
            last_rows = ctx[n_sub - 1]["last"]
        while active:
            for gen in list(active):
                try:
                    next(gen)
                except StopIteration:
                    active.remove(gen)

    if chain and n_steps > 1:
        assert n_seq == 1
        prev_ref[:, 0:rw] = last_rows[0]
        prev_ref[:, rw:2 * rw] = last_rows[1]
        prev_ref[:, 2 * rw:3 * rw] = last_rows[2]
        prev_ref[:, LB_OFF:LB_OFF + LORA_COLS] = last_rows[3]

    @pl.when(step == n_steps - 1)
    def _():
        for s in range(n_slots):
            for hd in range(N_HEADS):
                g, row0, half, upper = head_slot(hd)
                even = sbd_ref[s, g, half, pl.ds(row0, HEAD_DIM // 2, stride=2), :]
                odd = sbd_ref[s, g, half, pl.ds(row0 + 1, HEAD_DIM // 2, stride=2), :]
                if upper:
                    even = pltpu.roll(even, HEAD_DIM, axis=1)
                else:
                    odd = pltpu.roll(odd, HEAD_DIM, axis=1)
                sout_ref[s, hd] = jnp.where(low_lanes, even, odd)


def _wkv(z, prev0, vfirst, w2e, vecs, s_all, layer, *, n_batch, t_len, n_seq, n_sub):
    m = z.shape[0]
    t_seq = CHUNK // n_seq
    n_chunks = t_len // t_seq
    chain = n_seq == 1
    if chain:
        assert n_chunks % n_sub == 0
        n_steps, nb, n_slots = n_chunks // n_sub, n_batch, n_seq
    else:
        assert n_chunks == 1 and n_batch % (n_seq * n_sub) == 0
        n_steps, nb, n_slots = 1, n_batch // (n_seq * n_sub), n_seq * n_sub
    has_vres = vfirst is not None
    consts = _wkv_constants(n_seq, t_seq)
    n_levels = consts["levels"].shape[0]
    rw = RWKV_WIDTH
    rows = n_sub * CHUNK
    rb = lambda b, s: b * n_steps + s
    prev0 = prev0.reshape(nb, n_slots, PREV_COLS)

    zspec = lambda col: pl.BlockSpec((rows, rw), lambda b, s: (rb(b, s), col))
    full = lambda a: pl.BlockSpec(a.shape, lambda b, s: (0,) * a.ndim)
    sspec = pl.BlockSpec((None, n_slots, N_HEADS, HEAD_DIM // 2, 2 * HEAD_DIM), lambda b, s: (layer, b, 0, 0, 0))

    in_specs = [zspec(0), zspec(1), zspec(2), zspec(3),
                pl.BlockSpec((rows, PROJ_TN), lambda b, s: (rb(b, s), N_MAIN_TILES)),
                pl.BlockSpec((1, n_slots, PREV_COLS), lambda b, s: (b, 0, 0))]
    args = [z, z, z, z, z, prev0]
    if has_vres:
        in_specs.append(zspec(0))
        args.append(vfirst)
    else:
        w2e = w2e[:, :2 * rw]
    in_specs += [full(w2e), full(vecs), sspec]
    args += [w2e, vecs, s_all]
    state_arg = len(args) - 1
    for name in ("tri", "mask_sc", "bones", "bones_b", "levels", "qmask"):
        in_specs.append(full(consts[name]))
        args.append(consts[name])

    out_specs = [zspec(0)]
    out_shape = [jax.ShapeDtypeStruct((m, rw), BF16)]
    if not has_vres:
        out_specs.append(zspec(0))
        out_shape.append(jax.ShapeDtypeStruct((m, rw), F32))
    out_specs.append(sspec)
    out_shape.append(jax.ShapeDtypeStruct(s_all.shape, F32))

    outs = pl.pallas_call(
        functools.partial(_wkv_kernel, n_seq=n_seq, t_seq=t_seq, n_sub=n_sub, chain=chain, n_steps=n_steps,
                          has_vres=has_vres, n_levels=n_levels),
        grid=(nb, n_steps),
        in_specs=in_specs, out_specs=out_specs, out_shape=out_shape,
        input_output_aliases={state_arg: len(out_shape) - 1},
        scratch_shapes=[pltpu.VMEM((n_slots, N_GROUPS, 2, GROUP_LANES, GROUP_LANES // 2), F32),
                        pltpu.VMEM((n_slots, PREV_COLS), F32)],
        name="wkv",
        compiler_params=pltpu.CompilerParams(
            dimension_semantics=("parallel", "arbitrary"), vmem_limit_bytes=VMEM_LIMIT_BYTES),
    )(*args)
    if has_vres:
        ya, s_new = outs
        return ya, vfirst, s_new
    return outs


def _mix_kernel(*refs, n_seq, t_seq, tiles_per_seq, n_past, carry_pool):
    if carry_pool:
        (ya_ref, u_ref, gp_ref, ga_ref, gb_ref, pool_ref, perm_ref, permt_ref, pw_ref, ps_ref, wb0_ref, wb1_ref,
         m_ref, pool_out_ref) = refs
    else:
        ya_ref, u_ref, gp_ref, ga_ref, gb_ref, halo_ref, pw_ref, ps_ref, wb0_ref, wb1_ref, m_ref = refs
    rows = n_seq * t_seq
    tile = pl.program_id(0) % tiles_per_seq
    if carry_pool:
        perm = perm_ref[...]
        u_t = jnp.dot(perm, u_ref[...], preferred_element_type=F32).reshape(t_seq, n_seq, POOL_WIDTH)
        gp_all = jnp.dot(perm, gp_ref[...], preferred_element_type=F32)
        e = jnp.concatenate([jnp.zeros((HALO - POOL_BUF, n_seq, POOL_WIDTH), F32), pool_ref[...], u_t], axis=0)
        pool_out_ref[...] = e[HALO + t_seq - POOL_BUF:]
        t_axis = 0
        pos = lax.broadcasted_iota(jnp.int32, (t_seq, 1, 1), 0) + (1 + n_past)
    else:
        u = u_ref[...].astype(F32)
        halo = jnp.where(tile == 0, 0.0, halo_ref[...].astype(F32)).reshape(1, HALO, POOL_WIDTH)
        e = jnp.concatenate([halo, u.reshape(n_seq, t_seq, POOL_WIDTH)], axis=1)
        t_axis = 1
        pos = lax.broadcasted_iota(jnp.int32, (1, t_seq, 1), 1) + (tile * t_seq + 1 + n_past)
    take = lambda x, start, size: lax.slice_in_dim(x, start, start + size, axis=t_axis)
    n_col = len(POOL_WINDOWS)
    cw = D_MODEL // n_col
    cols = [slice(ci * cw, (ci + 1) * cw) for ci in range(n_col)]
    ya = ya_ref[...]
    gated_a, yb_parts = [], []
    for gi, win in enumerate(POOL_WINDOWS):
        pa = jnp.dot(ya, wb0_ref[:, cols[gi]], preferred_element_type=F32)
        sl = slice(gi * POOL_GROUP, (gi + 1) * POOL_GROUP)
        e_g = e[:, :, sl]
        acc = e_g
        span, length = 1, HALO + t_seq
        while span < win:
            length -= span
            acc = take(acc, span, length) + take(acc, 0, length)
            span *= 2
        ws = take(acc, length - t_seq, t_seq)
        inv_cnt = 1.0 / jnp.minimum(pos, win).astype(F32)
        mixed = (ws * inv_cnt - take(e_g, HALO, t_seq)).reshape(rows, POOL_GROUP)
        y_g = jnp.dot(mixed.astype(BF16), pw_ref[gi], preferred_element_type=F32)
        gp = gp_all[:, sl] if carry_pool else gp_ref[:, sl].astype(F32)
        yb_parts.append((y_g * ps_ref[:, sl] * (gp * _sigmoid(gp))).astype(BF16))
        gated_a.append(_sigmoid(ga_ref[:, cols[gi]].astype(F32)) * pa)
    y_b = jnp.concatenate(yb_parts, axis=1)
    if carry_pool:
        y_b = jnp.dot(permt_ref[...], y_b, preferred_element_type=F32).astype(BF16)
    pb = [None] * n_col
    pb[0] = jnp.dot(y_b, wb1_ref[:, cols[0]], preferred_element_type=F32)
    for ci in range(n_col):
        if ci + 1 < n_col:
            pb[ci + 1] = jnp.dot(y_b, wb1_ref[:, cols[ci + 1]], preferred_element_type=F32)
        m_ref[:, cols[ci]] = (gated_a[ci] + _sigmoid(gb_ref[:, cols[ci]].astype(F32)) * pb[ci]).astype(m_ref.dtype)


def _mix(ya, z, pool_all, pool_w_b, pool_scale, w_branch_b, layer, *, n_seq, t_seq, tiles_per_seq, n_past):
    m = z.shape[0]
    rows = n_seq * t_seq
    carry_pool = pool_all is not None
    in_specs = [
        pl.BlockSpec((rows, RWKV_WIDTH), lambda i: (i, 0)),
        pl.BlockSpec((rows, POOL_WIDTH), lambda i: (i, 4)),
        pl.BlockSpec((rows, POOL_WIDTH), lambda i: (i, 5)),
        pl.BlockSpec((rows, D_MODEL), lambda i: (i, 3)),
        pl.BlockSpec((rows, D_MODEL), lambda i: (i, 4)),
    ]
    args = [ya, z, z, z, z]
    out_specs = [pl.BlockSpec((rows, D_MODEL), lambda i: (i, 0))]
    out_shape = [jax.ShapeDtypeStruct((m, D_MODEL), BF16)]
    aliases = {}
    if carry_pool:
        assert tiles_per_seq == 1 and t_seq < POOL_BUF
        r = np.arange(rows)
        perm = np.zeros((rows, rows), np.float32)
        perm[(r % t_seq) * n_seq + r // t_seq, r] = 1.0
        pool_spec = pl.BlockSpec((None, POOL_BUF, n_seq, POOL_WIDTH), lambda i: (layer, 0, i, 0))
        const = pl.BlockSpec((rows, rows), lambda i: (0, 0))
        in_specs += [pool_spec, const, const]
        aliases = {len(args): 1}
        args += [pool_all, jnp.asarray(perm, BF16), jnp.asarray(perm.T, BF16)]
        out_specs.append(pool_spec)
        out_shape.append(jax.ShapeDtypeStruct(pool_all.shape, F32))
    else:
        assert n_seq == 1 and rows % HALO == 0
        in_specs.append(pl.BlockSpec((HALO, POOL_WIDTH), lambda i: (jnp.maximum(i * (rows // HALO) - 1, 0), 4)))
        args.append(z)
    in_specs += [
        pl.BlockSpec((None, len(POOL_WINDOWS), POOL_GROUP, POOL_GROUP), lambda i: (layer, 0, 0, 0)),
        pl.BlockSpec((None, 1, POOL_WIDTH), lambda i: (layer, 0, 0)),
        pl.BlockSpec((None, None, RWKV_WIDTH, D_MODEL), lambda i: (layer, 0, 0, 0)),
        pl.BlockSpec((None, None, POOL_WIDTH, D_MODEL), lambda i: (layer, 1, 0, 0)),
    ]
    args += [pool_w_b, pool_scale.reshape(DEPTH, 1, POOL_WIDTH), w_branch_b, w_branch_b]
    outs = pl.pallas_call(
        functools.partial(_mix_kernel, n_seq=n_seq, t_seq=t_seq, tiles_per_seq=tiles_per_seq, n_past=n_past,
                          carry_pool=carry_pool),
        grid=(m // rows,),
        in_specs=in_specs, out_specs=out_specs, out_shape=out_shape,
        input_output_aliases=aliases,
        name="mix",
        compiler_params=pltpu.CompilerParams(
            dimension_semantics=("parallel",), vmem_limit_bytes=VMEM_LIMIT_BYTES),
    )(*args)
    return outs if carry_pool else (outs[0], None)


def _out_kernel(*refs, n_split, emit_h):
    if emit_h:
        m_ref, x_ref, p_ref, g_ref, gn_ref, wo_ref, wg_ref, wp_ref, o_ref, h_ref = refs
    else:
        m_ref, x_ref, p_ref, g_ref, wo_ref, wg_ref, wp_ref, o_ref = refs
    rows = m_ref.shape[0] // n_split
    parts = [slice(i * rows, (i + 1) * rows) for i in range(n_split)]
    outs = [jnp.dot(m_ref[sl, :], wo_ref[...], preferred_element_type=F32) for sl in parts]
    ples = [jnp.dot(p_ref[sl, :].astype(BF16), wp_ref[...], preferred_element_type=F32) for sl in parts]
    xs = [x_ref[sl, :] + _rmsnorm(o, g_ref[...]) for sl, o in zip(parts, outs)]
    gates = [_sigmoid(jnp.dot(x.astype(BF16), wg_ref[...], preferred_element_type=F32)) for x in xs]
    for sl, x, gate, ple in zip(parts, xs, gates, ples):
        x_new = x + gate * ple
        o_ref[sl, :] = x_new
        if emit_h:
            h_ref[sl, :] = _rmsnorm(x_new, gn_ref[...]).astype(h_ref.dtype)


def _out(mm_, x2d, p3d, norm_post, norm_pre, w_out_b, w_gate_b, w_ple_b, layer, *, rows):
    m = x2d.shape[0]
    emit_h = layer + 1 < DEPTH
    once = pl.Buffered(1)
    wspec = lambda k, l=layer: pl.BlockSpec((None, k, D_MODEL), lambda i: (l, 0, 0), pipeline_mode=once)
    row_spec = pl.BlockSpec((rows, D_MODEL), lambda i: (i, 0))
    in_specs = [row_spec, row_spec, pl.BlockSpec((None, rows, PLE_DIM), lambda i: (layer, i, 0)), wspec(1)]
    args = [mm_, x2d, p3d, norm_post.reshape(DEPTH, 1, D_MODEL)]
    out_specs, out_shape = [row_spec], [jax.ShapeDtypeStruct((m, D_MODEL), F32)]
    if emit_h:
        in_specs.append(wspec(1, layer + 1))
        args.append(norm_pre.reshape(DEPTH, 1, D_MODEL))
        out_specs.append(row_spec)
        out_shape.append(jax.ShapeDtypeStruct((m, D_MODEL), BF16))
    in_specs += [wspec(D_MODEL), wspec(D_MODEL), wspec(PLE_DIM)]
    args += [w_out_b, w_gate_b, w_ple_b]
    outs = pl.pallas_call(
        functools.partial(_out_kernel, n_split=2, emit_h=emit_h),
        grid=(m // rows,),
        in_specs=in_specs, out_specs=out_specs, out_shape=out_shape,
        name="out",
        compiler_params=pltpu.CompilerParams(
            dimension_semantics=("parallel",), vmem_limit_bytes=VMEM_LIMIT_BYTES),
    )(*args)
    return (outs[0], outs[1]) if emit_h else (outs[0], None)


def _pair_rows(s):
    l, b = s.shape[:2]
    return s.reshape(l, b, N_HEADS, HEAD_DIM // 2, 2 * HEAD_DIM)


def _unpair_rows(s):
    l, b = s.shape[:2]
    return s.reshape(l, b, N_HEADS, HEAD_DIM, HEAD_DIM)


def _layer_weights(i, mu_rkv, mu_lora, w0, w1, w2, a0, a1, a2, v0, v1, v2, k_k, k_a, r_k, ln_w, ln_b):
    r_ = RWKV_WIDTH
    mu = mu_lora[i]
    if i > 0:
        v1_i, v2_i, v0_i = v1[i - 1], v2[i - 1], v0[i - 1]
    else:
        v1_i = jnp.zeros((D_MODEL, MV_RANK), F32)
        v2_i = jnp.zeros((MV_RANK, r_), F32)
        v0_i = jnp.zeros((r_,), F32)
    w_lora = jnp.concatenate(
        [(1.0 - mu[0])[:, None] * w1[i], (1.0 - mu[1])[:, None] * a1[i], (1.0 - mu[2])[:, None] * v1_i,
         mu[0][:, None] * w1[i], mu[1][:, None] * a1[i], mu[2][:, None] * v1_i], axis=1).astype(BF16)
    z96 = jnp.zeros((DECAY_RANK, r_), F32)
    z64 = jnp.zeros((MV_RANK, r_), F32)
    w2e = jnp.concatenate([
        jnp.concatenate([w2[i], z96, z64], axis=0),
        jnp.concatenate([z96, a2[i], z64], axis=0),
        jnp.concatenate([z96, z96, v2_i], axis=0)], axis=1).astype(BF16)
    rows = [mu_rkv[i, 0], mu_rkv[i, 1], mu_rkv[i, 2], w0[i], a0[i], v0_i, k_k[i], k_a[i], r_k[i], ln_w[i], ln_b[i]]
    vecs = jnp.concatenate([jnp.stack(rows), jnp.zeros((_N_VEC_ROWS - len(rows), r_), F32)], axis=0)
    return w_lora, w2e, vecs


def _run_group(x, p, wkv, shift, pool, n_past, lw, params, *, n_seq, n_sub, proj_tm, mix_seq, mix_t, out_rows):
    (w_in, norm_pre, norm_post, pool_w_b, pool_scale, w_branch_b, w_out_b, w_ple_b, w_gate_b) = params
    b, t, d = x.shape
    m = b * t
    x2d = x.reshape(m, d)
    p3d = p.reshape(DEPTH, m, PLE_DIM)
    zero_state = wkv is None
    shift_out, pool_out = [], []
    v_first = None
    tiles_per_seq = t // mix_t
    s_all = _pair_rows(jnp.zeros((DEPTH, b, N_HEADS, HEAD_DIM, HEAD_DIM), F32) if zero_state else wkv)
    pool_all = None if zero_state else jnp.transpose(pool, (0, 2, 1, 3))
    h2d = _norm(x2d, norm_pre[0], rows=out_rows, out_dtype=BF16)
    for i in range(DEPTH):
        w_lora, w2e, vecs = lw[i]
        z = _proj(h2d, w_in, i, w_lora, tm=proj_tm, n_main=N_MAIN_TILES, out_dtype=BF16)
        shift_out.append(_norm(x2d.reshape(b, t, d)[:, -1], norm_pre[i], rows=b, out_dtype=F32))
        if zero_state:
            assert t >= POOL_BUF
            prev0 = jnp.zeros((b, PREV_COLS), F32)
            pool_out.append(z.reshape(b, t, -1)[:, -POOL_BUF:, 4 * RWKV_WIDTH:4 * RWKV_WIDTH + POOL_WIDTH].astype(F32))
        else:
            prev0 = _proj(shift[i].astype(BF16), w_in, i, w_lora, tm=b, n_main=3 * RWKV_WIDTH // PROJ_TN,
                          out_dtype=F32)
        ya, v_first, s_all = _wkv(z, prev0, v_first, w2e, vecs, s_all, i, n_batch=b, t_len=t, n_seq=n_seq,
                                  n_sub=n_sub)
        mm_, pool_all = _mix(ya, z, pool_all, pool_w_b, pool_scale, w_branch_b, i,
                             n_seq=mix_seq, t_seq=mix_t, tiles_per_seq=tiles_per_seq, n_past=n_past)
        x2d, h2d = _out(mm_, x2d, p3d, norm_post, norm_pre, w_out_b, w_gate_b, w_ple_b, i, rows=out_rows)
    pool_new = jnp.stack(pool_out) if zero_state else jnp.transpose(pool_all, (0, 2, 1, 3))
    return x2d.reshape(b, t, d), _unpair_rows(s_all), jnp.stack(shift_out), pool_new


def kernel(x_prompt, x_sample, state_wkv, state_shift, state_pool, p_prompt, p_sample, norm_pre, norm_post, w_in, mu_rkv, mu_lora, w0, w1, w2, a0, a1, a2, v0, v1, v2, k_k, k_a, r_k, ln_w, ln_b, pool_w, pool_scale, w_branch, w_out, w_ple, w_ple_gate):
    lw = [_layer_weights(i, mu_rkv, mu_lora, w0, w1, w2, a0, a1, a2, v0, v1, v2, k_k, k_a, r_k, ln_w, ln_b)
          for i in range(DEPTH)]
    params = (w_in, norm_pre, norm_post, pool_w.astype(BF16), pool_scale, w_branch.astype(BF16),
              w_out.astype(BF16), w_ple.astype(BF16), w_ple_gate.astype(BF16))
    y_p, wkv_p, shift_p, pool_p = _run_group(
        x_prompt, p_prompt, None, None, None, 0, lw, params,
        n_seq=1, n_sub=4, proj_tm=2048, mix_seq=1, mix_t=256, out_rows=256)
    y_s, wkv_s, shift_s, pool_s = _run_group(
        x_sample, p_sample, state_wkv, state_shift, state_pool, PAST_LEN, lw, params,
        n_seq=8, n_sub=2, proj_tm=1024, mix_seq=32, mix_t=8, out_rows=256)
    return (y_p, y_s, wkv_p, shift_p, pool_p, wkv_s, shift_s, pool_s)
```

```python
import functools

import jax
import jax.numpy as jnp
import numpy as np
from jax import lax
from jax.experimental import pallas as pl
from jax.experimental.pallas import tpu as pltpu

F32 = jnp.float32
BF16 = jnp.bfloat16

D_MODEL = 2048
DEPTH = 4
PAST_LEN = 16384
RWKV_WIDTH = 1024
HEAD_DIM = 64
N_HEADS = 16
POOL_WIDTH = 1024
POOL_WINDOWS = (2, 4, 8, 16)
POOL_GROUP = 256
POOL_BUF = 15
PLE_DIM = 256
DECAY_RANK = 96
AAA_RANK = 96
MV_RANK = 64
LORA_COLS = DECAY_RANK + AAA_RANK + MV_RANK
IN_COLS = 10240
RMS_EPS = 1e-6
GN_EPS = 64e-5
DECAY_SCALE = 0.606531

HEADS_PER_GROUP = 4
GROUP_LANES = HEADS_PER_GROUP * HEAD_DIM
N_GROUPS = N_HEADS // HEADS_PER_GROUP
CHUNK = 64
EXPANDED = HEADS_PER_GROUP * CHUNK
F32_ROWS = 8
BF16_ROWS = 16
HALO = 16
PROJ_TN = 2 * LORA_COLS
N_MAIN_TILES = IN_COLS // PROJ_TN
PREV_COLS = 3 * RWKV_WIDTH + PROJ_TN
LB_OFF = 3 * RWKV_WIDTH + LORA_COLS

NORM_ROWS = 512
OUT_SUB_ROWS = 128

VMEM_LIMIT_BYTES = 56 * 1024 * 1024


def _sigmoid(x):
    return 0.5 * jnp.tanh(0.5 * x) + 0.5


def _split_bf16(x):
    hi = x.astype(BF16)
    lo = (x - hi.astype(F32)).astype(BF16)
    return hi, lo


_NN = (((1,), (0,)), ((), ()))
_NT = (((1,), (1,)), ((), ()))


def _dot(a, b, dims=_NN):
    return lax.dot_general(a, b, dimension_numbers=dims, preferred_element_type=F32)


def _segsum(xs, bones_b):
    rows = xs[0].shape[0]
    s = _dot(jnp.concatenate([x.astype(BF16) for x in xs], axis=0), bones_b)
    return [s[i * rows:(i + 1) * rows] for i in range(len(xs))]


def _proj_kernel(h_ref, w_ref, wl_ref, z_ref, *, n_main):
    j = pl.program_id(1)

    @pl.when(j < n_main)
    def _():
        z_ref[...] = jnp.dot(h_ref[...], w_ref[...].astype(BF16), preferred_element_type=F32).astype(z_ref.dtype)

    @pl.when(j == n_main)
    def _():
        z_ref[...] = jnp.dot(h_ref[...], wl_ref[...], preferred_element_type=F32).astype(z_ref.dtype)


def _proj(h2d, w_in, layer, w_lora, *, tm, n_main, out_dtype):
    m, d = h2d.shape
    tn = PROJ_TN
    assert m % tm == 0
    return pl.pallas_call(
        functools.partial(_proj_kernel, n_main=n_main),
        grid=(m // tm, n_main + 1),
        in_specs=[
            pl.BlockSpec((tm, d), lambda i, j: (i, 0)),
            pl.BlockSpec((None, d, tn), lambda i, j: (layer, 0, jnp.minimum(j, n_main - 1))),
            pl.BlockSpec((d, tn), lambda i, j: (0, 0), pipeline_mode=pl.Buffered(1)),
        ],
        out_specs=pl.BlockSpec((tm, tn), lambda i, j: (i, j)),
        out_shape=jax.ShapeDtypeStruct((m, (n_main + 1) * tn), out_dtype),
        name="proj",
        compiler_params=pltpu.CompilerParams(
            dimension_semantics=("parallel", "arbitrary"), vmem_limit_bytes=VMEM_LIMIT_BYTES),
    )(h2d, w_in, w_lora)


def _rmsnorm(x, gain):
    return x * lax.rsqrt(jnp.mean(x * x, axis=-1, keepdims=True) + RMS_EPS) * gain


def _norm_kernel(x_ref, g_ref, o_ref):
    o_ref[...] = _rmsnorm(x_ref[...], g_ref[...]).astype(o_ref.dtype)


def _norm(x2d, gain, *, rows, out_dtype):
    m, d = x2d.shape
    rows = min(rows, m)
    assert m % rows == 0
    return pl.pallas_call(
        _norm_kernel,
        grid=(m // rows,),
        in_specs=[pl.BlockSpec((rows, d), lambda i: (i, 0)), pl.BlockSpec((1, d), lambda i: (0, 0))],
        out_specs=pl.BlockSpec((rows, d), lambda i: (i, 0)),
        out_shape=jax.ShapeDtypeStruct((m, d), out_dtype),
        name="norm",
        compiler_params=pltpu.CompilerParams(dimension_semantics=("parallel",)),
    )(x2d, gain.reshape(1, d))


_V_MU_R, _V_MU_K, _V_MU_V, _V_W0, _V_A0, _V_V0, _V_KK, _V_KA, _V_RK, _V_LNW, _V_LNB = range(11)
_N_VEC_ROWS = 16


def _wkv_constants(n_seq, t_seq):
    c, e = CHUNK, EXPANDED
    assert n_seq * t_seq == c
    row = np.arange(c)
    q_of = row // t_seq
    tri = (row[:, None] >= row[None, :]) & (q_of[:, None] == q_of[None, :])
    erow = np.arange(e)
    es = erow % c
    same = q_of[:, None] == (es // t_seq)[None, :]
    mask_s = same & (row[:, None] > es[None, :])
    mask_i = same & (row[:, None] >= es[None, :])
    mask_sc = np.block([[mask_s, mask_s], [mask_i, mask_i]])
    lane = np.arange(GROUP_LANES)
    bones = (lane[:, None] // HEAD_DIM) == (lane[None, :] // HEAD_DIM)
    levels = [np.eye(e, dtype=bool), (erow[:, None] // 2) == (erow[None, :] // 2)]
    m = 2
    while m < t_seq:
        levels.append(((erow[:, None] // (2 * m)) == (erow[None, :] // (2 * m)))
                      & ((erow[:, None] % (2 * m)) >= m) & ((erow[None, :] % (2 * m)) < m))
        m *= 2
    col = np.arange(2 * c)
    qmask = ((col[None, :] % c) // t_seq) == np.arange(n_seq)[:, None]
    f = lambda a: jnp.asarray(a.astype(np.float32))
    return dict(tri=f(tri).astype(BF16), mask_sc=f(mask_sc).astype(BF16), bones=f(bones),
                bones_b=f(bones).astype(BF16), levels=f(np.stack(levels)).astype(BF16),
                qmask=f(qmask).reshape(n_seq, 1, 2 * c))


def _wkv_kernel(*refs, n_seq, t_seq, n_sub, chain, n_steps, has_vres, n_levels):
    it = iter(refs)
    zr_ref, zk_ref, zv_ref, zg_ref, zl_ref, p0_ref = (next(it) for _ in range(6))
    vfirst_ref = next(it) if has_vres else None
    w2_ref, vec_ref, s0_ref = next(it), next(it), next(it)
    tri_ref, msc_ref, bones_ref, bonesb_ref, lev_ref, qmask_ref = (next(it) for _ in range(6))
    ya_ref = next(it)
    vout_ref = None if has_vres else next(it)
    sout_ref = next(it)
    sbd_ref, prev_ref = next(it), next(it)

    step = pl.program_id(1)
    gl, rw, e = GROUP_LANES, RWKV_WIDTH, EXPANDED
    n_slots = n_seq if chain else n_sub * n_seq
    slot = (lambda j, q: q) if chain else (lambda j, q: j * n_seq + q)
    bones = bones_ref[...]
    bones_b = bonesb_ref[...]
    mask_sc = msc_ref[...]
    eye_b = lev_ref[0]
    tri = tri_ref[...]
    vec = vec_ref[...]
    vrow = lambda i: vec[i:i + 1, :]
    groups = range(N_GROUPS)
    sls = [slice(gi * gl, (gi + 1) * gl) for gi in groups]

    def expand(x):
        return jnp.concatenate([x] * HEADS_PER_GROUP, axis=0) * bones_b

    low_lanes = lax.broadcasted_iota(jnp.int32, (1, 2 * HEAD_DIM), 1) < HEAD_DIM

    def head_slot(hd):
        h = hd % HEADS_PER_GROUP
        return hd // HEADS_PER_GROUP, h * HEAD_DIM, h // 2, h % 2 == 1

    def load_state(s, g):
        return jnp.concatenate([sbd_ref[s, g, 0], sbd_ref[s, g, 1]], axis=1)

    def store_state(s, g, val):
        sbd_ref[s, g, 0] = val[:, :2 * HEAD_DIM]
        sbd_ref[s, g, 1] = val[:, 2 * HEAD_DIM:]

    @pl.when(step == 0)
    def _():
        for s in range(n_slots):
            for hd in range(N_HEADS):
                g, row0, half, upper = head_slot(hd)
                packed = s0_ref[s, hd]
                even = jnp.where(low_lanes, packed, 0.0)
                odd = jnp.where(low_lanes, 0.0, packed)
                if upper:
                    even = pltpu.roll(even, HEAD_DIM, axis=1)
                else:
                    odd = pltpu.roll(odd, HEAD_DIM, axis=1)
                sbd_ref[s, g, half, pl.ds(row0, HEAD_DIM // 2, stride=2), :] = even
                sbd_ref[s, g, half, pl.ds(row0 + 1, HEAD_DIM // 2, stride=2), :] = odd
                sbd_ref[s, g, 1 - half, row0:row0 + HEAD_DIM, :] = jnp.zeros((HEAD_DIM, 2 * HEAD_DIM), F32)
        prev_ref[...] = p0_ref[0]

    row = lax.broadcasted_iota(jnp.int32, (CHUNK, 1), 0)
    is_first = (row % t_seq) == 0
    lora_lane = lax.broadcasted_iota(jnp.int32, (1, LORA_COLS), 1)

    def shifted(z, p0):
        rolled = pltpu.roll(z, 1, axis=0)
        if n_seq == 1:
            first = jnp.broadcast_to(p0, z.shape)
        else:
            first = jnp.broadcast_to(p0[:, None, :], (n_seq, t_seq, z.shape[1])).reshape(z.shape)
        return jnp.where(is_first, first, rolled)

    n_rows = n_sub * CHUNK
    zl_all = zl_ref[...].astype(F32)
    lb_all = zl_all[:, LORA_COLS:]
    p_l = prev_ref[:, LB_OFF:LB_OFF + LORA_COLS]
    if n_slots == 1:
        lb_first = jnp.broadcast_to(p_l, lb_all.shape)
    else:
        lb_first = jnp.broadcast_to(p_l[:, None, :], (n_slots, t_seq, LORA_COLS)).reshape(lb_all.shape)
    row_all = lax.broadcasted_iota(jnp.int32, (n_rows, 1), 0)
    lb_prev = jnp.where(row_all % (n_rows if chain else t_seq) == 0, lb_first, pltpu.roll(lb_all, 1, axis=0))
    lin = zl_all[:, :LORA_COLS] + lb_prev
    lin = jnp.where(lora_lane < DECAY_RANK, jnp.tanh(lin), lin).astype(BF16)
    d_full = jnp.dot(lin, w2_ref[...], preferred_element_type=F32)
    lb_last = lb_all[n_rows - 1:n_rows, :]

    ctx = [dict() for _ in range(n_sub)]

    def prep(j):
        c = ctx[j]
        rows = slice(j * CHUNK, (j + 1) * CHUNK)
        zr, zk, zv = zr_ref[rows, :].astype(F32), zk_ref[rows, :].astype(F32), zv_ref[rows, :].astype(F32)
        c["last"] = [x[CHUNK - 1:CHUNK, :] for x in (zr, zk, zv)]
        if chain and j > 0:
            p_r, p_k, p_v = ctx[j - 1]["last"]
        else:
            ps = slice(0, n_seq) if chain else slice(j * n_seq, (j + 1) * n_seq)
            p_r, p_k, p_v = prev_ref[ps, 0:rw], prev_ref[ps, rw:2 * rw], prev_ref[ps, 2 * rw:3 * rw]
        r = zr + (shifted(zr, p_r) - zr) * vrow(_V_MU_R)
        k = zk + (shifted(zk, p_k) - zk) * vrow(_V_MU_K)
        v = zv + (shifted(zv, p_v) - zv) * vrow(_V_MU_V)
        d_all = d_full[rows]
        wlog = -DECAY_SCALE * _sigmoid(vrow(_V_W0) + d_all[:, 0:rw])
        alr = _sigmoid(vrow(_V_A0) + d_all[:, rw:2 * rw])
        if has_vres:
            v = v + (vfirst_ref[rows, :] - v) * _sigmoid(vrow(_V_V0) + d_all[:, 2 * rw:3 * rw])
        else:
            vout_ref[rows, :] = v
        kkr = k * vrow(_V_KK)
        kf = k * (1.0 + (alr - 1.0) * vrow(_V_KA))
        c["rkr"] = r * kf * vrow(_V_RK)
        g_all = zg_ref[rows, :].astype(F32)
        c["gate"] = g_all * _sigmoid(g_all)
        w_hi, w_lo = _split_bf16(wlog)
        cum = jnp.dot(tri, w_hi, preferred_element_type=F32) + jnp.dot(tri, w_lo, preferred_element_type=F32)
        p_inc = jnp.exp(cum)
        inv_p = jnp.exp(-cum)
        p_exc = jnp.exp(cum - wlog)
        c["p_inc"] = p_inc
        yield
        kk_n2 = _segsum([kkr[:, sl] * kkr[:, sl] for sl in sls], bones_b)
        kk = [kkr[:, sl] * lax.rsqrt(jnp.maximum(n2, 1e-24)) for sl, n2 in zip(sls, kk_n2)]
        c["v_g"] = [v[:, sl] for sl in sls]
        c["a_f"] = [-kk[gi] * p_exc[:, sls[gi]] for gi in groups]
        c["r_f"] = [r[:, sl] * p_inc[:, sl] for sl in sls]
        a_t = [x.astype(BF16) for x in c["a_f"]]
        r_t = [x.astype(BF16) for x in c["r_f"]]
        c["b_t"] = [(kk[gi] * alr[:, sls[gi]] * inv_p[:, sls[gi]]).astype(BF16) for gi in groups]
        c["k_t"] = [(kf[:, sl] * inv_p[:, sl]).astype(BF16) for sl in sls]
        c["v_e"] = [expand(x.astype(BF16)) for x in c["v_g"]]
        c["ar"] = [jnp.concatenate([a_t[gi], r_t[gi]], axis=0) for gi in groups]
        yield
        sc = [_dot(c["ar"][gi], jnp.concatenate([expand(c["b_t"][gi]), expand(c["k_t"][gi])], axis=0), _NT)
              for gi in groups]
        c["scm"] = [x.astype(BF16) * mask_sc for x in sc]
        c["n_bd"] = [expand(x[:CHUNK, :e]) for x in c["scm"]]
        c["xs"] = [eye_b + n * lev_ref[1] for n in c["n_bd"]]
        yield

    def inverse(j):
        c = ctx[j]
        m = 2
        xs32 = None
        for lvl in range(2, n_levels):
            n_c = [c["n_bd"][gi] * lev_ref[lvl] for gi in groups]
            if m % F32_ROWS == 0:
                wide = m % BF16_ROWS != 0
                src = xs32 if wide else c["xs"]
                n_blk = e // (2 * m)
                lower = lambda x: jnp.concatenate([x[b * 2 * m + m:(b + 1) * 2 * m] for b in range(n_blk)], axis=0)
                eye_l = lower(eye_b.astype(src[0].dtype)).astype(BF16)
                t1 = [_dot(lower(src[gi]).astype(BF16), n_c[gi]).astype(BF16) + eye_l for gi in groups]
                yield
                new = [_dot(t1[gi], c["xs"][gi]).astype(src[0].dtype) for gi in groups]
                c["xs"] = [jnp.concatenate(
                    [piece for b in range(n_blk)
                     for piece in (x[b * 2 * m:b * 2 * m + m], n[b * m:(b + 1) * m])], axis=0).astype(BF16)
                    for x, n in zip(src, new)]
                xs32 = None
            else:
                t1 = [_dot(c["xs"][gi], n_c[gi]).astype(BF16) + eye_b for gi in groups]
                yield
                full = [_dot(t1[gi], c["xs"][gi]) for gi in groups]
                if (2 * m) % F32_ROWS == 0 and (2 * m) % BF16_ROWS != 0:
                    xs32 = full
                c["xs"] = [x.astype(BF16) for x in full]
            yield
            m *= 2

    def tail(j):
        c = ctx[j]
        rows = slice(j * CHUNK, (j + 1) * CHUNK)
        x_c = [(x[0:CHUNK] + x[CHUNK:2 * CHUNK]) + (x[2 * CHUNK:3 * CHUNK] + x[3 * CHUNK:4 * CHUNK])
               for x in c["xs"]]
        a_s, r_s = [], []
        for gi in groups:
            parts_a, parts_r = [], []
            for q in range(n_seq):
                lo_, hi_ = q * t_seq, (q + 1) * t_seq
                lhs = c["ar"][gi] if n_seq == 1 else jnp.concatenate(
                    [c["a_f"][gi][lo_:hi_], c["r_f"][gi][lo_:hi_]], axis=0).astype(BF16)
                res = _dot(lhs, load_state(slot(j, q), gi).astype(BF16), _NT)
                parts_a.append(res[:t_seq])
                parts_r.append(res[t_seq:])
            a_s.append(parts_a[0] if n_seq == 1 else jnp.concatenate(parts_a, axis=0))
            r_s.append(parts_r[0] if n_seq == 1 else jnp.concatenate(parts_r, axis=0))
        yield
        y = [a_s[gi] + _dot(c["scm"][gi][:CHUNK, e:], c["v_e"][gi]) for gi in groups]
        yield
        u = [_dot(x_c[gi], expand(y[gi].astype(BF16))) for gi in groups]
        yield
        o = [r_s[gi] + _dot(c["scm"][gi][CHUNK:, :],
                            jnp.concatenate([expand(u[gi].astype(BF16)), c["v_e"][gi]], axis=0)) for gi in groups]
        yield
        for gi in groups:
            uv_t = jnp.concatenate([u[gi], c["v_g"][gi]], axis=0).T
            bk = jnp.concatenate([c["b_t"][gi], c["k_t"][gi]], axis=0)
            for q in range(n_seq):
                lhs = uv_t if n_seq == 1 else uv_t * qmask_ref[q]
                p_last = c["p_inc"][(q + 1) * t_seq - 1:(q + 1) * t_seq, sls[gi]]
                s = slot(j, q)
                store_state(s, gi, bones * ((load_state(s, gi) + _dot(lhs.astype(BF16), bk)) * p_last))
        yield
        mean = [x * (1.0 / HEAD_DIM) for x in _segsum(o, bones_b)]
        d = [o[gi] - mean[gi] for gi in groups]
        var = [x * (1.0 / HEAD_DIM) for x in _segsum([x * x for x in d], bones_b)]
        bonus = _segsum([c["rkr"][:, sl] for sl in sls], bones_b)
        yield
        for gi in groups:
            sl = sls[gi]
            on = d[gi] * lax.rsqrt(var[gi] + GN_EPS) * vec[_V_LNW:_V_LNW + 1, sl] + vec[_V_LNB:_V_LNB + 1, sl]
            ya_ref[rows, sl] = ((on + bonus[gi] * c["v_g"][gi]) * c["gate"][:, sl]).astype(ya_ref.dtype)
        c.clear()
        c["last"] = None

    last_rows = None
    preps, invs, tails = ([f(j) for j in range(n_sub)] for f in (prep, inverse, tail))
    for phase in range(n_sub + 2):
        active = []
        if 0 <= phase - 2 < n_sub:
            active.append(tails[phase - 2])
        if 0 <= phase - 1 < n_sub:
            active.append(invs[phase - 1])
        if phase < n_sub:
            active.append(preps[phase])
        if phase == n_sub - 1 + 1:
            last_rows = ctx[n_sub - 1]["last"]
        while active:
            for gen in list(active):
                try:
                    next(gen)
                except StopIteration:
                    active.remove(gen)

    if chain and n_steps > 1:
        assert n_seq == 1
        prev_ref[:, 0:rw] = last_rows[0]
        prev_ref[:, rw:2 * rw] = last_rows[1]
        prev_ref[:, 2 * rw:3 * rw] = last_rows[2]
        prev_ref[:, LB_OFF:LB_OFF + LORA_COLS] = lb_last

    @pl.when(step == n_steps - 1)
    def _():
        for s in range(n_slots):
            for hd in range(N_HEADS):
                g, row0, half, upper = head_slot(hd)
                even = sbd_ref[s, g, half, pl.ds(row0, HEAD_DIM // 2, stride=2), :]
                odd = sbd_ref[s, g, half, pl.ds(row0 + 1, HEAD_DIM // 2, stride=2), :]
                if upper:
                    even = pltpu.roll(even, HEAD_DIM, axis=1)
                else:
                    odd = pltpu.roll(odd, HEAD_DIM, axis=1)
                sout_ref[s, hd] = jnp.where(low_lanes, even, odd)


def _wkv(z, prev0, vfirst, w2e, vecs, s_all, layer, *, n_batch, t_len, n_seq, n_sub):
    m = z.shape[0]
    t_seq = CHUNK // n_seq
    n_chunks = t_len // t_seq
    chain = n_seq == 1
    if chain:
        assert n_chunks % n_sub == 0
        n_steps, nb, n_slots = n_chunks // n_sub, n_batch, n_seq
    else:
        assert n_chunks == 1 and n_batch % (n_seq * n_sub) == 0
        n_steps, nb, n_slots = 1, n_batch // (n_seq * n_sub), n_seq * n_sub
    has_vres = vfirst is not None
    consts = _wkv_constants(n_seq, t_seq)
    n_levels = consts["levels"].shape[0]
    rw = RWKV_WIDTH
    rows = n_sub * CHUNK
    rb = lambda b, s: b * n_steps + s
    prev0 = prev0.reshape(nb, n_slots, PREV_COLS)

    zspec = lambda col: pl.BlockSpec((rows, rw), lambda b, s: (rb(b, s), col))
    full = lambda a: pl.BlockSpec(a.shape, lambda b, s: (0,) * a.ndim)
    sspec = pl.BlockSpec((None, n_slots, N_HEADS, HEAD_DIM // 2, 2 * HEAD_DIM), lambda b, s: (layer, b, 0, 0, 0))

    in_specs = [zspec(0), zspec(1), zspec(2), zspec(3),
                pl.BlockSpec((rows, PROJ_TN), lambda b, s: (rb(b, s), N_MAIN_TILES)),
                pl.BlockSpec((1, n_slots, PREV_COLS), lambda b, s: (b, 0, 0))]
    args = [z, z, z, z, z, prev0]
    if has_vres:
        in_specs.append(zspec(0))
        args.append(vfirst)
    else:
        w2e = w2e[:, :2 * rw]
    in_specs += [full(w2e), full(vecs), sspec]
    args += [w2e, vecs, s_all]
    state_arg = len(args) - 1
    for name in ("tri", "mask_sc", "bones", "bones_b", "levels", "qmask"):
        in_specs.append(full(consts[name]))
        args.append(consts[name])

    out_specs = [zspec(0)]
    out_shape = [jax.ShapeDtypeStruct((m, rw), BF16)]
    if not has_vres:
        out_specs.append(zspec(0))
        out_shape.append(jax.ShapeDtypeStruct((m, rw), F32))
    out_specs.append(sspec)
    out_shape.append(jax.ShapeDtypeStruct(s_all.shape, F32))

    outs = pl.pallas_call(
        functools.partial(_wkv_kernel, n_seq=n_seq, t_seq=t_seq, n_sub=n_sub, chain=chain, n_steps=n_steps,
                          has_vres=has_vres, n_levels=n_levels),
        grid=(nb, n_steps),
        in_specs=in_specs, out_specs=out_specs, out_shape=out_shape,
        input_output_aliases={state_arg: len(out_shape) - 1},
        scratch_shapes=[pltpu.VMEM((n_slots, N_GROUPS, 2, GROUP_LANES, GROUP_LANES // 2), F32),
                        pltpu.VMEM((n_slots, PREV_COLS), F32)],
        name="wkv",
        compiler_params=pltpu.CompilerParams(
            dimension_semantics=("parallel", "arbitrary"), vmem_limit_bytes=VMEM_LIMIT_BYTES),
    )(*args)
    if has_vres:
        ya, s_new = outs
        return ya, vfirst, s_new
    return outs


def _mix_kernel(*refs, n_seq, t_seq, tiles_per_seq, n_past, carry_pool):
    if carry_pool:
        (ya_ref, u_ref, gp_ref, ga_ref, gb_ref, pool_ref, perm_ref, permt_ref, pw_ref, ps_ref, wb0_ref, wb1_ref,
         m_ref, pool_out_ref) = refs
    else:
        ya_ref, u_ref, gp_ref, ga_ref, gb_ref, halo_ref, pw_ref, ps_ref, wb0_ref, wb1_ref, m_ref = refs
    rows = n_seq * t_seq
    tile = pl.program_id(0) % tiles_per_seq
    if carry_pool:
        perm = perm_ref[...]
        u_t = jnp.dot(perm, u_ref[...], preferred_element_type=F32).reshape(t_seq, n_seq, POOL_WIDTH)
        gp_all = jnp.dot(perm, gp_ref[...], preferred_element_type=F32)
        e = jnp.concatenate([jnp.zeros((HALO - POOL_BUF, n_seq, POOL_WIDTH), F32), pool_ref[...], u_t], axis=0)
        pool_out_ref[...] = e[HALO + t_seq - POOL_BUF:]
        t_axis = 0
        pos = lax.broadcasted_iota(jnp.int32, (t_seq, 1, 1), 0) + (1 + n_past)
    else:
        u = u_ref[...].astype(F32)
        halo = jnp.where(tile == 0, 0.0, halo_ref[...].astype(F32)).reshape(1, HALO, POOL_WIDTH)
        e = jnp.concatenate([halo, u.reshape(n_seq, t_seq, POOL_WIDTH)], axis=1)
        t_axis = 1
        pos = lax.broadcasted_iota(jnp.int32, (1, t_seq, 1), 1) + (tile * t_seq + 1 + n_past)
    take = lambda x, start, size: lax.slice_in_dim(x, start, start + size, axis=t_axis)
    n_col = len(POOL_WINDOWS)
    cw = D_MODEL // n_col
    cols = [slice(ci * cw, (ci + 1) * cw) for ci in range(n_col)]
    ya = ya_ref[...]
    gated_a, yb_parts = [], []
    for gi, win in enumerate(POOL_WINDOWS):
        pa = jnp.dot(ya, wb0_ref[:, cols[gi]], preferred_element_type=F32)
        sl = slice(gi * POOL_GROUP, (gi + 1) * POOL_GROUP)
        e_g = e[:, :, sl]
        acc = e_g
        span, length = 1, HALO + t_seq
        while span < win:
            length -= span
            acc = take(acc, span, length) + take(acc, 0, length)
            span *= 2
        ws = take(acc, length - t_seq, t_seq)
        inv_cnt = 1.0 / jnp.minimum(pos, win).astype(F32)
        mixed = (ws * inv_cnt - take(e_g, HALO, t_seq)).reshape(rows, POOL_GROUP)
        y_g = jnp.dot(mixed.astype(BF16), pw_ref[gi], preferred_element_type=F32)
        gp = gp_all[:, sl] if carry_pool else gp_ref[:, sl].astype(F32)
        yb_parts.append((y_g * ps_ref[:, sl] * (gp * _sigmoid(gp))).astype(BF16))
        gated_a.append(_sigmoid(ga_ref[:, cols[gi]].astype(F32)) * pa)
    y_b = jnp.concatenate(yb_parts, axis=1)
    if carry_pool:
        y_b = jnp.dot(permt_ref[...], y_b, preferred_element_type=F32).astype(BF16)
    pb = [None] * n_col
    pb[0] = jnp.dot(y_b, wb1_ref[:, cols[0]], preferred_element_type=F32)
    for ci in range(n_col):
        if ci + 1 < n_col:
            pb[ci + 1] = jnp.dot(y_b, wb1_ref[:, cols[ci + 1]], preferred_element_type=F32)
        m_ref[:, cols[ci]] = (gated_a[ci] + _sigmoid(gb_ref[:, cols[ci]].astype(F32)) * pb[ci]).astype(m_ref.dtype)


def _mix(ya, z, pool_all, pool_w_b, pool_scale, w_branch_b, layer, *, n_seq, t_seq, tiles_per_seq, n_past):
    m = z.shape[0]
    rows = n_seq * t_seq
    carry_pool = pool_all is not None
    in_specs = [
        pl.BlockSpec((rows, RWKV_WIDTH), lambda i: (i, 0)),
        pl.BlockSpec((rows, POOL_WIDTH), lambda i: (i, 4)),
        pl.BlockSpec((rows, POOL_WIDTH), lambda i: (i, 5)),
        pl.BlockSpec((rows, D_MODEL), lambda i: (i, 3)),
        pl.BlockSpec((rows, D_MODEL), lambda i: (i, 4)),
    ]
    args = [ya, z, z, z, z]
    out_specs = [pl.BlockSpec((rows, D_MODEL), lambda i: (i, 0))]
    out_shape = [jax.ShapeDtypeStruct((m, D_MODEL), BF16)]
    aliases = {}
    if carry_pool:
        assert tiles_per_seq == 1 and t_seq < POOL_BUF
        r = np.arange(rows)
        perm = np.zeros((rows, rows), np.float32)
        perm[(r % t_seq) * n_seq + r // t_seq, r] = 1.0
        pool_spec = pl.BlockSpec((None, POOL_BUF, n_seq, POOL_WIDTH), lambda i: (layer, 0, i, 0))
        const = pl.BlockSpec((rows, rows), lambda i: (0, 0))
        in_specs += [pool_spec, const, const]
        aliases = {len(args): 1}
        args += [pool_all, jnp.asarray(perm, BF16), jnp.asarray(perm.T, BF16)]
        out_specs.append(pool_spec)
        out_shape.append(jax.ShapeDtypeStruct(pool_all.shape, F32))
    else:
        assert n_seq == 1 and rows % HALO == 0
        in_specs.append(pl.BlockSpec((HALO, POOL_WIDTH), lambda i: (jnp.maximum(i * (rows // HALO) - 1, 0), 4)))
        args.append(z)
    in_specs += [
        pl.BlockSpec((None, len(POOL_WINDOWS), POOL_GROUP, POOL_GROUP), lambda i: (layer, 0, 0, 0)),
        pl.BlockSpec((None, 1, POOL_WIDTH), lambda i: (layer, 0, 0)),
        pl.BlockSpec((None, None, RWKV_WIDTH, D_MODEL), lambda i: (layer, 0, 0, 0), pipeline_mode=pl.Buffered(1)),
        pl.BlockSpec((None, None, POOL_WIDTH, D_MODEL), lambda i: (layer, 1, 0, 0), pipeline_mode=pl.Buffered(1)),
    ]
    args += [pool_w_b, pool_scale.reshape(DEPTH, 1, POOL_WIDTH), w_branch_b, w_branch_b]
    outs = pl.pallas_call(
        functools.partial(_mix_kernel, n_seq=n_seq, t_seq=t_seq, tiles_per_seq=tiles_per_seq, n_past=n_past,
                          carry_pool=carry_pool),
        grid=(m // rows,),
        in_specs=in_specs, out_specs=out_specs, out_shape=out_shape,
        input_output_aliases=aliases,
        name="mix",
        compiler_params=pltpu.CompilerParams(
            dimension_semantics=("parallel",), vmem_limit_bytes=VMEM_LIMIT_BYTES),
    )(*args)
    return outs if carry_pool else (outs[0], None)


def _out_kernel(*refs, n_split, emit_h):
    if emit_h:
        m_ref, x_ref, p_ref, g_ref, gn_ref, wo_ref, wg_ref, wp_ref, o_ref, h_ref = refs
    else:
        m_ref, x_ref, p_ref, g_ref, wo_ref, wg_ref, wp_ref, o_ref = refs
    rows = m_ref.shape[0] // n_split
    parts = [slice(i * rows, (i + 1) * rows) for i in range(n_split)]
    outs = [jnp.dot(m_ref[sl, :], wo_ref[...], preferred_element_type=F32) for sl in parts]
    ples = [jnp.dot(p_ref[sl, :].astype(BF16), wp_ref[...], preferred_element_type=F32) for sl in parts]
    xs = [x_ref[sl, :] + _rmsnorm(o, g_ref[...]) for sl, o in zip(parts, outs)]
    gates = [_sigmoid(jnp.dot(x.astype(BF16), wg_ref[...], preferred_element_type=F32)) for x in xs]
    for sl, x, gate, ple in zip(parts, xs, gates, ples):
        x_new = x + gate * ple
        o_ref[sl, :] = x_new
        if emit_h:
            h_ref[sl, :] = _rmsnorm(x_new, gn_ref[...]).astype(h_ref.dtype)


def _out(mm_, x2d, p3d, norm_post, norm_pre, w_out_b, w_gate_b, w_ple_b, layer, *, rows):
    m = x2d.shape[0]
    emit_h = layer + 1 < DEPTH
    once = pl.Buffered(1)
    wspec = lambda k, l=layer: pl.BlockSpec((None, k, D_MODEL), lambda i: (l, 0, 0), pipeline_mode=once)
    row_spec = pl.BlockSpec((rows, D_MODEL), lambda i: (i, 0))
    in_specs = [row_spec, row_spec, pl.BlockSpec((None, rows, PLE_DIM), lambda i: (layer, i, 0)), wspec(1)]
    args = [mm_, x2d, p3d, norm_post.reshape(DEPTH, 1, D_MODEL)]
    out_specs, out_shape = [row_spec], [jax.ShapeDtypeStruct((m, D_MODEL), F32)]
    if emit_h:
        in_specs.append(wspec(1, layer + 1))
        args.append(norm_pre.reshape(DEPTH, 1, D_MODEL))
        out_specs.append(row_spec)
        out_shape.append(jax.ShapeDtypeStruct((m, D_MODEL), BF16))
    in_specs += [wspec(D_MODEL), wspec(D_MODEL), wspec(PLE_DIM)]
    args += [w_out_b, w_gate_b, w_ple_b]
    outs = pl.pallas_call(
        functools.partial(_out_kernel, n_split=rows // OUT_SUB_ROWS, emit_h=emit_h),
        grid=(m // rows,),
        in_specs=in_specs, out_specs=out_specs, out_shape=out_shape,
        name="out",
        compiler_params=pltpu.CompilerParams(
            dimension_semantics=("parallel",), vmem_limit_bytes=VMEM_LIMIT_BYTES),
    )(*args)
    return (outs[0], outs[1]) if emit_h else (outs[0], None)


def _pair_rows(s):
    l, b = s.shape[:2]
    return s.reshape(l, b, N_HEADS, HEAD_DIM // 2, 2 * HEAD_DIM)


def _unpair_rows(s):
    l, b = s.shape[:2]
    return s.reshape(l, b, N_HEADS, HEAD_DIM, HEAD_DIM)


def _layer_weights(i, mu_rkv, mu_lora, w0, w1, w2, a0, a1, a2, v0, v1, v2, k_k, k_a, r_k, ln_w, ln_b):
    r_ = RWKV_WIDTH
    mu = mu_lora[i]
    if i > 0:
        v1_i, v2_i, v0_i = v1[i - 1], v2[i - 1], v0[i - 1]
    else:
        v1_i = jnp.zeros((D_MODEL, MV_RANK), F32)
        v2_i = jnp.zeros((MV_RANK, r_), F32)
        v0_i = jnp.zeros((r_,), F32)
    w_lora = jnp.concatenate(
        [(1.0 - mu[0])[:, None] * w1[i], (1.0 - mu[1])[:, None] * a1[i], (1.0 - mu[2])[:, None] * v1_i,
         mu[0][:, None] * w1[i], mu[1][:, None] * a1[i], mu[2][:, None] * v1_i], axis=1).astype(BF16)
    z96 = jnp.zeros((DECAY_RANK, r_), F32)
    z64 = jnp.zeros((MV_RANK, r_), F32)
    w2e = jnp.concatenate([
        jnp.concatenate([w2[i], z96, z64], axis=0),
        jnp.concatenate([z96, a2[i], z64], axis=0),
        jnp.concatenate([z96, z96, v2_i], axis=0)], axis=1).astype(BF16)
    rows = [mu_rkv[i, 0], mu_rkv[i, 1], mu_rkv[i, 2], w0[i], a0[i], v0_i, k_k[i], k_a[i], r_k[i], ln_w[i], ln_b[i]]
    vecs = jnp.concatenate([jnp.stack(rows), jnp.zeros((_N_VEC_ROWS - len(rows), r_), F32)], axis=0)
    return w_lora, w2e, vecs


def _run_group(x, p, wkv, shift, pool, n_past, lw, params, *, n_seq, n_sub, proj_tm, mix_seq, mix_t, out_rows):
    (w_in, norm_pre, norm_post, pool_w_b, pool_scale, w_branch_b, w_out_b, w_ple_b, w_gate_b) = params
    b, t, d = x.shape
    m = b * t
    x2d = x.reshape(m, d)
    p3d = p.reshape(DEPTH, m, PLE_DIM)
    zero_state = wkv is None
    shift_out, pool_out = [], []
    v_first = None
    tiles_per_seq = t // mix_t
    s_all = _pair_rows(jnp.zeros((DEPTH, b, N_HEADS, HEAD_DIM, HEAD_DIM), F32) if zero_state else wkv)
    pool_all = None if zero_state else jnp.transpose(pool, (0, 2, 1, 3))
    h2d = _norm(x2d, norm_pre[0], rows=NORM_ROWS, out_dtype=BF16)
    for i in range(DEPTH):
        w_lora, w2e, vecs = lw[i]
        z = _proj(h2d, w_in, i, w_lora, tm=proj_tm, n_main=N_MAIN_TILES, out_dtype=BF16)
        shift_out.append(_norm(x2d.reshape(b, t, d)[:, -1], norm_pre[i], rows=b, out_dtype=F32))
        if zero_state:
            assert t >= POOL_BUF
            prev0 = jnp.zeros((b, PREV_COLS), F32)
            pool_out.append(z.reshape(b, t, -1)[:, -POOL_BUF:, 4 * RWKV_WIDTH:4 * RWKV_WIDTH + POOL_WIDTH].astype(F32))
        else:
            prev0 = _proj(shift[i].astype(BF16), w_in, i, w_lora, tm=b, n_main=3 * RWKV_WIDTH // PROJ_TN,
                          out_dtype=F32)
        ya, v_first, s_all = _wkv(z, prev0, v_first, w2e, vecs, s_all, i, n_batch=b, t_len=t, n_seq=n_seq,
                                  n_sub=n_sub)
        mm_, pool_all = _mix(ya, z, pool_all, pool_w_b, pool_scale, w_branch_b, i,
                             n_seq=mix_seq, t_seq=mix_t, tiles_per_seq=tiles_per_seq, n_past=n_past)
        x2d, h2d = _out(mm_, x2d, p3d, norm_post, norm_pre, w_out_b, w_gate_b, w_ple_b, i, rows=out_rows)
    pool_new = jnp.stack(pool_out) if zero_state else jnp.transpose(pool_all, (0, 2, 1, 3))
    return x2d.reshape(b, t, d), _unpair_rows(s_all), jnp.stack(shift_out), pool_new


def kernel(x_prompt, x_sample, state_wkv, state_shift, state_pool, p_prompt, p_sample, norm_pre, norm_post, w_in, mu_rkv, mu_lora, w0, w1, w2, a0, a1, a2, v0, v1, v2, k_k, k_a, r_k, ln_w, ln_b, pool_w, pool_scale, w_branch, w_out, w_ple, w_ple_gate):
    lw = [_layer_weights(i, mu_rkv, mu_lora, w0, w1, w2, a0, a1, a2, v0, v1, v2, k_k, k_a, r_k, ln_w, ln_b)
          for i in range(DEPTH)]
    params = (w_in, norm_pre, norm_post, pool_w.astype(BF16), pool_scale, w_branch.astype(BF16),
              w_out.astype(BF16), w_ple.astype(BF16), w_ple_gate.astype(BF16))
    y_p, wkv_p, shift_p, pool_p = _run_group(
        x_prompt, p_prompt, None, None, None, 0, lw, params,
        n_seq=1, n_sub=4, proj_tm=2048, mix_seq=1, mix_t=256, out_rows=256)
    y_s, wkv_s, shift_s, pool_s = _run_group(
        x_sample, p_sample, state_wkv, state_shift, state_pool, PAST_LEN, lw, params,
        n_seq=8, n_sub=2, proj_tm=1024, mix_seq=32, mix_t=8, out_rows=256)
    return (y_p, y_s, wkv_p, shift_p, pool_p, wkv_s, shift_s, pool_s)
```

```python
import functools

import jax
import jax.numpy as jnp
import numpy as np
from jax import lax
from jax.experimental import pallas as pl
from jax.experimental.pallas import tpu as pltpu

F32 = jnp.float32
BF16 = jnp.bfloat16

D_MODEL = 2048
DEPTH = 4
PAST_LEN = 16384
RWKV_WIDTH = 1024
HEAD_DIM = 64
N_HEADS = 16
POOL_WIDTH = 1024
POOL_WINDOWS = (2, 4, 8, 16)
POOL_GROUP = 256
POOL_BUF = 15
PLE_DIM = 256
DECAY_RANK = 96
AAA_RANK = 96
MV_RANK = 64
LORA_COLS = DECAY_RANK + AAA_RANK + MV_RANK
IN_COLS = 10240
RMS_EPS = 1e-6
GN_EPS = 64e-5
DECAY_SCALE = 0.606531

HEADS_PER_GROUP = 4
GROUP_LANES = HEADS_PER_GROUP * HEAD_DIM
N_GROUPS = N_HEADS // HEADS_PER_GROUP
CHUNK = 64
EXPANDED = HEADS_PER_GROUP * CHUNK
F32_ROWS = 8
BF16_ROWS = 16
HALO = 16
PROJ_TN = 2 * LORA_COLS
N_MAIN_TILES = IN_COLS // PROJ_TN
Z_R, Z_K, Z_V, Z_G_RWKV, Z_U, Z_G_POOL = range(6)
Z_GATE_A, Z_GATE_B = 3, 4
Z_LORA = N_MAIN_TILES
PREV_COLS = 3 * RWKV_WIDTH + PROJ_TN
LB_OFF = 3 * RWKV_WIDTH + LORA_COLS

NORM_ROWS = 512
OUT_SUB_ROWS = 128

VMEM_LIMIT_BYTES = 56 * 1024 * 1024


def _sigmoid(x):
    return 0.5 * jnp.tanh(0.5 * x) + 0.5


def _split_bf16(x):
    hi = x.astype(BF16)
    lo = (x - hi.astype(F32)).astype(BF16)
    return hi, lo


_NN = (((1,), (0,)), ((), ()))
_NT = (((1,), (1,)), ((), ()))


def _dot(a, b, dims=_NN):
    return lax.dot_general(a, b, dimension_numbers=dims, preferred_element_type=F32)


def _segsum(xs, bones_b):
    rows = xs[0].shape[0]
    s = _dot(jnp.concatenate([x.astype(BF16) for x in xs], axis=0), bones_b)
    return [s[i * rows:(i + 1) * rows] for i in range(len(xs))]


def _proj_kernel(h_ref, w_ref, wl_ref, z_ref, *, n_main):
    j = pl.program_id(1)

    @pl.when(j < n_main)
    def _():
        z_ref[...] = jnp.dot(h_ref[...], w_ref[...].astype(BF16), preferred_element_type=F32).astype(z_ref.dtype)

    @pl.when(j == n_main)
    def _():
        z_ref[...] = jnp.dot(h_ref[...], wl_ref[...], preferred_element_type=F32).astype(z_ref.dtype)


def _proj(h2d, w_in, layer, w_lora, *, tm, n_main, out_dtype):
    m, d = h2d.shape
    tn = PROJ_TN
    assert m % tm == 0
    return pl.pallas_call(
        functools.partial(_proj_kernel, n_main=n_main),
        grid=(m // tm, n_main + 1),
        in_specs=[
            pl.BlockSpec((tm, d), lambda i, j: (i, 0)),
            pl.BlockSpec((None, d, tn), lambda i, j: (layer, 0, jnp.minimum(j, n_main - 1))),
            pl.BlockSpec((d, tn), lambda i, j: (0, 0), pipeline_mode=pl.Buffered(1)),
        ],
        out_specs=pl.BlockSpec((tm, tn), lambda i, j: (i, j)),
        out_shape=jax.ShapeDtypeStruct((m, (n_main + 1) * tn), out_dtype),
        name="proj",
        compiler_params=pltpu.CompilerParams(
            dimension_semantics=("parallel", "arbitrary"), vmem_limit_bytes=VMEM_LIMIT_BYTES),
    )(h2d, w_in, w_lora)


def _rmsnorm(x, gain):
    return x * lax.rsqrt(jnp.mean(x * x, axis=-1, keepdims=True) + RMS_EPS) * gain


def _norm_kernel(x_ref, g_ref, o_ref):
    o_ref[...] = _rmsnorm(x_ref[...], g_ref[...]).astype(o_ref.dtype)


def _norm(x2d, gain, *, rows, out_dtype):
    m, d = x2d.shape
    rows = min(rows, m)
    assert m % rows == 0
    return pl.pallas_call(
        _norm_kernel,
        grid=(m // rows,),
        in_specs=[pl.BlockSpec((rows, d), lambda i: (i, 0)), pl.BlockSpec((1, d), lambda i: (0, 0))],
        out_specs=pl.BlockSpec((rows, d), lambda i: (i, 0)),
        out_shape=jax.ShapeDtypeStruct((m, d), out_dtype),
        name="norm",
        compiler_params=pltpu.CompilerParams(dimension_semantics=("parallel",)),
    )(x2d, gain.reshape(1, d))


_V_MU_R, _V_MU_K, _V_MU_V, _V_W0, _V_A0, _V_V0, _V_KK, _V_KA, _V_RK, _V_LNW, _V_LNB = range(11)
_N_VEC_ROWS = 16


def _wkv_constants(n_seq, t_seq):
    c, e = CHUNK, EXPANDED
    assert n_seq * t_seq == c
    row = np.arange(c)
    q_of = row // t_seq
    tri = (row[:, None] >= row[None, :]) & (q_of[:, None] == q_of[None, :])
    erow = np.arange(e)
    es = erow % c
    same = q_of[:, None] == (es // t_seq)[None, :]
    mask_s = same & (row[:, None] > es[None, :])
    mask_i = same & (row[:, None] >= es[None, :])
    mask_sc = np.block([[mask_s, mask_s], [mask_i, mask_i]])
    lane = np.arange(GROUP_LANES)
    bones = (lane[:, None] // HEAD_DIM) == (lane[None, :] // HEAD_DIM)
    levels = [np.eye(e, dtype=bool), (erow[:, None] // 2) == (erow[None, :] // 2)]
    m = 2
    while m < t_seq:
        levels.append(((erow[:, None] // (2 * m)) == (erow[None, :] // (2 * m)))
                      & ((erow[:, None] % (2 * m)) >= m) & ((erow[None, :] % (2 * m)) < m))
        m *= 2
    col = np.arange(2 * c)
    qmask = ((col[None, :] % c) // t_seq) == np.arange(n_seq)[:, None]
    f = lambda a: jnp.asarray(a.astype(np.float32))
    return dict(tri=f(tri).astype(BF16), mask_sc=f(mask_sc).astype(BF16), bones=f(bones),
                bones_b=f(bones).astype(BF16), levels=f(np.stack(levels)).astype(BF16),
                qmask=f(qmask).reshape(n_seq, 1, 2 * c))


def _wkv_kernel(*refs, n_seq, t_seq, n_sub, chain, n_steps, has_vres, n_levels):
    it = iter(refs)
    zr_ref, zk_ref, zv_ref, zg_ref, zl_ref, p0_ref = (next(it) for _ in range(6))
    vfirst_ref = next(it) if has_vres else None
    w2_ref, vec_ref, s0_ref = next(it), next(it), next(it)
    tri_ref, msc_ref, bones_ref, bonesb_ref, lev_ref, qmask_ref = (next(it) for _ in range(6))
    ya_ref = next(it)
    vout_ref = None if has_vres else next(it)
    sout_ref = next(it)
    sbd_ref, prev_ref = next(it), next(it)

    step = pl.program_id(1)
    gl, rw, e = GROUP_LANES, RWKV_WIDTH, EXPANDED
    n_slots = n_seq if chain else n_sub * n_seq
    slot = (lambda j, q: q) if chain else (lambda j, q: j * n_seq + q)
    bones = bones_ref[...]
    bones_b = bonesb_ref[...]
    mask_sc = msc_ref[...]
    eye_b = lev_ref[0]
    tri = tri_ref[...]
    vec = vec_ref[...]
    vrow = lambda i: vec[i:i + 1, :]
    groups = range(N_GROUPS)
    sls = [slice(gi * gl, (gi + 1) * gl) for gi in groups]

    def expand(x):
        return jnp.concatenate([x] * HEADS_PER_GROUP, axis=0) * bones_b

    low_lanes = lax.broadcasted_iota(jnp.int32, (1, 2 * HEAD_DIM), 1) < HEAD_DIM

    def head_slot(hd):
        h = hd % HEADS_PER_GROUP
        return hd // HEADS_PER_GROUP, h * HEAD_DIM, h // 2, h % 2 == 1

    def load_state(s, g):
        return jnp.concatenate([sbd_ref[s, g, 0], sbd_ref[s, g, 1]], axis=1)

    def store_state(s, g, val):
        sbd_ref[s, g, 0] = val[:, :2 * HEAD_DIM]
        sbd_ref[s, g, 1] = val[:, 2 * HEAD_DIM:]

    @pl.when(step == 0)
    def _():
        for s in range(n_slots):
            for hd in range(N_HEADS):
                g, row0, half, upper = head_slot(hd)
                packed = s0_ref[s, hd]
                even = jnp.where(low_lanes, packed, 0.0)
                odd = jnp.where(low_lanes, 0.0, packed)
                if upper:
                    even = pltpu.roll(even, HEAD_DIM, axis=1)
                else:
                    odd = pltpu.roll(odd, HEAD_DIM, axis=1)
                sbd_ref[s, g, half, pl.ds(row0, HEAD_DIM // 2, stride=2), :] = even
                sbd_ref[s, g, half, pl.ds(row0 + 1, HEAD_DIM // 2, stride=2), :] = odd
                sbd_ref[s, g, 1 - half, row0:row0 + HEAD_DIM, :] = jnp.zeros((HEAD_DIM, 2 * HEAD_DIM), F32)
        prev_ref[...] = p0_ref[0]

    row = lax.broadcasted_iota(jnp.int32, (CHUNK, 1), 0)
    is_first = (row % t_seq) == 0
    lora_lane = lax.broadcasted_iota(jnp.int32, (1, LORA_COLS), 1)

    def shifted(z, p0):
        rolled = pltpu.roll(z, 1, axis=0)
        if n_seq == 1:
            first = jnp.broadcast_to(p0, z.shape)
        else:
            first = jnp.broadcast_to(p0[:, None, :], (n_seq, t_seq, z.shape[1])).reshape(z.shape)
        return jnp.where(is_first, first, rolled)

    n_rows = n_sub * CHUNK
    zl_all = zl_ref[...].astype(F32)
    lb_all = zl_all[:, LORA_COLS:]
    p_l = prev_ref[:, LB_OFF:LB_OFF + LORA_COLS]
    if n_slots == 1:
        lb_first = jnp.broadcast_to(p_l, lb_all.shape)
    else:
        lb_first = jnp.broadcast_to(p_l[:, None, :], (n_slots, t_seq, LORA_COLS)).reshape(lb_all.shape)
    row_all = lax.broadcasted_iota(jnp.int32, (n_rows, 1), 0)
    lb_prev = jnp.where(row_all % (n_rows if chain else t_seq) == 0, lb_first, pltpu.roll(lb_all, 1, axis=0))
    lin = zl_all[:, :LORA_COLS] + lb_prev
    lin = jnp.where(lora_lane < DECAY_RANK, jnp.tanh(lin), lin).astype(BF16)
    d_full = jnp.dot(lin, w2_ref[...], preferred_element_type=F32)
    lb_last = lb_all[n_rows - 1:n_rows, :]

    ctx = [dict() for _ in range(n_sub)]

    def prep(j):
        c = ctx[j]
        rows = slice(j * CHUNK, (j + 1) * CHUNK)
        zr, zk, zv = zr_ref[rows, :].astype(F32), zk_ref[rows, :].astype(F32), zv_ref[rows, :].astype(F32)
        c["last"] = [x[CHUNK - 1:CHUNK, :] for x in (zr, zk, zv)]
        if chain and j > 0:
            p_r, p_k, p_v = ctx[j - 1]["last"]
        else:
            ps = slice(0, n_seq) if chain else slice(j * n_seq, (j + 1) * n_seq)
            p_r, p_k, p_v = prev_ref[ps, 0:rw], prev_ref[ps, rw:2 * rw], prev_ref[ps, 2 * rw:3 * rw]
        r = zr + (shifted(zr, p_r) - zr) * vrow(_V_MU_R)
        k = zk + (shifted(zk, p_k) - zk) * vrow(_V_MU_K)
        v = zv + (shifted(zv, p_v) - zv) * vrow(_V_MU_V)
        d_all = d_full[rows]
        wlog = -DECAY_SCALE * _sigmoid(vrow(_V_W0) + d_all[:, 0:rw])
        alr = _sigmoid(vrow(_V_A0) + d_all[:, rw:2 * rw])
        if has_vres:
            v = v + (vfirst_ref[rows, :] - v) * _sigmoid(vrow(_V_V0) + d_all[:, 2 * rw:3 * rw])
        else:
            vout_ref[rows, :] = v
        kkr = k * vrow(_V_KK)
        kf = k * (1.0 + (alr - 1.0) * vrow(_V_KA))
        c["rkr"] = r * kf * vrow(_V_RK)
        g_all = zg_ref[rows, :].astype(F32)
        c["gate"] = g_all * _sigmoid(g_all)
        w_hi, w_lo = _split_bf16(wlog)
        cum = jnp.dot(tri, w_hi, preferred_element_type=F32) + jnp.dot(tri, w_lo, preferred_element_type=F32)
        p_inc = jnp.exp(cum)
        inv_p = jnp.exp(-cum)
        p_exc = jnp.exp(cum - wlog)
        c["p_inc"] = p_inc
        yield
        kk_n2 = _segsum([kkr[:, sl] * kkr[:, sl] for sl in sls], bones_b)
        kk = [kkr[:, sl] * lax.rsqrt(jnp.maximum(n2, 1e-24)) for sl, n2 in zip(sls, kk_n2)]
        c["v_g"] = [v[:, sl] for sl in sls]
        c["a_f"] = [-kk[gi] * p_exc[:, sls[gi]] for gi in groups]
        c["r_f"] = [r[:, sl] * p_inc[:, sl] for sl in sls]
        a_t = [x.astype(BF16) for x in c["a_f"]]
        r_t = [x.astype(BF16) for x in c["r_f"]]
        c["b_t"] = [(kk[gi] * alr[:, sls[gi]] * inv_p[:, sls[gi]]).astype(BF16) for gi in groups]
        c["k_t"] = [(kf[:, sl] * inv_p[:, sl]).astype(BF16) for sl in sls]
        c["v_e"] = [expand(x.astype(BF16)) for x in c["v_g"]]
        c["ar"] = [jnp.concatenate([a_t[gi], r_t[gi]], axis=0) for gi in groups]
        yield
        sc = [_dot(c["ar"][gi], jnp.concatenate([expand(c["b_t"][gi]), expand(c["k_t"][gi])], axis=0), _NT)
              for gi in groups]
        c["scm"] = [x.astype(BF16) * mask_sc for x in sc]
        c["n_bd"] = [expand(x[:CHUNK, :e]) for x in c["scm"]]
        c["xs"] = [eye_b + n * lev_ref[1] for n in c["n_bd"]]
        yield
        c["akv"] = [_dot(c["scm"][gi][:, e:], c["v_e"][gi]) for gi in groups]
        yield

    def inverse(j):
        c = ctx[j]
        m = 2
        xs32 = None
        for lvl in range(2, n_levels):
            n_c = [c["n_bd"][gi] * lev_ref[lvl] for gi in groups]
            if m % F32_ROWS == 0:
                wide = m % BF16_ROWS != 0
                src = xs32 if wide else c["xs"]
                n_blk = e // (2 * m)
                lower = lambda x: jnp.concatenate([x[b * 2 * m + m:(b + 1) * 2 * m] for b in range(n_blk)], axis=0)
                eye_l = lower(eye_b.astype(src[0].dtype)).astype(BF16)
                t1 = [_dot(lower(src[gi]).astype(BF16), n_c[gi]).astype(BF16) + eye_l for gi in groups]
                yield
                new = [_dot(t1[gi], c["xs"][gi]).astype(src[0].dtype) for gi in groups]
                c["xs"] = [jnp.concatenate(
                    [piece for b in range(n_blk)
                     for piece in (x[b * 2 * m:b * 2 * m + m], n[b * m:(b + 1) * m])], axis=0).astype(BF16)
                    for x, n in zip(src, new)]
                xs32 = None
            else:
                t1 = [_dot(c["xs"][gi], n_c[gi]).astype(BF16) + eye_b for gi in groups]
                yield
                full = [_dot(t1[gi], c["xs"][gi]) for gi in groups]
                if (2 * m) % F32_ROWS == 0 and (2 * m) % BF16_ROWS != 0:
                    xs32 = full
                c["xs"] = [x.astype(BF16) for x in full]
            yield
            m *= 2

    def tail(j):
        c = ctx[j]
        rows = slice(j * CHUNK, (j + 1) * CHUNK)
        x_c = [(x[0:CHUNK] + x[CHUNK:2 * CHUNK]) + (x[2 * CHUNK:3 * CHUNK] + x[3 * CHUNK:4 * CHUNK])
               for x in c["xs"]]
        a_s, r_s = [], []
        for gi in groups:
            parts_a, parts_r = [], []
            for q in range(n_seq):
                lo_, hi_ = q * t_seq, (q + 1) * t_seq
                lhs = c["ar"][gi] if n_seq == 1 else jnp.concatenate(
                    [c["a_f"][gi][lo_:hi_], c["r_f"][gi][lo_:hi_]], axis=0).astype(BF16)
                res = _dot(lhs, load_state(slot(j, q), gi).astype(BF16), _NT)
                parts_a.append(res[:t_seq])
                parts_r.append(res[t_seq:])
            a_s.append(parts_a[0] if n_seq == 1 else jnp.concatenate(parts_a, axis=0))
            r_s.append(parts_r[0] if n_seq == 1 else jnp.concatenate(parts_r, axis=0))
        yield
        y = [a_s[gi] + c["akv"][gi][:CHUNK] for gi in groups]
        u = [_dot(x_c[gi], expand(y[gi].astype(BF16))) for gi in groups]
        yield
        o = [r_s[gi] + c["akv"][gi][CHUNK:] + _dot(c["scm"][gi][CHUNK:, :e], expand(u[gi].astype(BF16)))
             for gi in groups]
        yield
        for gi in groups:
            uv_t = jnp.concatenate([u[gi], c["v_g"][gi]], axis=0).T
            bk = jnp.concatenate([c["b_t"][gi], c["k_t"][gi]], axis=0)
            for q in range(n_seq):
                lhs = uv_t if n_seq == 1 else uv_t * qmask_ref[q]
                p_last = c["p_inc"][(q + 1) * t_seq - 1:(q + 1) * t_seq, sls[gi]]
                s = slot(j, q)
                store_state(s, gi, bones * ((load_state(s, gi) + _dot(lhs.astype(BF16), bk)) * p_last))
        yield
        mean = [x * (1.0 / HEAD_DIM) for x in _segsum(o, bones_b)]
        d = [o[gi] - mean[gi] for gi in groups]
        var = [x * (1.0 / HEAD_DIM) for x in _segsum([x * x for x in d], bones_b)]
        bonus = _segsum([c["rkr"][:, sl] for sl in sls], bones_b)
        yield
        for gi in groups:
            sl = sls[gi]
            on = d[gi] * lax.rsqrt(var[gi] + GN_EPS) * vec[_V_LNW:_V_LNW + 1, sl] + vec[_V_LNB:_V_LNB + 1, sl]
            ya_ref[rows, sl] = ((on + bonus[gi] * c["v_g"][gi]) * c["gate"][:, sl]).astype(ya_ref.dtype)

    preps, invs, tails = ([f(j) for j in range(n_sub)] for f in (prep, inverse, tail))
    for phase in range(n_sub + 2):
        active = []
        if 0 <= phase - 2 < n_sub:
            active.append(tails[phase - 2])
        if 0 <= phase - 1 < n_sub:
            active.append(invs[phase - 1])
        if phase < n_sub:
            active.append(preps[phase])
        while active:
            for gen in list(active):
                try:
                    next(gen)
                except StopIteration:
                    active.remove(gen)

    if chain and n_steps > 1:
        assert n_seq == 1
        last_r, last_k, last_v = ctx[n_sub - 1]["last"]
        prev_ref[:, 0:rw] = last_r
        prev_ref[:, rw:2 * rw] = last_k
        prev_ref[:, 2 * rw:3 * rw] = last_v
        prev_ref[:, LB_OFF:LB_OFF + LORA_COLS] = lb_last

    @pl.when(step == n_steps - 1)
    def _():
        for s in range(n_slots):
            for hd in range(N_HEADS):
                g, row0, half, upper = head_slot(hd)
                even = sbd_ref[s, g, half, pl.ds(row0, HEAD_DIM // 2, stride=2), :]
                odd = sbd_ref[s, g, half, pl.ds(row0 + 1, HEAD_DIM // 2, stride=2), :]
                if upper:
                    even = pltpu.roll(even, HEAD_DIM, axis=1)
                else:
                    odd = pltpu.roll(odd, HEAD_DIM, axis=1)
                sout_ref[s, hd] = jnp.where(low_lanes, even, odd)


def _wkv(z, prev0, vfirst, w2e, vecs, s_all, layer, *, n_batch, t_len, n_seq, n_sub):
    m = z.shape[0]
    t_seq = CHUNK // n_seq
    n_chunks = t_len // t_seq
    chain = n_seq == 1
    if chain:
        assert n_chunks % n_sub == 0
        n_steps, nb, n_slots = n_chunks // n_sub, n_batch, n_seq
    else:
        assert n_chunks == 1 and n_batch % (n_seq * n_sub) == 0
        n_steps, nb, n_slots = 1, n_batch // (n_seq * n_sub), n_seq * n_sub
    has_vres = vfirst is not None
    consts = _wkv_constants(n_seq, t_seq)
    n_levels = consts["levels"].shape[0]
    rw = RWKV_WIDTH
    rows = n_sub * CHUNK
    rb = lambda b, s: b * n_steps + s
    prev0 = prev0.reshape(nb, n_slots, PREV_COLS)

    zspec = lambda col: pl.BlockSpec((rows, rw), lambda b, s: (rb(b, s), col))
    row_spec = zspec(0)
    full = lambda a: pl.BlockSpec(a.shape, lambda b, s: (0,) * a.ndim)
    sspec = pl.BlockSpec((None, n_slots, N_HEADS, HEAD_DIM // 2, 2 * HEAD_DIM), lambda b, s: (layer, b, 0, 0, 0))

    in_specs = [zspec(Z_R), zspec(Z_K), zspec(Z_V), zspec(Z_G_RWKV),
                pl.BlockSpec((rows, PROJ_TN), lambda b, s: (rb(b, s), Z_LORA)),
                pl.BlockSpec((1, n_slots, PREV_COLS), lambda b, s: (b, 0, 0))]
    args = [z, z, z, z, z, prev0]
    if has_vres:
        in_specs.append(row_spec)
        args.append(vfirst)
    else:
        w2e = w2e[:, :2 * rw]
    in_specs += [full(w2e), full(vecs), sspec]
    args += [w2e, vecs, s_all]
    state_arg = len(args) - 1
    for name in ("tri", "mask_sc", "bones", "bones_b", "levels", "qmask"):
        in_specs.append(full(consts[name]))
        args.append(consts[name])

    out_specs = [row_spec]
    out_shape = [jax.ShapeDtypeStruct((m, rw), BF16)]
    if not has_vres:
        out_specs.append(row_spec)
        out_shape.append(jax.ShapeDtypeStruct((m, rw), F32))
    out_specs.append(sspec)
    out_shape.append(jax.ShapeDtypeStruct(s_all.shape, F32))

    outs = pl.pallas_call(
        functools.partial(_wkv_kernel, n_seq=n_seq, t_seq=t_seq, n_sub=n_sub, chain=chain, n_steps=n_steps,
                          has_vres=has_vres, n_levels=n_levels),
        grid=(nb, n_steps),
        in_specs=in_specs, out_specs=out_specs, out_shape=out_shape,
        input_output_aliases={state_arg: len(out_shape) - 1},
        scratch_shapes=[pltpu.VMEM((n_slots, N_GROUPS, 2, GROUP_LANES, GROUP_LANES // 2), F32),
                        pltpu.VMEM((n_slots, PREV_COLS), F32)],
        name="wkv",
        compiler_params=pltpu.CompilerParams(
            dimension_semantics=("parallel", "arbitrary"), vmem_limit_bytes=VMEM_LIMIT_BYTES),
    )(*args)
    if has_vres:
        ya, s_new = outs
        return ya, vfirst, s_new
    return outs


def _mix_kernel(*refs, n_seq, t_seq, tiles_per_seq, n_past, carry_pool):
    if carry_pool:
        (ya_ref, u_ref, gp_ref, ga_ref, gb_ref, pool_ref, perm_ref, permt_ref, pw_ref, ps_ref, wb0_ref, wb1_ref,
         m_ref, pool_out_ref) = refs
    else:
        ya_ref, u_ref, gp_ref, ga_ref, gb_ref, halo_ref, pw_ref, ps_ref, wb0_ref, wb1_ref, m_ref = refs
    rows = n_seq * t_seq
    tile = pl.program_id(0) % tiles_per_seq
    if carry_pool:
        perm = perm_ref[...]
        u_t = jnp.dot(perm, u_ref[...], preferred_element_type=F32).reshape(t_seq, n_seq, POOL_WIDTH)
        gp_all = jnp.dot(perm, gp_ref[...], preferred_element_type=F32)
        e = jnp.concatenate([jnp.zeros((HALO - POOL_BUF, n_seq, POOL_WIDTH), F32), pool_ref[...], u_t], axis=0)
        pool_out_ref[...] = e[HALO + t_seq - POOL_BUF:]
        t_axis = 0
        pos = lax.broadcasted_iota(jnp.int32, (t_seq, 1, 1), 0) + (1 + n_past)
    else:
        u = u_ref[...].astype(F32)
        halo = jnp.where(tile == 0, 0.0, halo_ref[...].astype(F32)).reshape(1, HALO, POOL_WIDTH)
        e = jnp.concatenate([halo, u.reshape(n_seq, t_seq, POOL_WIDTH)], axis=1)
        t_axis = 1
        pos = lax.broadcasted_iota(jnp.int32, (1, t_seq, 1), 1) + (tile * t_seq + 1 + n_past)
    take = lambda x, start, size: lax.slice_in_dim(x, start, start + size, axis=t_axis)
    n_col = len(POOL_WINDOWS)
    cw = D_MODEL // n_col
    cols = [slice(ci * cw, (ci + 1) * cw) for ci in range(n_col)]
    ya = ya_ref[...]
    gated_a, yb_parts = [], []
    for gi, win in enumerate(POOL_WINDOWS):
        pa = jnp.dot(ya, wb0_ref[:, cols[gi]], preferred_element_type=F32)
        sl = slice(gi * POOL_GROUP, (gi + 1) * POOL_GROUP)
        e_g = e[:, :, sl]
        acc = e_g
        span, length = 1, HALO + t_seq
        while span < win:
            length -= span
            acc = take(acc, span, length) + take(acc, 0, length)
            span *= 2
        ws = take(acc, length - t_seq, t_seq)
        inv_cnt = 1.0 / jnp.minimum(pos, win).astype(F32)
        mixed = (ws * inv_cnt - take(e_g, HALO, t_seq)).reshape(rows, POOL_GROUP)
        y_g = jnp.dot(mixed.astype(BF16), pw_ref[gi], preferred_element_type=F32)
        gp = gp_all[:, sl] if carry_pool else gp_ref[:, sl].astype(F32)
        yb_parts.append((y_g * ps_ref[:, sl] * (gp * _sigmoid(gp))).astype(BF16))
        gated_a.append(_sigmoid(ga_ref[:, cols[gi]].astype(F32)) * pa)
    y_b = jnp.concatenate(yb_parts, axis=1)
    if carry_pool:
        y_b = jnp.dot(permt_ref[...], y_b, preferred_element_type=F32).astype(BF16)
    pb = [None] * n_col
    pb[0] = jnp.dot(y_b, wb1_ref[:, cols[0]], preferred_element_type=F32)
    for ci in range(n_col):
        if ci + 1 < n_col:
            pb[ci + 1] = jnp.dot(y_b, wb1_ref[:, cols[ci + 1]], preferred_element_type=F32)
        m_ref[:, cols[ci]] = (gated_a[ci] + _sigmoid(gb_ref[:, cols[ci]].astype(F32)) * pb[ci]).astype(m_ref.dtype)


def _mix(ya, z, pool_all, pool_w_b, pool_scale, w_branch_b, layer, *, n_seq, t_seq, tiles_per_seq, n_past):
    m = z.shape[0]
    rows = n_seq * t_seq
    carry_pool = pool_all is not None
    in_specs = [
        pl.BlockSpec((rows, RWKV_WIDTH), lambda i: (i, 0)),
        pl.BlockSpec((rows, POOL_WIDTH), lambda i: (i, Z_U)),
        pl.BlockSpec((rows, POOL_WIDTH), lambda i: (i, Z_G_POOL)),
        pl.BlockSpec((rows, D_MODEL), lambda i: (i, Z_GATE_A)),
        pl.BlockSpec((rows, D_MODEL), lambda i: (i, Z_GATE_B)),
    ]
    args = [ya, z, z, z, z]
    out_specs = [pl.BlockSpec((rows, D_MODEL), lambda i: (i, 0))]
    out_shape = [jax.ShapeDtypeStruct((m, D_MODEL), BF16)]
    aliases = {}
    if carry_pool:
        assert tiles_per_seq == 1 and t_seq < POOL_BUF
        r = np.arange(rows)
        perm = np.zeros((rows, rows), np.float32)
        perm[(r % t_seq) * n_seq + r // t_seq, r] = 1.0
        pool_spec = pl.BlockSpec((None, POOL_BUF, n_seq, POOL_WIDTH), lambda i: (layer, 0, i, 0))
        const = pl.BlockSpec((rows, rows), lambda i: (0, 0))
        in_specs += [pool_spec, const, const]
        aliases = {len(args): 1}
        args += [pool_all, jnp.asarray(perm, BF16), jnp.asarray(perm.T, BF16)]
        out_specs.append(pool_spec)
        out_shape.append(jax.ShapeDtypeStruct(pool_all.shape, F32))
    else:
        assert n_seq == 1 and rows % HALO == 0
        in_specs.append(pl.BlockSpec((HALO, POOL_WIDTH), lambda i: (jnp.maximum(i * (rows // HALO) - 1, 0), Z_U)))
        args.append(z)
    in_specs += [
        pl.BlockSpec((None, len(POOL_WINDOWS), POOL_GROUP, POOL_GROUP), lambda i: (layer, 0, 0, 0)),
        pl.BlockSpec((None, 1, POOL_WIDTH), lambda i: (layer, 0, 0)),
        pl.BlockSpec((None, None, RWKV_WIDTH, D_MODEL), lambda i: (layer, 0, 0, 0), pipeline_mode=pl.Buffered(1)),
        pl.BlockSpec((None, None, POOL_WIDTH, D_MODEL), lambda i: (layer, 1, 0, 0), pipeline_mode=pl.Buffered(1)),
    ]
    args += [pool_w_b, pool_scale.reshape(DEPTH, 1, POOL_WIDTH), w_branch_b, w_branch_b]
    outs = pl.pallas_call(
        functools.partial(_mix_kernel, n_seq=n_seq, t_seq=t_seq, tiles_per_seq=tiles_per_seq, n_past=n_past,
                          carry_pool=carry_pool),
        grid=(m // rows,),
        in_specs=in_specs, out_specs=out_specs, out_shape=out_shape,
        input_output_aliases=aliases,
        name="mix",
        compiler_params=pltpu.CompilerParams(
            dimension_semantics=("parallel",), vmem_limit_bytes=VMEM_LIMIT_BYTES),
    )(*args)
    return outs if carry_pool else (outs[0], None)


def _out_kernel(*refs, n_split, emit_h):
    if emit_h:
        m_ref, x_ref, p_ref, g_ref, gn_ref, wo_ref, wg_ref, wp_ref, o_ref, h_ref = refs
    else:
        m_ref, x_ref, p_ref, g_ref, wo_ref, wg_ref, wp_ref, o_ref = refs
    rows = m_ref.shape[0] // n_split
    parts = [slice(i * rows, (i + 1) * rows) for i in range(n_split)]
    outs = [jnp.dot(m_ref[sl, :], wo_ref[...], preferred_element_type=F32) for sl in parts]
    ples = [jnp.dot(p_ref[sl, :].astype(BF16), wp_ref[...], preferred_element_type=F32) for sl in parts]
    xs = [x_ref[sl, :] + _rmsnorm(o, g_ref[...]) for sl, o in zip(parts, outs)]
    gates = [_sigmoid(jnp.dot(x.astype(BF16), wg_ref[...], preferred_element_type=F32)) for x in xs]
    for sl, x, gate, ple in zip(parts, xs, gates, ples):
        x_new = x + gate * ple
        o_ref[sl, :] = x_new
        if emit_h:
            h_ref[sl, :] = _rmsnorm(x_new, gn_ref[...]).astype(h_ref.dtype)


def _out(mm_, x2d, p3d, norm_post, norm_pre, w_out_b, w_gate_b, w_ple_b, layer, *, rows):
    m = x2d.shape[0]
    emit_h = layer + 1 < DEPTH
    once = pl.Buffered(1)
    wspec = lambda k, l=layer: pl.BlockSpec((None, k, D_MODEL), lambda i: (l, 0, 0), pipeline_mode=once)
    row_spec = pl.BlockSpec((rows, D_MODEL), lambda i: (i, 0))
    in_specs = [row_spec, row_spec, pl.BlockSpec((None, rows, PLE_DIM), lambda i: (layer, i, 0)), wspec(1)]
    args = [mm_, x2d, p3d, norm_post.reshape(DEPTH, 1, D_MODEL)]
    out_specs, out_shape = [row_spec], [jax.ShapeDtypeStruct((m, D_MODEL), F32)]
    if emit_h:
        in_specs.append(wspec(1, layer + 1))
        args.append(norm_pre.reshape(DEPTH, 1, D_MODEL))
        out_specs.append(row_spec)
        out_shape.append(jax.ShapeDtypeStruct((m, D_MODEL), BF16))
    in_specs += [wspec(D_MODEL), wspec(D_MODEL), wspec(PLE_DIM)]
    args += [w_out_b, w_gate_b, w_ple_b]
    outs = pl.pallas_call(
        functools.partial(_out_kernel, n_split=rows // OUT_SUB_ROWS, emit_h=emit_h),
        grid=(m // rows,),
        in_specs=in_specs, out_specs=out_specs, out_shape=out_shape,
        name="out",
        compiler_params=pltpu.CompilerParams(
            dimension_semantics=("parallel",), vmem_limit_bytes=VMEM_LIMIT_BYTES),
    )(*args)
    return (outs[0], outs[1]) if emit_h else (outs[0], None)


def _pair_rows(s):
    l, b = s.shape[:2]
    return s.reshape(l, b, N_HEADS, HEAD_DIM // 2, 2 * HEAD_DIM)


def _unpair_rows(s):
    l, b = s.shape[:2]
    return s.reshape(l, b, N_HEADS, HEAD_DIM, HEAD_DIM)


def _layer_weights(i, mu_rkv, mu_lora, w0, w1, w2, a0, a1, a2, v0, v1, v2, k_k, k_a, r_k, ln_w, ln_b):
    r_ = RWKV_WIDTH
    mu = mu_lora[i]
    if i > 0:
        v1_i, v2_i, v0_i = v1[i - 1], v2[i - 1], v0[i - 1]
    else:
        v1_i = jnp.zeros((D_MODEL, MV_RANK), F32)
        v2_i = jnp.zeros((MV_RANK, r_), F32)
        v0_i = jnp.zeros((r_,), F32)
    w_lora = jnp.concatenate(
        [(1.0 - mu[0])[:, None] * w1[i], (1.0 - mu[1])[:, None] * a1[i], (1.0 - mu[2])[:, None] * v1_i,
         mu[0][:, None] * w1[i], mu[1][:, None] * a1[i], mu[2][:, None] * v1_i], axis=1).astype(BF16)
    z96 = jnp.zeros((DECAY_RANK, r_), F32)
    z64 = jnp.zeros((MV_RANK, r_), F32)
    w2e = jnp.concatenate([
        jnp.concatenate([w2[i], z96, z64], axis=0),
        jnp.concatenate([z96, a2[i], z64], axis=0),
        jnp.concatenate([z96, z96, v2_i], axis=0)], axis=1).astype(BF16)
    rows = [mu_rkv[i, 0], mu_rkv[i, 1], mu_rkv[i, 2], w0[i], a0[i], v0_i, k_k[i], k_a[i], r_k[i], ln_w[i], ln_b[i]]
    vecs = jnp.concatenate([jnp.stack(rows), jnp.zeros((_N_VEC_ROWS - len(rows), r_), F32)], axis=0)
    return w_lora, w2e, vecs


def _run_group(x, p, wkv, shift, pool, n_past, lw, params, *, n_seq, n_sub, proj_tm, mix_seq, mix_t, out_rows):
    (w_in, norm_pre, norm_post, pool_w_b, pool_scale, w_branch_b, w_out_b, w_ple_b, w_gate_b) = params
    b, t, d = x.shape
    m = b * t
    x2d = x.reshape(m, d)
    p3d = p.reshape(DEPTH, m, PLE_DIM)
    zero_state = wkv is None
    shift_out, pool_out = [], []
    v_first = None
    tiles_per_seq = t // mix_t
    s_all = _pair_rows(jnp.zeros((DEPTH, b, N_HEADS, HEAD_DIM, HEAD_DIM), F32) if zero_state else wkv)
    pool_all = None if zero_state else jnp.transpose(pool, (0, 2, 1, 3))
    h2d = _norm(x2d, norm_pre[0], rows=NORM_ROWS, out_dtype=BF16)
    for i in range(DEPTH):
        w_lora, w2e, vecs = lw[i]
        z = _proj(h2d, w_in, i, w_lora, tm=proj_tm, n_main=N_MAIN_TILES, out_dtype=BF16)
        shift_out.append(_norm(x2d.reshape(b, t, d)[:, -1], norm_pre[i], rows=b, out_dtype=F32))
        if zero_state:
            assert t >= POOL_BUF
            prev0 = jnp.zeros((b, PREV_COLS), F32)
            pool_out.append(
                z.reshape(b, t, -1)[:, -POOL_BUF:, Z_U * POOL_WIDTH:(Z_U + 1) * POOL_WIDTH].astype(F32))
        else:
            prev0 = _proj(shift[i].astype(BF16), w_in, i, w_lora, tm=b, n_main=3 * RWKV_WIDTH // PROJ_TN,
                          out_dtype=F32)
        ya, v_first, s_all = _wkv(z, prev0, v_first, w2e, vecs, s_all, i, n_batch=b, t_len=t, n_seq=n_seq,
                                  n_sub=n_sub)
        mm_, pool_all = _mix(ya, z, pool_all, pool_w_b, pool_scale, w_branch_b, i,
                             n_seq=mix_seq, t_seq=mix_t, tiles_per_seq=tiles_per_seq, n_past=n_past)
        x2d, h2d = _out(mm_, x2d, p3d, norm_post, norm_pre, w_out_b, w_gate_b, w_ple_b, i, rows=out_rows)
    pool_new = jnp.stack(pool_out) if zero_state else jnp.transpose(pool_all, (0, 2, 1, 3))
    return x2d.reshape(b, t, d), _unpair_rows(s_all), jnp.stack(shift_out), pool_new


def kernel(x_prompt, x_sample, state_wkv, state_shift, state_pool, p_prompt, p_sample, norm_pre, norm_post, w_in, mu_rkv, mu_lora, w0, w1, w2, a0, a1, a2, v0, v1, v2, k_k, k_a, r_k, ln_w, ln_b, pool_w, pool_scale, w_branch, w_out, w_ple, w_ple_gate):
    lw = [_layer_weights(i, mu_rkv, mu_lora, w0, w1, w2, a0, a1, a2, v0, v1, v2, k_k, k_a, r_k, ln_w, ln_b)
          for i in range(DEPTH)]
    params = (w_in, norm_pre, norm_post, pool_w.astype(BF16), pool_scale, w_branch.astype(BF16),
              w_out.astype(BF16), w_ple.astype(BF16), w_ple_gate.astype(BF16))
    y_p, wkv_p, shift_p, pool_p = _run_group(
        x_prompt, p_prompt, None, None, None, 0, lw, params,
        n_seq=1, n_sub=4, proj_tm=2048, mix_seq=1, mix_t=256, out_rows=256)
    y_s, wkv_s, shift_s, pool_s = _run_group(
        x_sample, p_sample, state_wkv, state_shift, state_pool, PAST_LEN, lw, params,
        n_seq=8, n_sub=2, proj_tm=1024, mix_seq=32, mix_t=8, out_rows=256)
    return (y_p, y_s, wkv_p, shift_p, pool_p, wkv_s, shift_s, pool_s)
```

```python
import functools

import jax
import jax.numpy as jnp
import numpy as np
from jax import lax
from jax.experimental import pallas as pl
from jax.experimental.pallas import tpu as pltpu

F32 = jnp.float32
BF16 = jnp.bfloat16

D_MODEL = 2048
DEPTH = 4
PAST_LEN = 16384
RWKV_WIDTH = 1024
HEAD_DIM = 64
N_HEADS = 16
POOL_WIDTH = 1024
POOL_WINDOWS = (2, 4, 8, 16)
POOL_GROUP = 256
POOL_BUF = 15
PLE_DIM = 256
DECAY_RANK = 96
AAA_RANK = 96
MV_RANK = 64
LORA_COLS = DECAY_RANK + AAA_RANK + MV_RANK
IN_COLS = 10240
RMS_EPS = 1e-6
GN_EPS = 64e-5
DECAY_SCALE = 0.606531

HEADS_PER_GROUP = 4
GROUP_LANES = HEADS_PER_GROUP * HEAD_DIM
N_GROUPS = N_HEADS // HEADS_PER_GROUP
CHUNK = 64
EXPANDED = HEADS_PER_GROUP * CHUNK
F32_ROWS = 8
BF16_ROWS = 16
HALO = 16
PROJ_TN = 2 * LORA_COLS
N_MAIN_TILES = IN_COLS // PROJ_TN
Z_R, Z_K, Z_V, Z_G_RWKV, Z_U, Z_G_POOL = range(6)
Z_GATE_A, Z_GATE_B = 3, 4
Z_LORA = N_MAIN_TILES
PREV_COLS = 3 * RWKV_WIDTH + PROJ_TN
LB_OFF = 3 * RWKV_WIDTH + LORA_COLS

NORM_ROWS = 512
PROJ_TM = 3072
OUT_ROWS = 256
OUT_SUB_ROWS = 128

VMEM_LIMIT_BYTES = 56 * 1024 * 1024


def _sigmoid(x):
    return 0.5 * jnp.tanh(0.5 * x) + 0.5


def _split_bf16(x):
    hi = x.astype(BF16)
    lo = (x - hi.astype(F32)).astype(BF16)
    return hi, lo


_NN = (((1,), (0,)), ((), ()))
_NT = (((1,), (1,)), ((), ()))


def _dot(a, b, dims=_NN):
    return lax.dot_general(a, b, dimension_numbers=dims, preferred_element_type=F32)


def _segsum(xs, bones_b):
    rows = xs[0].shape[0]
    s = _dot(jnp.concatenate([x.astype(BF16) for x in xs], axis=0), bones_b)
    return [s[i * rows:(i + 1) * rows] for i in range(len(xs))]


def _proj_kernel(h_ref, w_ref, wl_ref, z_ref, *, n_main):
    j = pl.program_id(1)

    @pl.when(j < n_main)
    def _():
        z_ref[...] = jnp.dot(h_ref[...], w_ref[...].astype(BF16), preferred_element_type=F32).astype(z_ref.dtype)

    @pl.when(j == n_main)
    def _():
        z_ref[...] = jnp.dot(h_ref[...], wl_ref[...], preferred_element_type=F32).astype(z_ref.dtype)


def _proj(h2d, w_in, layer, w_lora, *, tm, n_main, out_dtype):
    m, d = h2d.shape
    tn = PROJ_TN
    assert m % tm == 0
    return pl.pallas_call(
        functools.partial(_proj_kernel, n_main=n_main),
        grid=(m // tm, n_main + 1),
        in_specs=[
            pl.BlockSpec((tm, d), lambda i, j: (i, 0)),
            pl.BlockSpec((None, d, tn), lambda i, j: (layer, 0, jnp.minimum(j, n_main - 1))),
            pl.BlockSpec((d, tn), lambda i, j: (0, 0), pipeline_mode=pl.Buffered(1)),
        ],
        out_specs=pl.BlockSpec((tm, tn), lambda i, j: (i, j)),
        out_shape=jax.ShapeDtypeStruct((m, (n_main + 1) * tn), out_dtype),
        name="proj",
        compiler_params=pltpu.CompilerParams(
            dimension_semantics=("parallel", "arbitrary"), vmem_limit_bytes=VMEM_LIMIT_BYTES),
    )(h2d, w_in, w_lora)


def _rmsnorm(x, gain):
    return x * lax.rsqrt(jnp.mean(x * x, axis=-1, keepdims=True) + RMS_EPS) * gain


def _norm_kernel(x_ref, g_ref, *rest):
    o_ref = rest[-1]
    o_ref[...] = _rmsnorm(x_ref[...], g_ref[...]).astype(o_ref.dtype)


def _shared_rows_out(row_spec_shape, row0, total_rows, d, dtype, into, n_inputs):
    rows = row_spec_shape[0]
    assert row0 % rows == 0
    blk0 = row0 // rows
    spec = pl.BlockSpec(row_spec_shape, lambda i: (i + blk0, 0))
    shape = jax.ShapeDtypeStruct((total_rows, d), dtype)
    extra_specs = [] if into is None else [pl.BlockSpec(memory_space=pl.ANY)]
    extra_args = [] if into is None else [into]
    return spec, shape, extra_specs, extra_args, (None if into is None else n_inputs)


def _norm(x2d, gain, *, rows, out_dtype, row0=0, total_rows=None, into=None):
    m, d = x2d.shape
    rows = min(rows, m)
    assert m % rows == 0
    o_spec, o_shape, x_specs, x_args, alias_in = _shared_rows_out(
        (rows, d), row0, total_rows or m, d, out_dtype, into, 2)
    return pl.pallas_call(
        _norm_kernel,
        grid=(m // rows,),
        in_specs=[pl.BlockSpec((rows, d), lambda i: (i, 0)), pl.BlockSpec((1, d), lambda i: (0, 0))] + x_specs,
        out_specs=o_spec, out_shape=o_shape,
        input_output_aliases={} if alias_in is None else {alias_in: 0},
        name="norm",
        compiler_params=pltpu.CompilerParams(dimension_semantics=("parallel",)),
    )(x2d, gain.reshape(1, d), *x_args)


_V_MU_R, _V_MU_K, _V_MU_V, _V_W0, _V_A0, _V_V0, _V_KK, _V_KA, _V_RK, _V_LNW, _V_LNB = range(11)
_N_VEC_ROWS = 16


def _wkv_constants(n_seq, t_seq):
    c, e = CHUNK, EXPANDED
    assert n_seq * t_seq == c
    row = np.arange(c)
    q_of = row // t_seq
    tri = (row[:, None] >= row[None, :]) & (q_of[:, None] == q_of[None, :])
    erow = np.arange(e)
    es = erow % c
    same = q_of[:, None] == (es // t_seq)[None, :]
    mask_s = same & (row[:, None] > es[None, :])
    mask_i = same & (row[:, None] >= es[None, :])
    mask_sc = np.block([[mask_s, mask_s], [mask_i, mask_i]])
    lane = np.arange(GROUP_LANES)
    bones = (lane[:, None] // HEAD_DIM) == (lane[None, :] // HEAD_DIM)
    levels = [np.eye(e, dtype=bool), (erow[:, None] // 2) == (erow[None, :] // 2)]
    m = 2
    while m < t_seq:
        levels.append(((erow[:, None] // (2 * m)) == (erow[None, :] // (2 * m)))
                      & ((erow[:, None] % (2 * m)) >= m) & ((erow[None, :] % (2 * m)) < m))
        m *= 2
    col = np.arange(2 * c)
    qmask = ((col[None, :] % c) // t_seq) == np.arange(n_seq)[:, None]
    f = lambda a: jnp.asarray(a.astype(np.float32))
    return dict(tri=f(tri).astype(BF16), mask_sc=f(mask_sc).astype(BF16), bones=f(bones),
                bones_b=f(bones).astype(BF16), levels=f(np.stack(levels)).astype(BF16),
                qmask=f(qmask).reshape(n_seq, 1, 2 * c))


def _wkv_kernel(*refs, n_seq, t_seq, n_sub, chain, n_steps, has_vres, n_levels):
    it = iter(refs)
    zr_ref, zk_ref, zv_ref, zg_ref, zl_ref, p0_ref = (next(it) for _ in range(6))
    vfirst_ref = next(it) if has_vres else None
    w2_ref, vec_ref, s0_ref = next(it), next(it), next(it)
    tri_ref, msc_ref, bones_ref, bonesb_ref, lev_ref, qmask_ref = (next(it) for _ in range(6))
    ya_ref = next(it)
    vout_ref = None if has_vres else next(it)
    sout_ref = next(it)
    sbd_ref, prev_ref = next(it), next(it)

    step = pl.program_id(1)
    gl, rw, e = GROUP_LANES, RWKV_WIDTH, EXPANDED
    n_slots = n_seq if chain else n_sub * n_seq
    slot = (lambda j, q: q) if chain else (lambda j, q: j * n_seq + q)
    bones = bones_ref[...]
    bones_b = bonesb_ref[...]
    mask_sc = msc_ref[...]
    eye_b = lev_ref[0]
    tri = tri_ref[...]
    vec = vec_ref[...]
    vrow = lambda i: vec[i:i + 1, :]
    groups = range(N_GROUPS)
    sls = [slice(gi * gl, (gi + 1) * gl) for gi in groups]

    def expand(x):
        return jnp.concatenate([x] * HEADS_PER_GROUP, axis=0) * bones_b

    low_lanes = lax.broadcasted_iota(jnp.int32, (1, 2 * HEAD_DIM), 1) < HEAD_DIM

    def head_slot(hd):
        h = hd % HEADS_PER_GROUP
        return hd // HEADS_PER_GROUP, h * HEAD_DIM, h // 2, h % 2 == 1

    def load_state(s, g):
        return jnp.concatenate([sbd_ref[s, g, 0], sbd_ref[s, g, 1]], axis=1)

    def store_state(s, g, val):
        sbd_ref[s, g, 0] = val[:, :2 * HEAD_DIM]
        sbd_ref[s, g, 1] = val[:, 2 * HEAD_DIM:]

    @pl.when(step == 0)
    def _():
        for s in range(n_slots):
            for hd in range(N_HEADS):
                g, row0, half, upper = head_slot(hd)
                packed = s0_ref[s, hd]
                even = jnp.where(low_lanes, packed, 0.0)
                odd = jnp.where(low_lanes, 0.0, packed)
                if upper:
                    even = pltpu.roll(even, HEAD_DIM, axis=1)
                else:
                    odd = pltpu.roll(odd, HEAD_DIM, axis=1)
                sbd_ref[s, g, half, pl.ds(row0, HEAD_DIM // 2, stride=2), :] = even
                sbd_ref[s, g, half, pl.ds(row0 + 1, HEAD_DIM // 2, stride=2), :] = odd
                sbd_ref[s, g, 1 - half, row0:row0 + HEAD_DIM, :] = jnp.zeros((HEAD_DIM, 2 * HEAD_DIM), F32)
        prev_ref[...] = p0_ref[0]

    row = lax.broadcasted_iota(jnp.int32, (CHUNK, 1), 0)
    is_first = (row % t_seq) == 0
    lora_lane = lax.broadcasted_iota(jnp.int32, (1, LORA_COLS), 1)

    def shifted(z, p0):
        rolled = pltpu.roll(z, 1, axis=0)
        if n_seq == 1:
            first = jnp.broadcast_to(p0, z.shape)
        else:
            first = jnp.broadcast_to(p0[:, None, :], (n_seq, t_seq, z.shape[1])).reshape(z.shape)
        return jnp.where(is_first, first, rolled)

    n_rows = n_sub * CHUNK
    zl_all = zl_ref[...].astype(F32)
    lb_all = zl_all[:, LORA_COLS:]
    p_l = prev_ref[:, LB_OFF:LB_OFF + LORA_COLS]
    if n_slots == 1:
        lb_first = jnp.broadcast_to(p_l, lb_all.shape)
    else:
        lb_first = jnp.broadcast_to(p_l[:, None, :], (n_slots, t_seq, LORA_COLS)).reshape(lb_all.shape)
    row_all = lax.broadcasted_iota(jnp.int32, (n_rows, 1), 0)
    lb_prev = jnp.where(row_all % (n_rows if chain else t_seq) == 0, lb_first, pltpu.roll(lb_all, 1, axis=0))
    lin = zl_all[:, :LORA_COLS] + lb_prev
    lin = jnp.where(lora_lane < DECAY_RANK, jnp.tanh(lin), lin).astype(BF16)
    d_full = jnp.dot(lin, w2_ref[...], preferred_element_type=F32)
    lb_last = lb_all[n_rows - 1:n_rows, :]

    ctx = [dict() for _ in range(n_sub)]

    def prep(j):
        c = ctx[j]
        rows = slice(j * CHUNK, (j + 1) * CHUNK)
        zr, zk, zv = zr_ref[rows, :].astype(F32), zk_ref[rows, :].astype(F32), zv_ref[rows, :].astype(F32)
        c["last"] = [x[CHUNK - 1:CHUNK, :] for x in (zr, zk, zv)]
        if chain and j > 0:
            p_r, p_k, p_v = ctx[j - 1]["last"]
        else:
            ps = slice(0, n_seq) if chain else slice(j * n_seq, (j + 1) * n_seq)
            p_r, p_k, p_v = prev_ref[ps, 0:rw], prev_ref[ps, rw:2 * rw], prev_ref[ps, 2 * rw:3 * rw]
        r = zr + (shifted(zr, p_r) - zr) * vrow(_V_MU_R)
        k = zk + (shifted(zk, p_k) - zk) * vrow(_V_MU_K)
        v = zv + (shifted(zv, p_v) - zv) * vrow(_V_MU_V)
        d_all = d_full[rows]
        wlog = -DECAY_SCALE * _sigmoid(vrow(_V_W0) + d_all[:, 0:rw])
        alr = _sigmoid(vrow(_V_A0) + d_all[:, rw:2 * rw])
        if has_vres:
            v = v + (vfirst_ref[rows, :] - v) * _sigmoid(vrow(_V_V0) + d_all[:, 2 * rw:3 * rw])
        else:
            vout_ref[rows, :] = v
        kkr = k * vrow(_V_KK)
        kf = k * (1.0 + (alr - 1.0) * vrow(_V_KA))
        c["rkr"] = r * kf * vrow(_V_RK)
        g_all = zg_ref[rows, :].astype(F32)
        c["gate"] = g_all * _sigmoid(g_all)
        w_hi, w_lo = _split_bf16(wlog)
        cum = jnp.dot(tri, w_hi, preferred_element_type=F32) + jnp.dot(tri, w_lo, preferred_element_type=F32)
        p_inc = jnp.exp(cum)
        inv_p = jnp.exp(-cum)
        p_exc = jnp.exp(cum - wlog)
        c["p_inc"] = p_inc
        yield
        kk_n2 = _segsum([kkr[:, sl] * kkr[:, sl] for sl in sls], bones_b)
        kk = [kkr[:, sl] * lax.rsqrt(jnp.maximum(n2, 1e-24)) for sl, n2 in zip(sls, kk_n2)]
        c["v_g"] = [v[:, sl] for sl in sls]
        c["a_f"] = [-kk[gi] * p_exc[:, sls[gi]] for gi in groups]
        c["r_f"] = [r[:, sl] * p_inc[:, sl] for sl in sls]
        a_t = [x.astype(BF16) for x in c["a_f"]]
        r_t = [x.astype(BF16) for x in c["r_f"]]
        c["b_t"] = [(kk[gi] * alr[:, sls[gi]] * inv_p[:, sls[gi]]).astype(BF16) for gi in groups]
        c["k_t"] = [(kf[:, sl] * inv_p[:, sl]).astype(BF16) for sl in sls]
        c["v_e"] = [expand(x.astype(BF16)) for x in c["v_g"]]
        c["ar"] = [jnp.concatenate([a_t[gi], r_t[gi]], axis=0) for gi in groups]
        yield
        sc = [_dot(c["ar"][gi], jnp.concatenate([expand(c["b_t"][gi]), expand(c["k_t"][gi])], axis=0), _NT)
              for gi in groups]
        c["scm"] = [x.astype(BF16) * mask_sc for x in sc]
        c["n_bd"] = [expand(x[:CHUNK, :e]) for x in c["scm"]]
        c["xs"] = [eye_b + n * lev_ref[1] for n in c["n_bd"]]
        yield
        c["akv"] = [_dot(c["scm"][gi][:, e:], c["v_e"][gi]) for gi in groups]
        yield

    def inverse(j):
        c = ctx[j]
        m = 2
        xs32 = None
        for lvl in range(2, n_levels):
            n_c = [c["n_bd"][gi] * lev_ref[lvl] for gi in groups]
            if m % F32_ROWS == 0:
                wide = m % BF16_ROWS != 0
                src = xs32 if wide else c["xs"]
                n_blk = e // (2 * m)
                lower = lambda x: jnp.concatenate([x[b * 2 * m + m:(b + 1) * 2 * m] for b in range(n_blk)], axis=0)
                eye_l = lower(eye_b.astype(src[0].dtype)).astype(BF16)
                t1 = [_dot(lower(src[gi]).astype(BF16), n_c[gi]).astype(BF16) + eye_l for gi in groups]
                yield
                new = [_dot(t1[gi], c["xs"][gi]).astype(src[0].dtype) for gi in groups]
                c["xs"] = [jnp.concatenate(
                    [piece for b in range(n_blk)
                     for piece in (x[b * 2 * m:b * 2 * m + m], n[b * m:(b + 1) * m])], axis=0).astype(BF16)
                    for x, n in zip(src, new)]
                xs32 = None
            else:
                t1 = [_dot(c["xs"][gi], n_c[gi]).astype(BF16) + eye_b for gi in groups]
                yield
                full = [_dot(t1[gi], c["xs"][gi]) for gi in groups]
                if (2 * m) % F32_ROWS == 0 and (2 * m) % BF16_ROWS != 0:
                    xs32 = full
                c["xs"] = [x.astype(BF16) for x in full]
            yield
            m *= 2

    def tail(j):
        c = ctx[j]
        rows = slice(j * CHUNK, (j + 1) * CHUNK)
        x_c = [(x[0:CHUNK] + x[CHUNK:2 * CHUNK]) + (x[2 * CHUNK:3 * CHUNK] + x[3 * CHUNK:4 * CHUNK])
               for x in c["xs"]]
        a_s, r_s = [], []
        for gi in groups:
            parts_a, parts_r = [], []
            for q in range(n_seq):
                lo_, hi_ = q * t_seq, (q + 1) * t_seq
                lhs = c["ar"][gi] if n_seq == 1 else jnp.concatenate(
                    [c["a_f"][gi][lo_:hi_], c["r_f"][gi][lo_:hi_]], axis=0).astype(BF16)
                res = _dot(lhs, load_state(slot(j, q), gi).astype(BF16), _NT)
                parts_a.append(res[:t_seq])
                parts_r.append(res[t_seq:])
            a_s.append(parts_a[0] if n_seq == 1 else jnp.concatenate(parts_a, axis=0))
            r_s.append(parts_r[0] if n_seq == 1 else jnp.concatenate(parts_r, axis=0))
        yield
        y = [a_s[gi] + c["akv"][gi][:CHUNK] for gi in groups]
        u = [_dot(x_c[gi], expand(y[gi].astype(BF16))) for gi in groups]
        yield
        o = [r_s[gi] + c["akv"][gi][CHUNK:] + _dot(c["scm"][gi][CHUNK:, :e], expand(u[gi].astype(BF16)))
             for gi in groups]
        yield
        for gi in groups:
            uv_t = jnp.concatenate([u[gi], c["v_g"][gi]], axis=0).T
            bk = jnp.concatenate([c["b_t"][gi], c["k_t"][gi]], axis=0)
            for q in range(n_seq):
                lhs = uv_t if n_seq == 1 else uv_t * qmask_ref[q]
                p_last = c["p_inc"][(q + 1) * t_seq - 1:(q + 1) * t_seq, sls[gi]]
                s = slot(j, q)
                store_state(s, gi, bones * ((load_state(s, gi) + _dot(lhs.astype(BF16), bk)) * p_last))
        yield
        mean = [x * (1.0 / HEAD_DIM) for x in _segsum(o, bones_b)]
        d = [o[gi] - mean[gi] for gi in groups]
        var = [x * (1.0 / HEAD_DIM) for x in _segsum([x * x for x in d], bones_b)]
        bonus = _segsum([c["rkr"][:, sl] for sl in sls], bones_b)
        yield
        for gi in groups:
            sl = sls[gi]
            on = d[gi] * lax.rsqrt(var[gi] + GN_EPS) * vec[_V_LNW:_V_LNW + 1, sl] + vec[_V_LNB:_V_LNB + 1, sl]
            ya_ref[rows, sl] = ((on + bonus[gi] * c["v_g"][gi]) * c["gate"][:, sl]).astype(ya_ref.dtype)

    preps, invs, tails = ([f(j) for j in range(n_sub)] for f in (prep, inverse, tail))
    for phase in range(n_sub + 2):
        active = []
        if 0 <= phase - 2 < n_sub:
            active.append(tails[phase - 2])
        if 0 <= phase - 1 < n_sub:
            active.append(invs[phase - 1])
        if phase < n_sub:
            active.append(preps[phase])
        while active:
            for gen in list(active):
                try:
                    next(gen)
                except StopIteration:
                    active.remove(gen)

    if chain and n_steps > 1:
        assert n_seq == 1
        last_r, last_k, last_v = ctx[n_sub - 1]["last"]
        prev_ref[:, 0:rw] = last_r
        prev_ref[:, rw:2 * rw] = last_k
        prev_ref[:, 2 * rw:3 * rw] = last_v
        prev_ref[:, LB_OFF:LB_OFF + LORA_COLS] = lb_last

    @pl.when(step == n_steps - 1)
    def _():
        for s in range(n_slots):
            for hd in range(N_HEADS):
                g, row0, half, upper = head_slot(hd)
                even = sbd_ref[s, g, half, pl.ds(row0, HEAD_DIM // 2, stride=2), :]
                odd = sbd_ref[s, g, half, pl.ds(row0 + 1, HEAD_DIM // 2, stride=2), :]
                if upper:
                    even = pltpu.roll(even, HEAD_DIM, axis=1)
                else:
                    odd = pltpu.roll(odd, HEAD_DIM, axis=1)
                sout_ref[s, hd] = jnp.where(low_lanes, even, odd)


def _wkv(z, prev0, vfirst, w2e, vecs, s_all, layer, *, n_batch, t_len, n_seq, n_sub, row0):
    m = n_batch * t_len
    t_seq = CHUNK // n_seq
    n_chunks = t_len // t_seq
    chain = n_seq == 1
    if chain:
        assert n_chunks % n_sub == 0
        n_steps, nb, n_slots = n_chunks // n_sub, n_batch, n_seq
    else:
        assert n_chunks == 1 and n_batch % (n_seq * n_sub) == 0
        n_steps, nb, n_slots = 1, n_batch // (n_seq * n_sub), n_seq * n_sub
    has_vres = vfirst is not None
    consts = _wkv_constants(n_seq, t_seq)
    n_levels = consts["levels"].shape[0]
    rw = RWKV_WIDTH
    rows = n_sub * CHUNK
    rb = lambda b, s: b * n_steps + s
    prev0 = prev0.reshape(nb, n_slots, PREV_COLS)

    assert row0 % rows == 0
    z_off = row0 // rows
    zspec = lambda col: pl.BlockSpec((rows, rw), lambda b, s: (rb(b, s) + z_off, col))
    row_spec = pl.BlockSpec((rows, rw), lambda b, s: (rb(b, s), 0))
    full = lambda a: pl.BlockSpec(a.shape, lambda b, s: (0,) * a.ndim)
    sspec = pl.BlockSpec((None, n_slots, N_HEADS, HEAD_DIM // 2, 2 * HEAD_DIM), lambda b, s: (layer, b, 0, 0, 0))

    in_specs = [zspec(Z_R), zspec(Z_K), zspec(Z_V), zspec(Z_G_RWKV),
                pl.BlockSpec((rows, PROJ_TN), lambda b, s: (rb(b, s) + z_off, Z_LORA)),
                pl.BlockSpec((1, n_slots, PREV_COLS), lambda b, s: (b, 0, 0))]
    args = [z, z, z, z, z, prev0]
    if has_vres:
        in_specs.append(row_spec)
        args.append(vfirst)
    else:
        w2e = w2e[:, :2 * rw]
    in_specs += [full(w2e), full(vecs), sspec]
    args += [w2e, vecs, s_all]
    state_arg = len(args) - 1
    for name in ("tri", "mask_sc", "bones", "bones_b", "levels", "qmask"):
        in_specs.append(full(consts[name]))
        args.append(consts[name])

    out_specs = [row_spec]
    out_shape = [jax.ShapeDtypeStruct((m, rw), BF16)]
    if not has_vres:
        out_specs.append(row_spec)
        out_shape.append(jax.ShapeDtypeStruct((m, rw), F32))
    out_specs.append(sspec)
    out_shape.append(jax.ShapeDtypeStruct(s_all.shape, F32))

    outs = pl.pallas_call(
        functools.partial(_wkv_kernel, n_seq=n_seq, t_seq=t_seq, n_sub=n_sub, chain=chain, n_steps=n_steps,
                          has_vres=has_vres, n_levels=n_levels),
        grid=(nb, n_steps),
        in_specs=in_specs, out_specs=out_specs, out_shape=out_shape,
        input_output_aliases={state_arg: len(out_shape) - 1},
        scratch_shapes=[pltpu.VMEM((n_slots, N_GROUPS, 2, GROUP_LANES, GROUP_LANES // 2), F32),
                        pltpu.VMEM((n_slots, PREV_COLS), F32)],
        name="wkv",
        compiler_params=pltpu.CompilerParams(
            dimension_semantics=("parallel", "arbitrary"), vmem_limit_bytes=VMEM_LIMIT_BYTES),
    )(*args)
    if has_vres:
        ya, s_new = outs
        return ya, vfirst, s_new
    return outs


def _mix_kernel(*refs, n_seq, t_seq, tiles_per_seq, n_past, carry_pool):
    if carry_pool:
        (ya_ref, u_ref, gp_ref, ga_ref, gb_ref, pool_ref, perm_ref, permt_ref, pw_ref, ps_ref, wb0_ref, wb1_ref,
         m_ref, pool_out_ref) = refs
    else:
        ya_ref, u_ref, gp_ref, ga_ref, gb_ref, halo_ref, pw_ref, ps_ref, wb0_ref, wb1_ref, m_ref = refs
    rows = n_seq * t_seq
    tile = pl.program_id(0) % tiles_per_seq
    if carry_pool:
        perm = perm_ref[...]
        u_t = jnp.dot(perm, u_ref[...], preferred_element_type=F32).reshape(t_seq, n_seq, POOL_WIDTH)
        gp_all = jnp.dot(perm, gp_ref[...], preferred_element_type=F32)
        e = jnp.concatenate([jnp.zeros((HALO - POOL_BUF, n_seq, POOL_WIDTH), F32), pool_ref[...], u_t], axis=0)
        pool_out_ref[...] = e[HALO + t_seq - POOL_BUF:]
        t_axis = 0
        pos = lax.broadcasted_iota(jnp.int32, (t_seq, 1, 1), 0) + (1 + n_past)
    else:
        u = u_ref[...].astype(F32)
        halo = jnp.where(tile == 0, 0.0, halo_ref[...].astype(F32)).reshape(1, HALO, POOL_WIDTH)
        e = jnp.concatenate([halo, u.reshape(n_seq, t_seq, POOL_WIDTH)], axis=1)
        t_axis = 1
        pos = lax.broadcasted_iota(jnp.int32, (1, t_seq, 1), 1) + (tile * t_seq + 1 + n_past)
    take = lambda x, start, size: lax.slice_in_dim(x, start, start + size, axis=t_axis)
    n_col = len(POOL_WINDOWS)
    cw = D_MODEL // n_col
    cols = [slice(ci * cw, (ci + 1) * cw) for ci in range(n_col)]
    ya = ya_ref[...]
    gated_a, yb_parts = [], []
    for gi, win in enumerate(POOL_WINDOWS):
        pa = jnp.dot(ya, wb0_ref[:, cols[gi]], preferred_element_type=F32)
        sl = slice(gi * POOL_GROUP, (gi + 1) * POOL_GROUP)
        e_g = e[:, :, sl]
        acc = e_g
        span, length = 1, HALO + t_seq
        while span < win:
            length -= span
            acc = take(acc, span, length) + take(acc, 0, length)
            span *= 2
        ws = take(acc, length - t_seq, t_seq)
        inv_cnt = 1.0 / jnp.minimum(pos, win).astype(F32)
        mixed = (ws * inv_cnt - take(e_g, HALO, t_seq)).reshape(rows, POOL_GROUP)
        y_g = jnp.dot(mixed.astype(BF16), pw_ref[gi], preferred_element_type=F32)
        gp = gp_all[:, sl] if carry_pool else gp_ref[:, sl].astype(F32)
        yb_parts.append((y_g * ps_ref[:, sl] * (gp * _sigmoid(gp))).astype(BF16))
        gated_a.append(_sigmoid(ga_ref[:, cols[gi]].astype(F32)) * pa)
    y_b = jnp.concatenate(yb_parts, axis=1)
    if carry_pool:
        y_b = jnp.dot(permt_ref[...], y_b, preferred_element_type=F32).astype(BF16)
    pb = [None] * n_col
    pb[0] = jnp.dot(y_b, wb1_ref[:, cols[0]], preferred_element_type=F32)
    for ci in range(n_col):
        if ci + 1 < n_col:
            pb[ci + 1] = jnp.dot(y_b, wb1_ref[:, cols[ci + 1]], preferred_element_type=F32)
        m_ref[:, cols[ci]] = (gated_a[ci] + _sigmoid(gb_ref[:, cols[ci]].astype(F32)) * pb[ci]).astype(m_ref.dtype)


def _mix(ya, z, pool_all, pool_w_b, pool_scale, w_branch_b, layer, *, n_seq, t_seq, tiles_per_seq, n_past, row0):
    m = ya.shape[0]
    rows = n_seq * t_seq
    assert row0 % rows == 0
    z_off = row0 // rows
    carry_pool = pool_all is not None
    in_specs = [
        pl.BlockSpec((rows, RWKV_WIDTH), lambda i: (i, 0)),
        pl.BlockSpec((rows, POOL_WIDTH), lambda i: (i + z_off, Z_U)),
        pl.BlockSpec((rows, POOL_WIDTH), lambda i: (i + z_off, Z_G_POOL)),
        pl.BlockSpec((rows, D_MODEL), lambda i: (i + z_off, Z_GATE_A)),
        pl.BlockSpec((rows, D_MODEL), lambda i: (i + z_off, Z_GATE_B)),
    ]
    args = [ya, z, z, z, z]
    out_specs = [pl.BlockSpec((rows, D_MODEL), lambda i: (i, 0))]
    out_shape = [jax.ShapeDtypeStruct((m, D_MODEL), BF16)]
    aliases = {}
    if carry_pool:
        assert tiles_per_seq == 1 and t_seq < POOL_BUF
        r = np.arange(rows)
        perm = np.zeros((rows, rows), np.float32)
        perm[(r % t_seq) * n_seq + r // t_seq, r] = 1.0
        pool_spec = pl.BlockSpec((None, POOL_BUF, n_seq, POOL_WIDTH), lambda i: (layer, 0, i, 0))
        const = pl.BlockSpec((rows, rows), lambda i: (0, 0))
        in_specs += [pool_spec, const, const]
        aliases = {len(args): 1}
        args += [pool_all, jnp.asarray(perm, BF16), jnp.asarray(perm.T, BF16)]
        out_specs.append(pool_spec)
        out_shape.append(jax.ShapeDtypeStruct(pool_all.shape, F32))
    else:
        assert n_seq == 1 and rows % HALO == 0 and row0 == 0
        in_specs.append(pl.BlockSpec((HALO, POOL_WIDTH), lambda i: (jnp.maximum(i * (rows // HALO) - 1, 0), Z_U)))
        args.append(z)
    in_specs += [
        pl.BlockSpec((None, len(POOL_WINDOWS), POOL_GROUP, POOL_GROUP), lambda i: (layer, 0, 0, 0)),
        pl.BlockSpec((None, 1, POOL_WIDTH), lambda i: (layer, 0, 0)),
        pl.BlockSpec((None, None, RWKV_WIDTH, D_MODEL), lambda i: (layer, 0, 0, 0), pipeline_mode=pl.Buffered(1)),
        pl.BlockSpec((None, None, POOL_WIDTH, D_MODEL), lambda i: (layer, 1, 0, 0), pipeline_mode=pl.Buffered(1)),
    ]
    args += [pool_w_b, pool_scale.reshape(DEPTH, 1, POOL_WIDTH), w_branch_b, w_branch_b]
    outs = pl.pallas_call(
        functools.partial(_mix_kernel, n_seq=n_seq, t_seq=t_seq, tiles_per_seq=tiles_per_seq, n_past=n_past,
                          carry_pool=carry_pool),
        grid=(m // rows,),
        in_specs=in_specs, out_specs=out_specs, out_shape=out_shape,
        input_output_aliases=aliases,
        name="mix",
        compiler_params=pltpu.CompilerParams(
            dimension_semantics=("parallel",), vmem_limit_bytes=VMEM_LIMIT_BYTES),
    )(*args)
    return outs if carry_pool else (outs[0], None)


def _out_kernel(*refs, n_split, emit_h):
    if emit_h:
        (m_ref, x_ref, p_ref, g_ref, gn_ref, wo_ref, wg_ref, wp_ref), (o_ref, h_ref) = refs[:8], refs[-2:]
    else:
        m_ref, x_ref, p_ref, g_ref, wo_ref, wg_ref, wp_ref, o_ref = refs
    rows = m_ref.shape[0] // n_split
    parts = [slice(i * rows, (i + 1) * rows) for i in range(n_split)]
    outs = [jnp.dot(m_ref[sl, :], wo_ref[...], preferred_element_type=F32) for sl in parts]
    ples = [jnp.dot(p_ref[sl, :].astype(BF16), wp_ref[...], preferred_element_type=F32) for sl in parts]
    xs = [x_ref[sl, :] + _rmsnorm(o, g_ref[...]) for sl, o in zip(parts, outs)]
    gates = [_sigmoid(jnp.dot(x.astype(BF16), wg_ref[...], preferred_element_type=F32)) for x in xs]
    for sl, x, gate, ple in zip(parts, xs, gates, ples):
        x_new = x + gate * ple
        o_ref[sl, :] = x_new
        if emit_h:
            h_ref[sl, :] = _rmsnorm(x_new, gn_ref[...]).astype(h_ref.dtype)


def _out(mm_, x2d, p3d, norm_post, norm_pre, w_out_b, w_gate_b, w_ple_b, layer, *, rows, row0, total_rows,
         h_into):
    m = x2d.shape[0]
    emit_h = layer + 1 < DEPTH
    once = pl.Buffered(1)
    wspec = lambda k, l=layer: pl.BlockSpec((None, k, D_MODEL), lambda i: (l, 0, 0), pipeline_mode=once)
    row_spec = pl.BlockSpec((rows, D_MODEL), lambda i: (i, 0))
    in_specs = [row_spec, row_spec, pl.BlockSpec((None, rows, PLE_DIM), lambda i: (layer, i, 0)), wspec(1)]
    args = [mm_, x2d, p3d, norm_post.reshape(DEPTH, 1, D_MODEL)]
    out_specs, out_shape = [row_spec], [jax.ShapeDtypeStruct((m, D_MODEL), F32)]
    if emit_h:
        in_specs.append(wspec(1, layer + 1))
        args.append(norm_pre.reshape(DEPTH, 1, D_MODEL))
    in_specs += [wspec(D_MODEL), wspec(D_MODEL), wspec(PLE_DIM)]
    args += [w_out_b, w_gate_b, w_ple_b]
    aliases = {}
    if emit_h:
        h_spec, h_shape, x_specs, x_args, alias_in = _shared_rows_out(
            (rows, D_MODEL), row0, total_rows, D_MODEL, BF16, h_into, len(args))
        out_specs.append(h_spec)
        out_shape.append(h_shape)
        in_specs += x_specs
        args += x_args
        if alias_in is not None:
            aliases = {alias_in: 1}
    outs = pl.pallas_call(
        functools.partial(_out_kernel, n_split=rows // OUT_SUB_ROWS, emit_h=emit_h),
        grid=(m // rows,),
        in_specs=in_specs, out_specs=out_specs, out_shape=out_shape,
        input_output_aliases=aliases,
        name="out",
        compiler_params=pltpu.CompilerParams(
            dimension_semantics=("parallel",), vmem_limit_bytes=VMEM_LIMIT_BYTES),
    )(*args)
    return (outs[0], outs[1]) if emit_h else (outs[0], None)


def _pair_rows(s):
    l, b = s.shape[:2]
    return s.reshape(l, b, N_HEADS, HEAD_DIM // 2, 2 * HEAD_DIM)


def _unpair_rows(s):
    l, b = s.shape[:2]
    return s.reshape(l, b, N_HEADS, HEAD_DIM, HEAD_DIM)


def _layer_weights(i, mu_rkv, mu_lora, w0, w1, w2, a0, a1, a2, v0, v1, v2, k_k, k_a, r_k, ln_w, ln_b):
    r_ = RWKV_WIDTH
    mu = mu_lora[i]
    if i > 0:
        v1_i, v2_i, v0_i = v1[i - 1], v2[i - 1], v0[i - 1]
    else:
        v1_i = jnp.zeros((D_MODEL, MV_RANK), F32)
        v2_i = jnp.zeros((MV_RANK, r_), F32)
        v0_i = jnp.zeros((r_,), F32)
    w_lora = jnp.concatenate(
        [(1.0 - mu[0])[:, None] * w1[i], (1.0 - mu[1])[:, None] * a1[i], (1.0 - mu[2])[:, None] * v1_i,
         mu[0][:, None] * w1[i], mu[1][:, None] * a1[i], mu[2][:, None] * v1_i], axis=1).astype(BF16)
    z96 = jnp.zeros((DECAY_RANK, r_), F32)
    z64 = jnp.zeros((MV_RANK, r_), F32)
    w2e = jnp.concatenate([
        jnp.concatenate([w2[i], z96, z64], axis=0),
        jnp.concatenate([z96, a2[i], z64], axis=0),
        jnp.concatenate([z96, z96, v2_i], axis=0)], axis=1).astype(BF16)
    rows = [mu_rkv[i, 0], mu_rkv[i, 1], mu_rkv[i, 2], w0[i], a0[i], v0_i, k_k[i], k_a[i], r_k[i], ln_w[i], ln_b[i]]
    vecs = jnp.concatenate([jnp.stack(rows), jnp.zeros((_N_VEC_ROWS - len(rows), r_), F32)], axis=0)
    return w_lora, w2e, vecs


class _Group:
    def __init__(self, x, p, wkv, shift, pool, *, n_past, row0, n_seq, n_sub, mix_seq, mix_t):
        self.b, self.t, d = x.shape
        self.m = self.b * self.t
        self.x2d = x.reshape(self.m, d)
        self.p3d = p.reshape(DEPTH, self.m, PLE_DIM)
        self.zero_state = wkv is None
        self.shift, self.n_past, self.row0 = shift, n_past, row0
        self.n_seq, self.n_sub, self.mix_seq, self.mix_t = n_seq, n_sub, mix_seq, mix_t
        self.s_all = _pair_rows(
            jnp.zeros((DEPTH, self.b, N_HEADS, HEAD_DIM, HEAD_DIM), F32) if self.zero_state else wkv)
        self.pool_all = None if self.zero_state else jnp.transpose(pool, (0, 2, 1, 3))
        self.v_first = None
        self.shift_out, self.pool_out = [], []

    def outputs(self):
        pool_new = jnp.stack(self.pool_out) if self.zero_state else jnp.transpose(self.pool_all, (0, 2, 1, 3))
        return (self.x2d.reshape(self.b, self.t, D_MODEL), _unpair_rows(self.s_all), jnp.stack(self.shift_out),
                pool_new)


def _run_layers(groups, lw, params):
    (w_in, norm_pre, norm_post, pool_w_b, pool_scale, w_branch_b, w_out_b, w_ple_b, w_gate_b) = params
    total_rows = sum(g.m for g in groups)
    assert total_rows % PROJ_TM == 0
    h_all = None
    for g in groups:
        h_all = _norm(g.x2d, norm_pre[0], rows=NORM_ROWS, out_dtype=BF16, row0=g.row0, total_rows=total_rows,
                      into=h_all)
    for i in range(DEPTH):
        w_lora, w2e, vecs = lw[i]
        z = _proj(h_all, w_in, i, w_lora, tm=PROJ_TM, n_main=N_MAIN_TILES, out_dtype=BF16)
        h_all = None
        for g in groups:
            g.shift_out.append(
                _norm(g.x2d.reshape(g.b, g.t, D_MODEL)[:, -1], norm_pre[i], rows=g.b, out_dtype=F32))
            if g.zero_state:
                assert g.t >= POOL_BUF
                prev0 = jnp.zeros((g.b, PREV_COLS), F32)
                g.pool_out.append(z[g.row0:g.row0 + g.m].reshape(g.b, g.t, -1)[
                    :, -POOL_BUF:, Z_U * POOL_WIDTH:(Z_U + 1) * POOL_WIDTH].astype(F32))
            else:
                prev0 = _proj(g.shift[i].astype(BF16), w_in, i, w_lora, tm=g.b, n_main=3 * RWKV_WIDTH // PROJ_TN,
                              out_dtype=F32)
            ya, g.v_first, g.s_all = _wkv(z, prev0, g.v_first, w2e, vecs, g.s_all, i, n_batch=g.b, t_len=g.t,
                                          n_seq=g.n_seq, n_sub=g.n_sub, row0=g.row0)
            mm_, g.pool_all = _mix(ya, z, g.pool_all, pool_w_b, pool_scale, w_branch_b, i, n_seq=g.mix_seq,
                                   t_seq=g.mix_t, tiles_per_seq=g.t // g.mix_t, n_past=g.n_past, row0=g.row0)
            g.x2d, h_all = _out(mm_, g.x2d, g.p3d, norm_post, norm_pre, w_out_b, w_gate_b, w_ple_b, i,
                                rows=OUT_ROWS, row0=g.row0, total_rows=total_rows, h_into=h_all)


def kernel(x_prompt, x_sample, state_wkv, state_shift, state_pool, p_prompt, p_sample, norm_pre, norm_post, w_in, mu_rkv, mu_lora, w0, w1, w2, a0, a1, a2, v0, v1, v2, k_k, k_a, r_k, ln_w, ln_b, pool_w, pool_scale, w_branch, w_out, w_ple, w_ple_gate):
    lw = [_layer_weights(i, mu_rkv, mu_lora, w0, w1, w2, a0, a1, a2, v0, v1, v2, k_k, k_a, r_k, ln_w, ln_b)
          for i in range(DEPTH)]
    params = (w_in, norm_pre, norm_post, pool_w.astype(BF16), pool_scale, w_branch.astype(BF16),
              w_out.astype(BF16), w_ple.astype(BF16), w_ple_gate.astype(BF16))
    prompt = _Group(x_prompt, p_prompt, None, None, None, n_past=0, row0=0,
                    n_seq=1, n_sub=4, mix_seq=1, mix_t=256)
    sample = _Group(x_sample, p_sample, state_wkv, state_shift, state_pool, n_past=PAST_LEN, row0=prompt.m,
                    n_seq=8, n_sub=2, mix_seq=32, mix_t=8)
    _run_layers([prompt, sample], lw, params)
    y_p, wkv_p, shift_p, pool_p = prompt.outputs()
    y_s, wkv_s, shift_s, pool_s = sample.outputs()
    return (y_p, y_s, wkv_p, shift_p, pool_p, wkv_s, shift_s, pool_s)
```

```python
import functools

import jax
import jax.numpy as jnp
import numpy as np
from jax import lax
from jax.experimental import pallas as pl
from jax.experimental.pallas import tpu as pltpu

F32 = jnp.float32
BF16 = jnp.bfloat16

D_MODEL = 2048
DEPTH = 4
PAST_LEN = 16384
RWKV_WIDTH = 1024
HEAD_DIM = 64
N_HEADS = 16
POOL_WIDTH = 1024
POOL_WINDOWS = (2, 4, 8, 16)
POOL_GROUP = 256
POOL_BUF = 15
PLE_DIM = 256
DECAY_RANK = 96
AAA_RANK = 96
MV_RANK = 64
LORA_COLS = DECAY_RANK + AAA_RANK + MV_RANK
IN_COLS = 10240
RMS_EPS = 1e-6
GN_EPS = 64e-5
DECAY_SCALE = 0.606531

HEADS_PER_GROUP = 4
GROUP_LANES = HEADS_PER_GROUP * HEAD_DIM
N_GROUPS = N_HEADS // HEADS_PER_GROUP
CHUNK = 64
EXPANDED = HEADS_PER_GROUP * CHUNK
F32_ROWS = 8
BF16_ROWS = 16
HALO = 16
PROJ_TN = 2 * LORA_COLS
N_MAIN_TILES = IN_COLS // PROJ_TN
Z_R, Z_K, Z_V, Z_G_RWKV, Z_U, Z_G_POOL = range(6)
Z_GATE_A, Z_GATE_B = 3, 4
Z_LORA = N_MAIN_TILES
PREV_COLS = 3 * RWKV_WIDTH + PROJ_TN
LB_OFF = 3 * RWKV_WIDTH + LORA_COLS

NORM_ROWS = 512
PROJ_TM = 3072
OUT_ROWS = 256
OUT_SUB_ROWS = 128

VMEM_LIMIT_BYTES = 56 * 1024 * 1024


def _sigmoid(x):
    return 0.5 * jnp.tanh(0.5 * x) + 0.5


def _split_bf16(x):
    hi = x.astype(BF16)
    lo = (x - hi.astype(F32)).astype(BF16)
    return hi, lo


_NN = (((1,), (0,)), ((), ()))
_NT = (((1,), (1,)), ((), ()))


def _dot(a, b, dims=_NN):
    return lax.dot_general(a, b, dimension_numbers=dims, preferred_element_type=F32)


def _segsum(xs, bones_b):
    rows = xs[0].shape[0]
    s = _dot(jnp.concatenate([x.astype(BF16) for x in xs], axis=0), bones_b)
    return [s[i * rows:(i + 1) * rows] for i in range(len(xs))]


def _proj_kernel(h_ref, w_ref, wl_ref, z_ref, *, n_main):
    j = pl.program_id(1)

    @pl.when(j < n_main)
    def _():
        z_ref[...] = jnp.dot(h_ref[...], w_ref[...].astype(BF16), preferred_element_type=F32).astype(z_ref.dtype)

    @pl.when(j == n_main)
    def _():
        z_ref[...] = jnp.dot(h_ref[...], wl_ref[...], preferred_element_type=F32).astype(z_ref.dtype)


def _proj(h2d, w_in, layer, w_lora, *, tm, n_main, out_dtype):
    m, d = h2d.shape
    tn = PROJ_TN
    assert m % tm == 0
    return pl.pallas_call(
        functools.partial(_proj_kernel, n_main=n_main),
        grid=(m // tm, n_main + 1),
        in_specs=[
            pl.BlockSpec((tm, d), lambda i, j: (i, 0)),
            pl.BlockSpec((None, d, tn), lambda i, j: (layer, 0, jnp.minimum(j, n_main - 1))),
            pl.BlockSpec((d, tn), lambda i, j: (0, 0), pipeline_mode=pl.Buffered(1)),
        ],
        out_specs=pl.BlockSpec((tm, tn), lambda i, j: (i, j)),
        out_shape=jax.ShapeDtypeStruct((m, (n_main + 1) * tn), out_dtype),
        name="proj",
        compiler_params=pltpu.CompilerParams(
            dimension_semantics=("parallel", "arbitrary"), vmem_limit_bytes=VMEM_LIMIT_BYTES),
    )(h2d, w_in, w_lora)


def _rmsnorm(x, gain):
    return x * lax.rsqrt(jnp.mean(x * x, axis=-1, keepdims=True) + RMS_EPS) * gain


def _norm_kernel(x_ref, g_ref, *rest):
    o_ref = rest[-1]
    o_ref[...] = _rmsnorm(x_ref[...], g_ref[...]).astype(o_ref.dtype)


def _shared_rows_out(row_spec_shape, row0, total_rows, d, dtype, into, n_inputs):
    rows = row_spec_shape[0]
    assert row0 % rows == 0
    blk0 = row0 // rows
    spec = pl.BlockSpec(row_spec_shape, lambda i: (i + blk0, 0))
    shape = jax.ShapeDtypeStruct((total_rows, d), dtype)
    extra_specs = [] if into is None else [pl.BlockSpec(memory_space=pl.ANY)]
    extra_args = [] if into is None else [into]
    return spec, shape, extra_specs, extra_args, (None if into is None else n_inputs)


def _norm(x2d, gain, *, rows, out_dtype, row0=0, total_rows=None, into=None):
    m, d = x2d.shape
    rows = min(rows, m)
    assert m % rows == 0
    o_spec, o_shape, x_specs, x_args, alias_in = _shared_rows_out(
        (rows, d), row0, total_rows or m, d, out_dtype, into, 2)
    return pl.pallas_call(
        _norm_kernel,
        grid=(m // rows,),
        in_specs=[pl.BlockSpec((rows, d), lambda i: (i, 0)), pl.BlockSpec((1, d), lambda i: (0, 0))] + x_specs,
        out_specs=o_spec, out_shape=o_shape,
        input_output_aliases={} if alias_in is None else {alias_in: 0},
        name="norm",
        compiler_params=pltpu.CompilerParams(dimension_semantics=("parallel",)),
    )(x2d, gain.reshape(1, d), *x_args)


_V_MU_R, _V_MU_K, _V_MU_V, _V_W0, _V_A0, _V_V0, _V_KK, _V_KA, _V_RK, _V_LNW, _V_LNB = range(11)
_N_VEC_ROWS = 16


def _wkv_constants(n_seq, t_seq):
    c, e = CHUNK, EXPANDED
    assert n_seq * t_seq == c
    row = np.arange(c)
    q_of = row // t_seq
    tri = (row[:, None] >= row[None, :]) & (q_of[:, None] == q_of[None, :])
    erow = np.arange(e)
    es = erow % c
    same = q_of[:, None] == (es // t_seq)[None, :]
    mask_s = same & (row[:, None] > es[None, :])
    mask_i = same & (row[:, None] >= es[None, :])
    mask_sc = np.block([[mask_s, mask_s], [mask_i, mask_i]])
    lane = np.arange(GROUP_LANES)
    bones = (lane[:, None] // HEAD_DIM) == (lane[None, :] // HEAD_DIM)
    levels = [np.eye(e, dtype=bool), (erow[:, None] // 2) == (erow[None, :] // 2)]
    m = 2
    while m < t_seq:
        levels.append(((erow[:, None] // (2 * m)) == (erow[None, :] // (2 * m)))
                      & ((erow[:, None] % (2 * m)) >= m) & ((erow[None, :] % (2 * m)) < m))
        m *= 2
    col = np.arange(2 * c)
    qmask = ((col[None, :] % c) // t_seq) == np.arange(n_seq)[:, None]
    f = lambda a: jnp.asarray(a.astype(np.float32))
    return dict(tri=f(tri).astype(BF16), mask_sc=f(mask_sc).astype(BF16), bones=f(bones),
                bones_b=f(bones).astype(BF16), levels=f(np.stack(levels)).astype(BF16),
                qmask=f(qmask).reshape(n_seq, 1, 2 * c))


def _wkv_kernel(*refs, n_seq, t_seq, n_sub, chain, n_steps, has_vres, n_levels):
    it = iter(refs)
    zr_ref, zk_ref, zv_ref, zg_ref, zl_ref, p0_ref = (next(it) for _ in range(6))
    vfirst_ref = next(it) if has_vres else None
    w2_ref, vec_ref, s0_ref = next(it), next(it), next(it)
    tri_ref, msc_ref, bones_ref, bonesb_ref, lev_ref, qmask_ref = (next(it) for _ in range(6))
    ya_ref = next(it)
    vout_ref = None if has_vres else next(it)
    sout_ref = next(it)
    sbd_ref, prev_ref = next(it), next(it)

    step = pl.program_id(1)
    gl, rw, e = GROUP_LANES, RWKV_WIDTH, EXPANDED
    n_slots = n_seq if chain else n_sub * n_seq
    slot = (lambda j, q: q) if chain else (lambda j, q: j * n_seq + q)
    bones = bones_ref[...]
    bones_b = bonesb_ref[...]
    mask_sc = msc_ref[...]
    eye_b = lev_ref[0]
    tri = tri_ref[...]
    vec = vec_ref[...]
    vrow = lambda i: vec[i:i + 1, :]
    groups = range(N_GROUPS)
    sls = [slice(gi * gl, (gi + 1) * gl) for gi in groups]

    def expand(x):
        return jnp.concatenate([x] * HEADS_PER_GROUP, axis=0) * bones_b

    low_lanes = lax.broadcasted_iota(jnp.int32, (1, 2 * HEAD_DIM), 1) < HEAD_DIM

    def head_slot(hd):
        h = hd % HEADS_PER_GROUP
        return hd // HEADS_PER_GROUP, h * HEAD_DIM, h // 2, h % 2 == 1

    def load_state(s, g):
        return jnp.concatenate([sbd_ref[s, g, 0], sbd_ref[s, g, 1]], axis=1)

    def store_state(s, g, val):
        sbd_ref[s, g, 0] = val[:, :2 * HEAD_DIM]
        sbd_ref[s, g, 1] = val[:, 2 * HEAD_DIM:]

    @pl.when(step == 0)
    def _():
        for s in range(n_slots):
            for hd in range(N_HEADS):
                g, row0, half, upper = head_slot(hd)
                packed = s0_ref[s, hd]
                even = jnp.where(low_lanes, packed, 0.0)
                odd = jnp.where(low_lanes, 0.0, packed)
                if upper:
                    even = pltpu.roll(even, HEAD_DIM, axis=1)
                else:
                    odd = pltpu.roll(odd, HEAD_DIM, axis=1)
                sbd_ref[s, g, half, pl.ds(row0, HEAD_DIM // 2, stride=2), :] = even
                sbd_ref[s, g, half, pl.ds(row0 + 1, HEAD_DIM // 2, stride=2), :] = odd
                sbd_ref[s, g, 1 - half, row0:row0 + HEAD_DIM, :] = jnp.zeros((HEAD_DIM, 2 * HEAD_DIM), F32)
        prev_ref[...] = p0_ref[0]

    row = lax.broadcasted_iota(jnp.int32, (CHUNK, 1), 0)
    is_first = (row % t_seq) == 0
    lora_lane = lax.broadcasted_iota(jnp.int32, (1, LORA_COLS), 1)

    def shifted(z, p0):
        rolled = pltpu.roll(z, 1, axis=0)
        if n_seq == 1:
            first = jnp.broadcast_to(p0, z.shape)
        else:
            first = jnp.broadcast_to(p0[:, None, :], (n_seq, t_seq, z.shape[1])).reshape(z.shape)
        return jnp.where(is_first, first, rolled)

    n_rows = n_sub * CHUNK
    zl_all = zl_ref[...].astype(F32)
    lb_all = zl_all[:, LORA_COLS:]
    p_l = prev_ref[:, LB_OFF:LB_OFF + LORA_COLS]
    if n_slots == 1:
        lb_first = jnp.broadcast_to(p_l, lb_all.shape)
    else:
        lb_first = jnp.broadcast_to(p_l[:, None, :], (n_slots, t_seq, LORA_COLS)).reshape(lb_all.shape)
    row_all = lax.broadcasted_iota(jnp.int32, (n_rows, 1), 0)
    lb_prev = jnp.where(row_all % (n_rows if chain else t_seq) == 0, lb_first, pltpu.roll(lb_all, 1, axis=0))
    lin = zl_all[:, :LORA_COLS] + lb_prev
    lin = jnp.where(lora_lane < DECAY_RANK, jnp.tanh(lin), lin).astype(BF16)
    d_full = jnp.dot(lin, w2_ref[...], preferred_element_type=F32)
    lb_last = lb_all[n_rows - 1:n_rows, :]

    ctx = [dict() for _ in range(n_sub)]

    def prep(j):
        c = ctx[j]
        rows = slice(j * CHUNK, (j + 1) * CHUNK)
        zr, zk, zv = zr_ref[rows, :].astype(F32), zk_ref[rows, :].astype(F32), zv_ref[rows, :].astype(F32)
        c["last"] = [x[CHUNK - 1:CHUNK, :] for x in (zr, zk, zv)]
        if chain and j > 0:
            p_r, p_k, p_v = ctx[j - 1]["last"]
        else:
            ps = slice(0, n_seq) if chain else slice(j * n_seq, (j + 1) * n_seq)
            p_r, p_k, p_v = prev_ref[ps, 0:rw], prev_ref[ps, rw:2 * rw], prev_ref[ps, 2 * rw:3 * rw]
        r = zr + (shifted(zr, p_r) - zr) * vrow(_V_MU_R)
        k = zk + (shifted(zk, p_k) - zk) * vrow(_V_MU_K)
        v = zv + (shifted(zv, p_v) - zv) * vrow(_V_MU_V)
        d_all = d_full[rows]
        wlog = -DECAY_SCALE * _sigmoid(vrow(_V_W0) + d_all[:, 0:rw])
        alr = _sigmoid(vrow(_V_A0) + d_all[:, rw:2 * rw])
        if has_vres:
            v = v + (vfirst_ref[rows, :] - v) * _sigmoid(vrow(_V_V0) + d_all[:, 2 * rw:3 * rw])
        else:
            vout_ref[rows, :] = v
        kkr = k * vrow(_V_KK)
        kf = k * (1.0 + (alr - 1.0) * vrow(_V_KA))
        c["rkr"] = r * kf * vrow(_V_RK)
        g_all = zg_ref[rows, :].astype(F32)
        c["gate"] = g_all * _sigmoid(g_all)
        w_hi, w_lo = _split_bf16(wlog)
        cum = jnp.dot(tri, w_hi, preferred_element_type=F32) + jnp.dot(tri, w_lo, preferred_element_type=F32)
        p_inc = jnp.exp(cum)
        inv_p = jnp.exp(-cum)
        p_exc = jnp.exp(cum - wlog)
        c["p_inc"] = p_inc
        yield
        kk_n2 = _segsum([kkr[:, sl] * kkr[:, sl] for sl in sls], bones_b)
        kk = [kkr[:, sl] * lax.rsqrt(jnp.maximum(n2, 1e-24)) for sl, n2 in zip(sls, kk_n2)]
        c["v_g"] = [v[:, sl] for sl in sls]
        c["a_f"] = [-kk[gi] * p_exc[:, sls[gi]] for gi in groups]
        c["r_f"] = [r[:, sl] * p_inc[:, sl] for sl in sls]
        a_t = [x.astype(BF16) for x in c["a_f"]]
        r_t = [x.astype(BF16) for x in c["r_f"]]
        c["b_t"] = [(kk[gi] * alr[:, sls[gi]] * inv_p[:, sls[gi]]).astype(BF16) for gi in groups]
        c["k_t"] = [(kf[:, sl] * inv_p[:, sl]).astype(BF16) for sl in sls]
        c["v_e"] = [expand(x.astype(BF16)) for x in c["v_g"]]
        c["ar"] = [jnp.concatenate([a_t[gi], r_t[gi]], axis=0) for gi in groups]
        yield
        sc = [_dot(c["ar"][gi], jnp.concatenate([expand(c["b_t"][gi]), expand(c["k_t"][gi])], axis=0), _NT)
              for gi in groups]
        c["scm"] = [x.astype(BF16) * mask_sc for x in sc]
        c["n_bd"] = [expand(x[:CHUNK, :e]) for x in c["scm"]]
        c["xs"] = [eye_b + n * lev_ref[1] for n in c["n_bd"]]
        yield
        c["akv"] = [_dot(c["scm"][gi][:, e:], c["v_e"][gi]) for gi in groups]
        yield

    def inverse(j):
        c = ctx[j]
        m = 2
        xs32 = None
        for lvl in range(2, n_levels):
            n_c = [c["n_bd"][gi] * lev_ref[lvl] for gi in groups]
            if m % F32_ROWS == 0:
                wide = m % BF16_ROWS != 0
                src = xs32 if wide else c["xs"]
                n_blk = e // (2 * m)
                lower = lambda x: jnp.concatenate([x[b * 2 * m + m:(b + 1) * 2 * m] for b in range(n_blk)], axis=0)
                eye_l = lower(eye_b.astype(src[0].dtype)).astype(BF16)
                t1 = [_dot(lower(src[gi]).astype(BF16), n_c[gi]).astype(BF16) + eye_l for gi in groups]
                yield
                new = [_dot(t1[gi], c["xs"][gi]).astype(src[0].dtype) for gi in groups]
                c["xs"] = [jnp.concatenate(
                    [piece for b in range(n_blk)
                     for piece in (x[b * 2 * m:b * 2 * m + m], n[b * m:(b + 1) * m])], axis=0).astype(BF16)
                    for x, n in zip(src, new)]
                xs32 = None
            else:
                t1 = [_dot(c["xs"][gi], n_c[gi]).astype(BF16) + eye_b for gi in groups]
                yield
                full = [_dot(t1[gi], c["xs"][gi]) for gi in groups]
                if (2 * m) % F32_ROWS == 0 and (2 * m) % BF16_ROWS != 0:
                    xs32 = full
                c["xs"] = [x.astype(BF16) for x in full]
            yield
            m *= 2

    def tail(j):
        c = ctx[j]
        rows = slice(j * CHUNK, (j + 1) * CHUNK)
        x_c = [(x[0:CHUNK] + x[CHUNK:2 * CHUNK]) + (x[2 * CHUNK:3 * CHUNK] + x[3 * CHUNK:4 * CHUNK])
               for x in c["xs"]]
        a_s, r_s = [], []
        for gi in groups:
            parts_a, parts_r = [], []
            for q in range(n_seq):
                lo_, hi_ = q * t_seq, (q + 1) * t_seq
                lhs = c["ar"][gi] if n_seq == 1 else jnp.concatenate(
                    [c["a_f"][gi][lo_:hi_], c["r_f"][gi][lo_:hi_]], axis=0).astype(BF16)
                res = _dot(lhs, load_state(slot(j, q), gi).astype(BF16), _NT)
                parts_a.append(res[:t_seq])
                parts_r.append(res[t_seq:])
            a_s.append(parts_a[0] if n_seq == 1 else jnp.concatenate(parts_a, axis=0))
            r_s.append(parts_r[0] if n_seq == 1 else jnp.concatenate(parts_r, axis=0))
        yield
        y = [a_s[gi] + c["akv"][gi][:CHUNK] for gi in groups]
        u = [_dot(x_c[gi], expand(y[gi].astype(BF16))) for gi in groups]
        yield
        o = [r_s[gi] + c["akv"][gi][CHUNK:] + _dot(c["scm"][gi][CHUNK:, :e], expand(u[gi].astype(BF16)))
             for gi in groups]
        yield
        for gi in groups:
            uv_t = jnp.concatenate([u[gi], c["v_g"][gi]], axis=0).T
            bk = jnp.concatenate([c["b_t"][gi], c["k_t"][gi]], axis=0)
            for q in range(n_seq):
                lhs = uv_t if n_seq == 1 else uv_t * qmask_ref[q]
                p_last = c["p_inc"][(q + 1) * t_seq - 1:(q + 1) * t_seq, sls[gi]]
                s = slot(j, q)
                store_state(s, gi, bones * ((load_state(s, gi) + _dot(lhs.astype(BF16), bk)) * p_last))
        yield
        mean = [x * (1.0 / HEAD_DIM) for x in _segsum(o, bones_b)]
        d = [o[gi] - mean[gi] for gi in groups]
        var = [x * (1.0 / HEAD_DIM) for x in _segsum([x * x for x in d], bones_b)]
        bonus = _segsum([c["rkr"][:, sl] for sl in sls], bones_b)
        yield
        for gi in groups:
            sl = sls[gi]
            on = d[gi] * lax.rsqrt(var[gi] + GN_EPS) * vec[_V_LNW:_V_LNW + 1, sl] + vec[_V_LNB:_V_LNB + 1, sl]
            ya_ref[rows, sl] = ((on + bonus[gi] * c["v_g"][gi]) * c["gate"][:, sl]).astype(ya_ref.dtype)

    preps, invs, tails = ([f(j) for j in range(n_sub)] for f in (prep, inverse, tail))
    for phase in range(n_sub + 2):
        active = []
        if 0 <= phase - 2 < n_sub:
            active.append(tails[phase - 2])
        if 0 <= phase - 1 < n_sub:
            active.append(invs[phase - 1])
        if phase < n_sub:
            active.append(preps[phase])
        while active:
            for gen in list(active):
                try:
                    next(gen)
                except StopIteration:
                    active.remove(gen)

    if chain and n_steps > 1:
        assert n_seq == 1
        last_r, last_k, last_v = ctx[n_sub - 1]["last"]
        prev_ref[:, 0:rw] = last_r
        prev_ref[:, rw:2 * rw] = last_k
        prev_ref[:, 2 * rw:3 * rw] = last_v
        prev_ref[:, LB_OFF:LB_OFF + LORA_COLS] = lb_last

    @pl.when(step == n_steps - 1)
    def _():
        for s in range(n_slots):
            for hd in range(N_HEADS):
                g, row0, half, upper = head_slot(hd)
                even = sbd_ref[s, g, half, pl.ds(row0, HEAD_DIM // 2, stride=2), :]
                odd = sbd_ref[s, g, half, pl.ds(row0 + 1, HEAD_DIM // 2, stride=2), :]
                if upper:
                    even = pltpu.roll(even, HEAD_DIM, axis=1)
                else:
                    odd = pltpu.roll(odd, HEAD_DIM, axis=1)
                sout_ref[s, hd] = jnp.where(low_lanes, even, odd)


def _wkv(z, prev0, vfirst, w2e, vecs, s_all, layer, *, n_batch, t_len, n_seq, n_sub, row0):
    m = n_batch * t_len
    t_seq = CHUNK // n_seq
    n_chunks = t_len // t_seq
    chain = n_seq == 1
    if chain:
        assert n_chunks % n_sub == 0
        n_steps, nb, n_slots = n_chunks // n_sub, n_batch, n_seq
    else:
        assert n_chunks == 1 and n_batch % (n_seq * n_sub) == 0
        n_steps, nb, n_slots = 1, n_batch // (n_seq * n_sub), n_seq * n_sub
    has_vres = vfirst is not None
    consts = _wkv_constants(n_seq, t_seq)
    n_levels = consts["levels"].shape[0]
    rw = RWKV_WIDTH
    rows = n_sub * CHUNK
    rb = lambda b, s: b * n_steps + s
    prev0 = prev0.reshape(nb, n_slots, PREV_COLS)

    assert row0 % rows == 0
    z_off = row0 // rows
    zspec = lambda col: pl.BlockSpec((rows, rw), lambda b, s: (rb(b, s) + z_off, col))
    row_spec = pl.BlockSpec((rows, rw), lambda b, s: (rb(b, s), 0))
    full = lambda a: pl.BlockSpec(a.shape, lambda b, s: (0,) * a.ndim)
    sspec = pl.BlockSpec((None, n_slots, N_HEADS, HEAD_DIM // 2, 2 * HEAD_DIM), lambda b, s: (layer, b, 0, 0, 0))

    in_specs = [zspec(Z_R), zspec(Z_K), zspec(Z_V), zspec(Z_G_RWKV),
                pl.BlockSpec((rows, PROJ_TN), lambda b, s: (rb(b, s) + z_off, Z_LORA)),
                pl.BlockSpec((1, n_slots, PREV_COLS), lambda b, s: (b, 0, 0))]
    args = [z, z, z, z, z, prev0]
    if has_vres:
        in_specs.append(row_spec)
        args.append(vfirst)
    else:
        w2e = w2e[:, :2 * rw]
    in_specs += [full(w2e), full(vecs), sspec]
    args += [w2e, vecs, s_all]
    state_arg = len(args) - 1
    for name in ("tri", "mask_sc", "bones", "bones_b", "levels", "qmask"):
        in_specs.append(full(consts[name]))
        args.append(consts[name])

    out_specs = [row_spec]
    out_shape = [jax.ShapeDtypeStruct((m, rw), BF16)]
    if not has_vres:
        out_specs.append(row_spec)
        out_shape.append(jax.ShapeDtypeStruct((m, rw), F32))
    out_specs.append(sspec)
    out_shape.append(jax.ShapeDtypeStruct(s_all.shape, F32))

    outs = pl.pallas_call(
        functools.partial(_wkv_kernel, n_seq=n_seq, t_seq=t_seq, n_sub=n_sub, chain=chain, n_steps=n_steps,
                          has_vres=has_vres, n_levels=n_levels),
        grid=(nb, n_steps),
        in_specs=in_specs, out_specs=out_specs, out_shape=out_shape,
        input_output_aliases={state_arg: len(out_shape) - 1},
        scratch_shapes=[pltpu.VMEM((n_slots, N_GROUPS, 2, GROUP_LANES, GROUP_LANES // 2), F32),
                        pltpu.VMEM((n_slots, PREV_COLS), F32)],
        name="wkv",
        compiler_params=pltpu.CompilerParams(
            dimension_semantics=("parallel", "arbitrary"), vmem_limit_bytes=VMEM_LIMIT_BYTES),
    )(*args)
    if has_vres:
        ya, s_new = outs
        return ya, vfirst, s_new
    return outs


def _mix_kernel(*refs, n_seq, t_seq, tiles_per_seq, n_past, carry_pool):
    if carry_pool:
        (ya_ref, u_ref, gp_ref, ga_ref, gb_ref, pool_ref, perm_ref, permt_ref, pw_ref, ps_ref, wb0_ref, wb1_ref,
         m_ref, pool_out_ref) = refs
    else:
        ya_ref, u_ref, gp_ref, ga_ref, gb_ref, halo_ref, pw_ref, ps_ref, wb0_ref, wb1_ref, m_ref = refs
    rows = n_seq * t_seq
    tile = pl.program_id(0) % tiles_per_seq
    if carry_pool:
        perm = perm_ref[...]
        u_t = jnp.dot(perm, u_ref[...], preferred_element_type=F32).reshape(t_seq, n_seq, POOL_WIDTH)
        gp_all = jnp.dot(perm, gp_ref[...], preferred_element_type=F32)
        e = jnp.concatenate([jnp.zeros((HALO - POOL_BUF, n_seq, POOL_WIDTH), F32), pool_ref[...], u_t], axis=0)
        pool_out_ref[...] = e[HALO + t_seq - POOL_BUF:]
        t_axis = 0
        pos = lax.broadcasted_iota(jnp.int32, (t_seq, 1, 1), 0) + (1 + n_past)
    else:
        u = u_ref[...].astype(F32)
        halo = jnp.where(tile == 0, 0.0, halo_ref[...].astype(F32)).reshape(1, HALO, POOL_WIDTH)
        e = jnp.concatenate([halo, u.reshape(n_seq, t_seq, POOL_WIDTH)], axis=1)
        t_axis = 1
        pos = lax.broadcasted_iota(jnp.int32, (1, t_seq, 1), 1) + (tile * t_seq + 1 + n_past)
    take = lambda x, start, size: lax.slice_in_dim(x, start, start + size, axis=t_axis)
    n_col = len(POOL_WINDOWS)
    cw = D_MODEL // n_col
    cols = [slice(ci * cw, (ci + 1) * cw) for ci in range(n_col)]
    ya = ya_ref[...]
    gated_a, yb_parts = [], []
    for gi, win in enumerate(POOL_WINDOWS):
        pa = jnp.dot(ya, wb0_ref[:, cols[gi]], preferred_element_type=F32)
        sl = slice(gi * POOL_GROUP, (gi + 1) * POOL_GROUP)
        e_g = e[:, :, sl]
        acc = e_g
        span, length = 1, HALO + t_seq
        while span < win:
            length -= span
            acc = take(acc, span, length) + take(acc, 0, length)
            span *= 2
        ws = take(acc, length - t_seq, t_seq)
        inv_cnt = 1.0 / jnp.minimum(pos, win).astype(F32)
        mixed = (ws * inv_cnt - take(e_g, HALO, t_seq)).reshape(rows, POOL_GROUP)
        y_g = jnp.dot(mixed.astype(BF16), pw_ref[gi], preferred_element_type=F32)
        gp = gp_all[:, sl] if carry_pool else gp_ref[:, sl].astype(F32)
        yb_parts.append((y_g * ps_ref[:, sl] * (gp * _sigmoid(gp))).astype(BF16))
        gated_a.append(_sigmoid(ga_ref[:, cols[gi]].astype(F32)) * pa)
    y_b = jnp.concatenate(yb_parts, axis=1)
    if carry_pool:
        y_b = jnp.dot(permt_ref[...], y_b, preferred_element_type=F32).astype(BF16)
    pb = [None] * n_col
    pb[0] = jnp.dot(y_b, wb1_ref[:, cols[0]], preferred_element_type=F32)
    for ci in range(n_col):
        if ci + 1 < n_col:
            pb[ci + 1] = jnp.dot(y_b, wb1_ref[:, cols[ci + 1]], preferred_element_type=F32)
        m_ref[:, cols[ci]] = (gated_a[ci] + _sigmoid(gb_ref[:, cols[ci]].astype(F32)) * pb[ci]).astype(m_ref.dtype)


def _mix(ya, z, pool_all, pool_w_b, pool_scale, w_branch_b, layer, *, n_seq, t_seq, tiles_per_seq, n_past, row0):
    m = ya.shape[0]
    rows = n_seq * t_seq
    assert row0 % rows == 0
    z_off = row0 // rows
    carry_pool = pool_all is not None
    in_specs = [
        pl.BlockSpec((rows, RWKV_WIDTH), lambda i: (i, 0)),
        pl.BlockSpec((rows, POOL_WIDTH), lambda i: (i + z_off, Z_U)),
        pl.BlockSpec((rows, POOL_WIDTH), lambda i: (i + z_off, Z_G_POOL)),
        pl.BlockSpec((rows, D_MODEL), lambda i: (i + z_off, Z_GATE_A)),
        pl.BlockSpec((rows, D_MODEL), lambda i: (i + z_off, Z_GATE_B)),
    ]
    args = [ya, z, z, z, z]
    out_specs = [pl.BlockSpec((rows, D_MODEL), lambda i: (i, 0))]
    out_shape = [jax.ShapeDtypeStruct((m, D_MODEL), BF16)]
    aliases = {}
    if carry_pool:
        assert tiles_per_seq == 1 and t_seq < POOL_BUF
        r = np.arange(rows)
        perm = np.zeros((rows, rows), np.float32)
        perm[(r % t_seq) * n_seq + r // t_seq, r] = 1.0
        pool_spec = pl.BlockSpec((None, POOL_BUF, n_seq, POOL_WIDTH), lambda i: (layer, 0, i, 0))
        const = pl.BlockSpec((rows, rows), lambda i: (0, 0))
        in_specs += [pool_spec, const, const]
        aliases = {len(args): 1}
        args += [pool_all, jnp.asarray(perm, BF16), jnp.asarray(perm.T, BF16)]
        out_specs.append(pool_spec)
        out_shape.append(jax.ShapeDtypeStruct(pool_all.shape, F32))
    else:
        assert n_seq == 1 and rows % HALO == 0 and row0 == 0
        in_specs.append(pl.BlockSpec((HALO, POOL_WIDTH), lambda i: (jnp.maximum(i * (rows // HALO) - 1, 0), Z_U)))
        args.append(z)
    in_specs += [
        pl.BlockSpec((None, len(POOL_WINDOWS), POOL_GROUP, POOL_GROUP), lambda i: (layer, 0, 0, 0)),
        pl.BlockSpec((None, 1, POOL_WIDTH), lambda i: (layer, 0, 0)),
        pl.BlockSpec((None, None, RWKV_WIDTH, D_MODEL), lambda i: (layer, 0, 0, 0), pipeline_mode=pl.Buffered(1)),
        pl.BlockSpec((None, None, POOL_WIDTH, D_MODEL), lambda i: (layer, 1, 0, 0), pipeline_mode=pl.Buffered(1)),
    ]
    args += [pool_w_b, pool_scale.reshape(DEPTH, 1, POOL_WIDTH), w_branch_b, w_branch_b]
    outs = pl.pallas_call(
        functools.partial(_mix_kernel, n_seq=n_seq, t_seq=t_seq, tiles_per_seq=tiles_per_seq, n_past=n_past,
                          carry_pool=carry_pool),
        grid=(m // rows,),
        in_specs=in_specs, out_specs=out_specs, out_shape=out_shape,
        input_output_aliases=aliases,
        name="mix",
        compiler_params=pltpu.CompilerParams(
            dimension_semantics=("parallel",), vmem_limit_bytes=VMEM_LIMIT_BYTES),
    )(*args)
    return outs if carry_pool else (outs[0], None)


def _out_kernel(*refs, n_split, emit_h):
    if emit_h:
        (m_ref, x_ref, p_ref, g_ref, gn_ref, wo_ref, wg_ref, wp_ref), (o_ref, h_ref) = refs[:8], refs[-2:]
    else:
        m_ref, x_ref, p_ref, g_ref, wo_ref, wg_ref, wp_ref, o_ref = refs
    rows = m_ref.shape[0] // n_split
    parts = [slice(i * rows, (i + 1) * rows) for i in range(n_split)]
    outs = [jnp.dot(m_ref[sl, :], wo_ref[...], preferred_element_type=F32) for sl in parts]
    ples = [jnp.dot(p_ref[sl, :].astype(BF16), wp_ref[...], preferred_element_type=F32) for sl in parts]
    xs = [x_ref[sl, :] + _rmsnorm(o, g_ref[...]) for sl, o in zip(parts, outs)]
    gates = [_sigmoid(jnp.dot(x.astype(BF16), wg_ref[...], preferred_element_type=F32)) for x in xs]
    for sl, x, gate, ple in zip(parts, xs, gates, ples):
        x_new = x + gate * ple
        o_ref[sl, :] = x_new
        if emit_h:
            h_ref[sl, :] = _rmsnorm(x_new, gn_ref[...]).astype(h_ref.dtype)


def _out(mm_, x2d, p3d, norm_post, norm_pre, w_out_b, w_gate_b, w_ple_b, layer, *, rows, row0, total_rows,
         h_into):
    m = x2d.shape[0]
    emit_h = layer + 1 < DEPTH
    once = pl.Buffered(1)
    wspec = lambda k, l=layer: pl.BlockSpec((None, k, D_MODEL), lambda i: (l, 0, 0), pipeline_mode=once)
    row_spec = pl.BlockSpec((rows, D_MODEL), lambda i: (i, 0))
    in_specs = [row_spec, row_spec, pl.BlockSpec((None, rows, PLE_DIM), lambda i: (layer, i, 0)), wspec(1)]
    args = [mm_, x2d, p3d, norm_post.reshape(DEPTH, 1, D_MODEL)]
    out_specs, out_shape = [row_spec], [jax.ShapeDtypeStruct((m, D_MODEL), F32)]
    if emit_h:
        in_specs.append(wspec(1, layer + 1))
        args.append(norm_pre.reshape(DEPTH, 1, D_MODEL))
    in_specs += [wspec(D_MODEL), wspec(D_MODEL), wspec(PLE_DIM)]
    args += [w_out_b, w_gate_b, w_ple_b]
    aliases = {}
    if emit_h:
        h_spec, h_shape, x_specs, x_args, alias_in = _shared_rows_out(
            (rows, D_MODEL), row0, total_rows, D_MODEL, BF16, h_into, len(args))
        out_specs.append(h_spec)
        out_shape.append(h_shape)
        in_specs += x_specs
        args += x_args
        if alias_in is not None:
            aliases = {alias_in: 1}
    outs = pl.pallas_call(
        functools.partial(_out_kernel, n_split=rows // OUT_SUB_ROWS, emit_h=emit_h),
        grid=(m // rows,),
        in_specs=in_specs, out_specs=out_specs, out_shape=out_shape,
        input_output_aliases=aliases,
        name="out",
        compiler_params=pltpu.CompilerParams(
            dimension_semantics=("parallel",), vmem_limit_bytes=VMEM_LIMIT_BYTES),
    )(*args)
    return (outs[0], outs[1]) if emit_h else (outs[0], None)


def _pair_rows(s):
    l, b = s.shape[:2]
    return s.reshape(l, b, N_HEADS, HEAD_DIM // 2, 2 * HEAD_DIM)


def _unpair_rows(s):
    l, b = s.shape[:2]
    return s.reshape(l, b, N_HEADS, HEAD_DIM, HEAD_DIM)


def _layer_weights(i, mu_rkv, mu_lora, w0, w1, w2, a0, a1, a2, v0, v1, v2, k_k, k_a, r_k, ln_w, ln_b):
    r_ = RWKV_WIDTH
    mu = mu_lora[i]
    if i > 0:
        v1_i, v2_i, v0_i = v1[i - 1], v2[i - 1], v0[i - 1]
    else:
        v1_i = jnp.zeros((D_MODEL, MV_RANK), F32)
        v2_i = jnp.zeros((MV_RANK, r_), F32)
        v0_i = jnp.zeros((r_,), F32)
    w_lora = jnp.concatenate(
        [(1.0 - mu[0])[:, None] * w1[i], (1.0 - mu[1])[:, None] * a1[i], (1.0 - mu[2])[:, None] * v1_i,
         mu[0][:, None] * w1[i], mu[1][:, None] * a1[i], mu[2][:, None] * v1_i], axis=1).astype(BF16)
    z96 = jnp.zeros((DECAY_RANK, r_), F32)
    z64 = jnp.zeros((MV_RANK, r_), F32)
    w2e = jnp.concatenate([
        jnp.concatenate([w2[i], z96, z64], axis=0),
        jnp.concatenate([z96, a2[i], z64], axis=0),
        jnp.concatenate([z96, z96, v2_i], axis=0)], axis=1).astype(BF16)
    rows = [mu_rkv[i, 0], mu_rkv[i, 1], mu_rkv[i, 2], w0[i], a0[i], v0_i, k_k[i], k_a[i], r_k[i], ln_w[i], ln_b[i]]
    vecs = jnp.concatenate([jnp.stack(rows), jnp.zeros((_N_VEC_ROWS - len(rows), r_), F32)], axis=0)
    return w_lora, w2e, vecs


class _Group:
    def __init__(self, x, p, wkv, shift, pool, *, n_past, row0, n_seq, n_sub, mix_seq, mix_t):
        self.b, self.t, d = x.shape
        self.m = self.b * self.t
        self.x2d = x.reshape(self.m, d)
        self.p3d = p.reshape(DEPTH, self.m, PLE_DIM)
        self.zero_state = wkv is None
        self.shift, self.n_past, self.row0 = shift, n_past, row0
        self.n_seq, self.n_sub, self.mix_seq, self.mix_t = n_seq, n_sub, mix_seq, mix_t
        self.s_all = _pair_rows(
            jnp.zeros((DEPTH, self.b, N_HEADS, HEAD_DIM, HEAD_DIM), F32) if self.zero_state else wkv)
        self.pool_all = None if self.zero_state else jnp.transpose(pool, (0, 2, 1, 3))
        self.v_first = None
        self.shift_out, self.pool_out = [], []

    def outputs(self):
        pool_new = jnp.stack(self.pool_out) if self.zero_state else jnp.transpose(self.pool_all, (0, 2, 1, 3))
        return (self.x2d.reshape(self.b, self.t, D_MODEL), _unpair_rows(self.s_all), jnp.stack(self.shift_out),
                pool_new)


def _run_layers(groups, lw, params):
    (w_in, norm_pre, norm_post, pool_w_b, pool_scale, w_branch_b, w_out_b, w_ple_b, w_gate_b) = params
    total_rows = sum(g.m for g in groups)
    assert total_rows % PROJ_TM == 0
    h_all = None
    for g in groups:
        h_all = _norm(g.x2d, norm_pre[0], rows=NORM_ROWS, out_dtype=BF16, row0=g.row0, total_rows=total_rows,
                      into=h_all)
    for i in range(DEPTH):
        w_lora, w2e, vecs = lw[i]
        z = _proj(h_all, w_in, i, w_lora, tm=PROJ_TM, n_main=N_MAIN_TILES, out_dtype=BF16)
        h_all = None
        for g in groups:
            g.shift_out.append(
                _norm(g.x2d.reshape(g.b, g.t, D_MODEL)[:, -1], norm_pre[i], rows=g.b, out_dtype=F32))
            if g.zero_state:
                assert g.t >= POOL_BUF
                prev0 = jnp.zeros((g.b, PREV_COLS), F32)
                g.pool_out.append(jnp.stack([
                    z[g.row0 + (q + 1) * g.t - POOL_BUF:g.row0 + (q + 1) * g.t, Z_U * POOL_WIDTH:(Z_U + 1) * POOL_WIDTH]
                    for q in range(g.b)]).astype(F32))
            else:
                prev0 = _proj(g.shift[i].astype(BF16), w_in, i, w_lora, tm=g.b, n_main=3 * RWKV_WIDTH // PROJ_TN,
                              out_dtype=F32)
            ya, g.v_first, g.s_all = _wkv(z, prev0, g.v_first, w2e, vecs, g.s_all, i, n_batch=g.b, t_len=g.t,
                                          n_seq=g.n_seq, n_sub=g.n_sub, row0=g.row0)
            mm_, g.pool_all = _mix(ya, z, g.pool_all, pool_w_b, pool_scale, w_branch_b, i, n_seq=g.mix_seq,
                                   t_seq=g.mix_t, tiles_per_seq=g.t // g.mix_t, n_past=g.n_past, row0=g.row0)
            g.x2d, h_all = _out(mm_, g.x2d, g.p3d, norm_post, norm_pre, w_out_b, w_gate_b, w_ple_b, i,
                                rows=OUT_ROWS, row0=g.row0, total_rows=total_rows, h_into=h_all)


def kernel(x_prompt, x_sample, state_wkv, state_shift, state_pool, p_prompt, p_sample, norm_pre, norm_post, w_in, mu_rkv, mu_lora, w0, w1, w2, a0, a1, a2, v0, v1, v2, k_k, k_a, r_k, ln_w, ln_b, pool_w, pool_scale, w_branch, w_out, w_ple, w_ple_gate):
    lw = [_layer_weights(i, mu_rkv, mu_lora, w0, w1, w2, a0, a1, a2, v0, v1, v2, k_k, k_a, r_k, ln_w, ln_b)
          for i in range(DEPTH)]
    params = (w_in, norm_pre, norm_post, pool_w.astype(BF16), pool_scale, w_branch.astype(BF16),
              w_out.astype(BF16), w_ple.astype(BF16), w_ple_gate.astype(BF16))
    prompt = _Group(x_prompt, p_prompt, None, None, None, n_past=0, row0=0,
                    n_seq=1, n_sub=4, mix_seq=1, mix_t=256)
    sample = _Group(x_sample, p_sample, state_wkv, state_shift, state_pool, n_past=PAST_LEN, row0=prompt.m,
                    n_seq=8, n_sub=2, mix_seq=32, mix_t=8)
    _run_layers([prompt, sample], lw, params)
    y_p, wkv_p, shift_p, pool_p = prompt.outputs()
    y_s, wkv_s, shift_s, pool_s = sample.outputs()
    return (y_p, y_s, wkv_p, shift_p, pool_p, wkv_s, shift_s, pool_s)
```

```python
import functools

import jax
import jax.numpy as jnp
import numpy as np
from jax import lax
from jax.experimental import pallas as pl
from jax.experimental.pallas import tpu as pltpu

F32 = jnp.float32
BF16 = jnp.bfloat16

D_MODEL = 2048
DEPTH = 4
PAST_LEN = 16384
RWKV_WIDTH = 1024
HEAD_DIM = 64
N_HEADS = 16
POOL_WIDTH = 1024
POOL_WINDOWS = (2, 4, 8, 16)
POOL_GROUP = 256
POOL_BUF = 15
PLE_DIM = 256
DECAY_RANK = 96
AAA_RANK = 96
MV_RANK = 64
LORA_COLS = DECAY_RANK + AAA_RANK + MV_RANK
IN_COLS = 10240
RMS_EPS = 1e-6
GN_EPS = 64e-5
DECAY_SCALE = 0.606531

HEADS_PER_GROUP = 4
GROUP_LANES = HEADS_PER_GROUP * HEAD_DIM
N_GROUPS = N_HEADS // HEADS_PER_GROUP
CHUNK = 64
EXPANDED = HEADS_PER_GROUP * CHUNK
F32_ROWS = 8
BF16_ROWS = 16
HALO = 16
PROJ_TN = 2 * LORA_COLS
N_MAIN_TILES = IN_COLS // PROJ_TN
Z_R, Z_K, Z_V, Z_G_RWKV, Z_U, Z_G_POOL = range(6)
Z_GATE_A, Z_GATE_B = 3, 4
Z_LORA = N_MAIN_TILES
PREV_COLS = 3 * RWKV_WIDTH + PROJ_TN
LB_OFF = 3 * RWKV_WIDTH + LORA_COLS

NORM_ROWS = 512
PROJ_TM = 3072
OUT_ROWS = 256
OUT_SUB_ROWS = 128

VMEM_LIMIT_BYTES = 56 * 1024 * 1024


def _sigmoid(x):
    return 0.5 * jnp.tanh(0.5 * x) + 0.5


def _split_bf16(x):
    hi = x.astype(BF16)
    lo = (x - hi.astype(F32)).astype(BF16)
    return hi, lo


_NN = (((1,), (0,)), ((), ()))
_NT = (((1,), (1,)), ((), ()))


def _dot(a, b, dims=_NN):
    return lax.dot_general(a, b, dimension_numbers=dims, preferred_element_type=F32)


def _segsum(xs, bones_b):
    rows = xs[0].shape[0]
    s = _dot(jnp.concatenate([x.astype(BF16) for x in xs], axis=0), bones_b)
    return [s[i * rows:(i + 1) * rows] for i in range(len(xs))]


def _proj_kernel(h_ref, w_ref, wl_ref, z_ref, *, n_main):
    j = pl.program_id(1)

    @pl.when(j < n_main)
    def _():
        z_ref[...] = jnp.dot(h_ref[...], w_ref[...].astype(BF16), preferred_element_type=F32).astype(z_ref.dtype)

    @pl.when(j == n_main)
    def _():
        z_ref[...] = jnp.dot(h_ref[...], wl_ref[...], preferred_element_type=F32).astype(z_ref.dtype)


def _proj(h2d, w_in, layer, w_lora, *, tm, n_main, out_dtype):
    m, d = h2d.shape
    tn = PROJ_TN
    assert m % tm == 0
    return pl.pallas_call(
        functools.partial(_proj_kernel, n_main=n_main),
        grid=(m // tm, n_main + 1),
        in_specs=[
            pl.BlockSpec((tm, d), lambda i, j: (i, 0)),
            pl.BlockSpec((None, d, tn), lambda i, j: (layer, 0, jnp.minimum(j, n_main - 1))),
            pl.BlockSpec((d, tn), lambda i, j: (0, 0), pipeline_mode=pl.Buffered(1)),
        ],
        out_specs=pl.BlockSpec((tm, tn), lambda i, j: (i, j)),
        out_shape=jax.ShapeDtypeStruct((m, (n_main + 1) * tn), out_dtype),
        name="proj",
        compiler_params=pltpu.CompilerParams(
            dimension_semantics=("parallel", "arbitrary"), vmem_limit_bytes=VMEM_LIMIT_BYTES),
    )(h2d, w_in, w_lora)


def _rmsnorm(x, gain):
    return x * lax.rsqrt(jnp.mean(x * x, axis=-1, keepdims=True) + RMS_EPS) * gain


def _norm_kernel(x_ref, g_ref, *rest):
    o_ref = rest[-1]
    o_ref[...] = _rmsnorm(x_ref[...], g_ref[...]).astype(o_ref.dtype)


def _shared_rows_out(row_spec_shape, row0, total_rows, d, dtype, into, n_inputs):
    rows = row_spec_shape[0]
    assert row0 % rows == 0
    blk0 = row0 // rows
    spec = pl.BlockSpec(row_spec_shape, lambda i: (i + blk0, 0))
    shape = jax.ShapeDtypeStruct((total_rows, d), dtype)
    extra_specs = [] if into is None else [pl.BlockSpec(memory_space=pl.ANY)]
    extra_args = [] if into is None else [into]
    return spec, shape, extra_specs, extra_args, (None if into is None else n_inputs)


def _norm(x2d, gain, *, rows, out_dtype, row0=0, total_rows=None, into=None):
    m, d = x2d.shape
    rows = min(rows, m)
    assert m % rows == 0
    o_spec, o_shape, x_specs, x_args, alias_in = _shared_rows_out(
        (rows, d), row0, total_rows or m, d, out_dtype, into, 2)
    return pl.pallas_call(
        _norm_kernel,
        grid=(m // rows,),
        in_specs=[pl.BlockSpec((rows, d), lambda i: (i, 0)), pl.BlockSpec((1, d), lambda i: (0, 0))] + x_specs,
        out_specs=o_spec, out_shape=o_shape,
        input_output_aliases={} if alias_in is None else {alias_in: 0},
        name="norm",
        compiler_params=pltpu.CompilerParams(dimension_semantics=("parallel",)),
    )(x2d, gain.reshape(1, d), *x_args)


_V_MU_R, _V_MU_K, _V_MU_V, _V_W0, _V_A0, _V_V0, _V_KK, _V_KA, _V_RK, _V_LNW, _V_LNB = range(11)
_N_VEC_ROWS = 16


def _wkv_constants(n_seq, t_seq):
    c, e = CHUNK, EXPANDED
    assert n_seq * t_seq == c
    row = np.arange(c)
    q_of = row // t_seq
    tri = (row[:, None] >= row[None, :]) & (q_of[:, None] == q_of[None, :])
    erow = np.arange(e)
    es = erow % c
    same = q_of[:, None] == (es // t_seq)[None, :]
    mask_s = same & (row[:, None] > es[None, :])
    mask_i = same & (row[:, None] >= es[None, :])
    mask_sc = np.block([[mask_s, mask_s], [mask_i, mask_i]])
    lane = np.arange(GROUP_LANES)
    bones = (lane[:, None] // HEAD_DIM) == (lane[None, :] // HEAD_DIM)
    levels = [np.eye(e, dtype=bool), (erow[:, None] // 2) == (erow[None, :] // 2)]
    m = 2
    while m < t_seq:
        levels.append(((erow[:, None] // (2 * m)) == (erow[None, :] // (2 * m)))
                      & ((erow[:, None] % (2 * m)) >= m) & ((erow[None, :] % (2 * m)) < m))
        m *= 2
    col = np.arange(2 * c)
    qmask = ((col[None, :] % c) // t_seq) == np.arange(n_seq)[:, None]
    f = lambda a: jnp.asarray(a.astype(np.float32))
    return dict(tri=f(tri).astype(BF16), mask_sc=f(mask_sc).astype(BF16), bones=f(bones),
                bones_b=f(bones).astype(BF16), levels=f(np.stack(levels)).astype(BF16),
                qmask=f(qmask).reshape(n_seq, 1, 2 * c))


def _wkv_kernel(*refs, n_seq, t_seq, n_sub, chain, n_steps, has_vres, n_levels):
    it = iter(refs)
    zr_ref, zk_ref, zv_ref, zg_ref, zl_ref, p0_ref = (next(it) for _ in range(6))
    vfirst_ref = next(it) if has_vres else None
    w2_ref, vec_ref, s0_ref = next(it), next(it), next(it)
    tri_ref, msc_ref, bones_ref, bonesb_ref, lev_ref, qmask_ref = (next(it) for _ in range(6))
    ya_ref = next(it)
    vout_ref = None if has_vres else next(it)
    sout_ref = next(it)
    sbd_ref, prev_ref = next(it), next(it)

    step = pl.program_id(1)
    gl, rw, e = GROUP_LANES, RWKV_WIDTH, EXPANDED
    n_slots = n_seq if chain else n_sub * n_seq
    slot = (lambda j, q: q) if chain else (lambda j, q: j * n_seq + q)
    bones = bones_ref[...]
    bones_b = bonesb_ref[...]
    mask_sc = msc_ref[...]
    eye_b = lev_ref[0]
    tri = tri_ref[...]
    vec = vec_ref[...]
    vrow = lambda i: vec[i:i + 1, :]
    groups = range(N_GROUPS)
    sls = [slice(gi * gl, (gi + 1) * gl) for gi in groups]

    def expand(x):
        return jnp.concatenate([x] * HEADS_PER_GROUP, axis=0) * bones_b

    low_lanes = lax.broadcasted_iota(jnp.int32, (1, 2 * HEAD_DIM), 1) < HEAD_DIM

    def head_slot(hd):
        h = hd % HEADS_PER_GROUP
        return hd // HEADS_PER_GROUP, h * HEAD_DIM, h // 2, h % 2 == 1

    def load_state(s, g):
        return jnp.concatenate([sbd_ref[s, g, 0], sbd_ref[s, g, 1]], axis=1)

    def store_state(s, g, val):
        sbd_ref[s, g, 0] = val[:, :2 * HEAD_DIM]
        sbd_ref[s, g, 1] = val[:, 2 * HEAD_DIM:]

    @pl.when(step == 0)
    def _():
        for s in range(n_slots):
            for hd in range(N_HEADS):
                g, row0, half, upper = head_slot(hd)
                packed = s0_ref[s, hd]
                even = jnp.where(low_lanes, packed, 0.0)
                odd = jnp.where(low_lanes, 0.0, packed)
                if upper:
                    even = pltpu.roll(even, HEAD_DIM, axis=1)
                else:
                    odd = pltpu.roll(odd, HEAD_DIM, axis=1)
                sbd_ref[s, g, half, pl.ds(row0, HEAD_DIM // 2, stride=2), :] = even
                sbd_ref[s, g, half, pl.ds(row0 + 1, HEAD_DIM // 2, stride=2), :] = odd
                sbd_ref[s, g, 1 - half, row0:row0 + HEAD_DIM, :] = jnp.zeros((HEAD_DIM, 2 * HEAD_DIM), F32)
        prev_ref[...] = p0_ref[0]

    row = lax.broadcasted_iota(jnp.int32, (CHUNK, 1), 0)
    is_first = (row % t_seq) == 0
    lora_lane = lax.broadcasted_iota(jnp.int32, (1, LORA_COLS), 1)

    def shifted(z, p0):
        rolled = pltpu.roll(z, 1, axis=0)
        if n_seq == 1:
            first = jnp.broadcast_to(p0, z.shape)
        else:
            first = jnp.broadcast_to(p0[:, None, :], (n_seq, t_seq, z.shape[1])).reshape(z.shape)
        return jnp.where(is_first, first, rolled)

    n_rows = n_sub * CHUNK
    zl_all = zl_ref[...].astype(F32)
    lb_all = zl_all[:, LORA_COLS:]
    p_l = prev_ref[:, LB_OFF:LB_OFF + LORA_COLS]
    if n_slots == 1:
        lb_first = jnp.broadcast_to(p_l, lb_all.shape)
    else:
        lb_first = jnp.broadcast_to(p_l[:, None, :], (n_slots, t_seq, LORA_COLS)).reshape(lb_all.shape)
    row_all = lax.broadcasted_iota(jnp.int32, (n_rows, 1), 0)
    lb_prev = jnp.where(row_all % (n_rows if chain else t_seq) == 0, lb_first, pltpu.roll(lb_all, 1, axis=0))
    lin = zl_all[:, :LORA_COLS] + lb_prev
    lin = jnp.where(lora_lane < DECAY_RANK, jnp.tanh(lin), lin).astype(BF16)
    d_full = jnp.dot(lin, w2_ref[...], preferred_element_type=F32)
    lb_last = lb_all[n_rows - 1:n_rows, :]

    ctx = [dict() for _ in range(n_sub)]

    def prep(j):
        c = ctx[j]
        rows = slice(j * CHUNK, (j + 1) * CHUNK)
        zr, zk, zv = zr_ref[rows, :].astype(F32), zk_ref[rows, :].astype(F32), zv_ref[rows, :].astype(F32)
        c["last"] = [x[CHUNK - 1:CHUNK, :] for x in (zr, zk, zv)]
        if chain and j > 0:
            p_r, p_k, p_v = ctx[j - 1]["last"]
        else:
            ps = slice(0, n_seq) if chain else slice(j * n_seq, (j + 1) * n_seq)
            p_r, p_k, p_v = prev_ref[ps, 0:rw], prev_ref[ps, rw:2 * rw], prev_ref[ps, 2 * rw:3 * rw]
        r = zr + (shifted(zr, p_r) - zr) * vrow(_V_MU_R)
        k = zk + (shifted(zk, p_k) - zk) * vrow(_V_MU_K)
        v = zv + (shifted(zv, p_v) - zv) * vrow(_V_MU_V)
        d_all = d_full[rows]
        wlog = -DECAY_SCALE * _sigmoid(vrow(_V_W0) + d_all[:, 0:rw])
        alr = _sigmoid(vrow(_V_A0) + d_all[:, rw:2 * rw])
        if has_vres:
            v = v + (vfirst_ref[rows, :] - v) * _sigmoid(vrow(_V_V0) + d_all[:, 2 * rw:3 * rw])
        else:
            vout_ref[rows, :] = v
        kkr = k * vrow(_V_KK)
        kf = k * (1.0 + (alr - 1.0) * vrow(_V_KA))
        c["rkr"] = r * kf * vrow(_V_RK)
        g_all = zg_ref[rows, :].astype(F32)
        c["gate"] = g_all * _sigmoid(g_all)
        w_hi, w_lo = _split_bf16(wlog)
        cum = jnp.dot(tri, w_hi, preferred_element_type=F32) + jnp.dot(tri, w_lo, preferred_element_type=F32)
        p_inc = jnp.exp(cum)
        inv_p = jnp.exp(-cum)
        p_exc = jnp.exp(cum - wlog)
        c["p_inc"] = p_inc
        yield
        kk_n2 = _segsum([kkr[:, sl] * kkr[:, sl] for sl in sls], bones_b)
        kk = [kkr[:, sl] * lax.rsqrt(jnp.maximum(n2, 1e-24)) for sl, n2 in zip(sls, kk_n2)]
        c["v_g"] = [v[:, sl] for sl in sls]
        c["a_f"] = [-kk[gi] * p_exc[:, sls[gi]] for gi in groups]
        c["r_f"] = [r[:, sl] * p_inc[:, sl] for sl in sls]
        a_t = [x.astype(BF16) for x in c["a_f"]]
        r_t = [x.astype(BF16) for x in c["r_f"]]
        c["b_t"] = [(kk[gi] * alr[:, sls[gi]] * inv_p[:, sls[gi]]).astype(BF16) for gi in groups]
        c["k_t"] = [(kf[:, sl] * inv_p[:, sl]).astype(BF16) for sl in sls]
        c["v_e"] = [expand(x.astype(BF16)) for x in c["v_g"]]
        c["ar"] = [jnp.concatenate([a_t[gi], r_t[gi]], axis=0) for gi in groups]
        yield
        sc = [_dot(c["ar"][gi], jnp.concatenate([expand(c["b_t"][gi]), expand(c["k_t"][gi])], axis=0), _NT)
              for gi in groups]
        c["scm"] = [x.astype(BF16) * mask_sc for x in sc]
        c["n_bd"] = [expand(x[:CHUNK, :e]) for x in c["scm"]]
        c["xs"] = [eye_b + n * lev_ref[1] for n in c["n_bd"]]
        yield
        c["akv"] = [_dot(c["scm"][gi][:, e:], c["v_e"][gi]) for gi in groups]
        yield

    def inverse(j):
        c = ctx[j]
        m = 2
        xs32 = None
        for lvl in range(2, n_levels):
            n_c = [c["n_bd"][gi] * lev_ref[lvl] for gi in groups]
            if m % F32_ROWS == 0:
                wide = m % BF16_ROWS != 0
                src = xs32 if wide else c["xs"]
                n_blk = e // (2 * m)
                lower = lambda x: jnp.concatenate([x[b * 2 * m + m:(b + 1) * 2 * m] for b in range(n_blk)], axis=0)
                eye_l = lower(eye_b.astype(src[0].dtype)).astype(BF16)
                t1 = [_dot(lower(src[gi]).astype(BF16), n_c[gi]).astype(BF16) + eye_l for gi in groups]
                yield
                new = [_dot(t1[gi], c["xs"][gi]).astype(src[0].dtype) for gi in groups]
                c["xs"] = [jnp.concatenate(
                    [piece for b in range(n_blk)
                     for piece in (x[b * 2 * m:b * 2 * m + m], n[b * m:(b + 1) * m])], axis=0).astype(BF16)
                    for x, n in zip(src, new)]
                xs32 = None
            else:
                t1 = [_dot(c["xs"][gi], n_c[gi]).astype(BF16) + eye_b for gi in groups]
                yield
                full = [_dot(t1[gi], c["xs"][gi]) for gi in groups]
                if (2 * m) % F32_ROWS == 0 and (2 * m) % BF16_ROWS != 0:
                    xs32 = full
                c["xs"] = [x.astype(BF16) for x in full]
            yield
            m *= 2

    def tail(j):
        c = ctx[j]
        rows = slice(j * CHUNK, (j + 1) * CHUNK)
        x_c = [(x[0:CHUNK] + x[CHUNK:2 * CHUNK]) + (x[2 * CHUNK:3 * CHUNK] + x[3 * CHUNK:4 * CHUNK])
               for x in c["xs"]]
        a_s, r_s = [], []
        for gi in groups:
            parts_a, parts_r = [], []
            for q in range(n_seq):
                lo_, hi_ = q * t_seq, (q + 1) * t_seq
                lhs = c["ar"][gi] if n_seq == 1 else jnp.concatenate(
                    [c["a_f"][gi][lo_:hi_], c["r_f"][gi][lo_:hi_]], axis=0).astype(BF16)
                res = _dot(lhs, load_state(slot(j, q), gi).astype(BF16), _NT)
                parts_a.append(res[:t_seq])
                parts_r.append(res[t_seq:])
            a_s.append(parts_a[0] if n_seq == 1 else jnp.concatenate(parts_a, axis=0))
            r_s.append(parts_r[0] if n_seq == 1 else jnp.concatenate(parts_r, axis=0))
        yield
        y = [a_s[gi] + c["akv"][gi][:CHUNK] for gi in groups]
        u = [_dot(x_c[gi], expand(y[gi].astype(BF16))) for gi in groups]
        yield
        o = [r_s[gi] + c["akv"][gi][CHUNK:] + _dot(c["scm"][gi][CHUNK:, :e], expand(u[gi].astype(BF16)))
             for gi in groups]
        yield
        for gi in groups:
            uv_t = jnp.concatenate([u[gi], c["v_g"][gi]], axis=0).T
            bk = jnp.concatenate([c["b_t"][gi], c["k_t"][gi]], axis=0)
            for q in range(n_seq):
                lhs = uv_t if n_seq == 1 else uv_t * qmask_ref[q]
                p_last = c["p_inc"][(q + 1) * t_seq - 1:(q + 1) * t_seq, sls[gi]]
                s = slot(j, q)
                store_state(s, gi, bones * ((load_state(s, gi) + _dot(lhs.astype(BF16), bk)) * p_last))
        yield
        mean = [x * (1.0 / HEAD_DIM) for x in _segsum(o, bones_b)]
        d = [o[gi] - mean[gi] for gi in groups]
        var = [x * (1.0 / HEAD_DIM) for x in _segsum([x * x for x in d], bones_b)]
        bonus = _segsum([c["rkr"][:, sl] for sl in sls], bones_b)
        yield
        for gi in groups:
            sl = sls[gi]
            on = d[gi] * lax.rsqrt(var[gi] + GN_EPS) * vec[_V_LNW:_V_LNW + 1, sl] + vec[_V_LNB:_V_LNB + 1, sl]
            ya_ref[rows, sl] = ((on + bonus[gi] * c["v_g"][gi]) * c["gate"][:, sl]).astype(ya_ref.dtype)

    preps, invs, tails = ([f(j) for j in range(n_sub)] for f in (prep, inverse, tail))
    for phase in range(n_sub + 2):
        active = []
        if 0 <= phase - 2 < n_sub:
            active.append(tails[phase - 2])
        if 0 <= phase - 1 < n_sub:
            active.append(invs[phase - 1])
        if phase < n_sub:
            active.append(preps[phase])
        while active:
            for gen in list(active):
                try:
                    next(gen)
                except StopIteration:
                    active.remove(gen)

    if chain and n_steps > 1:
        assert n_seq == 1
        last_r, last_k, last_v = ctx[n_sub - 1]["last"]
        prev_ref[:, 0:rw] = last_r
        prev_ref[:, rw:2 * rw] = last_k
        prev_ref[:, 2 * rw:3 * rw] = last_v
        prev_ref[:, LB_OFF:LB_OFF + LORA_COLS] = lb_last

    @pl.when(step == n_steps - 1)
    def _():
        for s in range(n_slots):
            for hd in range(N_HEADS):
                g, row0, half, upper = head_slot(hd)
                even = sbd_ref[s, g, half, pl.ds(row0, HEAD_DIM // 2, stride=2), :]
                odd = sbd_ref[s, g, half, pl.ds(row0 + 1, HEAD_DIM // 2, stride=2), :]
                if upper:
                    even = pltpu.roll(even, HEAD_DIM, axis=1)
                else:
                    odd = pltpu.roll(odd, HEAD_DIM, axis=1)
                sout_ref[s, hd] = jnp.where(low_lanes, even, odd)


def _wkv(z, prev0, vfirst, w2e, vecs, s_all, layer, *, n_batch, t_len, n_seq, n_sub, row0):
    m = n_batch * t_len
    t_seq = CHUNK // n_seq
    n_chunks = t_len // t_seq
    chain = n_seq == 1
    if chain:
        assert n_chunks % n_sub == 0
        n_steps, nb, n_slots = n_chunks // n_sub, n_batch, n_seq
    else:
        assert n_chunks == 1 and n_batch % (n_seq * n_sub) == 0
        n_steps, nb, n_slots = 1, n_batch // (n_seq * n_sub), n_seq * n_sub
    has_vres = vfirst is not None
    consts = _wkv_constants(n_seq, t_seq)
    n_levels = consts["levels"].shape[0]
    rw = RWKV_WIDTH
    rows = n_sub * CHUNK
    rb = lambda b, s: b * n_steps + s
    prev0 = prev0.reshape(nb, n_slots, PREV_COLS)

    assert row0 % rows == 0
    z_off = row0 // rows
    zspec = lambda col: pl.BlockSpec((rows, rw), lambda b, s: (rb(b, s) + z_off, col))
    row_spec = pl.BlockSpec((rows, rw), lambda b, s: (rb(b, s), 0))
    full = lambda a: pl.BlockSpec(a.shape, lambda b, s: (0,) * a.ndim)
    sspec = pl.BlockSpec((None, n_slots, N_HEADS, HEAD_DIM // 2, 2 * HEAD_DIM), lambda b, s: (layer, b, 0, 0, 0))

    in_specs = [zspec(Z_R), zspec(Z_K), zspec(Z_V), zspec(Z_G_RWKV),
                pl.BlockSpec((rows, PROJ_TN), lambda b, s: (rb(b, s) + z_off, Z_LORA)),
                pl.BlockSpec((1, n_slots, PREV_COLS), lambda b, s: (b, 0, 0))]
    args = [z, z, z, z, z, prev0]
    if has_vres:
        in_specs.append(row_spec)
        args.append(vfirst)
    else:
        w2e = w2e[:, :2 * rw]
    in_specs += [full(w2e), full(vecs), sspec]
    args += [w2e, vecs, s_all]
    state_arg = len(args) - 1
    for name in ("tri", "mask_sc", "bones", "bones_b", "levels", "qmask"):
        in_specs.append(full(consts[name]))
        args.append(consts[name])

    out_specs = [row_spec]
    out_shape = [jax.ShapeDtypeStruct((m, rw), BF16)]
    if not has_vres:
        out_specs.append(row_spec)
        out_shape.append(jax.ShapeDtypeStruct((m, rw), F32))
    out_specs.append(sspec)
    out_shape.append(jax.ShapeDtypeStruct(s_all.shape, F32))

    outs = pl.pallas_call(
        functools.partial(_wkv_kernel, n_seq=n_seq, t_seq=t_seq, n_sub=n_sub, chain=chain, n_steps=n_steps,
                          has_vres=has_vres, n_levels=n_levels),
        grid=(nb, n_steps),
        in_specs=in_specs, out_specs=out_specs, out_shape=out_shape,
        input_output_aliases={state_arg: len(out_shape) - 1},
        scratch_shapes=[pltpu.VMEM((n_slots, N_GROUPS, 2, GROUP_LANES, GROUP_LANES // 2), F32),
                        pltpu.VMEM((n_slots, PREV_COLS), F32)],
        name="wkv",
        compiler_params=pltpu.CompilerParams(
            dimension_semantics=("parallel", "arbitrary"), vmem_limit_bytes=VMEM_LIMIT_BYTES),
    )(*args)
    if has_vres:
        ya, s_new = outs
        return ya, vfirst, s_new
    return outs


def _mix_kernel(*refs, n_seq, t_seq, tiles_per_seq, n_past, carry_pool):
    if carry_pool:
        (ya_ref, u_ref, gp_ref, ga_ref, gb_ref, pool_ref, perm_ref, permt_ref, pw_ref, ps_ref, wb0_ref, wb1_ref,
         m_ref, pool_out_ref) = refs
    else:
        ya_ref, u_ref, gp_ref, ga_ref, gb_ref, halo_ref, pw_ref, ps_ref, wb0_ref, wb1_ref, m_ref = refs
    rows = n_seq * t_seq
    tile = pl.program_id(0) % tiles_per_seq
    if carry_pool:
        perm = perm_ref[...]
        u_t = jnp.dot(perm, u_ref[...], preferred_element_type=F32).reshape(t_seq, n_seq, POOL_WIDTH)
        gp_all = jnp.dot(perm, gp_ref[...], preferred_element_type=F32)
        e = jnp.concatenate([jnp.zeros((HALO - POOL_BUF, n_seq, POOL_WIDTH), F32), pool_ref[...], u_t], axis=0)
        pool_out_ref[...] = e[HALO + t_seq - POOL_BUF:]
        t_axis = 0
        pos = lax.broadcasted_iota(jnp.int32, (t_seq, 1, 1), 0) + (1 + n_past)
    else:
        u = u_ref[...].astype(F32)
        halo = jnp.where(tile == 0, 0.0, halo_ref[...].astype(F32)).reshape(1, HALO, POOL_WIDTH)
        e = jnp.concatenate([halo, u.reshape(n_seq, t_seq, POOL_WIDTH)], axis=1)
        t_axis = 1
        pos = lax.broadcasted_iota(jnp.int32, (1, t_seq, 1), 1) + (tile * t_seq + 1 + n_past)
    take = lambda x, start, size: lax.slice_in_dim(x, start, start + size, axis=t_axis)
    n_col = len(POOL_WINDOWS)
    cw = D_MODEL // n_col
    cols = [slice(ci * cw, (ci + 1) * cw) for ci in range(n_col)]
    ya = ya_ref[...]
    gated_a, yb_parts = [], []
    for gi, win in enumerate(POOL_WINDOWS):
        pa = jnp.dot(ya, wb0_ref[:, cols[gi]], preferred_element_type=F32)
        sl = slice(gi * POOL_GROUP, (gi + 1) * POOL_GROUP)
        e_g = e[:, :, sl]
        acc = e_g
        span, length = 1, HALO + t_seq
        while span < win:
            length -= span
            acc = take(acc, span, length) + take(acc, 0, length)
            span *= 2
        ws = take(acc, length - t_seq, t_seq)
        inv_cnt = 1.0 / jnp.minimum(pos, win).astype(F32)
        mixed = (ws * inv_cnt - take(e_g, HALO, t_seq)).reshape(rows, POOL_GROUP)
        y_g = jnp.dot(mixed.astype(BF16), pw_ref[gi], preferred_element_type=F32)
        gp = gp_all[:, sl] if carry_pool else gp_ref[:, sl].astype(F32)
        yb_parts.append((y_g * ps_ref[:, sl] * (gp * _sigmoid(gp))).astype(BF16))
        gated_a.append(_sigmoid(ga_ref[:, cols[gi]].astype(F32)) * pa)
    y_b = jnp.concatenate(yb_parts, axis=1)
    if carry_pool:
        y_b = jnp.dot(permt_ref[...], y_b, preferred_element_type=F32).astype(BF16)
    pb = [None] * n_col
    pb[0] = jnp.dot(y_b, wb1_ref[:, cols[0]], preferred_element_type=F32)
    for ci in range(n_col):
        if ci + 1 < n_col:
            pb[ci + 1] = jnp.dot(y_b, wb1_ref[:, cols[ci + 1]], preferred_element_type=F32)
        m_ref[:, cols[ci]] = (gated_a[ci] + _sigmoid(gb_ref[:, cols[ci]].astype(F32)) * pb[ci]).astype(m_ref.dtype)


def _mix(ya, z, pool_all, pool_w_b, pool_scale, w_branch_b, layer, *, n_seq, t_seq, tiles_per_seq, n_past, row0):
    m = ya.shape[0]
    rows = n_seq * t_seq
    assert row0 % rows == 0
    z_off = row0 // rows
    carry_pool = pool_all is not None
    in_specs = [
        pl.BlockSpec((rows, RWKV_WIDTH), lambda i: (i, 0)),
        pl.BlockSpec((rows, POOL_WIDTH), lambda i: (i + z_off, Z_U)),
        pl.BlockSpec((rows, POOL_WIDTH), lambda i: (i + z_off, Z_G_POOL)),
        pl.BlockSpec((rows, D_MODEL), lambda i: (i + z_off, Z_GATE_A)),
        pl.BlockSpec((rows, D_MODEL), lambda i: (i + z_off, Z_GATE_B)),
    ]
    args = [ya, z, z, z, z]
    out_specs = [pl.BlockSpec((rows, D_MODEL), lambda i: (i, 0))]
    out_shape = [jax.ShapeDtypeStruct((m, D_MODEL), BF16)]
    aliases = {}
    if carry_pool:
        assert tiles_per_seq == 1 and t_seq < POOL_BUF
        r = np.arange(rows)
        perm = np.zeros((rows, rows), np.float32)
        perm[(r % t_seq) * n_seq + r // t_seq, r] = 1.0
        pool_spec = pl.BlockSpec((None, POOL_BUF, n_seq, POOL_WIDTH), lambda i: (layer, 0, i, 0))
        const = pl.BlockSpec((rows, rows), lambda i: (0, 0))
        in_specs += [pool_spec, const, const]
        aliases = {len(args): 1}
        args += [pool_all, jnp.asarray(perm, BF16), jnp.asarray(perm.T, BF16)]
        out_specs.append(pool_spec)
        out_shape.append(jax.ShapeDtypeStruct(pool_all.shape, F32))
    else:
        assert n_seq == 1 and rows % HALO == 0 and row0 == 0
        in_specs.append(pl.BlockSpec((HALO, POOL_WIDTH), lambda i: (jnp.maximum(i * (rows // HALO) - 1, 0), Z_U)))
        args.append(z)
    in_specs += [
        pl.BlockSpec((None, len(POOL_WINDOWS), POOL_GROUP, POOL_GROUP), lambda i: (layer, 0, 0, 0)),
        pl.BlockSpec((None, 1, POOL_WIDTH), lambda i: (layer, 0, 0)),
        pl.BlockSpec((None, None, RWKV_WIDTH, D_MODEL), lambda i: (layer, 0, 0, 0), pipeline_mode=pl.Buffered(1)),
        pl.BlockSpec((None, None, POOL_WIDTH, D_MODEL), lambda i: (layer, 1, 0, 0), pipeline_mode=pl.Buffered(1)),
    ]
    args += [pool_w_b, pool_scale.reshape(DEPTH, 1, POOL_WIDTH), w_branch_b, w_branch_b]
    outs = pl.pallas_call(
        functools.partial(_mix_kernel, n_seq=n_seq, t_seq=t_seq, tiles_per_seq=tiles_per_seq, n_past=n_past,
                          carry_pool=carry_pool),
        grid=(m // rows,),
        in_specs=in_specs, out_specs=out_specs, out_shape=out_shape,
        input_output_aliases=aliases,
        name="mix",
        compiler_params=pltpu.CompilerParams(
            dimension_semantics=("parallel",), vmem_limit_bytes=VMEM_LIMIT_BYTES),
    )(*args)
    return outs if carry_pool else (outs[0], None)


def _out_kernel(*refs, n_split, emit_h):
    if emit_h:
        (m_ref, x_ref, p_ref, g_ref, gn_ref, wo_ref, wg_ref, wp_ref), (o_ref, h_ref) = refs[:8], refs[-2:]
    else:
        m_ref, x_ref, p_ref, g_ref, wo_ref, wg_ref, wp_ref, o_ref = refs
    rows = m_ref.shape[0] // n_split
    parts = [slice(i * rows, (i + 1) * rows) for i in range(n_split)]
    outs = [jnp.dot(m_ref[sl, :], wo_ref[...], preferred_element_type=F32) for sl in parts]
    ples = [jnp.dot(p_ref[sl, :].astype(BF16), wp_ref[...], preferred_element_type=F32) for sl in parts]
    xs = [x_ref[sl, :] + _rmsnorm(o, g_ref[...]) for sl, o in zip(parts, outs)]
    gates = [_sigmoid(jnp.dot(x.astype(BF16), wg_ref[...], preferred_element_type=F32)) for x in xs]
    for sl, x, gate, ple in zip(parts, xs, gates, ples):
        x_new = x + gate * ple
        o_ref[sl, :] = x_new
        if emit_h:
            h_ref[sl, :] = _rmsnorm(x_new, gn_ref[...]).astype(h_ref.dtype)


def _out(mm_, x2d, p3d, norm_post, norm_pre, w_out_b, w_gate_b, w_ple_b, layer, *, rows, row0, total_rows,
         h_into):
    m = x2d.shape[0]
    emit_h = layer + 1 < DEPTH
    once = pl.Buffered(1)
    wspec = lambda k, l=layer: pl.BlockSpec((None, k, D_MODEL), lambda i: (l, 0, 0), pipeline_mode=once)
    row_spec = pl.BlockSpec((rows, D_MODEL), lambda i: (i, 0))
    in_specs = [row_spec, row_spec, pl.BlockSpec((None, rows, PLE_DIM), lambda i: (layer, i, 0)), wspec(1)]
    args = [mm_, x2d, p3d, norm_post.reshape(DEPTH, 1, D_MODEL)]
    out_specs, out_shape = [row_spec], [jax.ShapeDtypeStruct((m, D_MODEL), F32)]
    if emit_h:
        in_specs.append(wspec(1, layer + 1))
        args.append(norm_pre.reshape(DEPTH, 1, D_MODEL))
    in_specs += [wspec(D_MODEL), wspec(D_MODEL), wspec(PLE_DIM)]
    args += [w_out_b, w_gate_b, w_ple_b]
    aliases = {}
    if emit_h:
        h_spec, h_shape, x_specs, x_args, alias_in = _shared_rows_out(
            (rows, D_MODEL), row0, total_rows, D_MODEL, BF16, h_into, len(args))
        out_specs.append(h_spec)
        out_shape.append(h_shape)
        in_specs += x_specs
        args += x_args
        if alias_in is not None:
            aliases = {alias_in: 1}
    outs = pl.pallas_call(
        functools.partial(_out_kernel, n_split=rows // OUT_SUB_ROWS, emit_h=emit_h),
        grid=(m // rows,),
        in_specs=in_specs, out_specs=out_specs, out_shape=out_shape,
        input_output_aliases=aliases,
        name="out",
        compiler_params=pltpu.CompilerParams(
            dimension_semantics=("parallel",), vmem_limit_bytes=VMEM_LIMIT_BYTES),
    )(*args)
    return (outs[0], outs[1]) if emit_h else (outs[0], None)


def _pair_rows(s):
    l, b = s.shape[:2]
    return s.reshape(l, b, N_HEADS, HEAD_DIM // 2, 2 * HEAD_DIM)


def _unpair_rows(s):
    l, b = s.shape[:2]
    return s.reshape(l, b, N_HEADS, HEAD_DIM, HEAD_DIM)


def _layer_weights(i, mu_rkv, mu_lora, w0, w1, w2, a0, a1, a2, v0, v1, v2, k_k, k_a, r_k, ln_w, ln_b):
    r_ = RWKV_WIDTH
    mu = mu_lora[i]
    if i > 0:
        v1_i, v2_i, v0_i = v1[i - 1], v2[i - 1], v0[i - 1]
    else:
        v1_i = jnp.zeros((D_MODEL, MV_RANK), F32)
        v2_i = jnp.zeros((MV_RANK, r_), F32)
        v0_i = jnp.zeros((r_,), F32)
    w_lora = jnp.concatenate(
        [(1.0 - mu[0])[:, None] * w1[i], (1.0 - mu[1])[:, None] * a1[i], (1.0 - mu[2])[:, None] * v1_i,
         mu[0][:, None] * w1[i], mu[1][:, None] * a1[i], mu[2][:, None] * v1_i], axis=1).astype(BF16)
    z96 = jnp.zeros((DECAY_RANK, r_), F32)
    z64 = jnp.zeros((MV_RANK, r_), F32)
    w2e = jnp.concatenate([
        jnp.concatenate([w2[i], z96, z64], axis=0),
        jnp.concatenate([z96, a2[i], z64], axis=0),
        jnp.concatenate([z96, z96, v2_i], axis=0)], axis=1).astype(BF16)
    rows = [mu_rkv[i, 0], mu_rkv[i, 1], mu_rkv[i, 2], w0[i], a0[i], v0_i, k_k[i], k_a[i], r_k[i], ln_w[i], ln_b[i]]
    vecs = jnp.concatenate([jnp.stack(rows), jnp.zeros((_N_VEC_ROWS - len(rows), r_), F32)], axis=0)
    return w_lora, w2e, vecs


class _Group:
    def __init__(self, x, p, wkv, shift, pool, *, n_past, row0, n_seq, n_sub, mix_seq, mix_t):
        self.b, self.t, d = x.shape
        self.m = self.b * self.t
        self.x2d = x.reshape(self.m, d)
        self.p3d = p.reshape(DEPTH, self.m, PLE_DIM)
        self.zero_state = wkv is None
        self.shift, self.n_past, self.row0 = shift, n_past, row0
        self.n_seq, self.n_sub, self.mix_seq, self.mix_t = n_seq, n_sub, mix_seq, mix_t
        self.s_all = _pair_rows(
            jnp.zeros((DEPTH, self.b, N_HEADS, HEAD_DIM, HEAD_DIM), F32) if self.zero_state else wkv)
        self.pool_all = None if self.zero_state else jnp.transpose(pool, (0, 2, 1, 3))
        self.v_first = None
        self.shift_out, self.pool_out = [], []

    def outputs(self):
        pool_new = jnp.stack(self.pool_out) if self.zero_state else jnp.transpose(self.pool_all, (0, 2, 1, 3))
        return (self.x2d.reshape(self.b, self.t, D_MODEL), _unpair_rows(self.s_all), jnp.stack(self.shift_out),
                pool_new)


def _run_layers(groups, lw, params):
    (w_in, norm_pre, norm_post, pool_w_b, pool_scale, w_branch_b, w_out_b, w_ple_b, w_gate_b) = params
    total_rows = sum(g.m for g in groups)
    assert total_rows % PROJ_TM == 0
    h_all = None
    for g in groups:
        h_all = _norm(g.x2d, norm_pre[0], rows=NORM_ROWS, out_dtype=BF16, row0=g.row0, total_rows=total_rows,
                      into=h_all)
    for i in range(DEPTH):
        w_lora, w2e, vecs = lw[i]
        z = _proj(h_all, w_in, i, w_lora, tm=PROJ_TM, n_main=N_MAIN_TILES, out_dtype=BF16)
        h_all = None
        for g in groups:
            g.shift_out.append(
                _norm(g.x2d.reshape(g.b, g.t, D_MODEL)[:, -1], norm_pre[i], rows=g.b, out_dtype=F32))
            if g.zero_state:
                assert g.t >= POOL_BUF
                prev0 = jnp.zeros((g.b, PREV_COLS), F32)
                g.pool_out.append(jnp.stack([
                    z[g.row0 + (q + 1) * g.t - POOL_BUF:g.row0 + (q + 1) * g.t, Z_U * POOL_WIDTH:(Z_U + 1) * POOL_WIDTH]
                    for q in range(g.b)]).astype(F32))
            else:
                prev0 = _proj(g.shift[i].astype(BF16), w_in, i, w_lora, tm=g.b, n_main=3 * RWKV_WIDTH // PROJ_TN,
                              out_dtype=F32)
            ya, g.v_first, g.s_all = _wkv(z, prev0, g.v_first, w2e, vecs, g.s_all, i, n_batch=g.b, t_len=g.t,
                                          n_seq=g.n_seq, n_sub=g.n_sub, row0=g.row0)
            mm_, g.pool_all = _mix(ya, z, g.pool_all, pool_w_b, pool_scale, w_branch_b, i, n_seq=g.mix_seq,
                                   t_seq=g.mix_t, tiles_per_seq=g.t // g.mix_t, n_past=g.n_past, row0=g.row0)
            g.x2d, h_all = _out(mm_, g.x2d, g.p3d, norm_post, norm_pre, w_out_b, w_gate_b, w_ple_b, i,
                                rows=OUT_ROWS, row0=g.row0, total_rows=total_rows, h_into=h_all)


def kernel(x_prompt, x_sample, state_wkv, state_shift, state_pool, p_prompt, p_sample, norm_pre, norm_post, w_in, mu_rkv, mu_lora, w0, w1, w2, a0, a1, a2, v0, v1, v2, k_k, k_a, r_k, ln_w, ln_b, pool_w, pool_scale, w_branch, w_out, w_ple, w_ple_gate):
    lw = [_layer_weights(i, mu_rkv, mu_lora, w0, w1, w2, a0, a1, a2, v0, v1, v2, k_k, k_a, r_k, ln_w, ln_b)
          for i in range(DEPTH)]
    params = (w_in, norm_pre, norm_post, pool_w.astype(BF16), pool_scale, w_branch.astype(BF16),
              w_out.astype(BF16), w_ple.astype(BF16), w_ple_gate.astype(BF16))
    prompt = _Group(x_prompt, p_prompt, None, None, None, n_past=0, row0=0,
                    n_seq=1, n_sub=8, mix_seq=1, mix_t=256)
    sample = _Group(x_sample, p_sample, state_wkv, state_shift, state_pool, n_past=PAST_LEN, row0=prompt.m,
                    n_seq=8, n_sub=2, mix_seq=32, mix_t=8)
    _run_layers([prompt, sample], lw, params)
    y_p, wkv_p, shift_p, pool_p = prompt.outputs()
    y_s, wkv_s, shift_s, pool_s = sample.outputs()
    return (y_p, y_s, wkv_p, shift_p, pool_p, wkv_s, shift_s, pool_s)
```

```python
import functools

import jax
import jax.numpy as jnp
import numpy as np
from jax import lax
from jax.experimental import pallas as pl
from jax.experimental.pallas import tpu as pltpu

F32 = jnp.float32
BF16 = jnp.bfloat16

D_MODEL = 2048
DEPTH = 4
PAST_LEN = 16384
RWKV_WIDTH = 1024
HEAD_DIM = 64
N_HEADS = 16
POOL_WIDTH = 1024
POOL_WINDOWS = (2, 4, 8, 16)
POOL_GROUP = 256
POOL_BUF = 15
PLE_DIM = 256
DECAY_RANK = 96
AAA_RANK = 96
MV_RANK = 64
LORA_COLS = DECAY_RANK + AAA_RANK + MV_RANK
IN_COLS = 10240
RMS_EPS = 1e-6
GN_EPS = 64e-5
DECAY_SCALE = 0.606531

HEADS_PER_GROUP = 4
GROUP_LANES = HEADS_PER_GROUP * HEAD_DIM
N_GROUPS = N_HEADS // HEADS_PER_GROUP
CHUNK = 64
EXPANDED = HEADS_PER_GROUP * CHUNK
F32_ROWS = 8
BF16_ROWS = 16
HALO = 16
PROJ_TN = 2 * LORA_COLS
N_MAIN_TILES = IN_COLS // PROJ_TN
Z_R, Z_K, Z_V, Z_G_RWKV, Z_U, Z_G_POOL = range(6)
Z_GATE_A, Z_GATE_B = 3, 4
Z_LORA = N_MAIN_TILES
PREV_COLS = 3 * RWKV_WIDTH + PROJ_TN
LB_OFF = 3 * RWKV_WIDTH + LORA_COLS

NORM_ROWS = 512
PROJ_TM = 3072
OUT_ROWS = 256
OUT_SUB_ROWS = 128

VMEM_LIMIT_BYTES = 56 * 1024 * 1024


def _sigmoid(x):
    return 0.5 * jnp.tanh(0.5 * x) + 0.5


def _split_bf16(x):
    hi = x.astype(BF16)
    lo = (x - hi.astype(F32)).astype(BF16)
    return hi, lo


_NN = (((1,), (0,)), ((), ()))
_NT = (((1,), (1,)), ((), ()))


def _dot(a, b, dims=_NN):
    return lax.dot_general(a, b, dimension_numbers=dims, preferred_element_type=F32)


def _segsum(xs, bones_b):
    rows = xs[0].shape[0]
    s = _dot(jnp.concatenate([x.astype(BF16) for x in xs], axis=0), bones_b)
    return [s[i * rows:(i + 1) * rows] for i in range(len(xs))]


def _proj_kernel(h_ref, w_ref, wl_ref, z_ref, *, n_main):
    j = pl.program_id(1)

    @pl.when(j < n_main)
    def _():
        z_ref[...] = jnp.dot(h_ref[...], w_ref[...].astype(BF16), preferred_element_type=F32).astype(z_ref.dtype)

    @pl.when(j == n_main)
    def _():
        z_ref[...] = jnp.dot(h_ref[...], wl_ref[...], preferred_element_type=F32).astype(z_ref.dtype)


def _proj(h2d, w_in, layer, w_lora, *, tm, n_main, out_dtype):
    m, d = h2d.shape
    tn = PROJ_TN
    assert m % tm == 0
    return pl.pallas_call(
        functools.partial(_proj_kernel, n_main=n_main),
        grid=(m // tm, n_main + 1),
        in_specs=[
            pl.BlockSpec((tm, d), lambda i, j: (i, 0)),
            pl.BlockSpec((None, d, tn), lambda i, j: (layer, 0, jnp.minimum(j, n_main - 1))),
            pl.BlockSpec((d, tn), lambda i, j: (0, 0), pipeline_mode=pl.Buffered(1)),
        ],
        out_specs=pl.BlockSpec((tm, tn), lambda i, j: (i, j)),
        out_shape=jax.ShapeDtypeStruct((m, (n_main + 1) * tn), out_dtype),
        name="proj",
        compiler_params=pltpu.CompilerParams(
            dimension_semantics=("parallel", "arbitrary"), vmem_limit_bytes=VMEM_LIMIT_BYTES),
    )(h2d, w_in, w_lora)


def _rmsnorm(x, gain):
    return x * lax.rsqrt(jnp.mean(x * x, axis=-1, keepdims=True) + RMS_EPS) * gain


def _norm_kernel(x_ref, g_ref, *rest, n_own):
    o_ref = rest[-1]

    @pl.when(pl.program_id(0) < n_own)
    def _():
        o_ref[...] = _rmsnorm(x_ref[...], g_ref[...]).astype(o_ref.dtype)

    @pl.when(pl.program_id(0) >= n_own)
    def _():
        o_ref[...] = jnp.zeros(o_ref.shape, o_ref.dtype)


class _SharedRows:
    def __init__(self, m, rows, row0, total_rows, d, dtype, into):
        assert m % rows == 0 and row0 % rows == 0 and total_rows % rows == 0
        assert into is not None or row0 == 0
        self.n_own = m // rows
        self.grid = (total_rows // rows,) if into is None else (self.n_own,)
        blk0 = row0 // rows
        self.out_spec = pl.BlockSpec((rows, d), lambda i: (i + blk0, 0))
        self.out_shape = jax.ShapeDtypeStruct((total_rows, d), dtype)
        self.extra_specs = [] if into is None else [pl.BlockSpec(memory_space=pl.ANY)]
        self.extra_args = [] if into is None else [into]
        self.aliased = into is not None

    def own(self, i):
        return jnp.minimum(i, self.n_own - 1)


def _norm(x2d, gain, *, rows, out_dtype, row0=0, total_rows=None, into=None):
    m, d = x2d.shape
    rows = min(rows, m)
    sh = _SharedRows(m, rows, row0, total_rows or m, d, out_dtype, into)
    return pl.pallas_call(
        functools.partial(_norm_kernel, n_own=sh.n_own),
        grid=sh.grid,
        in_specs=[pl.BlockSpec((rows, d), lambda i: (sh.own(i), 0)),
                  pl.BlockSpec((1, d), lambda i: (0, 0))] + sh.extra_specs,
        out_specs=sh.out_spec, out_shape=sh.out_shape,
        input_output_aliases={2: 0} if sh.aliased else {},
        name="norm",
        compiler_params=pltpu.CompilerParams(dimension_semantics=("arbitrary",)),
    )(x2d, gain.reshape(1, d), *sh.extra_args)


_V_MU_R, _V_MU_K, _V_MU_V, _V_W0, _V_A0, _V_V0, _V_KK, _V_KA, _V_RK, _V_LNW, _V_LNB = range(11)
_N_VEC_ROWS = 16


def _wkv_constants(n_seq, t_seq):
    c, e = CHUNK, EXPANDED
    assert n_seq * t_seq == c
    row = np.arange(c)
    q_of = row // t_seq
    tri = (row[:, None] >= row[None, :]) & (q_of[:, None] == q_of[None, :])
    erow = np.arange(e)
    es = erow % c
    same = q_of[:, None] == (es // t_seq)[None, :]
    mask_s = same & (row[:, None] > es[None, :])
    mask_i = same & (row[:, None] >= es[None, :])
    mask_sc = np.block([[mask_s, mask_s], [mask_i, mask_i]])
    lane = np.arange(GROUP_LANES)
    bones = (lane[:, None] // HEAD_DIM) == (lane[None, :] // HEAD_DIM)
    levels = [np.eye(e, dtype=bool), (erow[:, None] // 2) == (erow[None, :] // 2)]
    m = 2
    while m < t_seq:
        levels.append(((erow[:, None] // (2 * m)) == (erow[None, :] // (2 * m)))
                      & ((erow[:, None] % (2 * m)) >= m) & ((erow[None, :] % (2 * m)) < m))
        m *= 2
    col = np.arange(2 * c)
    qmask = ((col[None, :] % c) // t_seq) == np.arange(n_seq)[:, None]
    f = lambda a: jnp.asarray(a.astype(np.float32))
    return dict(tri=f(tri).astype(BF16), mask_sc=f(mask_sc).astype(BF16), bones=f(bones),
                bones_b=f(bones).astype(BF16), levels=f(np.stack(levels)).astype(BF16),
                qmask=f(qmask).reshape(n_seq, 1, 2 * c))


def _wkv_kernel(*refs, n_seq, t_seq, n_sub, chain, n_steps, has_vres, n_levels):
    it = iter(refs)
    zr_ref, zk_ref, zv_ref, zg_ref, zl_ref, p0_ref = (next(it) for _ in range(6))
    vfirst_ref = next(it) if has_vres else None
    w2_ref, vec_ref, s0_ref = next(it), next(it), next(it)
    tri_ref, msc_ref, bones_ref, bonesb_ref, lev_ref, qmask_ref = (next(it) for _ in range(6))
    ya_ref = next(it)
    vout_ref = None if has_vres else next(it)
    sout_ref = next(it)
    sbd_ref, prev_ref = next(it), next(it)

    step = pl.program_id(1)
    gl, rw, e = GROUP_LANES, RWKV_WIDTH, EXPANDED
    n_slots = n_seq if chain else n_sub * n_seq
    slot = (lambda j, q: q) if chain else (lambda j, q: j * n_seq + q)
    bones = bones_ref[...]
    bones_b = bonesb_ref[...]
    mask_sc = msc_ref[...]
    eye_b = lev_ref[0]
    tri = tri_ref[...]
    vec = vec_ref[...]
    vrow = lambda i: vec[i:i + 1, :]
    groups = range(N_GROUPS)
    sls = [slice(gi * gl, (gi + 1) * gl) for gi in groups]

    def expand(x):
        return jnp.concatenate([x] * HEADS_PER_GROUP, axis=0) * bones_b

    low_lanes = lax.broadcasted_iota(jnp.int32, (1, 2 * HEAD_DIM), 1) < HEAD_DIM

    def head_slot(hd):
        h = hd % HEADS_PER_GROUP
        return hd // HEADS_PER_GROUP, h * HEAD_DIM, h // 2, h % 2 == 1

    def load_state(s, g):
        return jnp.concatenate([sbd_ref[s, g, 0], sbd_ref[s, g, 1]], axis=1)

    def store_state(s, g, val):
        sbd_ref[s, g, 0] = val[:, :2 * HEAD_DIM]
        sbd_ref[s, g, 1] = val[:, 2 * HEAD_DIM:]

    @pl.when(step == 0)
    def _():
        for s in range(n_slots):
            for hd in range(N_HEADS):
                g, row0, half, upper = head_slot(hd)
                packed = s0_ref[s, hd]
                even = jnp.where(low_lanes, packed, 0.0)
                odd = jnp.where(low_lanes, 0.0, packed)
                if upper:
                    even = pltpu.roll(even, HEAD_DIM, axis=1)
                else:
                    odd = pltpu.roll(odd, HEAD_DIM, axis=1)
                sbd_ref[s, g, half, pl.ds(row0, HEAD_DIM // 2, stride=2), :] = even
                sbd_ref[s, g, half, pl.ds(row0 + 1, HEAD_DIM // 2, stride=2), :] = odd
                sbd_ref[s, g, 1 - half, row0:row0 + HEAD_DIM, :] = jnp.zeros((HEAD_DIM, 2 * HEAD_DIM), F32)
        prev_ref[...] = p0_ref[0]

    row = lax.broadcasted_iota(jnp.int32, (CHUNK, 1), 0)
    is_first = (row % t_seq) == 0
    lora_lane = lax.broadcasted_iota(jnp.int32, (1, LORA_COLS), 1)

    def shifted(z, p0):
        rolled = pltpu.roll(z, 1, axis=0)
        if n_seq == 1:
            first = jnp.broadcast_to(p0, z.shape)
        else:
            first = jnp.broadcast_to(p0[:, None, :], (n_seq, t_seq, z.shape[1])).reshape(z.shape)
        return jnp.where(is_first, first, rolled)

    n_rows = n_sub * CHUNK
    zl_all = zl_ref[...].astype(F32)
    lb_all = zl_all[:, LORA_COLS:]
    p_l = prev_ref[:, LB_OFF:LB_OFF + LORA_COLS]
    if n_slots == 1:
        lb_first = jnp.broadcast_to(p_l, lb_all.shape)
    else:
        lb_first = jnp.broadcast_to(p_l[:, None, :], (n_slots, t_seq, LORA_COLS)).reshape(lb_all.shape)
    row_all = lax.broadcasted_iota(jnp.int32, (n_rows, 1), 0)
    lb_prev = jnp.where(row_all % (n_rows if chain else t_seq) == 0, lb_first, pltpu.roll(lb_all, 1, axis=0))
    lin = zl_all[:, :LORA_COLS] + lb_prev
    lin = jnp.where(lora_lane < DECAY_RANK, jnp.tanh(lin), lin).astype(BF16)
    d_full = jnp.dot(lin, w2_ref[...], preferred_element_type=F32)
    lb_last = lb_all[n_rows - 1:n_rows, :]

    ctx = [dict() for _ in range(n_sub)]

    def prep(j):
        c = ctx[j]
        rows = slice(j * CHUNK, (j + 1) * CHUNK)
        zr, zk, zv = zr_ref[rows, :].astype(F32), zk_ref[rows, :].astype(F32), zv_ref[rows, :].astype(F32)
        c["last"] = [x[CHUNK - 1:CHUNK, :] for x in (zr, zk, zv)]
        if chain and j > 0:
            p_r, p_k, p_v = ctx[j - 1]["last"]
        else:
            ps = slice(0, n_seq) if chain else slice(j * n_seq, (j + 1) * n_seq)
            p_r, p_k, p_v = prev_ref[ps, 0:rw], prev_ref[ps, rw:2 * rw], prev_ref[ps, 2 * rw:3 * rw]
        r = zr + (shifted(zr, p_r) - zr) * vrow(_V_MU_R)
        k = zk + (shifted(zk, p_k) - zk) * vrow(_V_MU_K)
        v = zv + (shifted(zv, p_v) - zv) * vrow(_V_MU_V)
        d_all = d_full[rows]
        wlog = -DECAY_SCALE * _sigmoid(vrow(_V_W0) + d_all[:, 0:rw])
        alr = _sigmoid(vrow(_V_A0) + d_all[:, rw:2 * rw])
        if has_vres:
            v = v + (vfirst_ref[rows, :] - v) * _sigmoid(vrow(_V_V0) + d_all[:, 2 * rw:3 * rw])
        else:
            vout_ref[rows, :] = v
        kkr = k * vrow(_V_KK)
        kf = k * (1.0 + (alr - 1.0) * vrow(_V_KA))
        c["rkr"] = r * kf * vrow(_V_RK)
        g_all = zg_ref[rows, :].astype(F32)
        c["gate"] = g_all * _sigmoid(g_all)
        w_hi, w_lo = _split_bf16(wlog)
        cum = jnp.dot(tri, w_hi, preferred_element_type=F32) + jnp.dot(tri, w_lo, preferred_element_type=F32)
        p_inc = jnp.exp(cum)
        inv_p = jnp.exp(-cum)
        p_exc = jnp.exp(cum - wlog)
        c["p_inc"] = p_inc
        yield
        kk_n2 = _segsum([kkr[:, sl] * kkr[:, sl] for sl in sls], bones_b)
        kk = [kkr[:, sl] * lax.rsqrt(jnp.maximum(n2, 1e-24)) for sl, n2 in zip(sls, kk_n2)]
        c["v_g"] = [v[:, sl] for sl in sls]
        c["a_f"] = [-kk[gi] * p_exc[:, sls[gi]] for gi in groups]
        c["r_f"] = [r[:, sl] * p_inc[:, sl] for sl in sls]
        a_t = [x.astype(BF16) for x in c["a_f"]]
        r_t = [x.astype(BF16) for x in c["r_f"]]
        c["b_t"] = [(kk[gi] * alr[:, sls[gi]] * inv_p[:, sls[gi]]).astype(BF16) for gi in groups]
        c["k_t"] = [(kf[:, sl] * inv_p[:, sl]).astype(BF16) for sl in sls]
        c["v_e"] = [expand(x.astype(BF16)) for x in c["v_g"]]
        c["ar"] = [jnp.concatenate([a_t[gi], r_t[gi]], axis=0) for gi in groups]
        yield
        sc = [_dot(c["ar"][gi], jnp.concatenate([expand(c["b_t"][gi]), expand(c["k_t"][gi])], axis=0), _NT)
              for gi in groups]
        c["scm"] = [x.astype(BF16) * mask_sc for x in sc]
        c["n_bd"] = [expand(x[:CHUNK, :e]) for x in c["scm"]]
        c["xs"] = [eye_b + n * lev_ref[1] for n in c["n_bd"]]
        yield
        c["akv"] = [_dot(c["scm"][gi][:, e:], c["v_e"][gi]) for gi in groups]
        yield

    def inverse(j):
        c = ctx[j]
        m = 2
        xs32 = None
        for lvl in range(2, n_levels):
            n_c = [c["n_bd"][gi] * lev_ref[lvl] for gi in groups]
            if m % F32_ROWS == 0:
                wide = m % BF16_ROWS != 0
                src = xs32 if wide else c["xs"]
                n_blk = e // (2 * m)
                lower = lambda x: jnp.concatenate([x[b * 2 * m + m:(b + 1) * 2 * m] for b in range(n_blk)], axis=0)
                eye_l = lower(eye_b.astype(src[0].dtype)).astype(BF16)
                t1 = [_dot(lower(src[gi]).astype(BF16), n_c[gi]).astype(BF16) + eye_l for gi in groups]
                yield
                new = [_dot(t1[gi], c["xs"][gi]).astype(src[0].dtype) for gi in groups]
                c["xs"] = [jnp.concatenate(
                    [piece for b in range(n_blk)
                     for piece in (x[b * 2 * m:b * 2 * m + m], n[b * m:(b + 1) * m])], axis=0).astype(BF16)
                    for x, n in zip(src, new)]
                xs32 = None
            else:
                t1 = [_dot(c["xs"][gi], n_c[gi]).astype(BF16) + eye_b for gi in groups]
                yield
                full = [_dot(t1[gi], c["xs"][gi]) for gi in groups]
                if (2 * m) % F32_ROWS == 0 and (2 * m) % BF16_ROWS != 0:
                    xs32 = full
                c["xs"] = [x.astype(BF16) for x in full]
            yield
            m *= 2

    def tail(j):
        c = ctx[j]
        rows = slice(j * CHUNK, (j + 1) * CHUNK)
        x_c = [(x[0:CHUNK] + x[CHUNK:2 * CHUNK]) + (x[2 * CHUNK:3 * CHUNK] + x[3 * CHUNK:4 * CHUNK])
               for x in c["xs"]]
        a_s, r_s = [], []
        for gi in groups:
            parts_a, parts_r = [], []
            for q in range(n_seq):
                lo_, hi_ = q * t_seq, (q + 1) * t_seq
                lhs = c["ar"][gi] if n_seq == 1 else jnp.concatenate(
                    [c["a_f"][gi][lo_:hi_], c["r_f"][gi][lo_:hi_]], axis=0).astype(BF16)
                res = _dot(lhs, load_state(slot(j, q), gi).astype(BF16), _NT)
                parts_a.append(res[:t_seq])
                parts_r.append(res[t_seq:])
            a_s.append(parts_a[0] if n_seq == 1 else jnp.concatenate(parts_a, axis=0))
            r_s.append(parts_r[0] if n_seq == 1 else jnp.concatenate(parts_r, axis=0))
        yield
        y = [a_s[gi] + c["akv"][gi][:CHUNK] for gi in groups]
        u = [_dot(x_c[gi], expand(y[gi].astype(BF16))) for gi in groups]
        yield
        o = [r_s[gi] + c["akv"][gi][CHUNK:] + _dot(c["scm"][gi][CHUNK:, :e], expand(u[gi].astype(BF16)))
             for gi in groups]
        yield
        for gi in groups:
            uv_t = jnp.concatenate([u[gi], c["v_g"][gi]], axis=0).T
            bk = jnp.concatenate([c["b_t"][gi], c["k_t"][gi]], axis=0)
            for q in range(n_seq):
                lhs = uv_t if n_seq == 1 else uv_t * qmask_ref[q]
                p_last = c["p_inc"][(q + 1) * t_seq - 1:(q + 1) * t_seq, sls[gi]]
                s = slot(j, q)
                store_state(s, gi, bones * ((load_state(s, gi) + _dot(lhs.astype(BF16), bk)) * p_last))
        yield
        mean = [x * (1.0 / HEAD_DIM) for x in _segsum(o, bones_b)]
        d = [o[gi] - mean[gi] for gi in groups]
        var = [x * (1.0 / HEAD_DIM) for x in _segsum([x * x for x in d], bones_b)]
        bonus = _segsum([c["rkr"][:, sl] for sl in sls], bones_b)
        yield
        for gi in groups:
            sl = sls[gi]
            on = d[gi] * lax.rsqrt(var[gi] + GN_EPS) * vec[_V_LNW:_V_LNW + 1, sl] + vec[_V_LNB:_V_LNB + 1, sl]
            ya_ref[rows, sl] = ((on + bonus[gi] * c["v_g"][gi]) * c["gate"][:, sl]).astype(ya_ref.dtype)

    preps, invs, tails = ([f(j) for j in range(n_sub)] for f in (prep, inverse, tail))
    for phase in range(n_sub + 2):
        active = []
        if 0 <= phase - 2 < n_sub:
            active.append(tails[phase - 2])
        if 0 <= phase - 1 < n_sub:
            active.append(invs[phase - 1])
        if phase < n_sub:
            active.append(preps[phase])
        while active:
            for gen in list(active):
                try:
                    next(gen)
                except StopIteration:
                    active.remove(gen)

    if chain and n_steps > 1:
        assert n_seq == 1
        last_r, last_k, last_v = ctx[n_sub - 1]["last"]
        prev_ref[:, 0:rw] = last_r
        prev_ref[:, rw:2 * rw] = last_k
        prev_ref[:, 2 * rw:3 * rw] = last_v
        prev_ref[:, LB_OFF:LB_OFF + LORA_COLS] = lb_last

    @pl.when(step == n_steps - 1)
    def _():
        for s in range(n_slots):
            for hd in range(N_HEADS):
                g, row0, half, upper = head_slot(hd)
                even = sbd_ref[s, g, half, pl.ds(row0, HEAD_DIM // 2, stride=2), :]
                odd = sbd_ref[s, g, half, pl.ds(row0 + 1, HEAD_DIM // 2, stride=2), :]
                if upper:
                    even = pltpu.roll(even, HEAD_DIM, axis=1)
                else:
                    odd = pltpu.roll(odd, HEAD_DIM, axis=1)
                sout_ref[s, hd] = jnp.where(low_lanes, even, odd)


def _wkv(z, prev0, vfirst, w2e, vecs, s_all, layer, *, n_batch, t_len, n_seq, n_sub, row0):
    m = n_batch * t_len
    t_seq = CHUNK // n_seq
    n_chunks = t_len // t_seq
    chain = n_seq == 1
    if chain:
        assert n_chunks % n_sub == 0
        n_steps, nb, n_slots = n_chunks // n_sub, n_batch, n_seq
    else:
        assert n_chunks == 1 and n_batch % (n_seq * n_sub) == 0
        n_steps, nb, n_slots = 1, n_batch // (n_seq * n_sub), n_seq * n_sub
    has_vres = vfirst is not None
    consts = _wkv_constants(n_seq, t_seq)
    n_levels = consts["levels"].shape[0]
    rw = RWKV_WIDTH
    rows = n_sub * CHUNK
    rb = lambda b, s: b * n_steps + s
    prev0 = prev0.reshape(nb, n_slots, PREV_COLS)

    assert row0 % rows == 0
    z_off = row0 // rows
    zspec = lambda col: pl.BlockSpec((rows, rw), lambda b, s: (rb(b, s) + z_off, col))
    row_spec = pl.BlockSpec((rows, rw), lambda b, s: (rb(b, s), 0))
    full = lambda a: pl.BlockSpec(a.shape, lambda b, s: (0,) * a.ndim)
    sspec = pl.BlockSpec((None, n_slots, N_HEADS, HEAD_DIM // 2, 2 * HEAD_DIM), lambda b, s: (layer, b, 0, 0, 0))

    in_specs = [zspec(Z_R), zspec(Z_K), zspec(Z_V), zspec(Z_G_RWKV),
                pl.BlockSpec((rows, PROJ_TN), lambda b, s: (rb(b, s) + z_off, Z_LORA)),
                pl.BlockSpec((1, n_slots, PREV_COLS), lambda b, s: (b, 0, 0))]
    args = [z, z, z, z, z, prev0]
    if has_vres:
        in_specs.append(row_spec)
        args.append(vfirst)
    else:
        w2e = w2e[:, :2 * rw]
    in_specs += [full(w2e), full(vecs), sspec]
    args += [w2e, vecs, s_all]
    state_arg = len(args) - 1
    for name in ("tri", "mask_sc", "bones", "bones_b", "levels", "qmask"):
        in_specs.append(full(consts[name]))
        args.append(consts[name])

    out_specs = [row_spec]
    out_shape = [jax.ShapeDtypeStruct((m, rw), BF16)]
    if not has_vres:
        out_specs.append(row_spec)
        out_shape.append(jax.ShapeDtypeStruct((m, rw), F32))
    out_specs.append(sspec)
    out_shape.append(jax.ShapeDtypeStruct(s_all.shape, F32))

    outs = pl.pallas_call(
        functools.partial(_wkv_kernel, n_seq=n_seq, t_seq=t_seq, n_sub=n_sub, chain=chain, n_steps=n_steps,
                          has_vres=has_vres, n_levels=n_levels),
        grid=(nb, n_steps),
        in_specs=in_specs, out_specs=out_specs, out_shape=out_shape,
        input_output_aliases={state_arg: len(out_shape) - 1},
        scratch_shapes=[pltpu.VMEM((n_slots, N_GROUPS, 2, GROUP_LANES, GROUP_LANES // 2), F32),
                        pltpu.VMEM((n_slots, PREV_COLS), F32)],
        name="wkv",
        compiler_params=pltpu.CompilerParams(
            dimension_semantics=("parallel", "arbitrary"), vmem_limit_bytes=VMEM_LIMIT_BYTES),
    )(*args)
    if has_vres:
        ya, s_new = outs
        return ya, vfirst, s_new
    return outs


def _mix_kernel(*refs, n_seq, t_seq, tiles_per_seq, n_past, carry_pool):
    if carry_pool:
        (ya_ref, u_ref, gp_ref, ga_ref, gb_ref, pool_ref, perm_ref, permt_ref, pw_ref, ps_ref, wb0_ref, wb1_ref,
         m_ref, pool_out_ref) = refs
    else:
        ya_ref, u_ref, gp_ref, ga_ref, gb_ref, halo_ref, pw_ref, ps_ref, wb0_ref, wb1_ref, m_ref = refs
    rows = n_seq * t_seq
    tile = pl.program_id(0) % tiles_per_seq
    if carry_pool:
        perm = perm_ref[...]
        u_t = jnp.dot(perm, u_ref[...], preferred_element_type=F32).reshape(t_seq, n_seq, POOL_WIDTH)
        gp_all = jnp.dot(perm, gp_ref[...], preferred_element_type=F32)
        e = jnp.concatenate([jnp.zeros((HALO - POOL_BUF, n_seq, POOL_WIDTH), F32), pool_ref[...], u_t], axis=0)
        pool_out_ref[...] = e[HALO + t_seq - POOL_BUF:]
        t_axis = 0
        pos = lax.broadcasted_iota(jnp.int32, (t_seq, 1, 1), 0) + (1 + n_past)
    else:
        u = u_ref[...].astype(F32)
        halo = jnp.where(tile == 0, 0.0, halo_ref[...].astype(F32)).reshape(1, HALO, POOL_WIDTH)
        e = jnp.concatenate([halo, u.reshape(n_seq, t_seq, POOL_WIDTH)], axis=1)
        t_axis = 1
        pos = lax.broadcasted_iota(jnp.int32, (1, t_seq, 1), 1) + (tile * t_seq + 1 + n_past)
    take = lambda x, start, size: lax.slice_in_dim(x, start, start + size, axis=t_axis)
    n_col = len(POOL_WINDOWS)
    cw = D_MODEL // n_col
    cols = [slice(ci * cw, (ci + 1) * cw) for ci in range(n_col)]
    ya = ya_ref[...]
    gated_a, yb_parts = [], []
    for gi, win in enumerate(POOL_WINDOWS):
        pa = jnp.dot(ya, wb0_ref[:, cols[gi]], preferred_element_type=F32)
        sl = slice(gi * POOL_GROUP, (gi + 1) * POOL_GROUP)
        e_g = e[:, :, sl]
        acc = e_g
        span, length = 1, HALO + t_seq
        while span < win:
            length -= span
            acc = take(acc, span, length) + take(acc, 0, length)
            span *= 2
        ws = take(acc, length - t_seq, t_seq)
        inv_cnt = 1.0 / jnp.minimum(pos, win).astype(F32)
        mixed = (ws * inv_cnt - take(e_g, HALO, t_seq)).reshape(rows, POOL_GROUP)
        y_g = jnp.dot(mixed.astype(BF16), pw_ref[gi], preferred_element_type=F32)
        gp = gp_all[:, sl] if carry_pool else gp_ref[:, sl].astype(F32)
        yb_parts.append((y_g * ps_ref[:, sl] * (gp * _sigmoid(gp))).astype(BF16))
        gated_a.append(_sigmoid(ga_ref[:, cols[gi]].astype(F32)) * pa)
    y_b = jnp.concatenate(yb_parts, axis=1)
    if carry_pool:
        y_b = jnp.dot(permt_ref[...], y_b, preferred_element_type=F32).astype(BF16)
    pb = [None] * n_col
    pb[0] = jnp.dot(y_b, wb1_ref[:, cols[0]], preferred_element_type=F32)
    for ci in range(n_col):
        if ci + 1 < n_col:
            pb[ci + 1] = jnp.dot(y_b, wb1_ref[:, cols[ci + 1]], preferred_element_type=F32)
        m_ref[:, cols[ci]] = (gated_a[ci] + _sigmoid(gb_ref[:, cols[ci]].astype(F32)) * pb[ci]).astype(m_ref.dtype)


def _mix(ya, z, pool_all, pool_w_b, pool_scale, w_branch_b, layer, *, n_seq, t_seq, tiles_per_seq, n_past, row0):
    m = ya.shape[0]
    rows = n_seq * t_seq
    assert row0 % rows == 0
    z_off = row0 // rows
    carry_pool = pool_all is not None
    in_specs = [
        pl.BlockSpec((rows, RWKV_WIDTH), lambda i: (i, 0)),
        pl.BlockSpec((rows, POOL_WIDTH), lambda i: (i + z_off, Z_U)),
        pl.BlockSpec((rows, POOL_WIDTH), lambda i: (i + z_off, Z_G_POOL)),
        pl.BlockSpec((rows, D_MODEL), lambda i: (i + z_off, Z_GATE_A)),
        pl.BlockSpec((rows, D_MODEL), lambda i: (i + z_off, Z_GATE_B)),
    ]
    args = [ya, z, z, z, z]
    out_specs = [pl.BlockSpec((rows, D_MODEL), lambda i: (i, 0))]
    out_shape = [jax.ShapeDtypeStruct((m, D_MODEL), BF16)]
    aliases = {}
    if carry_pool:
        assert tiles_per_seq == 1 and t_seq < POOL_BUF
        r = np.arange(rows)
        perm = np.zeros((rows, rows), np.float32)
        perm[(r % t_seq) * n_seq + r // t_seq, r] = 1.0
        pool_spec = pl.BlockSpec((None, POOL_BUF, n_seq, POOL_WIDTH), lambda i: (layer, 0, i, 0))
        const = pl.BlockSpec((rows, rows), lambda i: (0, 0))
        in_specs += [pool_spec, const, const]
        aliases = {len(args): 1}
        args += [pool_all, jnp.asarray(perm, BF16), jnp.asarray(perm.T, BF16)]
        out_specs.append(pool_spec)
        out_shape.append(jax.ShapeDtypeStruct(pool_all.shape, F32))
    else:
        assert n_seq == 1 and rows % HALO == 0 and row0 == 0
        in_specs.append(pl.BlockSpec((HALO, POOL_WIDTH), lambda i: (jnp.maximum(i * (rows // HALO) - 1, 0), Z_U)))
        args.append(z)
    in_specs += [
        pl.BlockSpec((None, len(POOL_WINDOWS), POOL_GROUP, POOL_GROUP), lambda i: (layer, 0, 0, 0)),
        pl.BlockSpec((None, 1, POOL_WIDTH), lambda i: (layer, 0, 0)),
        pl.BlockSpec((None, None, RWKV_WIDTH, D_MODEL), lambda i: (layer, 0, 0, 0), pipeline_mode=pl.Buffered(1)),
        pl.BlockSpec((None, None, POOL_WIDTH, D_MODEL), lambda i: (layer, 1, 0, 0), pipeline_mode=pl.Buffered(1)),
    ]
    args += [pool_w_b, pool_scale.reshape(DEPTH, 1, POOL_WIDTH), w_branch_b, w_branch_b]
    outs = pl.pallas_call(
        functools.partial(_mix_kernel, n_seq=n_seq, t_seq=t_seq, tiles_per_seq=tiles_per_seq, n_past=n_past,
                          carry_pool=carry_pool),
        grid=(m // rows,),
        in_specs=in_specs, out_specs=out_specs, out_shape=out_shape,
        input_output_aliases=aliases,
        name="mix",
        compiler_params=pltpu.CompilerParams(
            dimension_semantics=("parallel",), vmem_limit_bytes=VMEM_LIMIT_BYTES),
    )(*args)
    return outs if carry_pool else (outs[0], None)


def _out_kernel(*refs, n_split, emit_h, n_own):
    if emit_h:
        (m_ref, x_ref, p_ref, g_ref, gn_ref, wo_ref, wg_ref, wp_ref), (o_ref, h_ref) = refs[:8], refs[-2:]
    else:
        m_ref, x_ref, p_ref, g_ref, wo_ref, wg_ref, wp_ref, o_ref = refs

    @pl.when(pl.program_id(0) < n_own)
    def _():
        rows = m_ref.shape[0] // n_split
        parts = [slice(i * rows, (i + 1) * rows) for i in range(n_split)]
        outs = [jnp.dot(m_ref[sl, :], wo_ref[...], preferred_element_type=F32) for sl in parts]
        ples = [jnp.dot(p_ref[sl, :].astype(BF16), wp_ref[...], preferred_element_type=F32) for sl in parts]
        xs = [x_ref[sl, :] + _rmsnorm(o, g_ref[...]) for sl, o in zip(parts, outs)]
        gates = [_sigmoid(jnp.dot(x.astype(BF16), wg_ref[...], preferred_element_type=F32)) for x in xs]
        for sl, x, gate, ple in zip(parts, xs, gates, ples):
            x_new = x + gate * ple
            o_ref[sl, :] = x_new
            if emit_h:
                h_ref[sl, :] = _rmsnorm(x_new, gn_ref[...]).astype(h_ref.dtype)

    if emit_h:
        @pl.when(pl.program_id(0) >= n_own)
        def _():
            h_ref[...] = jnp.zeros(h_ref.shape, h_ref.dtype)


def _out(mm_, x2d, p3d, norm_post, norm_pre, w_out_b, w_gate_b, w_ple_b, layer, *, rows, row0, total_rows,
         h_into):
    m = x2d.shape[0]
    emit_h = layer + 1 < DEPTH
    sh = _SharedRows(m, rows, row0, total_rows, D_MODEL, BF16, h_into) if emit_h else _SharedRows(
        m, rows, 0, m, D_MODEL, BF16, None)
    once = pl.Buffered(1)
    wspec = lambda k, l=layer: pl.BlockSpec((None, k, D_MODEL), lambda i: (l, 0, 0), pipeline_mode=once)
    row_spec = pl.BlockSpec((rows, D_MODEL), lambda i: (sh.own(i), 0))
    in_specs = [row_spec, row_spec, pl.BlockSpec((None, rows, PLE_DIM), lambda i: (layer, sh.own(i), 0)), wspec(1)]
    args = [mm_, x2d, p3d, norm_post.reshape(DEPTH, 1, D_MODEL)]
    out_specs, out_shape = [row_spec], [jax.ShapeDtypeStruct((m, D_MODEL), F32)]
    if emit_h:
        in_specs.append(wspec(1, layer + 1))
        args.append(norm_pre.reshape(DEPTH, 1, D_MODEL))
    in_specs += [wspec(D_MODEL), wspec(D_MODEL), wspec(PLE_DIM)]
    args += [w_out_b, w_gate_b, w_ple_b]
    aliases = {}
    if emit_h:
        out_specs.append(sh.out_spec)
        out_shape.append(sh.out_shape)
        if sh.aliased:
            aliases = {len(args): 1}
        in_specs += sh.extra_specs
        args += sh.extra_args
    outs = pl.pallas_call(
        functools.partial(_out_kernel, n_split=rows // OUT_SUB_ROWS, emit_h=emit_h, n_own=sh.n_own),
        grid=sh.grid,
        in_specs=in_specs, out_specs=out_specs, out_shape=out_shape,
        input_output_aliases=aliases,
        name="out",
        compiler_params=pltpu.CompilerParams(
            dimension_semantics=("arbitrary",), vmem_limit_bytes=VMEM_LIMIT_BYTES),
    )(*args)
    return (outs[0], outs[1]) if emit_h else (outs[0], None)


def _pair_rows(s):
    l, b = s.shape[:2]
    return s.reshape(l, b, N_HEADS, HEAD_DIM // 2, 2 * HEAD_DIM)


def _unpair_rows(s):
    l, b = s.shape[:2]
    return s.reshape(l, b, N_HEADS, HEAD_DIM, HEAD_DIM)


def _layer_weights(i, mu_rkv, mu_lora, w0, w1, w2, a0, a1, a2, v0, v1, v2, k_k, k_a, r_k, ln_w, ln_b):
    r_ = RWKV_WIDTH
    mu = mu_lora[i]
    if i > 0:
        v1_i, v2_i, v0_i = v1[i - 1], v2[i - 1], v0[i - 1]
    else:
        v1_i = jnp.zeros((D_MODEL, MV_RANK), F32)
        v2_i = jnp.zeros((MV_RANK, r_), F32)
        v0_i = jnp.zeros((r_,), F32)
    w_lora = jnp.concatenate(
        [(1.0 - mu[0])[:, None] * w1[i], (1.0 - mu[1])[:, None] * a1[i], (1.0 - mu[2])[:, None] * v1_i,
         mu[0][:, None] * w1[i], mu[1][:, None] * a1[i], mu[2][:, None] * v1_i], axis=1).astype(BF16)
    z96 = jnp.zeros((DECAY_RANK, r_), F32)
    z64 = jnp.zeros((MV_RANK, r_), F32)
    w2e = jnp.concatenate([
        jnp.concatenate([w2[i], z96, z64], axis=0),
        jnp.concatenate([z96, a2[i], z64], axis=0),
        jnp.concatenate([z96, z96, v2_i], axis=0)], axis=1).astype(BF16)
    rows = [mu_rkv[i, 0], mu_rkv[i, 1], mu_rkv[i, 2], w0[i], a0[i], v0_i, k_k[i], k_a[i], r_k[i], ln_w[i], ln_b[i]]
    vecs = jnp.concatenate([jnp.stack(rows), jnp.zeros((_N_VEC_ROWS - len(rows), r_), F32)], axis=0)
    return w_lora, w2e, vecs


class _Group:
    def __init__(self, x, p, wkv, shift, pool, *, n_past, row0, n_seq, n_sub, mix_seq, mix_t):
        self.b, self.t, d = x.shape
        self.m = self.b * self.t
        self.x2d = x.reshape(self.m, d)
        self.p3d = p.reshape(DEPTH, self.m, PLE_DIM)
        self.zero_state = wkv is None
        self.shift, self.n_past, self.row0 = shift, n_past, row0
        self.n_seq, self.n_sub, self.mix_seq, self.mix_t = n_seq, n_sub, mix_seq, mix_t
        self.s_all = _pair_rows(
            jnp.zeros((DEPTH, self.b, N_HEADS, HEAD_DIM, HEAD_DIM), F32) if self.zero_state else wkv)
        self.pool_all = None if self.zero_state else jnp.transpose(pool, (0, 2, 1, 3))
        self.v_first = None
        self.shift_out, self.pool_out = [], []

    def outputs(self):
        pool_new = jnp.stack(self.pool_out) if self.zero_state else jnp.transpose(self.pool_all, (0, 2, 1, 3))
        return (self.x2d.reshape(self.b, self.t, D_MODEL), _unpair_rows(self.s_all), jnp.stack(self.shift_out),
                pool_new)


def _run_layers(groups, lw, params):
    (w_in, norm_pre, norm_post, pool_w_b, pool_scale, w_branch_b, w_out_b, w_ple_b, w_gate_b) = params
    total_rows = sum(g.m for g in groups)
    assert total_rows % PROJ_TM == 0
    h_all = None
    for g in groups:
        h_all = _norm(g.x2d, norm_pre[0], rows=NORM_ROWS, out_dtype=BF16, row0=g.row0, total_rows=total_rows,
                      into=h_all)
    for i in range(DEPTH):
        w_lora, w2e, vecs = lw[i]
        z = _proj(h_all, w_in, i, w_lora, tm=PROJ_TM, n_main=N_MAIN_TILES, out_dtype=BF16)
        h_all = None
        for g in groups:
            g.shift_out.append(
                _norm(g.x2d.reshape(g.b, g.t, D_MODEL)[:, -1], norm_pre[i], rows=g.b, out_dtype=F32))
            if g.zero_state:
                assert g.t >= POOL_BUF
                prev0 = jnp.zeros((g.b, PREV_COLS), F32)
                g.pool_out.append(jnp.stack([
                    z[g.row0 + (q + 1) * g.t - POOL_BUF:g.row0 + (q + 1) * g.t, Z_U * POOL_WIDTH:(Z_U + 1) * POOL_WIDTH]
                    for q in range(g.b)]).astype(F32))
            else:
                prev0 = _proj(g.shift[i].astype(BF16), w_in, i, w_lora, tm=g.b, n_main=3 * RWKV_WIDTH // PROJ_TN,
                              out_dtype=F32)
            ya, g.v_first, g.s_all = _wkv(z, prev0, g.v_first, w2e, vecs, g.s_all, i, n_batch=g.b, t_len=g.t,
                                          n_seq=g.n_seq, n_sub=g.n_sub, row0=g.row0)
            mm_, g.pool_all = _mix(ya, z, g.pool_all, pool_w_b, pool_scale, w_branch_b, i, n_seq=g.mix_seq,
                                   t_seq=g.mix_t, tiles_per_seq=g.t // g.mix_t, n_past=g.n_past, row0=g.row0)
            g.x2d, h_all = _out(mm_, g.x2d, g.p3d, norm_post, norm_pre, w_out_b, w_gate_b, w_ple_b, i,
                                rows=OUT_ROWS, row0=g.row0, total_rows=total_rows, h_into=h_all)


def kernel(x_prompt, x_sample, state_wkv, state_shift, state_pool, p_prompt, p_sample, norm_pre, norm_post, w_in, mu_rkv, mu_lora, w0, w1, w2, a0, a1, a2, v0, v1, v2, k_k, k_a, r_k, ln_w, ln_b, pool_w, pool_scale, w_branch, w_out, w_ple, w_ple_gate):
    lw = [_layer_weights(i, mu_rkv, mu_lora, w0, w1, w2, a0, a1, a2, v0, v1, v2, k_k, k_a, r_k, ln_w, ln_b)
          for i in range(DEPTH)]
    params = (w_in, norm_pre, norm_post, pool_w.astype(BF16), pool_scale, w_branch.astype(BF16),
              w_out.astype(BF16), w_ple.astype(BF16), w_ple_gate.astype(BF16))
    prompt = _Group(x_prompt, p_prompt, None, None, None, n_past=0, row0=0,
                    n_seq=1, n_sub=8, mix_seq=1, mix_t=256)
    sample = _Group(x_sample, p_sample, state_wkv, state_shift, state_pool, n_past=PAST_LEN, row0=prompt.m,
                    n_seq=8, n_sub=2, mix_seq=32, mix_t=8)
    _run_layers([prompt, sample], lw, params)
    y_p, wkv_p, shift_p, pool_p = prompt.outputs()
    y_s, wkv_s, shift_s, pool_s = sample.outputs()
    return (y_p, y_s, wkv_p, shift_p, pool_p, wkv_s, shift_s, pool_s)
```

```python
import functools

import jax
import jax.numpy as jnp
import numpy as np
from jax import lax
from jax.experimental import pallas as pl
from jax.experimental.pallas import tpu as pltpu

F32 = jnp.float32
BF16 = jnp.bfloat16

D_MODEL = 2048
DEPTH = 4
PAST_LEN = 16384
RWKV_WIDTH = 1024
HEAD_DIM = 64
N_HEADS = 16
POOL_WIDTH = 1024
POOL_WINDOWS = (2, 4, 8, 16)
POOL_GROUP = 256
POOL_BUF = 15
PLE_DIM = 256
DECAY_RANK = 96
AAA_RANK = 96
MV_RANK = 64
LORA_COLS = DECAY_RANK + AAA_RANK + MV_RANK
IN_COLS = 10240
RMS_EPS = 1e-6
GN_EPS = 64e-5
DECAY_SCALE = 0.606531

HEADS_PER_GROUP = 4
GROUP_LANES = HEADS_PER_GROUP * HEAD_DIM
N_GROUPS = N_HEADS // HEADS_PER_GROUP
CHUNK = 64
EXPANDED = HEADS_PER_GROUP * CHUNK
F32_ROWS = 8
BF16_ROWS = 16
HALO = 16
PROJ_TN = 2 * LORA_COLS
N_MAIN_TILES = IN_COLS // PROJ_TN
Z_R, Z_K, Z_V, Z_G_RWKV, Z_U, Z_G_POOL = range(6)
Z_GATE_A, Z_GATE_B = 3, 4
Z_LORA = N_MAIN_TILES
PREV_COLS = 3 * RWKV_WIDTH + PROJ_TN
LB_OFF = 3 * RWKV_WIDTH + LORA_COLS

NORM_ROWS = 512
PROJ_TM = 3072
OUT_ROWS = 256
OUT_SUB_ROWS = 128

V7X_VMEM_BYTES = 64 * 1024 * 1024
VMEM_LIMIT_BYTES = V7X_VMEM_BYTES * 7 // 8


def _sigmoid(x):
    return 0.5 * jnp.tanh(0.5 * x) + 0.5


def _split_bf16(x):
    hi = x.astype(BF16)
    lo = (x - hi.astype(F32)).astype(BF16)
    return hi, lo


_NN = (((1,), (0,)), ((), ()))
_NT = (((1,), (1,)), ((), ()))


def _dot(a, b, dims=_NN):
    return lax.dot_general(a, b, dimension_numbers=dims, preferred_element_type=F32)


def _segsum(xs, bones_b):
    rows = xs[0].shape[0]
    s = _dot(jnp.concatenate([x.astype(BF16) for x in xs], axis=0), bones_b)
    return [s[i * rows:(i + 1) * rows] for i in range(len(xs))]


def _proj_kernel(h_ref, w_ref, wl_ref, z_ref, *, n_main):
    j = pl.program_id(1)

    @pl.when(j < n_main)
    def _():
        z_ref[...] = jnp.dot(h_ref[...], w_ref[...].astype(BF16), preferred_element_type=F32).astype(z_ref.dtype)

    @pl.when(j == n_main)
    def _():
        z_ref[...] = jnp.dot(h_ref[...], wl_ref[...], preferred_element_type=F32).astype(z_ref.dtype)


def _proj(h2d, w_in, layer, w_lora, *, tm, n_main, out_dtype):
    m, d = h2d.shape
    tn = PROJ_TN
    assert m % tm == 0
    return pl.pallas_call(
        functools.partial(_proj_kernel, n_main=n_main),
        grid=(m // tm, n_main + 1),
        in_specs=[
            pl.BlockSpec((tm, d), lambda i, j: (i, 0)),
            pl.BlockSpec((None, d, tn), lambda i, j: (layer, 0, jnp.minimum(j, n_main - 1))),
            pl.BlockSpec((d, tn), lambda i, j: (0, 0), pipeline_mode=pl.Buffered(1)),
        ],
        out_specs=pl.BlockSpec((tm, tn), lambda i, j: (i, j)),
        out_shape=jax.ShapeDtypeStruct((m, (n_main + 1) * tn), out_dtype),
        name="proj",
        compiler_params=pltpu.CompilerParams(
            dimension_semantics=("parallel", "arbitrary"), vmem_limit_bytes=VMEM_LIMIT_BYTES),
    )(h2d, w_in, w_lora)


def _rmsnorm(x, gain):
    return x * lax.rsqrt(jnp.mean(x * x, axis=-1, keepdims=True) + RMS_EPS) * gain


def _norm_kernel(x_ref, g_ref, *rest, n_own):
    o_ref = rest[-1]

    @pl.when(pl.program_id(0) < n_own)
    def _():
        o_ref[...] = _rmsnorm(x_ref[...], g_ref[...]).astype(o_ref.dtype)

    @pl.when(pl.program_id(0) >= n_own)
    def _():
        o_ref[...] = jnp.zeros(o_ref.shape, o_ref.dtype)


class _SharedRows:
    def __init__(self, m, rows, row0, total_rows, d, dtype, into):
        assert m % rows == 0 and row0 % rows == 0 and total_rows % rows == 0
        assert into is not None or row0 == 0
        self.n_own = m // rows
        self.grid = (total_rows // rows,) if into is None else (self.n_own,)
        blk0 = row0 // rows
        self.out_spec = pl.BlockSpec((rows, d), lambda i: (i + blk0, 0))
        self.out_shape = jax.ShapeDtypeStruct((total_rows, d), dtype)
        self.extra_specs = [] if into is None else [pl.BlockSpec(memory_space=pl.ANY)]
        self.extra_args = [] if into is None else [into]
        self.aliased = into is not None

    def own(self, i):
        return jnp.minimum(i, self.n_own - 1)


def _norm(x2d, gain, *, rows, out_dtype, row0=0, total_rows=None, into=None):
    m, d = x2d.shape
    rows = min(rows, m)
    sh = _SharedRows(m, rows, row0, total_rows or m, d, out_dtype, into)
    return pl.pallas_call(
        functools.partial(_norm_kernel, n_own=sh.n_own),
        grid=sh.grid,
        in_specs=[pl.BlockSpec((rows, d), lambda i: (sh.own(i), 0)),
                  pl.BlockSpec((1, d), lambda i: (0, 0))] + sh.extra_specs,
        out_specs=sh.out_spec, out_shape=sh.out_shape,
        input_output_aliases={2: 0} if sh.aliased else {},
        name="norm",
        compiler_params=pltpu.CompilerParams(dimension_semantics=("arbitrary",)),
    )(x2d, gain.reshape(1, d), *sh.extra_args)


_V_MU_R, _V_MU_K, _V_MU_V, _V_W0, _V_A0, _V_V0, _V_KK, _V_KA, _V_RK, _V_LNW, _V_LNB = range(11)
_N_VEC_ROWS = 16


def _wkv_constants(n_seq, t_seq):
    c, e = CHUNK, EXPANDED
    assert n_seq * t_seq == c
    row = np.arange(c)
    q_of = row // t_seq
    tri = (row[:, None] >= row[None, :]) & (q_of[:, None] == q_of[None, :])
    erow = np.arange(e)
    es = erow % c
    same = q_of[:, None] == (es // t_seq)[None, :]
    mask_s = same & (row[:, None] > es[None, :])
    mask_i = same & (row[:, None] >= es[None, :])
    mask_sc = np.block([[mask_s, mask_s], [mask_i, mask_i]])
    lane = np.arange(GROUP_LANES)
    bones = (lane[:, None] // HEAD_DIM) == (lane[None, :] // HEAD_DIM)
    levels = [np.eye(e, dtype=bool), (erow[:, None] // 2) == (erow[None, :] // 2)]
    m = 2
    while m < t_seq:
        levels.append(((erow[:, None] // (2 * m)) == (erow[None, :] // (2 * m)))
                      & ((erow[:, None] % (2 * m)) >= m) & ((erow[None, :] % (2 * m)) < m))
        m *= 2
    col = np.arange(2 * c)
    qmask = ((col[None, :] % c) // t_seq) == np.arange(n_seq)[:, None]
    f = lambda a: jnp.asarray(a.astype(np.float32))
    return dict(tri=f(tri).astype(BF16), mask_sc=f(mask_sc).astype(BF16), bones=f(bones),
                bones_b=f(bones).astype(BF16), levels=f(np.stack(levels)).astype(BF16),
                qmask=f(qmask).reshape(n_seq, 1, 2 * c))


def _wkv_kernel(*refs, n_seq, t_seq, n_sub, chain, n_steps, has_vres, n_levels):
    it = iter(refs)
    zr_ref, zk_ref, zv_ref, zg_ref, zl_ref, p0_ref = (next(it) for _ in range(6))
    vfirst_ref = next(it) if has_vres else None
    w2_ref, vec_ref, s0_ref = next(it), next(it), next(it)
    tri_ref, msc_ref, bones_ref, bonesb_ref, lev_ref, qmask_ref = (next(it) for _ in range(6))
    ya_ref = next(it)
    vout_ref = None if has_vres else next(it)
    sout_ref = next(it)
    sbd_ref, prev_ref = next(it), next(it)

    step = pl.program_id(1)
    gl, rw, e = GROUP_LANES, RWKV_WIDTH, EXPANDED
    n_slots = n_seq if chain else n_sub * n_seq
    slot = (lambda j, q: q) if chain else (lambda j, q: j * n_seq + q)
    bones = bones_ref[...]
    bones_b = bonesb_ref[...]
    mask_sc = msc_ref[...]
    eye_b = lev_ref[0]
    tri = tri_ref[...]
    vec = vec_ref[...]
    vrow = lambda i: vec[i:i + 1, :]
    groups = range(N_GROUPS)
    sls = [slice(gi * gl, (gi + 1) * gl) for gi in groups]

    def expand(x):
        return jnp.concatenate([x] * HEADS_PER_GROUP, axis=0) * bones_b

    low_lanes = lax.broadcasted_iota(jnp.int32, (1, 2 * HEAD_DIM), 1) < HEAD_DIM

    def head_slot(hd):
        h = hd % HEADS_PER_GROUP
        return hd // HEADS_PER_GROUP, h * HEAD_DIM, h // 2, h % 2 == 1

    def load_state(s, g):
        return jnp.concatenate([sbd_ref[s, g, 0], sbd_ref[s, g, 1]], axis=1)

    def store_state(s, g, val):
        sbd_ref[s, g, 0] = val[:, :2 * HEAD_DIM]
        sbd_ref[s, g, 1] = val[:, 2 * HEAD_DIM:]

    @pl.when(step == 0)
    def _():
        for s in range(n_slots):
            for hd in range(N_HEADS):
                g, row0, half, upper = head_slot(hd)
                packed = s0_ref[s, hd]
                even = jnp.where(low_lanes, packed, 0.0)
                odd = jnp.where(low_lanes, 0.0, packed)
                if upper:
                    even = pltpu.roll(even, HEAD_DIM, axis=1)
                else:
                    odd = pltpu.roll(odd, HEAD_DIM, axis=1)
                sbd_ref[s, g, half, pl.ds(row0, HEAD_DIM // 2, stride=2), :] = even
                sbd_ref[s, g, half, pl.ds(row0 + 1, HEAD_DIM // 2, stride=2), :] = odd
                sbd_ref[s, g, 1 - half, row0:row0 + HEAD_DIM, :] = jnp.zeros((HEAD_DIM, 2 * HEAD_DIM), F32)
        prev_ref[...] = p0_ref[0]

    row = lax.broadcasted_iota(jnp.int32, (CHUNK, 1), 0)
    is_first = (row % t_seq) == 0
    lora_lane = lax.broadcasted_iota(jnp.int32, (1, LORA_COLS), 1)

    def shifted(z, p0):
        rolled = pltpu.roll(z, 1, axis=0)
        if n_seq == 1:
            first = jnp.broadcast_to(p0, z.shape)
        else:
            first = jnp.broadcast_to(p0[:, None, :], (n_seq, t_seq, z.shape[1])).reshape(z.shape)
        return jnp.where(is_first, first, rolled)

    n_rows = n_sub * CHUNK
    zl_all = zl_ref[...].astype(F32)
    lb_all = zl_all[:, LORA_COLS:]
    p_l = prev_ref[:, LB_OFF:LB_OFF + LORA_COLS]
    if n_slots == 1:
        lb_first = jnp.broadcast_to(p_l, lb_all.shape)
    else:
        lb_first = jnp.broadcast_to(p_l[:, None, :], (n_slots, t_seq, LORA_COLS)).reshape(lb_all.shape)
    row_all = lax.broadcasted_iota(jnp.int32, (n_rows, 1), 0)
    lb_prev = jnp.where(row_all % (n_rows if chain else t_seq) == 0, lb_first, pltpu.roll(lb_all, 1, axis=0))
    lin = zl_all[:, :LORA_COLS] + lb_prev
    lin = jnp.where(lora_lane < DECAY_RANK, jnp.tanh(lin), lin).astype(BF16)
    d_full = jnp.dot(lin, w2_ref[...], preferred_element_type=F32)
    lb_last = lb_all[n_rows - 1:n_rows, :]

    ctx = [dict() for _ in range(n_sub)]

    def prep(j):
        c = ctx[j]
        rows = slice(j * CHUNK, (j + 1) * CHUNK)
        zr, zk, zv = zr_ref[rows, :].astype(F32), zk_ref[rows, :].astype(F32), zv_ref[rows, :].astype(F32)
        c["last"] = [x[CHUNK - 1:CHUNK, :] for x in (zr, zk, zv)]
        if chain and j > 0:
            p_r, p_k, p_v = ctx[j - 1]["last"]
        else:
            ps = slice(0, n_seq) if chain else slice(j * n_seq, (j + 1) * n_seq)
            p_r, p_k, p_v = prev_ref[ps, 0:rw], prev_ref[ps, rw:2 * rw], prev_ref[ps, 2 * rw:3 * rw]
        r = zr + (shifted(zr, p_r) - zr) * vrow(_V_MU_R)
        k = zk + (shifted(zk, p_k) - zk) * vrow(_V_MU_K)
        v = zv + (shifted(zv, p_v) - zv) * vrow(_V_MU_V)
        d_all = d_full[rows]
        wlog = -DECAY_SCALE * _sigmoid(vrow(_V_W0) + d_all[:, 0:rw])
        alr = _sigmoid(vrow(_V_A0) + d_all[:, rw:2 * rw])
        if has_vres:
            v = v + (vfirst_ref[rows, :] - v) * _sigmoid(vrow(_V_V0) + d_all[:, 2 * rw:3 * rw])
        else:
            vout_ref[rows, :] = v
        kkr = k * vrow(_V_KK)
        kf = k * (1.0 + (alr - 1.0) * vrow(_V_KA))
        c["rkr"] = r * kf * vrow(_V_RK)
        g_all = zg_ref[rows, :].astype(F32)
        c["gate"] = g_all * _sigmoid(g_all)
        w_hi, w_lo = _split_bf16(wlog)
        cum = jnp.dot(tri, w_hi, preferred_element_type=F32) + jnp.dot(tri, w_lo, preferred_element_type=F32)
        p_inc = jnp.exp(cum)
        inv_p = jnp.exp(-cum)
        p_exc = jnp.exp(cum - wlog)
        c["p_inc"] = p_inc
        yield
        kk_n2 = _segsum([kkr[:, sl] * kkr[:, sl] for sl in sls], bones_b)
        kk = [kkr[:, sl] * lax.rsqrt(jnp.maximum(n2, 1e-24)) for sl, n2 in zip(sls, kk_n2)]
        c["v_g"] = [v[:, sl] for sl in sls]
        c["a_f"] = [-kk[gi] * p_exc[:, sls[gi]] for gi in groups]
        c["r_f"] = [r[:, sl] * p_inc[:, sl] for sl in sls]
        a_t = [x.astype(BF16) for x in c["a_f"]]
        r_t = [x.astype(BF16) for x in c["r_f"]]
        c["b_t"] = [(kk[gi] * alr[:, sls[gi]] * inv_p[:, sls[gi]]).astype(BF16) for gi in groups]
        c["k_t"] = [(kf[:, sl] * inv_p[:, sl]).astype(BF16) for sl in sls]
        c["v_e"] = [expand(x.astype(BF16)) for x in c["v_g"]]
        c["ar"] = [jnp.concatenate([a_t[gi], r_t[gi]], axis=0) for gi in groups]
        yield
        sc = [_dot(c["ar"][gi], jnp.concatenate([expand(c["b_t"][gi]), expand(c["k_t"][gi])], axis=0), _NT)
              for gi in groups]
        c["scm"] = [x.astype(BF16) * mask_sc for x in sc]
        c["n_bd"] = [expand(x[:CHUNK, :e]) for x in c["scm"]]
        c["xs"] = [eye_b + n * lev_ref[1] for n in c["n_bd"]]
        yield
        c["akv"] = [_dot(c["scm"][gi][:, e:], c["v_e"][gi]) for gi in groups]
        yield

    def inverse(j):
        c = ctx[j]
        m = 2
        xs32 = None
        for lvl in range(2, n_levels):
            n_c = [c["n_bd"][gi] * lev_ref[lvl] for gi in groups]
            if m % F32_ROWS == 0:
                wide = m % BF16_ROWS != 0
                src = xs32 if wide else c["xs"]
                n_blk = e // (2 * m)
                lower = lambda x: jnp.concatenate([x[b * 2 * m + m:(b + 1) * 2 * m] for b in range(n_blk)], axis=0)
                eye_l = lower(eye_b.astype(src[0].dtype)).astype(BF16)
                t1 = [_dot(lower(src[gi]).astype(BF16), n_c[gi]).astype(BF16) + eye_l for gi in groups]
                yield
                new = [_dot(t1[gi], c["xs"][gi]).astype(src[0].dtype) for gi in groups]
                c["xs"] = [jnp.concatenate(
                    [piece for b in range(n_blk)
                     for piece in (x[b * 2 * m:b * 2 * m + m], n[b * m:(b + 1) * m])], axis=0).astype(BF16)
                    for x, n in zip(src, new)]
                xs32 = None
            else:
                t1 = [_dot(c["xs"][gi], n_c[gi]).astype(BF16) + eye_b for gi in groups]
                yield
                full = [_dot(t1[gi], c["xs"][gi]) for gi in groups]
                if (2 * m) % F32_ROWS == 0 and (2 * m) % BF16_ROWS != 0:
                    xs32 = full
                c["xs"] = [x.astype(BF16) for x in full]
            yield
            m *= 2

    def tail(j):
        c = ctx[j]
        rows = slice(j * CHUNK, (j + 1) * CHUNK)
        x_c = [(x[0:CHUNK] + x[CHUNK:2 * CHUNK]) + (x[2 * CHUNK:3 * CHUNK] + x[3 * CHUNK:4 * CHUNK])
               for x in c["xs"]]
        a_s, r_s = [], []
        for gi in groups:
            parts_a, parts_r = [], []
            for q in range(n_seq):
                lo_, hi_ = q * t_seq, (q + 1) * t_seq
                lhs = c["ar"][gi] if n_seq == 1 else jnp.concatenate(
                    [c["a_f"][gi][lo_:hi_], c["r_f"][gi][lo_:hi_]], axis=0).astype(BF16)
                res = _dot(lhs, load_state(slot(j, q), gi).astype(BF16), _NT)
                parts_a.append(res[:t_seq])
                parts_r.append(res[t_seq:])
            a_s.append(parts_a[0] if n_seq == 1 else jnp.concatenate(parts_a, axis=0))
            r_s.append(parts_r[0] if n_seq == 1 else jnp.concatenate(parts_r, axis=0))
        yield
        y = [a_s[gi] + c["akv"][gi][:CHUNK] for gi in groups]
        u = [_dot(x_c[gi], expand(y[gi].astype(BF16))) for gi in groups]
        yield
        o = [r_s[gi] + c["akv"][gi][CHUNK:] + _dot(c["scm"][gi][CHUNK:, :e], expand(u[gi].astype(BF16)))
             for gi in groups]
        yield
        for gi in groups:
            uv_t = jnp.concatenate([u[gi], c["v_g"][gi]], axis=0).T
            bk = jnp.concatenate([c["b_t"][gi], c["k_t"][gi]], axis=0)
            for q in range(n_seq):
                lhs = uv_t if n_seq == 1 else uv_t * qmask_ref[q]
                p_last = c["p_inc"][(q + 1) * t_seq - 1:(q + 1) * t_seq, sls[gi]]
                s = slot(j, q)
                store_state(s, gi, bones * ((load_state(s, gi) + _dot(lhs.astype(BF16), bk)) * p_last))
        yield
        mean = [x * (1.0 / HEAD_DIM) for x in _segsum(o, bones_b)]
        d = [o[gi] - mean[gi] for gi in groups]
        var = [x * (1.0 / HEAD_DIM) for x in _segsum([x * x for x in d], bones_b)]
        bonus = _segsum([c["rkr"][:, sl] for sl in sls], bones_b)
        yield
        for gi in groups:
            sl = sls[gi]
            on = d[gi] * lax.rsqrt(var[gi] + GN_EPS) * vec[_V_LNW:_V_LNW + 1, sl] + vec[_V_LNB:_V_LNB + 1, sl]
            ya_ref[rows, sl] = ((on + bonus[gi] * c["v_g"][gi]) * c["gate"][:, sl]).astype(ya_ref.dtype)

    preps, invs, tails = ([f(j) for j in range(n_sub)] for f in (prep, inverse, tail))
    for phase in range(n_sub + 2):
        active = []
        if 0 <= phase - 2 < n_sub:
            active.append(tails[phase - 2])
        if 0 <= phase - 1 < n_sub:
            active.append(invs[phase - 1])
        if phase < n_sub:
            active.append(preps[phase])
        while active:
            for gen in list(active):
                try:
                    next(gen)
                except StopIteration:
                    active.remove(gen)

    if chain and n_steps > 1:
        assert n_seq == 1
        last_r, last_k, last_v = ctx[n_sub - 1]["last"]
        prev_ref[:, 0:rw] = last_r
        prev_ref[:, rw:2 * rw] = last_k
        prev_ref[:, 2 * rw:3 * rw] = last_v
        prev_ref[:, LB_OFF:LB_OFF + LORA_COLS] = lb_last

    @pl.when(step == n_steps - 1)
    def _():
        for s in range(n_slots):
            for hd in range(N_HEADS):
                g, row0, half, upper = head_slot(hd)
                even = sbd_ref[s, g, half, pl.ds(row0, HEAD_DIM // 2, stride=2), :]
                odd = sbd_ref[s, g, half, pl.ds(row0 + 1, HEAD_DIM // 2, stride=2), :]
                if upper:
                    even = pltpu.roll(even, HEAD_DIM, axis=1)
                else:
                    odd = pltpu.roll(odd, HEAD_DIM, axis=1)
                sout_ref[s, hd] = jnp.where(low_lanes, even, odd)


def _wkv(z, prev0, vfirst, w2e, vecs, s_all, layer, *, n_batch, t_len, n_seq, n_sub, row0):
    m = n_batch * t_len
    t_seq = CHUNK // n_seq
    n_chunks = t_len // t_seq
    chain = n_seq == 1
    if chain:
        assert n_chunks % n_sub == 0
        n_steps, nb, n_slots = n_chunks // n_sub, n_batch, n_seq
    else:
        assert n_chunks == 1 and n_batch % (n_seq * n_sub) == 0
        n_steps, nb, n_slots = 1, n_batch // (n_seq * n_sub), n_seq * n_sub
    has_vres = vfirst is not None
    consts = _wkv_constants(n_seq, t_seq)
    n_levels = consts["levels"].shape[0]
    rw = RWKV_WIDTH
    rows = n_sub * CHUNK
    rb = lambda b, s: b * n_steps + s
    prev0 = prev0.reshape(nb, n_slots, PREV_COLS)

    assert row0 % rows == 0
    z_off = row0 // rows
    zspec = lambda col: pl.BlockSpec((rows, rw), lambda b, s: (rb(b, s) + z_off, col))
    row_spec = pl.BlockSpec((rows, rw), lambda b, s: (rb(b, s), 0))
    full = lambda a: pl.BlockSpec(a.shape, lambda b, s: (0,) * a.ndim)
    sspec = pl.BlockSpec((None, n_slots, N_HEADS, HEAD_DIM // 2, 2 * HEAD_DIM), lambda b, s: (layer, b, 0, 0, 0))

    in_specs = [zspec(Z_R), zspec(Z_K), zspec(Z_V), zspec(Z_G_RWKV),
                pl.BlockSpec((rows, PROJ_TN), lambda b, s: (rb(b, s) + z_off, Z_LORA)),
                pl.BlockSpec((1, n_slots, PREV_COLS), lambda b, s: (b, 0, 0))]
    args = [z, z, z, z, z, prev0]
    if has_vres:
        in_specs.append(row_spec)
        args.append(vfirst)
    else:
        w2e = w2e[:, :2 * rw]
    in_specs += [full(w2e), full(vecs), sspec]
    args += [w2e, vecs, s_all]
    state_arg = len(args) - 1
    for name in ("tri", "mask_sc", "bones", "bones_b", "levels", "qmask"):
        in_specs.append(full(consts[name]))
        args.append(consts[name])

    out_specs = [row_spec]
    out_shape = [jax.ShapeDtypeStruct((m, rw), BF16)]
    if not has_vres:
        out_specs.append(row_spec)
        out_shape.append(jax.ShapeDtypeStruct((m, rw), F32))
    out_specs.append(sspec)
    out_shape.append(jax.ShapeDtypeStruct(s_all.shape, F32))

    outs = pl.pallas_call(
        functools.partial(_wkv_kernel, n_seq=n_seq, t_seq=t_seq, n_sub=n_sub, chain=chain, n_steps=n_steps,
                          has_vres=has_vres, n_levels=n_levels),
        grid=(nb, n_steps),
        in_specs=in_specs, out_specs=out_specs, out_shape=out_shape,
        input_output_aliases={state_arg: len(out_shape) - 1},
        scratch_shapes=[pltpu.VMEM((n_slots, N_GROUPS, 2, GROUP_LANES, GROUP_LANES // 2), F32),
                        pltpu.VMEM((n_slots, PREV_COLS), F32)],
        name="wkv",
        compiler_params=pltpu.CompilerParams(
            dimension_semantics=("parallel", "arbitrary"), vmem_limit_bytes=VMEM_LIMIT_BYTES),
    )(*args)
    if has_vres:
        ya, s_new = outs
        return ya, vfirst, s_new
    return outs


def _mix_kernel(*refs, n_seq, t_seq, tiles_per_seq, n_past, carry_pool):
    if carry_pool:
        (ya_ref, u_ref, gp_ref, ga_ref, gb_ref, pool_ref, perm_ref, permt_ref, pw_ref, ps_ref, wb0_ref, wb1_ref,
         m_ref, pool_out_ref) = refs
    else:
        ya_ref, u_ref, gp_ref, ga_ref, gb_ref, halo_ref, pw_ref, ps_ref, wb0_ref, wb1_ref, m_ref = refs
    rows = n_seq * t_seq
    tile = pl.program_id(0) % tiles_per_seq
    if carry_pool:
        perm = perm_ref[...]
        u_t = jnp.dot(perm, u_ref[...], preferred_element_type=F32).reshape(t_seq, n_seq, POOL_WIDTH)
        gp_all = jnp.dot(perm, gp_ref[...], preferred_element_type=F32)
        e = jnp.concatenate([jnp.zeros((HALO - POOL_BUF, n_seq, POOL_WIDTH), F32), pool_ref[...], u_t], axis=0)
        pool_out_ref[...] = e[HALO + t_seq - POOL_BUF:]
        t_axis = 0
        pos = lax.broadcasted_iota(jnp.int32, (t_seq, 1, 1), 0) + (1 + n_past)
    else:
        u = u_ref[...].astype(F32)
        halo = jnp.where(tile == 0, 0.0, halo_ref[...].astype(F32)).reshape(1, HALO, POOL_WIDTH)
        e = jnp.concatenate([halo, u.reshape(n_seq, t_seq, POOL_WIDTH)], axis=1)
        t_axis = 1
        pos = lax.broadcasted_iota(jnp.int32, (1, t_seq, 1), 1) + (tile * t_seq + 1 + n_past)
    take = lambda x, start, size: lax.slice_in_dim(x, start, start + size, axis=t_axis)
    n_col = len(POOL_WINDOWS)
    cw = D_MODEL // n_col
    cols = [slice(ci * cw, (ci + 1) * cw) for ci in range(n_col)]
    ya = ya_ref[...]
    gated_a, yb_parts = [], []
    for gi, win in enumerate(POOL_WINDOWS):
        pa = jnp.dot(ya, wb0_ref[:, cols[gi]], preferred_element_type=F32)
        sl = slice(gi * POOL_GROUP, (gi + 1) * POOL_GROUP)
        e_g = e[:, :, sl]
        acc = e_g
        span, length = 1, HALO + t_seq
        while span < win:
            length -= span
            acc = take(acc, span, length) + take(acc, 0, length)
            span *= 2
        ws = take(acc, length - t_seq, t_seq)
        inv_cnt = 1.0 / jnp.minimum(pos, win).astype(F32)
        mixed = (ws * inv_cnt - take(e_g, HALO, t_seq)).reshape(rows, POOL_GROUP)
        y_g = jnp.dot(mixed.astype(BF16), pw_ref[gi], preferred_element_type=F32)
        gp = gp_all[:, sl] if carry_pool else gp_ref[:, sl].astype(F32)
        yb_parts.append((y_g * ps_ref[:, sl] * (gp * _sigmoid(gp))).astype(BF16))
        gated_a.append(_sigmoid(ga_ref[:, cols[gi]].astype(F32)) * pa)
    y_b = jnp.concatenate(yb_parts, axis=1)
    if carry_pool:
        y_b = jnp.dot(permt_ref[...], y_b, preferred_element_type=F32).astype(BF16)
    pb = [None] * n_col
    pb[0] = jnp.dot(y_b, wb1_ref[:, cols[0]], preferred_element_type=F32)
    for ci in range(n_col):
        if ci + 1 < n_col:
            pb[ci + 1] = jnp.dot(y_b, wb1_ref[:, cols[ci + 1]], preferred_element_type=F32)
        m_ref[:, cols[ci]] = (gated_a[ci] + _sigmoid(gb_ref[:, cols[ci]].astype(F32)) * pb[ci]).astype(m_ref.dtype)


def _mix(ya, z, pool_all, pool_w_b, pool_scale, w_branch_b, layer, *, n_seq, t_seq, tiles_per_seq, n_past, row0):
    m = ya.shape[0]
    rows = n_seq * t_seq
    assert row0 % rows == 0
    z_off = row0 // rows
    carry_pool = pool_all is not None
    in_specs = [
        pl.BlockSpec((rows, RWKV_WIDTH), lambda i: (i, 0)),
        pl.BlockSpec((rows, POOL_WIDTH), lambda i: (i + z_off, Z_U)),
        pl.BlockSpec((rows, POOL_WIDTH), lambda i: (i + z_off, Z_G_POOL)),
        pl.BlockSpec((rows, D_MODEL), lambda i: (i + z_off, Z_GATE_A)),
        pl.BlockSpec((rows, D_MODEL), lambda i: (i + z_off, Z_GATE_B)),
    ]
    args = [ya, z, z, z, z]
    out_specs = [pl.BlockSpec((rows, D_MODEL), lambda i: (i, 0))]
    out_shape = [jax.ShapeDtypeStruct((m, D_MODEL), BF16)]
    aliases = {}
    if carry_pool:
        assert tiles_per_seq == 1 and t_seq < POOL_BUF
        r = np.arange(rows)
        perm = np.zeros((rows, rows), np.float32)
        perm[(r % t_seq) * n_seq + r // t_seq, r] = 1.0
        pool_spec = pl.BlockSpec((None, POOL_BUF, n_seq, POOL_WIDTH), lambda i: (layer, 0, i, 0))
        const = pl.BlockSpec((rows, rows), lambda i: (0, 0))
        in_specs += [pool_spec, const, const]
        aliases = {len(args): 1}
        args += [pool_all, jnp.asarray(perm, BF16), jnp.asarray(perm.T, BF16)]
        out_specs.append(pool_spec)
        out_shape.append(jax.ShapeDtypeStruct(pool_all.shape, F32))
    else:
        assert n_seq == 1 and rows % HALO == 0 and row0 == 0
        in_specs.append(pl.BlockSpec((HALO, POOL_WIDTH), lambda i: (jnp.maximum(i * (rows // HALO) - 1, 0), Z_U)))
        args.append(z)
    in_specs += [
        pl.BlockSpec((None, len(POOL_WINDOWS), POOL_GROUP, POOL_GROUP), lambda i: (layer, 0, 0, 0)),
        pl.BlockSpec((None, 1, POOL_WIDTH), lambda i: (layer, 0, 0)),
        pl.BlockSpec((None, None, RWKV_WIDTH, D_MODEL), lambda i: (layer, 0, 0, 0), pipeline_mode=pl.Buffered(1)),
        pl.BlockSpec((None, None, POOL_WIDTH, D_MODEL), lambda i: (layer, 1, 0, 0), pipeline_mode=pl.Buffered(1)),
    ]
    args += [pool_w_b, pool_scale.reshape(DEPTH, 1, POOL_WIDTH), w_branch_b, w_branch_b]
    outs = pl.pallas_call(
        functools.partial(_mix_kernel, n_seq=n_seq, t_seq=t_seq, tiles_per_seq=tiles_per_seq, n_past=n_past,
                          carry_pool=carry_pool),
        grid=(m // rows,),
        in_specs=in_specs, out_specs=out_specs, out_shape=out_shape,
        input_output_aliases=aliases,
        name="mix",
        compiler_params=pltpu.CompilerParams(
            dimension_semantics=("parallel",), vmem_limit_bytes=VMEM_LIMIT_BYTES),
    )(*args)
    return outs if carry_pool else (outs[0], None)


def _out_kernel(*refs, n_split, emit_h, n_own):
    if emit_h:
        (m_ref, x_ref, p_ref, g_ref, gn_ref, wo_ref, wg_ref, wp_ref), (o_ref, h_ref) = refs[:8], refs[-2:]
    else:
        m_ref, x_ref, p_ref, g_ref, wo_ref, wg_ref, wp_ref, o_ref = refs

    @pl.when(pl.program_id(0) < n_own)
    def _():
        rows = m_ref.shape[0] // n_split
        parts = [slice(i * rows, (i + 1) * rows) for i in range(n_split)]
        outs = [jnp.dot(m_ref[sl, :], wo_ref[...], preferred_element_type=F32) for sl in parts]
        ples = [jnp.dot(p_ref[sl, :].astype(BF16), wp_ref[...], preferred_element_type=F32) for sl in parts]
        xs = [x_ref[sl, :] + _rmsnorm(o, g_ref[...]) for sl, o in zip(parts, outs)]
        gates = [_sigmoid(jnp.dot(x.astype(BF16), wg_ref[...], preferred_element_type=F32)) for x in xs]
        for sl, x, gate, ple in zip(parts, xs, gates, ples):
            x_new = x + gate * ple
            o_ref[sl, :] = x_new
            if emit_h:
                h_ref[sl, :] = _rmsnorm(x_new, gn_ref[...]).astype(h_ref.dtype)

    if emit_h:
        @pl.when(pl.program_id(0) >= n_own)
        def _():
            h_ref[...] = jnp.zeros(h_ref.shape, h_ref.dtype)


def _out(mm_, x2d, p3d, norm_post, norm_pre, w_out_b, w_gate_b, w_ple_b, layer, *, rows, row0, total_rows,
         h_into):
    m = x2d.shape[0]
    emit_h = layer + 1 < DEPTH
    sh = _SharedRows(m, rows, row0, total_rows, D_MODEL, BF16, h_into) if emit_h else _SharedRows(
        m, rows, 0, m, D_MODEL, BF16, None)
    once = pl.Buffered(1)
    wspec = lambda k, l=layer: pl.BlockSpec((None, k, D_MODEL), lambda i: (l, 0, 0), pipeline_mode=once)
    row_spec = pl.BlockSpec((rows, D_MODEL), lambda i: (sh.own(i), 0))
    in_specs = [row_spec, row_spec, pl.BlockSpec((None, rows, PLE_DIM), lambda i: (layer, sh.own(i), 0)), wspec(1)]
    args = [mm_, x2d, p3d, norm_post.reshape(DEPTH, 1, D_MODEL)]
    out_specs, out_shape = [row_spec], [jax.ShapeDtypeStruct((m, D_MODEL), F32)]
    if emit_h:
        in_specs.append(wspec(1, layer + 1))
        args.append(norm_pre.reshape(DEPTH, 1, D_MODEL))
    in_specs += [wspec(D_MODEL), wspec(D_MODEL), wspec(PLE_DIM)]
    args += [w_out_b, w_gate_b, w_ple_b]
    aliases = {}
    if emit_h:
        out_specs.append(sh.out_spec)
        out_shape.append(sh.out_shape)
        if sh.aliased:
            aliases = {len(args): 1}
        in_specs += sh.extra_specs
        args += sh.extra_args
    outs = pl.pallas_call(
        functools.partial(_out_kernel, n_split=rows // OUT_SUB_ROWS, emit_h=emit_h, n_own=sh.n_own),
        grid=sh.grid,
        in_specs=in_specs, out_specs=out_specs, out_shape=out_shape,
        input_output_aliases=aliases,
        name="out",
        compiler_params=pltpu.CompilerParams(
            dimension_semantics=("arbitrary",), vmem_limit_bytes=VMEM_LIMIT_BYTES),
    )(*args)
    return (outs[0], outs[1]) if emit_h else (outs[0], None)


def _pair_rows(s):
    l, b = s.shape[:2]
    return s.reshape(l, b, N_HEADS, HEAD_DIM // 2, 2 * HEAD_DIM)


def _unpair_rows(s):
    l, b = s.shape[:2]
    return s.reshape(l, b, N_HEADS, HEAD_DIM, HEAD_DIM)


def _layer_weights(i, mu_rkv, mu_lora, w0, w1, w2, a0, a1, a2, v0, v1, v2, k_k, k_a, r_k, ln_w, ln_b):
    r_ = RWKV_WIDTH
    mu = mu_lora[i]
    if i > 0:
        v1_i, v2_i, v0_i = v1[i - 1], v2[i - 1], v0[i - 1]
    else:
        v1_i = jnp.zeros((D_MODEL, MV_RANK), F32)
        v2_i = jnp.zeros((MV_RANK, r_), F32)
        v0_i = jnp.zeros((r_,), F32)
    w_lora = jnp.concatenate(
        [(1.0 - mu[0])[:, None] * w1[i], (1.0 - mu[1])[:, None] * a1[i], (1.0 - mu[2])[:, None] * v1_i,
         mu[0][:, None] * w1[i], mu[1][:, None] * a1[i], mu[2][:, None] * v1_i], axis=1).astype(BF16)
    z96 = jnp.zeros((DECAY_RANK, r_), F32)
    z64 = jnp.zeros((MV_RANK, r_), F32)
    w2e = jnp.concatenate([
        jnp.concatenate([w2[i], z96, z64], axis=0),
        jnp.concatenate([z96, a2[i], z64], axis=0),
        jnp.concatenate([z96, z96, v2_i], axis=0)], axis=1).astype(BF16)
    rows = [mu_rkv[i, 0], mu_rkv[i, 1], mu_rkv[i, 2], w0[i], a0[i], v0_i, k_k[i], k_a[i], r_k[i], ln_w[i], ln_b[i]]
    vecs = jnp.concatenate([jnp.stack(rows), jnp.zeros((_N_VEC_ROWS - len(rows), r_), F32)], axis=0)
    return w_lora, w2e, vecs


class _Group:
    def __init__(self, x, p, wkv, shift, pool, *, n_past, row0, n_seq, n_sub, mix_seq, mix_t):
        self.b, self.t, d = x.shape
        self.m = self.b * self.t
        self.x2d = x.reshape(self.m, d)
        self.p3d = p.reshape(DEPTH, self.m, PLE_DIM)
        self.zero_state = wkv is None
        self.shift, self.n_past, self.row0 = shift, n_past, row0
        self.n_seq, self.n_sub, self.mix_seq, self.mix_t = n_seq, n_sub, mix_seq, mix_t
        self.s_all = _pair_rows(
            jnp.zeros((DEPTH, self.b, N_HEADS, HEAD_DIM, HEAD_DIM), F32) if self.zero_state else wkv)
        self.pool_all = None if self.zero_state else jnp.transpose(pool, (0, 2, 1, 3))
        self.v_first = None
        self.shift_out, self.pool_out = [], []

    def outputs(self):
        pool_new = jnp.stack(self.pool_out) if self.zero_state else jnp.transpose(self.pool_all, (0, 2, 1, 3))
        return (self.x2d.reshape(self.b, self.t, D_MODEL), _unpair_rows(self.s_all), jnp.stack(self.shift_out),
                pool_new)


def _run_layers(groups, lw, params):
    (w_in, norm_pre, norm_post, pool_w_b, pool_scale, w_branch_b, w_out_b, w_ple_b, w_gate_b) = params
    total_rows = sum(g.m for g in groups)
    assert total_rows % PROJ_TM == 0
    h_all = None
    for g in groups:
        h_all = _norm(g.x2d, norm_pre[0], rows=NORM_ROWS, out_dtype=BF16, row0=g.row0, total_rows=total_rows,
                      into=h_all)
    for i in range(DEPTH):
        w_lora, w2e, vecs = lw[i]
        z = _proj(h_all, w_in, i, w_lora, tm=PROJ_TM, n_main=N_MAIN_TILES, out_dtype=BF16)
        h_all = None
        for g in groups:
            g.shift_out.append(
                _norm(g.x2d.reshape(g.b, g.t, D_MODEL)[:, -1], norm_pre[i], rows=g.b, out_dtype=F32))
            if g.zero_state:
                assert g.t >= POOL_BUF
                prev0 = jnp.zeros((g.b, PREV_COLS), F32)
                g.pool_out.append(jnp.stack([
                    z[g.row0 + (q + 1) * g.t - POOL_BUF:g.row0 + (q + 1) * g.t, Z_U * POOL_WIDTH:(Z_U + 1) * POOL_WIDTH]
                    for q in range(g.b)]).astype(F32))
            else:
                prev0 = _proj(g.shift[i].astype(BF16), w_in, i, w_lora, tm=g.b, n_main=3 * RWKV_WIDTH // PROJ_TN,
                              out_dtype=F32)
            ya, g.v_first, g.s_all = _wkv(z, prev0, g.v_first, w2e, vecs, g.s_all, i, n_batch=g.b, t_len=g.t,
                                          n_seq=g.n_seq, n_sub=g.n_sub, row0=g.row0)
            mm_, g.pool_all = _mix(ya, z, g.pool_all, pool_w_b, pool_scale, w_branch_b, i, n_seq=g.mix_seq,
                                   t_seq=g.mix_t, tiles_per_seq=g.t // g.mix_t, n_past=g.n_past, row0=g.row0)
            g.x2d, h_all = _out(mm_, g.x2d, g.p3d, norm_post, norm_pre, w_out_b, w_gate_b, w_ple_b, i,
                                rows=OUT_ROWS, row0=g.row0, total_rows=total_rows, h_into=h_all)


def kernel(x_prompt, x_sample, state_wkv, state_shift, state_pool, p_prompt, p_sample, norm_pre, norm_post, w_in, mu_rkv, mu_lora, w0, w1, w2, a0, a1, a2, v0, v1, v2, k_k, k_a, r_k, ln_w, ln_b, pool_w, pool_scale, w_branch, w_out, w_ple, w_ple_gate):
    lw = [_layer_weights(i, mu_rkv, mu_lora, w0, w1, w2, a0, a1, a2, v0, v1, v2, k_k, k_a, r_k, ln_w, ln_b)
          for i in range(DEPTH)]
    params = (w_in, norm_pre, norm_post, pool_w.astype(BF16), pool_scale, w_branch.astype(BF16),
              w_out.astype(BF16), w_ple.astype(BF16), w_ple_gate.astype(BF16))
    prompt = _Group(x_prompt, p_prompt, None, None, None, n_past=0, row0=0,
                    n_seq=1, n_sub=8, mix_seq=1, mix_t=256)
    sample = _Group(x_sample, p_sample, state_wkv, state_shift, state_pool, n_past=PAST_LEN, row0=prompt.m,
                    n_seq=8, n_sub=2, mix_seq=32, mix_t=8)
    _run_layers([prompt, sample], lw, params)
    y_p, wkv_p, shift_p, pool_p = prompt.outputs()
    y_s, wkv_s, shift_s, pool_s = sample.outputs()
    return (y_p, y_s, wkv_p, shift_p, pool_p, wkv_s, shift_s, pool_s)
```

```python
import functools

import jax
import jax.numpy as jnp
import numpy as np
from jax import lax
from jax.experimental import pallas as pl
from jax.experimental.pallas import tpu as pltpu

F32 = jnp.float32
BF16 = jnp.bfloat16

D_MODEL = 2048
DEPTH = 4
PAST_LEN = 16384
RWKV_WIDTH = 1024
HEAD_DIM = 64
N_HEADS = 16
POOL_WIDTH = 1024
POOL_WINDOWS = (2, 4, 8, 16)
POOL_GROUP = 256
POOL_BUF = 15
PLE_DIM = 256
DECAY_RANK = 96
AAA_RANK = 96
MV_RANK = 64
LORA_COLS = DECAY_RANK + AAA_RANK + MV_RANK
IN_COLS = 10240
RMS_EPS = 1e-6
GN_EPS = 64e-5
DECAY_SCALE = 0.606531

HEADS_PER_GROUP = 4
GROUP_LANES = HEADS_PER_GROUP * HEAD_DIM
N_GROUPS = N_HEADS // HEADS_PER_GROUP
CHUNK = 64
EXPANDED = HEADS_PER_GROUP * CHUNK
F32_ROWS = 8
BF16_ROWS = 16
HALO = 16
PROJ_TN = 2 * LORA_COLS
N_MAIN_TILES = IN_COLS // PROJ_TN
Z_R, Z_K, Z_V, Z_G_RWKV, Z_U, Z_G_POOL = range(6)
Z_GATE_A, Z_GATE_B = 3, 4
Z_LORA = N_MAIN_TILES
PREV_COLS = 3 * RWKV_WIDTH + PROJ_TN
LB_OFF = 3 * RWKV_WIDTH + LORA_COLS

NORM_ROWS = 512
PROJ_TM = 3072
OUT_ROWS = 256
OUT_SUB_ROWS = 128

V7X_VMEM_BYTES = 64 * 1024 * 1024
VMEM_LIMIT_BYTES = V7X_VMEM_BYTES * 7 // 8


def _sigmoid(x):
    return 0.5 * jnp.tanh(0.5 * x) + 0.5


def _split_bf16(x):
    hi = x.astype(BF16)
    lo = (x - hi.astype(F32)).astype(BF16)
    return hi, lo


_NN = (((1,), (0,)), ((), ()))
_NT = (((1,), (1,)), ((), ()))


def _dot(a, b, dims=_NN):
    return lax.dot_general(a, b, dimension_numbers=dims, preferred_element_type=F32)


def _segsum(xs, bones_b):
    rows = xs[0].shape[0]
    s = _dot(jnp.concatenate([x.astype(BF16) for x in xs], axis=0), bones_b)
    return [s[i * rows:(i + 1) * rows] for i in range(len(xs))]


def _proj_kernel(h_ref, w_ref, wl_ref, z_ref, *, n_main):
    j = pl.program_id(1)

    @pl.when(j < n_main)
    def _():
        z_ref[...] = jnp.dot(h_ref[...], w_ref[...].astype(BF16), preferred_element_type=F32).astype(z_ref.dtype)

    @pl.when(j == n_main)
    def _():
        z_ref[...] = jnp.dot(h_ref[...], wl_ref[...], preferred_element_type=F32).astype(z_ref.dtype)


def _proj(h2d, w_in, layer, w_lora, *, tm, n_main, out_dtype):
    m, d = h2d.shape
    tn = PROJ_TN
    assert m % tm == 0
    return pl.pallas_call(
        functools.partial(_proj_kernel, n_main=n_main),
        grid=(m // tm, n_main + 1),
        in_specs=[
            pl.BlockSpec((tm, d), lambda i, j: (i, 0)),
            pl.BlockSpec((None, d, tn), lambda i, j: (layer, 0, jnp.minimum(j, n_main - 1))),
            pl.BlockSpec((d, tn), lambda i, j: (0, 0), pipeline_mode=pl.Buffered(1)),
        ],
        out_specs=pl.BlockSpec((tm, tn), lambda i, j: (i, j)),
        out_shape=jax.ShapeDtypeStruct((m, (n_main + 1) * tn), out_dtype),
        name="proj",
        compiler_params=pltpu.CompilerParams(
            dimension_semantics=("parallel", "arbitrary"), vmem_limit_bytes=VMEM_LIMIT_BYTES),
    )(h2d, w_in, w_lora)


def _rmsnorm(x, gain):
    return x * lax.rsqrt(jnp.mean(x * x, axis=-1, keepdims=True) + RMS_EPS) * gain


def _norm_kernel(x_ref, g_ref, *rest, n_own):
    o_ref = rest[-1]

    @pl.when(pl.program_id(0) < n_own)
    def _():
        o_ref[...] = _rmsnorm(x_ref[...], g_ref[...]).astype(o_ref.dtype)

    @pl.when(pl.program_id(0) >= n_own)
    def _():
        o_ref[...] = jnp.zeros(o_ref.shape, o_ref.dtype)


class _SharedRows:
    def __init__(self, m, rows, row0, total_rows, d, dtype, into):
        assert m % rows == 0 and row0 % rows == 0 and total_rows % rows == 0
        assert into is not None or row0 == 0
        self.n_own = m // rows
        self.grid = (total_rows // rows,) if into is None else (self.n_own,)
        blk0 = row0 // rows
        self.out_spec = pl.BlockSpec((rows, d), lambda i: (i + blk0, 0))
        self.out_shape = jax.ShapeDtypeStruct((total_rows, d), dtype)
        self.extra_specs = [] if into is None else [pl.BlockSpec(memory_space=pl.ANY)]
        self.extra_args = [] if into is None else [into]
        self.aliased = into is not None

    def own(self, i):
        return jnp.minimum(i, self.n_own - 1)


def _norm(x2d, gain, *, rows, out_dtype, row0=0, total_rows=None, into=None):
    m, d = x2d.shape
    rows = min(rows, m)
    sh = _SharedRows(m, rows, row0, total_rows or m, d, out_dtype, into)
    return pl.pallas_call(
        functools.partial(_norm_kernel, n_own=sh.n_own),
        grid=sh.grid,
        in_specs=[pl.BlockSpec((rows, d), lambda i: (sh.own(i), 0)),
                  pl.BlockSpec((1, d), lambda i: (0, 0))] + sh.extra_specs,
        out_specs=sh.out_spec, out_shape=sh.out_shape,
        input_output_aliases={2: 0} if sh.aliased else {},
        name="norm",
        compiler_params=pltpu.CompilerParams(dimension_semantics=("arbitrary",)),
    )(x2d, gain.reshape(1, d), *sh.extra_args)


_V_MU_R, _V_MU_K, _V_MU_V, _V_W0, _V_A0, _V_V0, _V_KK, _V_KA, _V_RK, _V_LNW, _V_LNB = range(11)
_N_VEC_ROWS = 16


def _wkv_constants(n_seq, t_seq):
    c, e = CHUNK, EXPANDED
    assert n_seq * t_seq == c
    row = np.arange(c)
    q_of = row // t_seq
    tri = (row[:, None] >= row[None, :]) & (q_of[:, None] == q_of[None, :])
    erow = np.arange(e)
    es = erow % c
    same = q_of[:, None] == (es // t_seq)[None, :]
    mask_s = same & (row[:, None] > es[None, :])
    mask_i = same & (row[:, None] >= es[None, :])
    mask_sc = np.block([[mask_s, mask_s], [mask_i, mask_i]])
    lane = np.arange(GROUP_LANES)
    bones = (lane[:, None] // HEAD_DIM) == (lane[None, :] // HEAD_DIM)
    levels = [np.eye(e, dtype=bool), (erow[:, None] // 2) == (erow[None, :] // 2)]
    m = 2
    while m < t_seq:
        levels.append(((erow[:, None] // (2 * m)) == (erow[None, :] // (2 * m)))
                      & ((erow[:, None] % (2 * m)) >= m) & ((erow[None, :] % (2 * m)) < m))
        m *= 2
    col = np.arange(2 * c)
    qmask = ((col[None, :] % c) // t_seq) == np.arange(n_seq)[:, None]
    f = lambda a: jnp.asarray(a.astype(np.float32))
    return dict(tri=f(tri).astype(BF16), mask_sc=f(mask_sc).astype(BF16), bones=f(bones),
                bones_b=f(bones).astype(BF16), levels=f(np.stack(levels)).astype(BF16),
                qmask=f(qmask).reshape(n_seq, 1, 2 * c))


def _wkv_kernel(*refs, n_seq, t_seq, n_sub, chain, n_steps, has_vres, n_levels):
    it = iter(refs)
    zr_ref, zk_ref, zv_ref, zg_ref, zl_ref, p0_ref = (next(it) for _ in range(6))
    vfirst_ref = next(it) if has_vres else None
    w2_ref, vec_ref, s0_ref = next(it), next(it), next(it)
    tri_ref, msc_ref, bones_ref, bonesb_ref, lev_ref, qmask_ref = (next(it) for _ in range(6))
    ya_ref = next(it)
    vout_ref = None if has_vres else next(it)
    sout_ref = next(it)
    sbd_ref, prev_ref = next(it), next(it)

    step = pl.program_id(1)
    gl, rw, e = GROUP_LANES, RWKV_WIDTH, EXPANDED
    n_slots = n_seq if chain else n_sub * n_seq
    slot = (lambda j, q: q) if chain else (lambda j, q: j * n_seq + q)
    bones = bones_ref[...]
    bones_b = bonesb_ref[...]
    mask_sc = msc_ref[...]
    eye_b = lev_ref[0]
    tri = tri_ref[...]
    vec = vec_ref[...]
    vrow = lambda i: vec[i:i + 1, :]
    groups = range(N_GROUPS)
    sls = [slice(gi * gl, (gi + 1) * gl) for gi in groups]

    def expand(x):
        return jnp.concatenate([x] * HEADS_PER_GROUP, axis=0) * bones_b

    low_lanes = lax.broadcasted_iota(jnp.int32, (1, 2 * HEAD_DIM), 1) < HEAD_DIM

    def head_slot(hd):
        h = hd % HEADS_PER_GROUP
        return hd // HEADS_PER_GROUP, h * HEAD_DIM, h // 2, h % 2 == 1

    def load_state(s, g):
        return jnp.concatenate([sbd_ref[s, g, 0], sbd_ref[s, g, 1]], axis=1)

    def store_state(s, g, val):
        sbd_ref[s, g, 0] = val[:, :2 * HEAD_DIM]
        sbd_ref[s, g, 1] = val[:, 2 * HEAD_DIM:]

    @pl.when(step == 0)
    def _():
        for s in range(n_slots):
            for hd in range(N_HEADS):
                g, row0, half, upper = head_slot(hd)
                packed = s0_ref[s, hd]
                even = jnp.where(low_lanes, packed, 0.0)
                odd = jnp.where(low_lanes, 0.0, packed)
                if upper:
                    even = pltpu.roll(even, HEAD_DIM, axis=1)
                else:
                    odd = pltpu.roll(odd, HEAD_DIM, axis=1)
                sbd_ref[s, g, half, pl.ds(row0, HEAD_DIM // 2, stride=2), :] = even
                sbd_ref[s, g, half, pl.ds(row0 + 1, HEAD_DIM // 2, stride=2), :] = odd
                sbd_ref[s, g, 1 - half, row0:row0 + HEAD_DIM, :] = jnp.zeros((HEAD_DIM, 2 * HEAD_DIM), F32)
        prev_ref[...] = p0_ref[0]

    row = lax.broadcasted_iota(jnp.int32, (CHUNK, 1), 0)
    is_first = (row % t_seq) == 0
    lora_lane = lax.broadcasted_iota(jnp.int32, (1, LORA_COLS), 1)

    def shifted(z, p0):
        rolled = pltpu.roll(z, 1, axis=0)
        if n_seq == 1:
            first = jnp.broadcast_to(p0, z.shape)
        else:
            first = jnp.broadcast_to(p0[:, None, :], (n_seq, t_seq, z.shape[1])).reshape(z.shape)
        return jnp.where(is_first, first, rolled)

    n_rows = n_sub * CHUNK
    zl_all = zl_ref[...].astype(F32)
    lb_all = zl_all[:, LORA_COLS:]
    p_l = prev_ref[:, LB_OFF:LB_OFF + LORA_COLS]
    if n_slots == 1:
        lb_first = jnp.broadcast_to(p_l, lb_all.shape)
    else:
        lb_first = jnp.broadcast_to(p_l[:, None, :], (n_slots, t_seq, LORA_COLS)).reshape(lb_all.shape)
    row_all = lax.broadcasted_iota(jnp.int32, (n_rows, 1), 0)
    lb_prev = jnp.where(row_all % (n_rows if chain else t_seq) == 0, lb_first, pltpu.roll(lb_all, 1, axis=0))
    lin = zl_all[:, :LORA_COLS] + lb_prev
    lin = jnp.where(lora_lane < DECAY_RANK, jnp.tanh(lin), lin).astype(BF16)
    d_full = jnp.dot(lin, w2_ref[...], preferred_element_type=F32)
    lb_last = lb_all[n_rows - 1:n_rows, :]

    ctx = [dict() for _ in range(n_sub)]

    def prep(j):
        c = ctx[j]
        rows = slice(j * CHUNK, (j + 1) * CHUNK)
        zr, zk, zv = zr_ref[rows, :].astype(F32), zk_ref[rows, :].astype(F32), zv_ref[rows, :].astype(F32)
        c["last"] = [x[CHUNK - 1:CHUNK, :] for x in (zr, zk, zv)]
        if chain and j > 0:
            p_r, p_k, p_v = ctx[j - 1]["last"]
        else:
            ps = slice(0, n_seq) if chain else slice(j * n_seq, (j + 1) * n_seq)
            p_r, p_k, p_v = prev_ref[ps, 0:rw], prev_ref[ps, rw:2 * rw], prev_ref[ps, 2 * rw:3 * rw]
        r = zr + (shifted(zr, p_r) - zr) * vrow(_V_MU_R)
        k = zk + (shifted(zk, p_k) - zk) * vrow(_V_MU_K)
        v = zv + (shifted(zv, p_v) - zv) * vrow(_V_MU_V)
        d_all = d_full[rows]
        wlog = -DECAY_SCALE * _sigmoid(vrow(_V_W0) + d_all[:, 0:rw])
        alr = _sigmoid(vrow(_V_A0) + d_all[:, rw:2 * rw])
        if has_vres:
            v = v + (vfirst_ref[rows, :] - v) * _sigmoid(vrow(_V_V0) + d_all[:, 2 * rw:3 * rw])
        else:
            vout_ref[rows, :] = v
        kkr = k * vrow(_V_KK)
        kf = k * (1.0 + (alr - 1.0) * vrow(_V_KA))
        c["rkr"] = r * kf * vrow(_V_RK)
        g_all = zg_ref[rows, :].astype(F32)
        c["gate"] = g_all * _sigmoid(g_all)
        w_hi, w_lo = _split_bf16(wlog)
        cum = jnp.dot(tri, w_hi, preferred_element_type=F32) + jnp.dot(tri, w_lo, preferred_element_type=F32)
        p_inc = jnp.exp(cum)
        inv_p = jnp.exp(-cum)
        p_exc = jnp.exp(cum - wlog)
        c["p_inc"] = p_inc
        yield
        kk_n2 = _segsum([kkr[:, sl] * kkr[:, sl] for sl in sls], bones_b)
        kk = [kkr[:, sl] * lax.rsqrt(jnp.maximum(n2, 1e-24)) for sl, n2 in zip(sls, kk_n2)]
        c["v_g"] = [v[:, sl] for sl in sls]
        c["a_f"] = [-kk[gi] * p_exc[:, sls[gi]] for gi in groups]
        c["r_f"] = [r[:, sl] * p_inc[:, sl] for sl in sls]
        a_t = [x.astype(BF16) for x in c["a_f"]]
        r_t = [x.astype(BF16) for x in c["r_f"]]
        c["b_t"] = [(kk[gi] * alr[:, sls[gi]] * inv_p[:, sls[gi]]).astype(BF16) for gi in groups]
        c["k_t"] = [(kf[:, sl] * inv_p[:, sl]).astype(BF16) for sl in sls]
        c["v_e"] = [expand(x.astype(BF16)) for x in c["v_g"]]
        c["ar"] = [jnp.concatenate([a_t[gi], r_t[gi]], axis=0) for gi in groups]
        yield
        sc = [_dot(c["ar"][gi], jnp.concatenate([expand(c["b_t"][gi]), expand(c["k_t"][gi])], axis=0), _NT)
              for gi in groups]
        c["scm"] = [x.astype(BF16) * mask_sc for x in sc]
        c["n_bd"] = [expand(x[:CHUNK, :e]) for x in c["scm"]]
        c["xs"] = [eye_b + n * lev_ref[1] for n in c["n_bd"]]
        yield
        c["akv"] = [_dot(c["scm"][gi][:, e:], c["v_e"][gi]) for gi in groups]
        yield

    def inverse(j):
        c = ctx[j]
        m = 2
        xs32 = None
        for lvl in range(2, n_levels):
            n_c = [c["n_bd"][gi] * lev_ref[lvl] for gi in groups]
            if m % F32_ROWS == 0:
                wide = m % BF16_ROWS != 0
                src = xs32 if wide else c["xs"]
                n_blk = e // (2 * m)
                lower = lambda x: jnp.concatenate([x[b * 2 * m + m:(b + 1) * 2 * m] for b in range(n_blk)], axis=0)
                eye_l = lower(eye_b.astype(src[0].dtype)).astype(BF16)
                t1 = [_dot(lower(src[gi]).astype(BF16), n_c[gi]).astype(BF16) + eye_l for gi in groups]
                yield
                new = [_dot(t1[gi], c["xs"][gi]).astype(src[0].dtype) for gi in groups]
                c["xs"] = [jnp.concatenate(
                    [piece for b in range(n_blk)
                     for piece in (x[b * 2 * m:b * 2 * m + m], n[b * m:(b + 1) * m])], axis=0).astype(BF16)
                    for x, n in zip(src, new)]
                xs32 = None
            else:
                t1 = [_dot(c["xs"][gi], n_c[gi]).astype(BF16) + eye_b for gi in groups]
                yield
                full = [_dot(t1[gi], c["xs"][gi]) for gi in groups]
                if (2 * m) % F32_ROWS == 0 and (2 * m) % BF16_ROWS != 0:
                    xs32 = full
                c["xs"] = [x.astype(BF16) for x in full]
            yield
            m *= 2

    def tail(j):
        c = ctx[j]
        rows = slice(j * CHUNK, (j + 1) * CHUNK)
        x_c = [(x[0:CHUNK] + x[CHUNK:2 * CHUNK]) + (x[2 * CHUNK:3 * CHUNK] + x[3 * CHUNK:4 * CHUNK])
               for x in c["xs"]]
        a_s, r_s = [], []
        for gi in groups:
            parts_a, parts_r = [], []
            for q in range(n_seq):
                lo_, hi_ = q * t_seq, (q + 1) * t_seq
                lhs = c["ar"][gi] if n_seq == 1 else jnp.concatenate(
                    [c["a_f"][gi][lo_:hi_], c["r_f"][gi][lo_:hi_]], axis=0).astype(BF16)
                res = _dot(lhs, load_state(slot(j, q), gi).astype(BF16), _NT)
                parts_a.append(res[:t_seq])
                parts_r.append(res[t_seq:])
            a_s.append(parts_a[0] if n_seq == 1 else jnp.concatenate(parts_a, axis=0))
            r_s.append(parts_r[0] if n_seq == 1 else jnp.concatenate(parts_r, axis=0))
        yield
        y = [a_s[gi] + c["akv"][gi][:CHUNK] for gi in groups]
        u = [_dot(x_c[gi], expand(y[gi].astype(BF16))) for gi in groups]
        yield
        o = [r_s[gi] + c["akv"][gi][CHUNK:] + _dot(c["scm"][gi][CHUNK:, :e], expand(u[gi].astype(BF16)))
             for gi in groups]
        yield
        for gi in groups:
            uv_t = jnp.concatenate([u[gi], c["v_g"][gi]], axis=0).T
            bk = jnp.concatenate([c["b_t"][gi], c["k_t"][gi]], axis=0)
            for q in range(n_seq):
                lhs = uv_t if n_seq == 1 else uv_t * qmask_ref[q]
                p_last = c["p_inc"][(q + 1) * t_seq - 1:(q + 1) * t_seq, sls[gi]]
                s = slot(j, q)
                store_state(s, gi, bones * ((load_state(s, gi) + _dot(lhs.astype(BF16), bk)) * p_last))
        yield
        mean = [x * (1.0 / HEAD_DIM) for x in _segsum(o, bones_b)]
        d = [o[gi] - mean[gi] for gi in groups]
        var = [x * (1.0 / HEAD_DIM) for x in _segsum([x * x for x in d], bones_b)]
        bonus = _segsum([c["rkr"][:, sl] for sl in sls], bones_b)
        yield
        for gi in groups:
            sl = sls[gi]
            on = d[gi] * lax.rsqrt(var[gi] + GN_EPS) * vec[_V_LNW:_V_LNW + 1, sl] + vec[_V_LNB:_V_LNB + 1, sl]
            ya_ref[rows, sl] = ((on + bonus[gi] * c["v_g"][gi]) * c["gate"][:, sl]).astype(ya_ref.dtype)

    preps, invs, tails = ([f(j) for j in range(n_sub)] for f in (prep, inverse, tail))
    for phase in range(n_sub + 2):
        active = []
        if 0 <= phase - 2 < n_sub:
            active.append(tails[phase - 2])
        if 0 <= phase - 1 < n_sub:
            active.append(invs[phase - 1])
        if phase < n_sub:
            active.append(preps[phase])
        inv_gen = invs[phase - 1] if 0 <= phase - 1 < n_sub else None
        rnd = 0
        while active:
            for gen in list(active):
                if inv_gen in active and gen is not inv_gen and len(active) == 3 and (rnd + active.index(gen) // 2) % 2:
                    continue
                try:
                    next(gen)
                except StopIteration:
                    active.remove(gen)
            rnd += 1

    if chain and n_steps > 1:
        assert n_seq == 1
        last_r, last_k, last_v = ctx[n_sub - 1]["last"]
        prev_ref[:, 0:rw] = last_r
        prev_ref[:, rw:2 * rw] = last_k
        prev_ref[:, 2 * rw:3 * rw] = last_v
        prev_ref[:, LB_OFF:LB_OFF + LORA_COLS] = lb_last

    @pl.when(step == n_steps - 1)
    def _():
        for s in range(n_slots):
            for hd in range(N_HEADS):
                g, row0, half, upper = head_slot(hd)
                even = sbd_ref[s, g, half, pl.ds(row0, HEAD_DIM // 2, stride=2), :]
                odd = sbd_ref[s, g, half, pl.ds(row0 + 1, HEAD_DIM // 2, stride=2), :]
                if upper:
                    even = pltpu.roll(even, HEAD_DIM, axis=1)
                else:
                    odd = pltpu.roll(odd, HEAD_DIM, axis=1)
                sout_ref[s, hd] = jnp.where(low_lanes, even, odd)


def _wkv(z, prev0, vfirst, w2e, vecs, s_all, layer, *, n_batch, t_len, n_seq, n_sub, row0):
    m = n_batch * t_len
    t_seq = CHUNK // n_seq
    n_chunks = t_len // t_seq
    chain = n_seq == 1
    if chain:
        assert n_chunks % n_sub == 0
        n_steps, nb, n_slots = n_chunks // n_sub, n_batch, n_seq
    else:
        assert n_chunks == 1 and n_batch % (n_seq * n_sub) == 0
        n_steps, nb, n_slots = 1, n_batch // (n_seq * n_sub), n_seq * n_sub
    has_vres = vfirst is not None
    consts = _wkv_constants(n_seq, t_seq)
    n_levels = consts["levels"].shape[0]
    rw = RWKV_WIDTH
    rows = n_sub * CHUNK
    rb = lambda b, s: b * n_steps + s
    prev0 = prev0.reshape(nb, n_slots, PREV_COLS)

    assert row0 % rows == 0
    z_off = row0 // rows
    zspec = lambda col: pl.BlockSpec((rows, rw), lambda b, s: (rb(b, s) + z_off, col))
    row_spec = pl.BlockSpec((rows, rw), lambda b, s: (rb(b, s), 0))
    full = lambda a: pl.BlockSpec(a.shape, lambda b, s: (0,) * a.ndim)
    sspec = pl.BlockSpec((None, n_slots, N_HEADS, HEAD_DIM // 2, 2 * HEAD_DIM), lambda b, s: (layer, b, 0, 0, 0))

    in_specs = [zspec(Z_R), zspec(Z_K), zspec(Z_V), zspec(Z_G_RWKV),
                pl.BlockSpec((rows, PROJ_TN), lambda b, s: (rb(b, s) + z_off, Z_LORA)),
                pl.BlockSpec((1, n_slots, PREV_COLS), lambda b, s: (b, 0, 0))]
    args = [z, z, z, z, z, prev0]
    if has_vres:
        in_specs.append(row_spec)
        args.append(vfirst)
    else:
        w2e = w2e[:, :2 * rw]
    in_specs += [full(w2e), full(vecs), sspec]
    args += [w2e, vecs, s_all]
    state_arg = len(args) - 1
    for name in ("tri", "mask_sc", "bones", "bones_b", "levels", "qmask"):
        in_specs.append(full(consts[name]))
        args.append(consts[name])

    out_specs = [row_spec]
    out_shape = [jax.ShapeDtypeStruct((m, rw), BF16)]
    if not has_vres:
        out_specs.append(row_spec)
        out_shape.append(jax.ShapeDtypeStruct((m, rw), F32))
    out_specs.append(sspec)
    out_shape.append(jax.ShapeDtypeStruct(s_all.shape, F32))

    outs = pl.pallas_call(
        functools.partial(_wkv_kernel, n_seq=n_seq, t_seq=t_seq, n_sub=n_sub, chain=chain, n_steps=n_steps,
                          has_vres=has_vres, n_levels=n_levels),
        grid=(nb, n_steps),
        in_specs=in_specs, out_specs=out_specs, out_shape=out_shape,
        input_output_aliases={state_arg: len(out_shape) - 1},
        scratch_shapes=[pltpu.VMEM((n_slots, N_GROUPS, 2, GROUP_LANES, GROUP_LANES // 2), F32),
                        pltpu.VMEM((n_slots, PREV_COLS), F32)],
        name="wkv",
        compiler_params=pltpu.CompilerParams(
            dimension_semantics=("parallel", "arbitrary"), vmem_limit_bytes=VMEM_LIMIT_BYTES),
    )(*args)
    if has_vres:
        ya, s_new = outs
        return ya, vfirst, s_new
    return outs


def _mix_kernel(*refs, n_seq, t_seq, tiles_per_seq, n_past, carry_pool):
    if carry_pool:
        (ya_ref, u_ref, gp_ref, ga_ref, gb_ref, pool_ref, perm_ref, permt_ref, pw_ref, ps_ref, wb0_ref, wb1_ref,
         m_ref, pool_out_ref) = refs
    else:
        ya_ref, u_ref, gp_ref, ga_ref, gb_ref, halo_ref, pw_ref, ps_ref, wb0_ref, wb1_ref, m_ref = refs
    rows = n_seq * t_seq
    tile = pl.program_id(0) % tiles_per_seq
    if carry_pool:
        perm = perm_ref[...]
        u_t = jnp.dot(perm, u_ref[...], preferred_element_type=F32).reshape(t_seq, n_seq, POOL_WIDTH)
        gp_all = jnp.dot(perm, gp_ref[...], preferred_element_type=F32)
        e = jnp.concatenate([jnp.zeros((HALO - POOL_BUF, n_seq, POOL_WIDTH), F32), pool_ref[...], u_t], axis=0)
        pool_out_ref[...] = e[HALO + t_seq - POOL_BUF:]
        t_axis = 0
        pos = lax.broadcasted_iota(jnp.int32, (t_seq, 1, 1), 0) + (1 + n_past)
    else:
        u = u_ref[...].astype(F32)
        halo = jnp.where(tile == 0, 0.0, halo_ref[...].astype(F32)).reshape(1, HALO, POOL_WIDTH)
        e = jnp.concatenate([halo, u.reshape(n_seq, t_seq, POOL_WIDTH)], axis=1)
        t_axis = 1
        pos = lax.broadcasted_iota(jnp.int32, (1, t_seq, 1), 1) + (tile * t_seq + 1 + n_past)
    take = lambda x, start, size: lax.slice_in_dim(x, start, start + size, axis=t_axis)
    n_col = len(POOL_WINDOWS)
    cw = D_MODEL // n_col
    cols = [slice(ci * cw, (ci + 1) * cw) for ci in range(n_col)]
    ya = ya_ref[...]
    gated_a, yb_parts = [], []
    for gi, win in enumerate(POOL_WINDOWS):
        pa = jnp.dot(ya, wb0_ref[:, cols[gi]], preferred_element_type=F32)
        sl = slice(gi * POOL_GROUP, (gi + 1) * POOL_GROUP)
        e_g = e[:, :, sl]
        acc = e_g
        span, length = 1, HALO + t_seq
        while span < win:
            length -= span
            acc = take(acc, span, length) + take(acc, 0, length)
            span *= 2
        ws = take(acc, length - t_seq, t_seq)
        inv_cnt = 1.0 / jnp.minimum(pos, win).astype(F32)
        mixed = (ws * inv_cnt - take(e_g, HALO, t_seq)).reshape(rows, POOL_GROUP)
        y_g = jnp.dot(mixed.astype(BF16), pw_ref[gi], preferred_element_type=F32)
        gp = gp_all[:, sl] if carry_pool else gp_ref[:, sl].astype(F32)
        yb_parts.append((y_g * ps_ref[:, sl] * (gp * _sigmoid(gp))).astype(BF16))
        gated_a.append(_sigmoid(ga_ref[:, cols[gi]].astype(F32)) * pa)
    y_b = jnp.concatenate(yb_parts, axis=1)
    if carry_pool:
        y_b = jnp.dot(permt_ref[...], y_b, preferred_element_type=F32).astype(BF16)
    pb = [None] * n_col
    pb[0] = jnp.dot(y_b, wb1_ref[:, cols[0]], preferred_element_type=F32)
    for ci in range(n_col):
        if ci + 1 < n_col:
            pb[ci + 1] = jnp.dot(y_b, wb1_ref[:, cols[ci + 1]], preferred_element_type=F32)
        m_ref[:, cols[ci]] = (gated_a[ci] + _sigmoid(gb_ref[:, cols[ci]].astype(F32)) * pb[ci]).astype(m_ref.dtype)


def _mix(ya, z, pool_all, pool_w_b, pool_scale, w_branch_b, layer, *, n_seq, t_seq, tiles_per_seq, n_past, row0):
    m = ya.shape[0]
    rows = n_seq * t_seq
    assert row0 % rows == 0
    z_off = row0 // rows
    carry_pool = pool_all is not None
    in_specs = [
        pl.BlockSpec((rows, RWKV_WIDTH), lambda i: (i, 0)),
        pl.BlockSpec((rows, POOL_WIDTH), lambda i: (i + z_off, Z_U)),
        pl.BlockSpec((rows, POOL_WIDTH), lambda i: (i + z_off, Z_G_POOL)),
        pl.BlockSpec((rows, D_MODEL), lambda i: (i + z_off, Z_GATE_A)),
        pl.BlockSpec((rows, D_MODEL), lambda i: (i + z_off, Z_GATE_B)),
    ]
    args = [ya, z, z, z, z]
    out_specs = [pl.BlockSpec((rows, D_MODEL), lambda i: (i, 0))]
    out_shape = [jax.ShapeDtypeStruct((m, D_MODEL), BF16)]
    aliases = {}
    if carry_pool:
        assert tiles_per_seq == 1 and t_seq < POOL_BUF
        r = np.arange(rows)
        perm = np.zeros((rows, rows), np.float32)
        perm[(r % t_seq) * n_seq + r // t_seq, r] = 1.0
        pool_spec = pl.BlockSpec((None, POOL_BUF, n_seq, POOL_WIDTH), lambda i: (layer, 0, i, 0))
        const = pl.BlockSpec((rows, rows), lambda i: (0, 0))
        in_specs += [pool_spec, const, const]
        aliases = {len(args): 1}
        args += [pool_all, jnp.asarray(perm, BF16), jnp.asarray(perm.T, BF16)]
        out_specs.append(pool_spec)
        out_shape.append(jax.ShapeDtypeStruct(pool_all.shape, F32))
    else:
        assert n_seq == 1 and rows % HALO == 0 and row0 == 0
        in_specs.append(pl.BlockSpec((HALO, POOL_WIDTH), lambda i: (jnp.maximum(i * (rows // HALO) - 1, 0), Z_U)))
        args.append(z)
    in_specs += [
        pl.BlockSpec((None, len(POOL_WINDOWS), POOL_GROUP, POOL_GROUP), lambda i: (layer, 0, 0, 0)),
        pl.BlockSpec((None, 1, POOL_WIDTH), lambda i: (layer, 0, 0)),
        pl.BlockSpec((None, None, RWKV_WIDTH, D_MODEL), lambda i: (layer, 0, 0, 0), pipeline_mode=pl.Buffered(1)),
        pl.BlockSpec((None, None, POOL_WIDTH, D_MODEL), lambda i: (layer, 1, 0, 0), pipeline_mode=pl.Buffered(1)),
    ]
    args += [pool_w_b, pool_scale.reshape(DEPTH, 1, POOL_WIDTH), w_branch_b, w_branch_b]
    outs = pl.pallas_call(
        functools.partial(_mix_kernel, n_seq=n_seq, t_seq=t_seq, tiles_per_seq=tiles_per_seq, n_past=n_past,
                          carry_pool=carry_pool),
        grid=(m // rows,),
        in_specs=in_specs, out_specs=out_specs, out_shape=out_shape,
        input_output_aliases=aliases,
        name="mix",
        compiler_params=pltpu.CompilerParams(
            dimension_semantics=("parallel",), vmem_limit_bytes=VMEM_LIMIT_BYTES),
    )(*args)
    return outs if carry_pool else (outs[0], None)


def _out_kernel(*refs, n_split, emit_h, n_own):
    if emit_h:
        (m_ref, x_ref, p_ref, g_ref, gn_ref, wo_ref, wg_ref, wp_ref), (o_ref, h_ref) = refs[:8], refs[-2:]
    else:
        m_ref, x_ref, p_ref, g_ref, wo_ref, wg_ref, wp_ref, o_ref = refs

    @pl.when(pl.program_id(0) < n_own)
    def _():
        rows = m_ref.shape[0] // n_split
        parts = [slice(i * rows, (i + 1) * rows) for i in range(n_split)]
        outs = [jnp.dot(m_ref[sl, :], wo_ref[...], preferred_element_type=F32) for sl in parts]
        ples = [jnp.dot(p_ref[sl, :].astype(BF16), wp_ref[...], preferred_element_type=F32) for sl in parts]
        xs = [x_ref[sl, :] + _rmsnorm(o, g_ref[...]) for sl, o in zip(parts, outs)]
        gates = [_sigmoid(jnp.dot(x.astype(BF16), wg_ref[...], preferred_element_type=F32)) for x in xs]
        for sl, x, gate, ple in zip(parts, xs, gates, ples):
            x_new = x + gate * ple
            o_ref[sl, :] = x_new
            if emit_h:
                h_ref[sl, :] = _rmsnorm(x_new, gn_ref[...]).astype(h_ref.dtype)

    if emit_h:
        @pl.when(pl.program_id(0) >= n_own)
        def _():
            h_ref[...] = jnp.zeros(h_ref.shape, h_ref.dtype)


def _out(mm_, x2d, p3d, norm_post, norm_pre, w_out_b, w_gate_b, w_ple_b, layer, *, rows, row0, total_rows,
         h_into):
    m = x2d.shape[0]
    emit_h = layer + 1 < DEPTH
    sh = _SharedRows(m, rows, row0, total_rows, D_MODEL, BF16, h_into) if emit_h else _SharedRows(
        m, rows, 0, m, D_MODEL, BF16, None)
    once = pl.Buffered(1)
    wspec = lambda k, l=layer: pl.BlockSpec((None, k, D_MODEL), lambda i: (l, 0, 0), pipeline_mode=once)
    row_spec = pl.BlockSpec((rows, D_MODEL), lambda i: (sh.own(i), 0))
    in_specs = [row_spec, row_spec, pl.BlockSpec((None, rows, PLE_DIM), lambda i: (layer, sh.own(i), 0)), wspec(1)]
    args = [mm_, x2d, p3d, norm_post.reshape(DEPTH, 1, D_MODEL)]
    out_specs, out_shape = [row_spec], [jax.ShapeDtypeStruct((m, D_MODEL), F32)]
    if emit_h:
        in_specs.append(wspec(1, layer + 1))
        args.append(norm_pre.reshape(DEPTH, 1, D_MODEL))
    in_specs += [wspec(D_MODEL), wspec(D_MODEL), wspec(PLE_DIM)]
    args += [w_out_b, w_gate_b, w_ple_b]
    aliases = {}
    if emit_h:
        out_specs.append(sh.out_spec)
        out_shape.append(sh.out_shape)
        if sh.aliased:
            aliases = {len(args): 1}
        in_specs += sh.extra_specs
        args += sh.extra_args
    outs = pl.pallas_call(
        functools.partial(_out_kernel, n_split=rows // OUT_SUB_ROWS, emit_h=emit_h, n_own=sh.n_own),
        grid=sh.grid,
        in_specs=in_specs, out_specs=out_specs, out_shape=out_shape,
        input_output_aliases=aliases,
        name="out",
        compiler_params=pltpu.CompilerParams(
            dimension_semantics=("arbitrary",), vmem_limit_bytes=VMEM_LIMIT_BYTES),
    )(*args)
    return (outs[0], outs[1]) if emit_h else (outs[0], None)


def _pair_rows(s):
    l, b = s.shape[:2]
    return s.reshape(l, b, N_HEADS, HEAD_DIM // 2, 2 * HEAD_DIM)


def _unpair_rows(s):
    l, b = s.shape[:2]
    return s.reshape(l, b, N_HEADS, HEAD_DIM, HEAD_DIM)


def _layer_weights(i, mu_rkv, mu_lora, w0, w1, w2, a0, a1, a2, v0, v1, v2, k_k, k_a, r_k, ln_w, ln_b):
    r_ = RWKV_WIDTH
    mu = mu_lora[i]
    if i > 0:
        v1_i, v2_i, v0_i = v1[i - 1], v2[i - 1], v0[i - 1]
    else:
        v1_i = jnp.zeros((D_MODEL, MV_RANK), F32)
        v2_i = jnp.zeros((MV_RANK, r_), F32)
        v0_i = jnp.zeros((r_,), F32)
    w_lora = jnp.concatenate(
        [(1.0 - mu[0])[:, None] * w1[i], (1.0 - mu[1])[:, None] * a1[i], (1.0 - mu[2])[:, None] * v1_i,
         mu[0][:, None] * w1[i], mu[1][:, None] * a1[i], mu[2][:, None] * v1_i], axis=1).astype(BF16)
    z96 = jnp.zeros((DECAY_RANK, r_), F32)
    z64 = jnp.zeros((MV_RANK, r_), F32)
    w2e = jnp.concatenate([
        jnp.concatenate([w2[i], z96, z64], axis=0),
        jnp.concatenate([z96, a2[i], z64], axis=0),
        jnp.concatenate([z96, z96, v2_i], axis=0)], axis=1).astype(BF16)
    rows = [mu_rkv[i, 0], mu_rkv[i, 1], mu_rkv[i, 2], w0[i], a0[i], v0_i, k_k[i], k_a[i], r_k[i], ln_w[i], ln_b[i]]
    vecs = jnp.concatenate([jnp.stack(rows), jnp.zeros((_N_VEC_ROWS - len(rows), r_), F32)], axis=0)
    return w_lora, w2e, vecs


class _Group:
    def __init__(self, x, p, wkv, shift, pool, *, n_past, row0, n_seq, n_sub, mix_seq, mix_t):
        self.b, self.t, d = x.shape
        self.m = self.b * self.t
        self.x2d = x.reshape(self.m, d)
        self.p3d = p.reshape(DEPTH, self.m, PLE_DIM)
        self.zero_state = wkv is None
        self.shift, self.n_past, self.row0 = shift, n_past, row0
        self.n_seq, self.n_sub, self.mix_seq, self.mix_t = n_seq, n_sub, mix_seq, mix_t
        self.s_all = _pair_rows(
            jnp.zeros((DEPTH, self.b, N_HEADS, HEAD_DIM, HEAD_DIM), F32) if self.zero_state else wkv)
        self.pool_all = None if self.zero_state else jnp.transpose(pool, (0, 2, 1, 3))
        self.v_first = None
        self.shift_out, self.pool_out = [], []

    def outputs(self):
        pool_new = jnp.stack(self.pool_out) if self.zero_state else jnp.transpose(self.pool_all, (0, 2, 1, 3))
        return (self.x2d.reshape(self.b, self.t, D_MODEL), _unpair_rows(self.s_all), jnp.stack(self.shift_out),
                pool_new)


def _run_layers(groups, lw, params):
    (w_in, norm_pre, norm_post, pool_w_b, pool_scale, w_branch_b, w_out_b, w_ple_b, w_gate_b) = params
    total_rows = sum(g.m for g in groups)
    assert total_rows % PROJ_TM == 0
    h_all = None
    for g in groups:
        h_all = _norm(g.x2d, norm_pre[0], rows=NORM_ROWS, out_dtype=BF16, row0=g.row0, total_rows=total_rows,
                      into=h_all)
    for i in range(DEPTH):
        w_lora, w2e, vecs = lw[i]
        z = _proj(h_all, w_in, i, w_lora, tm=PROJ_TM, n_main=N_MAIN_TILES, out_dtype=BF16)
        h_all = None
        for g in groups:
            g.shift_out.append(
                _norm(g.x2d.reshape(g.b, g.t, D_MODEL)[:, -1], norm_pre[i], rows=g.b, out_dtype=F32))
            if g.zero_state:
                assert g.t >= POOL_BUF
                prev0 = jnp.zeros((g.b, PREV_COLS), F32)
                g.pool_out.append(jnp.stack([
                    z[g.row0 + (q + 1) * g.t - POOL_BUF:g.row0 + (q + 1) * g.t, Z_U * POOL_WIDTH:(Z_U + 1) * POOL_WIDTH]
                    for q in range(g.b)]).astype(F32))
            else:
                prev0 = _proj(g.shift[i].astype(BF16), w_in, i, w_lora, tm=g.b, n_main=3 * RWKV_WIDTH // PROJ_TN,
                              out_dtype=F32)
            ya, g.v_first, g.s_all = _wkv(z, prev0, g.v_first, w2e, vecs, g.s_all, i, n_batch=g.b, t_len=g.t,
                                          n_seq=g.n_seq, n_sub=g.n_sub, row0=g.row0)
            mm_, g.pool_all = _mix(ya, z, g.pool_all, pool_w_b, pool_scale, w_branch_b, i, n_seq=g.mix_seq,
                                   t_seq=g.mix_t, tiles_per_seq=g.t // g.mix_t, n_past=g.n_past, row0=g.row0)
            g.x2d, h_all = _out(mm_, g.x2d, g.p3d, norm_post, norm_pre, w_out_b, w_gate_b, w_ple_b, i,
                                rows=OUT_ROWS, row0=g.row0, total_rows=total_rows, h_into=h_all)


def kernel(x_prompt, x_sample, state_wkv, state_shift, state_pool, p_prompt, p_sample, norm_pre, norm_post, w_in, mu_rkv, mu_lora, w0, w1, w2, a0, a1, a2, v0, v1, v2, k_k, k_a, r_k, ln_w, ln_b, pool_w, pool_scale, w_branch, w_out, w_ple, w_ple_gate):
    lw = [_layer_weights(i, mu_rkv, mu_lora, w0, w1, w2, a0, a1, a2, v0, v1, v2, k_k, k_a, r_k, ln_w, ln_b)
          for i in range(DEPTH)]
    params = (w_in, norm_pre, norm_post, pool_w.astype(BF16), pool_scale, w_branch.astype(BF16),
              w_out.astype(BF16), w_ple.astype(BF16), w_ple_gate.astype(BF16))
    prompt = _Group(x_prompt, p_prompt, None, None, None, n_past=0, row0=0,
                    n_seq=1, n_sub=8, mix_seq=1, mix_t=256)
    sample = _Group(x_sample, p_sample, state_wkv, state_shift, state_pool, n_past=PAST_LEN, row0=prompt.m,
                    n_seq=8, n_sub=2, mix_seq=32, mix_t=8)
    _run_layers([prompt, sample], lw, params)
    y_p, wkv_p, shift_p, pool_p = prompt.outputs()
    y_s, wkv_s, shift_s, pool_s = sample.outputs()
    return (y_p, y_s, wkv_p, shift_p, pool_p, wkv_s, shift_s, pool_s)
```

```python
import functools

import jax
import jax.numpy as jnp
import numpy as np
from jax import lax
from jax.experimental import pallas as pl
from jax.experimental.pallas import tpu as pltpu

F32 = jnp.float32
BF16 = jnp.bfloat16

D_MODEL = 2048
DEPTH = 4
PAST_LEN = 16384
RWKV_WIDTH = 1024
HEAD_DIM = 64
N_HEADS = 16
POOL_WIDTH = 1024
POOL_WINDOWS = (2, 4, 8, 16)
POOL_GROUP = 256
POOL_BUF = 15
PLE_DIM = 256
DECAY_RANK = 96
AAA_RANK = 96
MV_RANK = 64
LORA_COLS = DECAY_RANK + AAA_RANK + MV_RANK
IN_COLS = 10240
RMS_EPS = 1e-6
GN_EPS = 64e-5
DECAY_SCALE = 0.606531

HEADS_PER_GROUP = 4
GROUP_LANES = HEADS_PER_GROUP * HEAD_DIM
N_GROUPS = N_HEADS // HEADS_PER_GROUP
CHUNK = 64
EXPANDED = HEADS_PER_GROUP * CHUNK
F32_ROWS = 8
BF16_ROWS = 16
HALO = 16
PROJ_TN = 2 * LORA_COLS
N_MAIN_TILES = IN_COLS // PROJ_TN
Z_R, Z_K, Z_V, Z_G_RWKV, Z_U, Z_G_POOL = range(6)
Z_GATE_A, Z_GATE_B = 3, 4
Z_LORA = N_MAIN_TILES
PREV_COLS = 3 * RWKV_WIDTH + PROJ_TN
LB_OFF = 3 * RWKV_WIDTH + LORA_COLS

NORM_ROWS = 512
PROJ_TM = 3072
OUT_ROWS = 256
OUT_SUB_ROWS = 128

V7X_VMEM_BYTES = 64 * 1024 * 1024
VMEM_LIMIT_BYTES = V7X_VMEM_BYTES * 7 // 8


def _sigmoid(x):
    return 0.5 * jnp.tanh(0.5 * x) + 0.5


def _split_bf16(x):
    hi = x.astype(BF16)
    lo = (x - hi.astype(F32)).astype(BF16)
    return hi, lo


_NN = (((1,), (0,)), ((), ()))
_NT = (((1,), (1,)), ((), ()))


def _dot(a, b, dims=_NN):
    return lax.dot_general(a, b, dimension_numbers=dims, preferred_element_type=F32)


def _segsum(xs, bones_b):
    rows = xs[0].shape[0]
    s = _dot(jnp.concatenate([x.astype(BF16) for x in xs], axis=0), bones_b)
    return [s[i * rows:(i + 1) * rows] for i in range(len(xs))]


def _proj_kernel(h_ref, w_ref, wl_ref, z_ref, *, n_main):
    j = pl.program_id(1)

    @pl.when(j < n_main)
    def _():
        z_ref[...] = jnp.dot(h_ref[...], w_ref[...].astype(BF16), preferred_element_type=F32).astype(z_ref.dtype)

    @pl.when(j == n_main)
    def _():
        z_ref[...] = jnp.dot(h_ref[...], wl_ref[...], preferred_element_type=F32).astype(z_ref.dtype)


def _proj(h2d, w_in, layer, w_lora, *, tm, n_main, out_dtype):
    m, d = h2d.shape
    tn = PROJ_TN
    assert m % tm == 0
    return pl.pallas_call(
        functools.partial(_proj_kernel, n_main=n_main),
        grid=(m // tm, n_main + 1),
        in_specs=[
            pl.BlockSpec((tm, d), lambda i, j: (i, 0)),
            pl.BlockSpec((None, d, tn), lambda i, j: (layer, 0, jnp.minimum(j, n_main - 1))),
            pl.BlockSpec((d, tn), lambda i, j: (0, 0), pipeline_mode=pl.Buffered(1)),
        ],
        out_specs=pl.BlockSpec((tm, tn), lambda i, j: (i, j)),
        out_shape=jax.ShapeDtypeStruct((m, (n_main + 1) * tn), out_dtype),
        name="proj",
        compiler_params=pltpu.CompilerParams(
            dimension_semantics=("parallel", "arbitrary"), vmem_limit_bytes=VMEM_LIMIT_BYTES),
    )(h2d, w_in, w_lora)


def _rmsnorm(x, gain):
    return x * lax.rsqrt(jnp.mean(x * x, axis=-1, keepdims=True) + RMS_EPS) * gain


def _norm_kernel(x_ref, g_ref, *rest, n_own):
    o_ref = rest[-1]

    @pl.when(pl.program_id(0) < n_own)
    def _():
        o_ref[...] = _rmsnorm(x_ref[...], g_ref[...]).astype(o_ref.dtype)

    @pl.when(pl.program_id(0) >= n_own)
    def _():
        o_ref[...] = jnp.zeros(o_ref.shape, o_ref.dtype)


class _SharedRows:
    def __init__(self, m, rows, row0, total_rows, d, dtype, into):
        assert m % rows == 0 and row0 % rows == 0 and total_rows % rows == 0
        assert into is not None or row0 == 0
        self.n_own = m // rows
        self.grid = (total_rows // rows,) if into is None else (self.n_own,)
        blk0 = row0 // rows
        self.out_spec = pl.BlockSpec((rows, d), lambda i: (i + blk0, 0))
        self.out_shape = jax.ShapeDtypeStruct((total_rows, d), dtype)
        self.extra_specs = [] if into is None else [pl.BlockSpec(memory_space=pl.ANY)]
        self.extra_args = [] if into is None else [into]
        self.aliased = into is not None

    def own(self, i):
        return jnp.minimum(i, self.n_own - 1)


def _norm(x2d, gain, *, rows, out_dtype, row0=0, total_rows=None, into=None):
    m, d = x2d.shape
    rows = min(rows, m)
    sh = _SharedRows(m, rows, row0, total_rows or m, d, out_dtype, into)
    return pl.pallas_call(
        functools.partial(_norm_kernel, n_own=sh.n_own),
        grid=sh.grid,
        in_specs=[pl.BlockSpec((rows, d), lambda i: (sh.own(i), 0)),
                  pl.BlockSpec((1, d), lambda i: (0, 0))] + sh.extra_specs,
        out_specs=sh.out_spec, out_shape=sh.out_shape,
        input_output_aliases={2: 0} if sh.aliased else {},
        name="norm",
        compiler_params=pltpu.CompilerParams(dimension_semantics=("arbitrary",)),
    )(x2d, gain.reshape(1, d), *sh.extra_args)


_V_MU_R, _V_MU_K, _V_MU_V, _V_W0, _V_A0, _V_V0, _V_KK, _V_KA, _V_RK, _V_LNW, _V_LNB = range(11)
_N_VEC_ROWS = 16


def _wkv_constants(n_seq, t_seq):
    c, e = CHUNK, EXPANDED
    assert n_seq * t_seq == c
    row = np.arange(c)
    q_of = row // t_seq
    tri = (row[:, None] >= row[None, :]) & (q_of[:, None] == q_of[None, :])
    erow = np.arange(e)
    es = erow % c
    same = q_of[:, None] == (es // t_seq)[None, :]
    mask_s = same & (row[:, None] > es[None, :])
    mask_i = same & (row[:, None] >= es[None, :])
    mask_sc = np.block([[mask_s, mask_s], [mask_i, mask_i]])
    lane = np.arange(GROUP_LANES)
    bones = (lane[:, None] // HEAD_DIM) == (lane[None, :] // HEAD_DIM)
    levels = [np.eye(e, dtype=bool), (erow[:, None] // 2) == (erow[None, :] // 2)]
    m = 2
    while m < t_seq:
        levels.append(((erow[:, None] // (2 * m)) == (erow[None, :] // (2 * m)))
                      & ((erow[:, None] % (2 * m)) >= m) & ((erow[None, :] % (2 * m)) < m))
        m *= 2
    col = np.arange(2 * c)
    qmask = ((col[None, :] % c) // t_seq) == np.arange(n_seq)[:, None]
    f = lambda a: jnp.asarray(a.astype(np.float32))
    return dict(tri=f(tri).astype(BF16), mask_sc=f(mask_sc).astype(BF16), bones=f(bones),
                bones_b=f(bones).astype(BF16), levels=f(np.stack(levels)).astype(BF16),
                qmask=f(qmask).reshape(n_seq, 1, 2 * c))


def _wkv_kernel(*refs, n_seq, t_seq, n_sub, chain, n_steps, has_vres, n_levels):
    it = iter(refs)
    zr_ref, zk_ref, zv_ref, zg_ref, zl_ref, p0_ref = (next(it) for _ in range(6))
    vfirst_ref = next(it) if has_vres else None
    w2_ref, vec_ref, s0_ref = next(it), next(it), next(it)
    tri_ref, msc_ref, bones_ref, bonesb_ref, lev_ref, qmask_ref = (next(it) for _ in range(6))
    ya_ref = next(it)
    vout_ref = None if has_vres else next(it)
    sout_ref = next(it)
    sbd_ref, prev_ref = next(it), next(it)

    step = pl.program_id(1)
    gl, rw, e = GROUP_LANES, RWKV_WIDTH, EXPANDED
    n_slots = n_seq if chain else n_sub * n_seq
    slot = (lambda j, q: q) if chain else (lambda j, q: j * n_seq + q)
    bones = bones_ref[...]
    bones_b = bonesb_ref[...]
    mask_sc = msc_ref[...]
    eye_b = lev_ref[0]
    tri = tri_ref[...]
    vec = vec_ref[...]
    vrow = lambda i: vec[i:i + 1, :]
    groups = range(N_GROUPS)
    sls = [slice(gi * gl, (gi + 1) * gl) for gi in groups]

    def expand(x):
        return jnp.concatenate([x] * HEADS_PER_GROUP, axis=0) * bones_b

    low_lanes = lax.broadcasted_iota(jnp.int32, (1, 2 * HEAD_DIM), 1) < HEAD_DIM

    def head_slot(hd):
        h = hd % HEADS_PER_GROUP
        return hd // HEADS_PER_GROUP, h * HEAD_DIM, h // 2, h % 2 == 1

    def load_state(s, g):
        return jnp.concatenate([sbd_ref[s, g, 0], sbd_ref[s, g, 1]], axis=1)

    def store_state(s, g, val):
        sbd_ref[s, g, 0] = val[:, :2 * HEAD_DIM]
        sbd_ref[s, g, 1] = val[:, 2 * HEAD_DIM:]

    @pl.when(step == 0)
    def _():
        for s in range(n_slots):
            for hd in range(N_HEADS):
                g, row0, half, upper = head_slot(hd)
                packed = s0_ref[s, hd]
                even = jnp.where(low_lanes, packed, 0.0)
                odd = jnp.where(low_lanes, 0.0, packed)
                if upper:
                    even = pltpu.roll(even, HEAD_DIM, axis=1)
                else:
                    odd = pltpu.roll(odd, HEAD_DIM, axis=1)
                sbd_ref[s, g, half, pl.ds(row0, HEAD_DIM // 2, stride=2), :] = even
                sbd_ref[s, g, half, pl.ds(row0 + 1, HEAD_DIM // 2, stride=2), :] = odd
                sbd_ref[s, g, 1 - half, row0:row0 + HEAD_DIM, :] = jnp.zeros((HEAD_DIM, 2 * HEAD_DIM), F32)
        prev_ref[...] = p0_ref[0]

    row = lax.broadcasted_iota(jnp.int32, (CHUNK, 1), 0)
    is_first = (row % t_seq) == 0
    lora_lane = lax.broadcasted_iota(jnp.int32, (1, LORA_COLS), 1)

    def shifted(z, p0):
        rolled = pltpu.roll(z, 1, axis=0)
        if n_seq == 1:
            first = jnp.broadcast_to(p0, z.shape)
        else:
            first = jnp.broadcast_to(p0[:, None, :], (n_seq, t_seq, z.shape[1])).reshape(z.shape)
        return jnp.where(is_first, first, rolled)

    n_rows = n_sub * CHUNK
    zl_all = zl_ref[...].astype(F32)
    lb_all = zl_all[:, LORA_COLS:]
    p_l = prev_ref[:, LB_OFF:LB_OFF + LORA_COLS]
    if n_slots == 1:
        lb_first = jnp.broadcast_to(p_l, lb_all.shape)
    else:
        lb_first = jnp.broadcast_to(p_l[:, None, :], (n_slots, t_seq, LORA_COLS)).reshape(lb_all.shape)
    row_all = lax.broadcasted_iota(jnp.int32, (n_rows, 1), 0)
    lb_prev = jnp.where(row_all % (n_rows if chain else t_seq) == 0, lb_first, pltpu.roll(lb_all, 1, axis=0))
    lin = zl_all[:, :LORA_COLS] + lb_prev
    lin = jnp.where(lora_lane < DECAY_RANK, jnp.tanh(lin), lin).astype(BF16)
    d_full = jnp.dot(lin, w2_ref[...], preferred_element_type=F32)
    lb_last = lb_all[n_rows - 1:n_rows, :]

    ctx = [dict() for _ in range(n_sub)]

    def prep(j):
        c = ctx[j]
        rows = slice(j * CHUNK, (j + 1) * CHUNK)
        zr, zk, zv = zr_ref[rows, :].astype(F32), zk_ref[rows, :].astype(F32), zv_ref[rows, :].astype(F32)
        c["last"] = [x[CHUNK - 1:CHUNK, :] for x in (zr, zk, zv)]
        if chain and j > 0:
            p_r, p_k, p_v = ctx[j - 1]["last"]
        else:
            ps = slice(0, n_seq) if chain else slice(j * n_seq, (j + 1) * n_seq)
            p_r, p_k, p_v = prev_ref[ps, 0:rw], prev_ref[ps, rw:2 * rw], prev_ref[ps, 2 * rw:3 * rw]
        r = zr + (shifted(zr, p_r) - zr) * vrow(_V_MU_R)
        k = zk + (shifted(zk, p_k) - zk) * vrow(_V_MU_K)
        v = zv + (shifted(zv, p_v) - zv) * vrow(_V_MU_V)
        d_all = d_full[rows]
        wlog = -DECAY_SCALE * _sigmoid(vrow(_V_W0) + d_all[:, 0:rw])
        alr = _sigmoid(vrow(_V_A0) + d_all[:, rw:2 * rw])
        if has_vres:
            v = v + (vfirst_ref[rows, :] - v) * _sigmoid(vrow(_V_V0) + d_all[:, 2 * rw:3 * rw])
        else:
            vout_ref[rows, :] = v
        kkr = k * vrow(_V_KK)
        kf = k * (1.0 + (alr - 1.0) * vrow(_V_KA))
        c["rkr"] = r * kf * vrow(_V_RK)
        g_all = zg_ref[rows, :].astype(F32)
        c["gate"] = g_all * _sigmoid(g_all)
        w_hi, w_lo = _split_bf16(wlog)
        cum = jnp.dot(tri, w_hi, preferred_element_type=F32) + jnp.dot(tri, w_lo, preferred_element_type=F32)
        p_inc = jnp.exp(cum)
        inv_p = jnp.exp(-cum)
        p_exc = jnp.exp(cum - wlog)
        c["p_inc"] = p_inc
        yield
        kk_n2 = _segsum([kkr[:, sl] * kkr[:, sl] for sl in sls], bones_b)
        kk = [kkr[:, sl] * lax.rsqrt(jnp.maximum(n2, 1e-24)) for sl, n2 in zip(sls, kk_n2)]
        c["v_g"] = [v[:, sl] for sl in sls]
        c["a_f"] = [-kk[gi] * p_exc[:, sls[gi]] for gi in groups]
        c["r_f"] = [r[:, sl] * p_inc[:, sl] for sl in sls]
        a_t = [x.astype(BF16) for x in c["a_f"]]
        r_t = [x.astype(BF16) for x in c["r_f"]]
        c["b_t"] = [(kk[gi] * alr[:, sls[gi]] * inv_p[:, sls[gi]]).astype(BF16) for gi in groups]
        c["k_t"] = [(kf[:, sl] * inv_p[:, sl]).astype(BF16) for sl in sls]
        c["v_e"] = [expand(x.astype(BF16)) for x in c["v_g"]]
        c["ar"] = [jnp.concatenate([a_t[gi], r_t[gi]], axis=0) for gi in groups]
        yield
        sc = [_dot(c["ar"][gi], jnp.concatenate([expand(c["b_t"][gi]), expand(c["k_t"][gi])], axis=0), _NT)
              for gi in groups]
        c["scm"] = [x.astype(BF16) * mask_sc for x in sc]
        c["n_bd"] = [expand(x[:CHUNK, :e]) for x in c["scm"]]
        c["xs"] = [eye_b + n * lev_ref[1] for n in c["n_bd"]]
        yield
        c["akv"] = [_dot(c["scm"][gi][:, e:], c["v_e"][gi]) for gi in groups]
        yield

    def inverse(j):
        c = ctx[j]
        m = 2
        xs32 = None
        for lvl in range(2, n_levels):
            n_c = [c["n_bd"][gi] * lev_ref[lvl] for gi in groups]
            if m % F32_ROWS == 0:
                wide = m % BF16_ROWS != 0
                src = xs32 if wide else c["xs"]
                n_blk = e // (2 * m)
                lower = lambda x: jnp.concatenate([x[b * 2 * m + m:(b + 1) * 2 * m] for b in range(n_blk)], axis=0)
                eye_l = lower(eye_b.astype(src[0].dtype)).astype(BF16)
                t1 = [_dot(lower(src[gi]).astype(BF16), n_c[gi]).astype(BF16) + eye_l for gi in groups]
                yield
                new = [_dot(t1[gi], c["xs"][gi]).astype(src[0].dtype) for gi in groups]
                c["xs"] = [jnp.concatenate(
                    [piece for b in range(n_blk)
                     for piece in (x[b * 2 * m:b * 2 * m + m], n[b * m:(b + 1) * m])], axis=0).astype(BF16)
                    for x, n in zip(src, new)]
                xs32 = None
            else:
                t1 = [_dot(c["xs"][gi], n_c[gi]).astype(BF16) + eye_b for gi in groups]
                yield
                full = [_dot(t1[gi], c["xs"][gi]) for gi in groups]
                if (2 * m) % F32_ROWS == 0 and (2 * m) % BF16_ROWS != 0:
                    xs32 = full
                c["xs"] = [x.astype(BF16) for x in full]
            yield
            m *= 2

    def tail(j):
        c = ctx[j]
        rows = slice(j * CHUNK, (j + 1) * CHUNK)
        x_c = [(x[0:CHUNK] + x[CHUNK:2 * CHUNK]) + (x[2 * CHUNK:3 * CHUNK] + x[3 * CHUNK:4 * CHUNK])
               for x in c["xs"]]
        a_s, r_s = [], []
        for gi in groups:
            parts_a, parts_r = [], []
            for q in range(n_seq):
                lo_, hi_ = q * t_seq, (q + 1) * t_seq
                lhs = c["ar"][gi] if n_seq == 1 else jnp.concatenate(
                    [c["a_f"][gi][lo_:hi_], c["r_f"][gi][lo_:hi_]], axis=0).astype(BF16)
                res = _dot(lhs, load_state(slot(j, q), gi).astype(BF16), _NT)
                parts_a.append(res[:t_seq])
                parts_r.append(res[t_seq:])
            a_s.append(parts_a[0] if n_seq == 1 else jnp.concatenate(parts_a, axis=0))
            r_s.append(parts_r[0] if n_seq == 1 else jnp.concatenate(parts_r, axis=0))
        yield
        y = [a_s[gi] + c["akv"][gi][:CHUNK] for gi in groups]
        u = [_dot(x_c[gi], expand(y[gi].astype(BF16))) for gi in groups]
        yield
        o = [r_s[gi] + c["akv"][gi][CHUNK:] + _dot(c["scm"][gi][CHUNK:, :e], expand(u[gi].astype(BF16)))
             for gi in groups]
        yield
        for gi in groups:
            uv_t = jnp.concatenate([u[gi], c["v_g"][gi]], axis=0).T
            bk = jnp.concatenate([c["b_t"][gi], c["k_t"][gi]], axis=0)
            for q in range(n_seq):
                lhs = uv_t if n_seq == 1 else uv_t * qmask_ref[q]
                p_last = c["p_inc"][(q + 1) * t_seq - 1:(q + 1) * t_seq, sls[gi]]
                s = slot(j, q)
                store_state(s, gi, bones * ((load_state(s, gi) + _dot(lhs.astype(BF16), bk)) * p_last))
        yield
        mean = [x * (1.0 / HEAD_DIM) for x in _segsum(o, bones_b)]
        d = [o[gi] - mean[gi] for gi in groups]
        var = [x * (1.0 / HEAD_DIM) for x in _segsum([x * x for x in d], bones_b)]
        bonus = _segsum([c["rkr"][:, sl] for sl in sls], bones_b)
        yield
        for gi in groups:
            sl = sls[gi]
            on = d[gi] * lax.rsqrt(var[gi] + GN_EPS) * vec[_V_LNW:_V_LNW + 1, sl] + vec[_V_LNB:_V_LNB + 1, sl]
            ya_ref[rows, sl] = ((on + bonus[gi] * c["v_g"][gi]) * c["gate"][:, sl]).astype(ya_ref.dtype)

    preps, invs, tails = ([f(j) for j in range(n_sub)] for f in (prep, inverse, tail))
    for phase in range(n_sub + 2):
        active = []
        if 0 <= phase - 1 < n_sub:
            active.append(invs[phase - 1])
        if 0 <= phase - 2 < n_sub:
            active.append(tails[phase - 2])
        if phase < n_sub:
            active.append(preps[phase])
        inv_gen = invs[phase - 1] if 0 <= phase - 1 < n_sub else None
        rnd = 0
        while active:
            for gen in list(active):
                if inv_gen in active and gen is not inv_gen and len(active) == 3 and (rnd + active.index(gen) // 2) % 2:
                    continue
                try:
                    next(gen)
                except StopIteration:
                    active.remove(gen)
            rnd += 1

    if chain and n_steps > 1:
        assert n_seq == 1
        last_r, last_k, last_v = ctx[n_sub - 1]["last"]
        prev_ref[:, 0:rw] = last_r
        prev_ref[:, rw:2 * rw] = last_k
        prev_ref[:, 2 * rw:3 * rw] = last_v
        prev_ref[:, LB_OFF:LB_OFF + LORA_COLS] = lb_last

    @pl.when(step == n_steps - 1)
    def _():
        for s in range(n_slots):
            for hd in range(N_HEADS):
                g, row0, half, upper = head_slot(hd)
                even = sbd_ref[s, g, half, pl.ds(row0, HEAD_DIM // 2, stride=2), :]
                odd = sbd_ref[s, g, half, pl.ds(row0 + 1, HEAD_DIM // 2, stride=2), :]
                if upper:
                    even = pltpu.roll(even, HEAD_DIM, axis=1)
                else:
                    odd = pltpu.roll(odd, HEAD_DIM, axis=1)
                sout_ref[s, hd] = jnp.where(low_lanes, even, odd)


def _wkv(z, prev0, vfirst, w2e, vecs, s_all, layer, *, n_batch, t_len, n_seq, n_sub, row0):
    m = n_batch * t_len
    t_seq = CHUNK // n_seq
    n_chunks = t_len // t_seq
    chain = n_seq == 1
    if chain:
        assert n_chunks % n_sub == 0
        n_steps, nb, n_slots = n_chunks // n_sub, n_batch, n_seq
    else:
        assert n_chunks == 1 and n_batch % (n_seq * n_sub) == 0
        n_steps, nb, n_slots = 1, n_batch // (n_seq * n_sub), n_seq * n_sub
    has_vres = vfirst is not None
    consts = _wkv_constants(n_seq, t_seq)
    n_levels = consts["levels"].shape[0]
    rw = RWKV_WIDTH
    rows = n_sub * CHUNK
    rb = lambda b, s: b * n_steps + s
    prev0 = prev0.reshape(nb, n_slots, PREV_COLS)

    assert row0 % rows == 0
    z_off = row0 // rows
    zspec = lambda col: pl.BlockSpec((rows, rw), lambda b, s: (rb(b, s) + z_off, col))
    row_spec = pl.BlockSpec((rows, rw), lambda b, s: (rb(b, s), 0))
    full = lambda a: pl.BlockSpec(a.shape, lambda b, s: (0,) * a.ndim)
    sspec = pl.BlockSpec((None, n_slots, N_HEADS, HEAD_DIM // 2, 2 * HEAD_DIM), lambda b, s: (layer, b, 0, 0, 0))

    in_specs = [zspec(Z_R), zspec(Z_K), zspec(Z_V), zspec(Z_G_RWKV),
                pl.BlockSpec((rows, PROJ_TN), lambda b, s: (rb(b, s) + z_off, Z_LORA)),
                pl.BlockSpec((1, n_slots, PREV_COLS), lambda b, s: (b, 0, 0))]
    args = [z, z, z, z, z, prev0]
    if has_vres:
        in_specs.append(row_spec)
        args.append(vfirst)
    else:
        w2e = w2e[:, :2 * rw]
    in_specs += [full(w2e), full(vecs), sspec]
    args += [w2e, vecs, s_all]
    state_arg = len(args) - 1
    for name in ("tri", "mask_sc", "bones", "bones_b", "levels", "qmask"):
        in_specs.append(full(consts[name]))
        args.append(consts[name])

    out_specs = [row_spec]
    out_shape = [jax.ShapeDtypeStruct((m, rw), BF16)]
    if not has_vres:
        out_specs.append(row_spec)
        out_shape.append(jax.ShapeDtypeStruct((m, rw), F32))
    out_specs.append(sspec)
    out_shape.append(jax.ShapeDtypeStruct(s_all.shape, F32))

    outs = pl.pallas_call(
        functools.partial(_wkv_kernel, n_seq=n_seq, t_seq=t_seq, n_sub=n_sub, chain=chain, n_steps=n_steps,
                          has_vres=has_vres, n_levels=n_levels),
        grid=(nb, n_steps),
        in_specs=in_specs, out_specs=out_specs, out_shape=out_shape,
        input_output_aliases={state_arg: len(out_shape) - 1},
        scratch_shapes=[pltpu.VMEM((n_slots, N_GROUPS, 2, GROUP_LANES, GROUP_LANES // 2), F32),
                        pltpu.VMEM((n_slots, PREV_COLS), F32)],
        name="wkv",
        compiler_params=pltpu.CompilerParams(
            dimension_semantics=("parallel", "arbitrary"), vmem_limit_bytes=VMEM_LIMIT_BYTES),
    )(*args)
    if has_vres:
        ya, s_new = outs
        return ya, vfirst, s_new
    return outs


def _mix_kernel(*refs, n_seq, t_seq, tiles_per_seq, n_past, carry_pool):
    if carry_pool:
        (ya_ref, u_ref, gp_ref, ga_ref, gb_ref, pool_ref, perm_ref, permt_ref, pw_ref, ps_ref, wb0_ref, wb1_ref,
         m_ref, pool_out_ref) = refs
    else:
        ya_ref, u_ref, gp_ref, ga_ref, gb_ref, halo_ref, pw_ref, ps_ref, wb0_ref, wb1_ref, m_ref = refs
    rows = n_seq * t_seq
    tile = pl.program_id(0) % tiles_per_seq
    if carry_pool:
        perm = perm_ref[...]
        u_t = jnp.dot(perm, u_ref[...], preferred_element_type=F32).reshape(t_seq, n_seq, POOL_WIDTH)
        gp_all = jnp.dot(perm, gp_ref[...], preferred_element_type=F32)
        e = jnp.concatenate([jnp.zeros((HALO - POOL_BUF, n_seq, POOL_WIDTH), F32), pool_ref[...], u_t], axis=0)
        pool_out_ref[...] = e[HALO + t_seq - POOL_BUF:]
        t_axis = 0
        pos = lax.broadcasted_iota(jnp.int32, (t_seq, 1, 1), 0) + (1 + n_past)
    else:
        u = u_ref[...].astype(F32)
        halo = jnp.where(tile == 0, 0.0, halo_ref[...].astype(F32)).reshape(1, HALO, POOL_WIDTH)
        e = jnp.concatenate([halo, u.reshape(n_seq, t_seq, POOL_WIDTH)], axis=1)
        t_axis = 1
        pos = lax.broadcasted_iota(jnp.int32, (1, t_seq, 1), 1) + (tile * t_seq + 1 + n_past)
    take = lambda x, start, size: lax.slice_in_dim(x, start, start + size, axis=t_axis)
    n_col = len(POOL_WINDOWS)
    cw = D_MODEL // n_col
    cols = [slice(ci * cw, (ci + 1) * cw) for ci in range(n_col)]
    ya = ya_ref[...]
    gated_a, yb_parts = [], []
    for gi, win in enumerate(POOL_WINDOWS):
        pa = jnp.dot(ya, wb0_ref[:, cols[gi]], preferred_element_type=F32)
        sl = slice(gi * POOL_GROUP, (gi + 1) * POOL_GROUP)
        e_g = e[:, :, sl]
        acc = e_g
        span, length = 1, HALO + t_seq
        while span < win:
            length -= span
            acc = take(acc, span, length) + take(acc, 0, length)
            span *= 2
        ws = take(acc, length - t_seq, t_seq)
        inv_cnt = 1.0 / jnp.minimum(pos, win).astype(F32)
        mixed = (ws * inv_cnt - take(e_g, HALO, t_seq)).reshape(rows, POOL_GROUP)
        y_g = jnp.dot(mixed.astype(BF16), pw_ref[gi], preferred_element_type=F32)
        gp = gp_all[:, sl] if carry_pool else gp_ref[:, sl].astype(F32)
        yb_parts.append((y_g * ps_ref[:, sl] * (gp * _sigmoid(gp))).astype(BF16))
        gated_a.append(_sigmoid(ga_ref[:, cols[gi]].astype(F32)) * pa)
    y_b = jnp.concatenate(yb_parts, axis=1)
    if carry_pool:
        y_b = jnp.dot(permt_ref[...], y_b, preferred_element_type=F32).astype(BF16)
    pb = [None] * n_col
    pb[0] = jnp.dot(y_b, wb1_ref[:, cols[0]], preferred_element_type=F32)
    for ci in range(n_col):
        if ci + 1 < n_col:
            pb[ci + 1] = jnp.dot(y_b, wb1_ref[:, cols[ci + 1]], preferred_element_type=F32)
        m_ref[:, cols[ci]] = (gated_a[ci] + _sigmoid(gb_ref[:, cols[ci]].astype(F32)) * pb[ci]).astype(m_ref.dtype)


def _mix(ya, z, pool_all, pool_w_b, pool_scale, w_branch_b, layer, *, n_seq, t_seq, tiles_per_seq, n_past, row0):
    m = ya.shape[0]
    rows = n_seq * t_seq
    assert row0 % rows == 0
    z_off = row0 // rows
    carry_pool = pool_all is not None
    in_specs = [
        pl.BlockSpec((rows, RWKV_WIDTH), lambda i: (i, 0)),
        pl.BlockSpec((rows, POOL_WIDTH), lambda i: (i + z_off, Z_U)),
        pl.BlockSpec((rows, POOL_WIDTH), lambda i: (i + z_off, Z_G_POOL)),
        pl.BlockSpec((rows, D_MODEL), lambda i: (i + z_off, Z_GATE_A)),
        pl.BlockSpec((rows, D_MODEL), lambda i: (i + z_off, Z_GATE_B)),
    ]
    args = [ya, z, z, z, z]
    out_specs = [pl.BlockSpec((rows, D_MODEL), lambda i: (i, 0))]
    out_shape = [jax.ShapeDtypeStruct((m, D_MODEL), BF16)]
    aliases = {}
    if carry_pool:
        assert tiles_per_seq == 1 and t_seq < POOL_BUF
        r = np.arange(rows)
        perm = np.zeros((rows, rows), np.float32)
        perm[(r % t_seq) * n_seq + r // t_seq, r] = 1.0
        pool_spec = pl.BlockSpec((None, POOL_BUF, n_seq, POOL_WIDTH), lambda i: (layer, 0, i, 0))
        const = pl.BlockSpec((rows, rows), lambda i: (0, 0))
        in_specs += [pool_spec, const, const]
        aliases = {len(args): 1}
        args += [pool_all, jnp.asarray(perm, BF16), jnp.asarray(perm.T, BF16)]
        out_specs.append(pool_spec)
        out_shape.append(jax.ShapeDtypeStruct(pool_all.shape, F32))
    else:
        assert n_seq == 1 and rows % HALO == 0 and row0 == 0
        in_specs.append(pl.BlockSpec((HALO, POOL_WIDTH), lambda i: (jnp.maximum(i * (rows // HALO) - 1, 0), Z_U)))
        args.append(z)
    in_specs += [
        pl.BlockSpec((None, len(POOL_WINDOWS), POOL_GROUP, POOL_GROUP), lambda i: (layer, 0, 0, 0)),
        pl.BlockSpec((None, 1, POOL_WIDTH), lambda i: (layer, 0, 0)),
        pl.BlockSpec((None, None, RWKV_WIDTH, D_MODEL), lambda i: (layer, 0, 0, 0), pipeline_mode=pl.Buffered(1)),
        pl.BlockSpec((None, None, POOL_WIDTH, D_MODEL), lambda i: (layer, 1, 0, 0), pipeline_mode=pl.Buffered(1)),
    ]
    args += [pool_w_b, pool_scale.reshape(DEPTH, 1, POOL_WIDTH), w_branch_b, w_branch_b]
    outs = pl.pallas_call(
        functools.partial(_mix_kernel, n_seq=n_seq, t_seq=t_seq, tiles_per_seq=tiles_per_seq, n_past=n_past,
                          carry_pool=carry_pool),
        grid=(m // rows,),
        in_specs=in_specs, out_specs=out_specs, out_shape=out_shape,
        input_output_aliases=aliases,
        name="mix",
        compiler_params=pltpu.CompilerParams(
            dimension_semantics=("parallel",), vmem_limit_bytes=VMEM_LIMIT_BYTES),
    )(*args)
    return outs if carry_pool else (outs[0], None)


def _out_kernel(*refs, n_split, emit_h, n_own):
    if emit_h:
        (m_ref, x_ref, p_ref, g_ref, gn_ref, wo_ref, wg_ref, wp_ref), (o_ref, h_ref) = refs[:8], refs[-2:]
    else:
        m_ref, x_ref, p_ref, g_ref, wo_ref, wg_ref, wp_ref, o_ref = refs

    @pl.when(pl.program_id(0) < n_own)
    def _():
        rows = m_ref.shape[0] // n_split
        parts = [slice(i * rows, (i + 1) * rows) for i in range(n_split)]
        outs = [jnp.dot(m_ref[sl, :], wo_ref[...], preferred_element_type=F32) for sl in parts]
        ples = [jnp.dot(p_ref[sl, :].astype(BF16), wp_ref[...], preferred_element_type=F32) for sl in parts]
        xs = [x_ref[sl, :] + _rmsnorm(o, g_ref[...]) for sl, o in zip(parts, outs)]
        gates = [_sigmoid(jnp.dot(x.astype(BF16), wg_ref[...], preferred_element_type=F32)) for x in xs]
        for sl, x, gate, ple in zip(parts, xs, gates, ples):
            x_new = x + gate * ple
            o_ref[sl, :] = x_new
            if emit_h:
                h_ref[sl, :] = _rmsnorm(x_new, gn_ref[...]).astype(h_ref.dtype)

    if emit_h:
        @pl.when(pl.program_id(0) >= n_own)
        def _():
            h_ref[...] = jnp.zeros(h_ref.shape, h_ref.dtype)


def _out(mm_, x2d, p3d, norm_post, norm_pre, w_out_b, w_gate_b, w_ple_b, layer, *, rows, row0, total_rows,
         h_into):
    m = x2d.shape[0]
    emit_h = layer + 1 < DEPTH
    sh = _SharedRows(m, rows, row0, total_rows, D_MODEL, BF16, h_into) if emit_h else _SharedRows(
        m, rows, 0, m, D_MODEL, BF16, None)
    once = pl.Buffered(1)
    wspec = lambda k, l=layer: pl.BlockSpec((None, k, D_MODEL), lambda i: (l, 0, 0), pipeline_mode=once)
    row_spec = pl.BlockSpec((rows, D_MODEL), lambda i: (sh.own(i), 0))
    in_specs = [row_spec, row_spec, pl.BlockSpec((None, rows, PLE_DIM), lambda i: (layer, sh.own(i), 0)), wspec(1)]
    args = [mm_, x2d, p3d, norm_post.reshape(DEPTH, 1, D_MODEL)]
    out_specs, out_shape = [row_spec], [jax.ShapeDtypeStruct((m, D_MODEL), F32)]
    if emit_h:
        in_specs.append(wspec(1, layer + 1))
        args.append(norm_pre.reshape(DEPTH, 1, D_MODEL))
    in_specs += [wspec(D_MODEL), wspec(D_MODEL), wspec(PLE_DIM)]
    args += [w_out_b, w_gate_b, w_ple_b]
    aliases = {}
    if emit_h:
        out_specs.append(sh.out_spec)
        out_shape.append(sh.out_shape)
        if sh.aliased:
            aliases = {len(args): 1}
        in_specs += sh.extra_specs
        args += sh.extra_args
    outs = pl.pallas_call(
        functools.partial(_out_kernel, n_split=rows // OUT_SUB_ROWS, emit_h=emit_h, n_own=sh.n_own),
        grid=sh.grid,
        in_specs=in_specs, out_specs=out_specs, out_shape=out_shape,
        input_output_aliases=aliases,
        name="out",
        compiler_params=pltpu.CompilerParams(
            dimension_semantics=("arbitrary",), vmem_limit_bytes=VMEM_LIMIT_BYTES),
    )(*args)
    return (outs[0], outs[1]) if emit_h else (outs[0], None)


def _pair_rows(s):
    l, b = s.shape[:2]
    return s.reshape(l, b, N_HEADS, HEAD_DIM // 2, 2 * HEAD_DIM)


def _unpair_rows(s):
    l, b = s.shape[:2]
    return s.reshape(l, b, N_HEADS, HEAD_DIM, HEAD_DIM)


def _layer_weights(i, mu_rkv, mu_lora, w0, w1, w2, a0, a1, a2, v0, v1, v2, k_k, k_a, r_k, ln_w, ln_b):
    r_ = RWKV_WIDTH
    mu = mu_lora[i]
    if i > 0:
        v1_i, v2_i, v0_i = v1[i - 1], v2[i - 1], v0[i - 1]
    else:
        v1_i = jnp.zeros((D_MODEL, MV_RANK), F32)
        v2_i = jnp.zeros((MV_RANK, r_), F32)
        v0_i = jnp.zeros((r_,), F32)
    w_lora = jnp.concatenate(
        [(1.0 - mu[0])[:, None] * w1[i], (1.0 - mu[1])[:, None] * a1[i], (1.0 - mu[2])[:, None] * v1_i,
         mu[0][:, None] * w1[i], mu[1][:, None] * a1[i], mu[2][:, None] * v1_i], axis=1).astype(BF16)
    z96 = jnp.zeros((DECAY_RANK, r_), F32)
    z64 = jnp.zeros((MV_RANK, r_), F32)
    w2e = jnp.concatenate([
        jnp.concatenate([w2[i], z96, z64], axis=0),
        jnp.concatenate([z96, a2[i], z64], axis=0),
        jnp.concatenate([z96, z96, v2_i], axis=0)], axis=1).astype(BF16)
    rows = [mu_rkv[i, 0], mu_rkv[i, 1], mu_rkv[i, 2], w0[i], a0[i], v0_i, k_k[i], k_a[i], r_k[i], ln_w[i], ln_b[i]]
    vecs = jnp.concatenate([jnp.stack(rows), jnp.zeros((_N_VEC_ROWS - len(rows), r_), F32)], axis=0)
    return w_lora, w2e, vecs


class _Group:
    def __init__(self, x, p, wkv, shift, pool, *, n_past, row0, n_seq, n_sub, mix_seq, mix_t):
        self.b, self.t, d = x.shape
        self.m = self.b * self.t
        self.x2d = x.reshape(self.m, d)
        self.p3d = p.reshape(DEPTH, self.m, PLE_DIM)
        self.zero_state = wkv is None
        self.shift, self.n_past, self.row0 = shift, n_past, row0
        self.n_seq, self.n_sub, self.mix_seq, self.mix_t = n_seq, n_sub, mix_seq, mix_t
        self.s_all = _pair_rows(
            jnp.zeros((DEPTH, self.b, N_HEADS, HEAD_DIM, HEAD_DIM), F32) if self.zero_state else wkv)
        self.pool_all = None if self.zero_state else jnp.transpose(pool, (0, 2, 1, 3))
        self.v_first = None
        self.shift_out, self.pool_out = [], []

    def outputs(self):
        pool_new = jnp.stack(self.pool_out) if self.zero_state else jnp.transpose(self.pool_all, (0, 2, 1, 3))
        return (self.x2d.reshape(self.b, self.t, D_MODEL), _unpair_rows(self.s_all), jnp.stack(self.shift_out),
                pool_new)


def _run_layers(groups, lw, params):
    (w_in, norm_pre, norm_post, pool_w_b, pool_scale, w_branch_b, w_out_b, w_ple_b, w_gate_b) = params
    total_rows = sum(g.m for g in groups)
    assert total_rows % PROJ_TM == 0
    h_all = None
    for g in groups:
        h_all = _norm(g.x2d, norm_pre[0], rows=NORM_ROWS, out_dtype=BF16, row0=g.row0, total_rows=total_rows,
                      into=h_all)
    for i in range(DEPTH):
        w_lora, w2e, vecs = lw[i]
        z = _proj(h_all, w_in, i, w_lora, tm=PROJ_TM, n_main=N_MAIN_TILES, out_dtype=BF16)
        h_all = None
        for g in groups:
            g.shift_out.append(
                _norm(g.x2d.reshape(g.b, g.t, D_MODEL)[:, -1], norm_pre[i], rows=g.b, out_dtype=F32))
            if g.zero_state:
                assert g.t >= POOL_BUF
                prev0 = jnp.zeros((g.b, PREV_COLS), F32)
                g.pool_out.append(jnp.stack([
                    z[g.row0 + (q + 1) * g.t - POOL_BUF:g.row0 + (q + 1) * g.t, Z_U * POOL_WIDTH:(Z_U + 1) * POOL_WIDTH]
                    for q in range(g.b)]).astype(F32))
            else:
                prev0 = _proj(g.shift[i].astype(BF16), w_in, i, w_lora, tm=g.b, n_main=3 * RWKV_WIDTH // PROJ_TN,
                              out_dtype=F32)
            ya, g.v_first, g.s_all = _wkv(z, prev0, g.v_first, w2e, vecs, g.s_all, i, n_batch=g.b, t_len=g.t,
                                          n_seq=g.n_seq, n_sub=g.n_sub, row0=g.row0)
            mm_, g.pool_all = _mix(ya, z, g.pool_all, pool_w_b, pool_scale, w_branch_b, i, n_seq=g.mix_seq,
                                   t_seq=g.mix_t, tiles_per_seq=g.t // g.mix_t, n_past=g.n_past, row0=g.row0)
            g.x2d, h_all = _out(mm_, g.x2d, g.p3d, norm_post, norm_pre, w_out_b, w_gate_b, w_ple_b, i,
                                rows=OUT_ROWS, row0=g.row0, total_rows=total_rows, h_into=h_all)


def kernel(x_prompt, x_sample, state_wkv, state_shift, state_pool, p_prompt, p_sample, norm_pre, norm_post, w_in, mu_rkv, mu_lora, w0, w1, w2, a0, a1, a2, v0, v1, v2, k_k, k_a, r_k, ln_w, ln_b, pool_w, pool_scale, w_branch, w_out, w_ple, w_ple_gate):
    lw = [_layer_weights(i, mu_rkv, mu_lora, w0, w1, w2, a0, a1, a2, v0, v1, v2, k_k, k_a, r_k, ln_w, ln_b)
          for i in range(DEPTH)]
    params = (w_in, norm_pre, norm_post, pool_w.astype(BF16), pool_scale, w_branch.astype(BF16),
              w_out.astype(BF16), w_ple.astype(BF16), w_ple_gate.astype(BF16))
    prompt = _Group(x_prompt, p_prompt, None, None, None, n_past=0, row0=0,
                    n_seq=1, n_sub=8, mix_seq=1, mix_t=256)
    sample = _Group(x_sample, p_sample, state_wkv, state_shift, state_pool, n_past=PAST_LEN, row0=prompt.m,
                    n_seq=8, n_sub=2, mix_seq=32, mix_t=8)
    _run_layers([prompt, sample], lw, params)
    y_p, wkv_p, shift_p, pool_p = prompt.outputs()
    y_s, wkv_s, shift_s, pool_s = sample.outputs()
    return (y_p, y_s, wkv_p, shift_p, pool_p, wkv_s, shift_s, pool_s)
```
